```python
import jax, jax.numpy as jnp
from jax import lax
import numpy as np

D_MODEL = 2048
BATCH = 4
SEQ = 2048
DEPTH = 1
DEC_BATCH = 16
DEC_SEQ = 32
PAST_LEN = 2048

CHUNK = 64
Q_BLOCK = 128
ROPE_THETA = 500000.0
EPS = 1e-6

A_HEADS = 8
A_KV_HEADS = 2
A_HEAD_DIM = 128
A_ROT = A_HEAD_DIM // 4
A_WIDTH = A_HEADS * A_HEAD_DIM
IDX_HEADS = 16
IDX_DIM = 64
IDX_ROT = IDX_DIM // 4
TOPK_MAX = 256

B_HEADS = 8
Q_LORA = 512
KV_LORA = 512
NOPE_DIM = 128
ROPE_DIM = 64
V_DIM = 128
B_WIDTH = B_HEADS * V_DIM

SPLIT_SIZES = (A_WIDTH, A_KV_HEADS * A_HEAD_DIM, A_KV_HEADS * A_HEAD_DIM, A_WIDTH,
               IDX_HEADS * IDX_DIM, IDX_DIM, IDX_HEADS,
               Q_LORA, KV_LORA, ROPE_DIM, B_WIDTH, 2 * D_MODEL)
PROJ_WIDTH = sum(SPLIT_SIZES)

kernel_name = 'dsa_mla_gated_hybrid_stream_step'


def rmsnorm(x, g):
    xf = x.astype(jnp.float32)
    xf = xf * lax.rsqrt(jnp.mean(xf * xf, axis=-1, keepdims=True) + EPS)
    return xf.astype(x.dtype) * g


def rope(x, pos, rot):
    half = rot // 2
    inv_freq = ROPE_THETA ** (-jnp.arange(half, dtype=jnp.float32) * (2.0 / rot))
    ang = pos.astype(jnp.float32)[:, None] * inv_freq[None, :]
    cos = jnp.cos(ang)[:, None, :]
    sin = jnp.sin(ang)[:, None, :]
    x1 = x[..., :half].astype(jnp.float32)
    x2 = x[..., half:rot].astype(jnp.float32)
    r = jnp.concatenate([x1 * cos - x2 * sin, x2 * cos + x1 * sin], axis=-1).astype(x.dtype)
    return jnp.concatenate([r, x[..., rot:]], axis=-1)


def split_proj(z):
    parts, off = [], 0
    for s in SPLIT_SIZES:
        parts.append(z[..., off:off + s])
        off += s
    return parts


def sweep_blocks(fn, qpos, *qs):
    T = qpos.shape[0]
    bs = min(T, Q_BLOCK)
    nb = T // bs
    def blk(a):
        return jnp.moveaxis(a.reshape((a.shape[0], nb, bs) + a.shape[2:]), 1, 0)
    out = lax.map(lambda args: fn(*args), (qpos.reshape(nb, bs),) + tuple(blk(a) for a in qs))
    out = jnp.moveaxis(out, 0, 1)
    return out.reshape((out.shape[0], T) + out.shape[3:])


def dsa_attend(q, k, v, iq, ik, iw, qpos, kpos):
    L = k.shape[1]
    topk = min(TOPK_MAX, L // 4)
    grp = A_HEADS // A_KV_HEADS
    kchunk = kpos // CHUNK
    ikf = ik.astype(jnp.float32)

    def block(qp, qb, iqb, iwb):
        B, Tb = qb.shape[0], qb.shape[1]
        qchunk = qp // CHUNK
        allowed = kchunk[None, :] <= qchunk[:, None]
        rel = jax.nn.relu(jnp.einsum('bthd,bsd->bths', iqb.astype(jnp.float32), ikf) * IDX_DIM ** -0.5)
        score = jnp.einsum('bths,bth->bts', rel, iwb.astype(jnp.float32))
        score = jnp.where(allowed[None], score, -jnp.inf)
        _, sel = lax.top_k(score, topk)
        valid = kchunk[sel] <= qchunk[None, :, None]
        ks = jax.vmap(lambda kk, ii: kk[ii])(k, sel)
        vs = jax.vmap(lambda vv, ii: vv[ii])(v, sel)
        qg = qb.reshape(B, Tb, A_KV_HEADS, grp, A_HEAD_DIM)
        s = jnp.einsum('btkgd,btskd->btkgs', qg, ks).astype(jnp.float32) * A_HEAD_DIM ** -0.5
        s = jnp.where(valid[:, :, None, None, :], s, -jnp.inf)
        p = jax.nn.softmax(s, axis=-1).astype(vs.dtype)
        o = jnp.einsum('btkgs,btskd->btkgd', p, vs)
        return o.reshape(B, Tb, A_WIDTH)

    return sweep_blocks(block, qpos, q, iq, iw)


def mla_attend(q_nope, q_rope, ckv, krope, w_uk, w_uv, qpos, kpos):
    k_nope = jnp.einsum('bsc,chd->bshd', ckv, w_uk)
    v = jnp.einsum('bsc,chd->bshd', ckv, w_uv)
    kchunk = kpos // CHUNK
    scale = (NOPE_DIM + ROPE_DIM) ** -0.5

    def block(qp, qn, qr):
        B, Tb = qn.shape[0], qn.shape[1]
        allowed = kchunk[None, :] <= (qp // CHUNK)[:, None]
        s = (jnp.einsum('bthd,bshd->bhts', qn, k_nope)
             + jnp.einsum('bthr,bsr->bhts', qr, krope)).astype(jnp.float32) * scale
        s = jnp.where(allowed[None, None], s, -jnp.inf)
        p = jax.nn.softmax(s, axis=-1).astype(v.dtype)
        o = jnp.einsum('bhts,bshd->bthd', p, v)
        return o.reshape(B, Tb, B_WIDTH)

    return sweep_blocks(block, qpos, q_nope, q_rope)


def layer(h, qpos, past, norm_g, w_in, w_uq, q_norm_g, kv_norm_g, w_uk, w_uv, w_o_a, w_o_b, w_out):
    B, T, _ = h.shape
    pk, pv, pik, pckv, pkr = past
    L = pk.shape[1] + T
    kpos = jnp.arange(L, dtype=jnp.int32)
    xn = rmsnorm(h, norm_g)
    aq, ak, av, agate, iq, ik, iw, cq, ckv, kr, bgate, mg = split_proj(xn @ w_in)
    aq = rope(aq.reshape(B, T, A_HEADS, A_HEAD_DIM), qpos, A_ROT)
    ak = rope(ak.reshape(B, T, A_KV_HEADS, A_HEAD_DIM), qpos, A_ROT)
    av = av.reshape(B, T, A_KV_HEADS, A_HEAD_DIM)
    iq = rope(iq.reshape(B, T, IDX_HEADS, IDX_DIM), qpos, IDX_ROT)
    ik = rope(ik[:, :, None, :], qpos, IDX_ROT)[:, :, 0, :]
    iw = iw * IDX_HEADS ** -0.5
    o_a = dsa_attend(aq, jnp.concatenate([pk, ak], axis=1), jnp.concatenate([pv, av], axis=1),
                     iq, jnp.concatenate([pik, ik], axis=1), iw, qpos, kpos)
    y_a = (o_a * jax.nn.silu(agate)) @ w_o_a
    cq = rmsnorm(cq, q_norm_g)
    qb = jnp.einsum('btc,chd->bthd', cq, w_uq)
    q_nope = qb[..., :NOPE_DIM]
    q_rope = rope(qb[..., NOPE_DIM:], qpos, ROPE_DIM)
    ckv = rmsnorm(ckv, kv_norm_g)
    kr = rope(kr[:, :, None, :], qpos, ROPE_DIM)[:, :, 0, :]
    o_b = mla_attend(q_nope, q_rope, jnp.concatenate([pckv, ckv], axis=1),
                     jnp.concatenate([pkr, kr], axis=1), w_uk, w_uv, qpos, kpos)
    y_b = (o_b * jax.nn.silu(bgate)) @ w_o_b
    g = jax.nn.sigmoid(mg.astype(jnp.float32)).astype(h.dtype)
    m = g[..., :D_MODEL] * y_a + g[..., D_MODEL:] * y_b
    h = h + m @ w_out
    return h, (ak, av, ik, ckv, kr)


def trunk(x, caches, final_g, params):
    past_len = caches[0].shape[2]
    qpos = past_len + jnp.arange(x.shape[1], dtype=jnp.int32)
    h = x
    new = []
    for l in range(DEPTH):
        h, rows = layer(h, qpos, tuple(c[l] for c in caches), *(p[l] for p in params))
        new.append(rows)
    y = rmsnorm(h, final_g)
    stacked = [jnp.stack([r[i] for r in new], axis=0) for i in range(5)]
    return y, stacked


def setup_inputs(seed: int = 0) -> dict:
    key = jax.random.key(seed)
    ks = jax.random.split(key, 20)
    f32 = jnp.float32
    nrm = lambda k, shape, s: jax.random.normal(k, shape, f32) * s
    return {
        'x_prompt': nrm(ks[0], (BATCH, SEQ, D_MODEL), 1.0),
        'x_sample': nrm(ks[1], (DEC_BATCH, DEC_SEQ, D_MODEL), 1.0),
        'cache_a_k': nrm(ks[2], (DEPTH, DEC_BATCH, PAST_LEN, A_KV_HEADS, A_HEAD_DIM), 1.0),
        'cache_a_v': nrm(ks[3], (DEPTH, DEC_BATCH, PAST_LEN, A_KV_HEADS, A_HEAD_DIM), 1.0),
        'cache_idx_k': nrm(ks[4], (DEPTH, DEC_BATCH, PAST_LEN, IDX_DIM), 1.0),
        'cache_mla_ckv': nrm(ks[5], (DEPTH, DEC_BATCH, PAST_LEN, KV_LORA), 1.0),
        'cache_mla_krope': nrm(ks[6], (DEPTH, DEC_BATCH, PAST_LEN, ROPE_DIM), 1.0),
        'norm_g': 1.0 + nrm(ks[7], (DEPTH, D_MODEL), 0.01),
        'w_in': nrm(ks[8], (DEPTH, D_MODEL, PROJ_WIDTH), D_MODEL ** -0.5),
        'w_uq': nrm(ks[9], (DEPTH, Q_LORA, B_HEADS, NOPE_DIM + ROPE_DIM), Q_LORA ** -0.5),
        'q_norm_g': 1.0 + nrm(ks[10], (DEPTH, Q_LORA), 0.01),
        'kv_norm_g': 1.0 + nrm(ks[11], (DEPTH, KV_LORA), 0.01),
        'w_uk': nrm(ks[12], (DEPTH, KV_LORA, B_HEADS, NOPE_DIM), KV_LORA ** -0.5),
        'w_uv': nrm(ks[13], (DEPTH, KV_LORA, B_HEADS, V_DIM), KV_LORA ** -0.5),
        'w_o_a': nrm(ks[14], (DEPTH, A_WIDTH, D_MODEL), A_WIDTH ** -0.5),
        'w_o_b': nrm(ks[15], (DEPTH, B_WIDTH, D_MODEL), B_WIDTH ** -0.5),
        'w_out': nrm(ks[16], (DEPTH, D_MODEL, D_MODEL), D_MODEL ** -0.5),
        'final_g': 1.0 + nrm(ks[17], (D_MODEL,), 0.01),
    }


def reference(x_prompt, x_sample, cache_a_k, cache_a_v, cache_idx_k, cache_mla_ckv, cache_mla_krope,
              norm_g, w_in, w_uq, q_norm_g, kv_norm_g, w_uk, w_uv, w_o_a, w_o_b, w_out, final_g):
    params = (norm_g, w_in, w_uq, q_norm_g, kv_norm_g, w_uk, w_uv, w_o_a, w_o_b, w_out)
    dt = x_prompt.dtype
    empty = (jnp.zeros((DEPTH, x_prompt.shape[0], 0, A_KV_HEADS, A_HEAD_DIM), dt),
             jnp.zeros((DEPTH, x_prompt.shape[0], 0, A_KV_HEADS, A_HEAD_DIM), dt),
             jnp.zeros((DEPTH, x_prompt.shape[0], 0, IDX_DIM), dt),
             jnp.zeros((DEPTH, x_prompt.shape[0], 0, KV_LORA), dt),
             jnp.zeros((DEPTH, x_prompt.shape[0], 0, ROPE_DIM), dt))
    y_prompt, (pk, pv, pik, pckv, pkr) = trunk(x_prompt, empty, final_g, params)
    caches = (cache_a_k, cache_a_v, cache_idx_k, cache_mla_ckv, cache_mla_krope)
    y_sample, (sk, sv, sik, sckv, skr) = trunk(x_sample, caches, final_g, params)
    return (y_prompt, y_sample, pk, pv, pik, pckv, pkr, sk, sv, sik, sckv, skr)
```

```python
import functools

import jax
import jax.numpy as jnp
from jax import lax
from jax.experimental import pallas as pl
from jax.experimental.pallas import tpu as pltpu

F32 = jnp.float32
BF16 = jnp.bfloat16

CHUNK = 64
ROPE_THETA = 500000.0
EPS = 1e-6

A_HEADS = 8
A_KV_HEADS = 2
A_HEAD_DIM = 128
A_ROT = A_HEAD_DIM // 4
A_WIDTH = A_HEADS * A_HEAD_DIM
A_KV_WIDTH = A_KV_HEADS * A_HEAD_DIM
IDX_HEADS = 16
IDX_DIM = 64
IDX_ROT = IDX_DIM // 4
TOPK_MAX = 256

B_HEADS = 8
Q_LORA = 512
KV_LORA = 512
NOPE_DIM = 128
ROPE_DIM = 64
V_DIM = 128
B_WIDTH = B_HEADS * V_DIM

LANES = 128
INT_MIN = -(2 ** 31)
VMEM_LIMIT = 56 * 1024 * 1024

_SEGS = (("aq", A_WIDTH), ("ak", A_KV_WIDTH), ("av", A_KV_WIDTH), ("agate", A_WIDTH),
         ("iq", IDX_HEADS * IDX_DIM), ("ik", IDX_DIM), ("iw", IDX_HEADS),
         ("cq", Q_LORA), ("ckv", KV_LORA), ("kr", ROPE_DIM), ("bgate", B_WIDTH), ("mg", None))


def _layout(d_model):
    src, off = {}, 0
    for name, w in _SEGS:
        w = 2 * d_model if w is None else w
        src[name] = (off, w)
        off += w
    order = (("mg", 2 * d_model), ("aq", A_WIDTH), ("agate", A_WIDTH), ("iq", IDX_HEADS * IDX_DIM),
             ("bgate", B_WIDTH), ("cq", Q_LORA), ("ckv", KV_LORA), ("ak", A_KV_WIDTH), ("av", A_KV_WIDTH),
             ("ikw", LANES), ("krp", LANES))
    dst, off = {}, 0
    for name, w in order:
        assert off % w == 0
        dst[name] = (off, w)
        off += w
    return src, dst, off


def _regroup_w_in(w_in, d_model):
    src, dst, total = _layout(d_model)
    def cols(name):
        o, w = src[name]
        return w_in[:, o:o + w]
    zeros = lambda n: jnp.zeros((w_in.shape[0], n), w_in.dtype)
    pieces = [cols("mg"), cols("aq"), cols("agate"), cols("iq"), cols("bgate"), cols("cq"), cols("ckv"),
              cols("ak"), cols("av"),
              cols("ik"), cols("iw"), zeros(LANES - IDX_DIM - IDX_HEADS),
              cols("kr"), zeros(LANES - ROPE_DIM)]
    w = jnp.concatenate(pieces, axis=1).astype(BF16)
    assert w.shape[1] == total
    return w


def _rope_tables(pos, rot, head, heads_per_vreg, extra=None):
    half = rot // 2
    inv_freq = ROPE_THETA ** (-jnp.arange(half, dtype=F32) * (2.0 / rot))
    ang = pos.astype(F32)[:, None] * inv_freq[None, :]
    cos, sin = jnp.cos(ang), jnp.sin(ang)
    t = pos.shape[0]
    one = jnp.ones((t, head - rot), F32)
    zero = lambda n: jnp.zeros((t, n), F32)
    c = jnp.concatenate([cos, cos, one], axis=1)
    sa = jnp.concatenate([-sin, zero(head - half)], axis=1)
    sb = jnp.concatenate([zero(half), sin, zero(head - rot)], axis=1)
    c, sa, sb = (jnp.tile(a, (1, heads_per_vreg)) for a in (c, sa, sb))
    if extra is not None:
        c = jnp.concatenate([c, extra], axis=1)
        sa = jnp.concatenate([sa, jnp.zeros_like(extra)], axis=1)
        sb = jnp.concatenate([sb, jnp.zeros_like(extra)], axis=1)
    assert c.shape[1] == LANES
    return c, sa, sb


def _rope(x, c, sa, sb, half):
    up = pltpu.roll(x, LANES - half, 1)
    dn = pltpu.roll(x, half, 1)
    return x * c + up * sa + dn * sb


def _rms(x, g):
    ms = jnp.mean(x * x, axis=-1, keepdims=True)
    return (x * lax.rsqrt(ms + EPS)) * g


def _proj_kernel(x_ref, g_ref, w_ref, z_ref, xn_ref):
    @pl.when(pl.program_id(1) == 0)
    def _():
        xn_ref[...] = _rms(x_ref[...], g_ref[...]).astype(BF16)
    z_ref[...] = jnp.dot(xn_ref[...], w_ref[...], preferred_element_type=F32)


def _proj(x, g, w):
    m, d = x.shape
    n = w.shape[1]
    tm = min(m, 1024)
    tn = 768
    assert m % tm == 0 and n % tn == 0
    return pl.pallas_call(
        _proj_kernel,
        grid=(m // tm, n // tn),
        in_specs=[pl.BlockSpec((tm, d), lambda i, j: (i, 0)),
                  pl.BlockSpec((1, d), lambda i, j: (0, 0)),
                  pl.BlockSpec((d, tn), lambda i, j: (0, j))],
        out_specs=pl.BlockSpec((tm, tn), lambda i, j: (i, j)),
        out_shape=jax.ShapeDtypeStruct((m, n), F32),
        scratch_shapes=[pltpu.VMEM((tm, d), BF16)],
        compiler_params=pltpu.CompilerParams(dimension_semantics=("arbitrary", "arbitrary"),
                                             vmem_limit_bytes=VMEM_LIMIT),
        name="proj",
    )(x, g, w)


def _prep_kernel(aq_ref, iq_ref, cq_ref, ckv_ref, ak_ref, ikw_ref, krp_ref,
                 ta_ref, ti_ref, tk_ref, tr_ref, qg_ref, kvg_ref, wuq_ref,
                 aq_o, iq_o, ak_o, ikw_o, krp_o, ckv_o, qn_o, qr_o):
    ca, saa, sba = ta_ref[0], ta_ref[1], ta_ref[2]
    ci, sai, sbi = ti_ref[0], ti_ref[1], ti_ref[2]
    ck, sak, sbk = tk_ref[0], tk_ref[1], tk_ref[2]
    cr, sar, sbr = tr_ref[0], tr_ref[1], tr_ref[2]
    for h in range(A_HEADS):
        sl = slice(h * LANES, (h + 1) * LANES)
        aq_o[:, sl] = _rope(aq_ref[:, sl], ca, saa, sba, A_ROT // 2).astype(BF16)
    for h in range(A_KV_HEADS):
        sl = slice(h * LANES, (h + 1) * LANES)
        ak_o[:, sl] = _rope(ak_ref[:, sl], ca, saa, sba, A_ROT // 2)
    for v in range(IDX_HEADS * IDX_DIM // LANES):
        sl = slice(v * LANES, (v + 1) * LANES)
        iq_o[:, sl] = _rope(iq_ref[:, sl], ci, sai, sbi, IDX_ROT // 2).astype(BF16)
    ikw_o[...] = _rope(ikw_ref[...], ck, sak, sbk, IDX_ROT // 2)
    krp_o[...] = _rope(krp_ref[...], cr, sar, sbr, ROPE_DIM // 2)
    ckv_o[...] = _rms(ckv_ref[...], kvg_ref[...])
    cqn = _rms(cq_ref[...], qg_ref[...]).astype(BF16)
    qb = jnp.dot(cqn, wuq_ref[...], preferred_element_type=F32)
    nope = B_HEADS * NOPE_DIM
    qn_o[...] = qb[:, :nope].astype(BF16)
    for v in range(B_HEADS * ROPE_DIM // LANES):
        sl = slice(v * LANES, (v + 1) * LANES)
        qr_o[:, sl] = _rope(qb[:, nope + v * LANES: nope + (v + 1) * LANES], cr, sar, sbr, ROPE_DIM // 2).astype(BF16)


def _prep(z, dst, bsz, t, tabs, q_norm_g, kv_norm_g, wuq):
    m = bsz * t
    tm = min(t, 256)
    nt = t // tm
    assert t % tm == 0

    def zspec(name):
        o, w = dst[name]
        return pl.BlockSpec((tm, w), lambda b, i, blk=o // w: (b * nt + i, blk))

    def tspec():
        return pl.BlockSpec((3, tm, LANES), lambda b, i: (0, i, 0))

    def ospec(w):
        return pl.BlockSpec((tm, w), lambda b, i: (b * nt + i, 0))

    def full(a):
        return pl.BlockSpec(a.shape, lambda b, i: (0,) * a.ndim)

    outs = [(A_WIDTH, BF16), (IDX_HEADS * IDX_DIM, BF16), (A_KV_WIDTH, F32), (LANES, F32), (LANES, F32),
            (KV_LORA, F32), (B_HEADS * NOPE_DIM, BF16), (B_HEADS * ROPE_DIM, BF16)]
    return pl.pallas_call(
        _prep_kernel,
        grid=(bsz, nt),
        in_specs=[zspec("aq"), zspec("iq"), zspec("cq"), zspec("ckv"), zspec("ak"), zspec("ikw"), zspec("krp"),
                  tspec(), tspec(), tspec(), tspec(), full(q_norm_g), full(kv_norm_g), full(wuq)],
        out_specs=[ospec(w) for w, _ in outs],
        out_shape=[jax.ShapeDtypeStruct((m, w), dt) for w, dt in outs],
        compiler_params=pltpu.CompilerParams(dimension_semantics=("arbitrary", "arbitrary"),
                                             vmem_limit_bytes=VMEM_LIMIT),
        name="prep",
    )(z, z, z, z, z, z, z, *tabs, q_norm_g, kv_norm_g, wuq)


def _kvup_kernel(c_ref, w_ref, kn_ref, v_ref):
    r = jnp.dot(c_ref[...], w_ref[...], preferred_element_type=F32)
    n = kn_ref.shape[1]
    kn_ref[...] = r[:, :n].astype(BF16)
    v_ref[...] = r[:, n:].astype(BF16)


def _kvup(ckv, w):
    m, c = ckv.shape
    n = w.shape[1] // 2
    tm = next(c for c in (512, 544, 384, 256, 128) if m % c == 0)
    return pl.pallas_call(
        _kvup_kernel,
        grid=(m // tm,),
        in_specs=[pl.BlockSpec((tm, c), lambda i: (i, 0)), pl.BlockSpec(w.shape, lambda i: (0, 0))],
        out_specs=[pl.BlockSpec((tm, n), lambda i: (i, 0)), pl.BlockSpec((tm, n), lambda i: (i, 0))],
        out_shape=[jax.ShapeDtypeStruct((m, n), BF16), jax.ShapeDtypeStruct((m, n), BF16)],
        compiler_params=pltpu.CompilerParams(dimension_semantics=("arbitrary",), vmem_limit_bytes=VMEM_LIMIT),
        name="kv_up",
    )(ckv, w)


_NT = (((1,), (1,)), ((), ()))


def _allowed(tq, lp, l_valid, qpos0):
    rows = lax.broadcasted_iota(jnp.int32, (tq, lp), 0)
    cols = lax.broadcasted_iota(jnp.int32, (tq, lp), 1)
    qchunk = (qpos0 + pl.program_id(1) * tq + rows) // CHUNK
    return ((cols // CHUNK) <= qchunk) & (cols < l_valid), cols


def _dsa_kernel(aq_ref, iq_ref, ikw_ref, agate_ref, k_ref, v_ref, ikt_ref, o_ref, key_ref, jcut_ref,
                *, tq, l_valid, lp, qpos0, topk):
    allowed, cols = _allowed(tq, lp, l_valid, qpos0)

    iw = ikw_ref[:, IDX_DIM:IDX_DIM + IDX_HEADS] * (IDX_DIM ** -0.5)
    ikt = ikt_ref[0]
    sc = jnp.zeros((tq, lp), F32)
    for h in range(IDX_HEADS):
        rel = jnp.dot(iq_ref[:, h * IDX_DIM:(h + 1) * IDX_DIM], ikt, preferred_element_type=F32)
        sc = sc + jnp.maximum(rel, 0.0) * iw[:, h:h + 1]
    sc = sc + 0.0

    bits = lax.bitcast_convert_type(sc, jnp.int32)
    key = bits ^ ((bits >> 31) & 0x7FFFFFFF)
    key_ref[...] = jnp.where(allowed, key, INT_MIN)

    kf = float(topk)

    def count(pred):
        return jnp.sum(jnp.where(pred, 1.0, 0.0), axis=-1, keepdims=True)

    c0 = count(key_ref[...] >= 0)
    t0 = jnp.where(c0 >= kf, 0, INT_MIN).astype(jnp.int32)

    def bit_step(b, t):
        cand = t | jnp.left_shift(jnp.int32(1), 30 - b)
        return jnp.where(count(key_ref[...] >= cand) >= kf, cand, t)

    thr = lax.fori_loop(0, 31, bit_step, t0)

    key = key_ref[...]
    gt = key > thr
    eq = key == thr
    need = kf - count(gt)
    excess = (count(eq) > need) & (thr > INT_MIN)
    jcut_ref[...] = jnp.full((tq, 1), lp, jnp.int32)

    @pl.when(jnp.max(jnp.where(excess, 1, 0)) > 0)
    def _():
        def col_step(b, j):
            cand = j | jnp.left_shift(jnp.int32(1), 11 - b)
            f = count((key_ref[...] == thr) & (cols < cand))
            return jnp.where(f <= need, cand, j)
        jcut_ref[...] = lax.fori_loop(0, 12, col_step, jnp.zeros((tq, 1), jnp.int32))

    sel = (gt | (eq & (cols < jcut_ref[...]))) & allowed
    bias = jnp.where(sel, 0.0, -jnp.inf)

    scale = A_HEAD_DIM ** -0.5
    grp = A_HEADS // A_KV_HEADS
    for g in range(A_KV_HEADS):
        kg = k_ref[0, :, g * A_HEAD_DIM:(g + 1) * A_HEAD_DIM]
        vg = v_ref[0, :, g * A_HEAD_DIM:(g + 1) * A_HEAD_DIM]
        for hh in range(grp):
            sl = slice((g * grp + hh) * A_HEAD_DIM, (g * grp + hh + 1) * A_HEAD_DIM)
            s = lax.dot_general(aq_ref[:, sl], kg, _NT, preferred_element_type=F32) * scale + bias
            e = jnp.exp(s - jnp.max(s, axis=-1, keepdims=True))
            o = jnp.dot(e.astype(BF16), vg, preferred_element_type=F32) / jnp.sum(e, axis=-1, keepdims=True)
            gate = agate_ref[:, sl]
            o_ref[:, sl] = (o * (gate * jax.nn.sigmoid(gate))).astype(BF16)


def _dsa(aq, iq, ikw, z, dst, k, v, ikt, bsz, t, l_valid, qpos0):
    lp = k.shape[1]
    assert lp <= 4096
    tq = min(t, 128)
    nt = t // tq
    topk = min(TOPK_MAX, l_valid // 4)
    go, gw = dst["agate"]
    row = lambda b, i: (b * nt + i, 0)
    kern = functools.partial(_dsa_kernel, tq=tq, l_valid=l_valid, lp=lp, qpos0=qpos0, topk=topk)
    return pl.pallas_call(
        kern,
        grid=(bsz, nt),
        in_specs=[pl.BlockSpec((tq, A_WIDTH), row),
                  pl.BlockSpec((tq, IDX_HEADS * IDX_DIM), row),
                  pl.BlockSpec((tq, LANES), row),
                  pl.BlockSpec((tq, gw), lambda b, i: (b * nt + i, go // gw)),
                  pl.BlockSpec((1, lp, A_KV_WIDTH), lambda b, i: (b, 0, 0)),
                  pl.BlockSpec((1, lp, A_KV_WIDTH), lambda b, i: (b, 0, 0)),
                  pl.BlockSpec((1, IDX_DIM, lp), lambda b, i: (b, 0, 0))],
        out_specs=pl.BlockSpec((tq, A_WIDTH), row),
        out_shape=jax.ShapeDtypeStruct((bsz * t, A_WIDTH), BF16),
        scratch_shapes=[pltpu.VMEM((tq, lp), jnp.int32), pltpu.VMEM((tq, 1), jnp.int32)],
        compiler_params=pltpu.CompilerParams(dimension_semantics=("arbitrary", "arbitrary"),
                                             vmem_limit_bytes=VMEM_LIMIT),
        name="dsa",
    )(aq, iq, ikw, z, k, v, ikt)


def _mla_kernel(qn_ref, qr_ref, bgate_ref, kn_ref, kr_ref, v_ref, o_ref, *, tq, l_valid, lp, qpos0):
    allowed, _ = _allowed(tq, lp, l_valid, qpos0)
    bias = jnp.where(allowed, 0.0, -jnp.inf)
    scale = (NOPE_DIM + ROPE_DIM) ** -0.5
    kr = kr_ref[0]
    for h in range(B_HEADS):
        sl = slice(h * NOPE_DIM, (h + 1) * NOPE_DIM)
        s = lax.dot_general(qn_ref[:, sl], kn_ref[0, :, sl], _NT, preferred_element_type=F32)
        s = s + lax.dot_general(qr_ref[:, h * ROPE_DIM:(h + 1) * ROPE_DIM], kr, _NT, preferred_element_type=F32)
        s = s * scale + bias
        e = jnp.exp(s - jnp.max(s, axis=-1, keepdims=True))
        o = jnp.dot(e.astype(BF16), v_ref[0, :, sl], preferred_element_type=F32) / jnp.sum(e, axis=-1, keepdims=True)
        gate = bgate_ref[:, sl]
        o_ref[:, sl] = (o * (gate * jax.nn.sigmoid(gate))).astype(BF16)


def _mla(qn, qr, z, dst, kn, kr, v, bsz, t, l_valid, qpos0):
    lp = kn.shape[1]
    tq = min(t, 128)
    nt = t // tq
    go, gw = dst["bgate"]
    row = lambda b, i: (b * nt + i, 0)
    kern = functools.partial(_mla_kernel, tq=tq, l_valid=l_valid, lp=lp, qpos0=qpos0)
    return pl.pallas_call(
        kern,
        grid=(bsz, nt),
        in_specs=[pl.BlockSpec((tq, B_HEADS * NOPE_DIM), row),
                  pl.BlockSpec((tq, B_HEADS * ROPE_DIM), row),
                  pl.BlockSpec((tq, gw), lambda b, i: (b * nt + i, go // gw)),
                  pl.BlockSpec((1, lp, B_HEADS * NOPE_DIM), lambda b, i: (b, 0, 0)),
                  pl.BlockSpec((1, lp, ROPE_DIM), lambda b, i: (b, 0, 0)),
                  pl.BlockSpec((1, lp, B_WIDTH), lambda b, i: (b, 0, 0))],
        out_specs=pl.BlockSpec((tq, B_WIDTH), row),
        out_shape=jax.ShapeDtypeStruct((bsz * t, B_WIDTH), BF16),
        compiler_params=pltpu.CompilerParams(dimension_semantics=("arbitrary", "arbitrary"),
                                             vmem_limit_bytes=VMEM_LIMIT),
        name="mla",
    )(qn, qr, z, kn, kr, v)


def _out_kernel(ga_ref, gb_ref, mg_ref, x_ref, woa_ref, wob_ref, wout_ref, fg_ref, o_ref, *, d, final):
    ya = jnp.dot(ga_ref[...], woa_ref[...], preferred_element_type=F32)
    yb = jnp.dot(gb_ref[...], wob_ref[...], preferred_element_type=F32)
    m = jax.nn.sigmoid(mg_ref[:, :d]) * ya + jax.nn.sigmoid(mg_ref[:, d:]) * yb
    h = x_ref[...] + jnp.dot(m.astype(BF16), wout_ref[...], preferred_element_type=F32)
    o_ref[...] = _rms(h, fg_ref[...]) if final else h


def _out(ga, gb, z, dst, x, woa, wob, wout, fg, final):
    m, d = x.shape
    tm = min(m, 256)
    assert m % tm == 0 and dst["mg"][0] == 0
    row = lambda i: (i, 0)
    const = lambda a: pl.BlockSpec(a.shape, lambda i: (0, 0), pipeline_mode=pl.Buffered(1))
    return pl.pallas_call(
        functools.partial(_out_kernel, d=d, final=final),
        grid=(m // tm,),
        in_specs=[pl.BlockSpec((tm, A_WIDTH), row), pl.BlockSpec((tm, B_WIDTH), row),
                  pl.BlockSpec((tm, 2 * d), row), pl.BlockSpec((tm, d), row),
                  const(woa), const(wob), const(wout), const(fg)],
        out_specs=pl.BlockSpec((tm, d), row),
        out_shape=jax.ShapeDtypeStruct((m, d), F32),
        compiler_params=pltpu.CompilerParams(dimension_semantics=("arbitrary",), vmem_limit_bytes=VMEM_LIMIT),
        name="out",
    )(ga, gb, z, x, woa, wob, wout, fg)


def _pad_keys(cache, new, lp):
    parts = ([] if cache is None else [cache.astype(BF16)]) + [new.astype(BF16)]
    kk = jnp.concatenate(parts, axis=1) if len(parts) > 1 else parts[0]
    pad = lp - kk.shape[1]
    return jnp.pad(kk, ((0, 0), (0, pad), (0, 0))) if pad else kk


def _trunk(x, caches, final_g, weights):
    bsz, t, d = x.shape
    depth = weights["w_in"].shape[0]
    past = 0 if caches is None else caches[0].shape[2]
    l_valid = past + t
    lp = -(-l_valid // LANES) * LANES
    _, dst, _ = _layout(d)
    pos = past + jnp.arange(t, dtype=jnp.int32)
    t_scale = jnp.concatenate([jnp.full((t, IDX_HEADS), IDX_HEADS ** -0.5, F32),
                               jnp.ones((t, LANES - IDX_DIM - IDX_HEADS), F32)], axis=1)
    tabs = [jnp.stack(_rope_tables(pos, A_ROT, A_HEAD_DIM, 1)),
            jnp.stack(_rope_tables(pos, IDX_ROT, IDX_DIM, 2)),
            jnp.stack(_rope_tables(pos, IDX_ROT, IDX_DIM, 1, extra=t_scale)),
            jnp.stack(_rope_tables(pos, ROPE_DIM, ROPE_DIM, 2))]
    h = x.reshape(bsz * t, d)
    new_rows = []
    for l in range(depth):
        w = {k: v[l] for k, v in weights.items()}
        z = _proj(h, w["norm_g"], w["w_in"])
        aq, iq, ak, ikw, krp, ckv, qn, qr = _prep(z, dst, bsz, t, tabs, w["q_norm_g"], w["kv_norm_g"], w["w_uq"])
        o, wd = dst["av"]
        av = z[:, o:o + wd]
        ik = ikw[:, :IDX_DIM]
        kr = krp[:, :ROPE_DIM]
        c = (None,) * 5 if caches is None else tuple(cc[l] for cc in caches)
        b3 = lambda a: a.reshape(bsz, t, a.shape[-1])
        k_all = _pad_keys(None if c[0] is None else c[0].reshape(bsz, past, A_KV_WIDTH), b3(ak), lp)
        v_all = _pad_keys(None if c[1] is None else c[1].reshape(bsz, past, A_KV_WIDTH), b3(av), lp)
        ikt_all = jnp.swapaxes(_pad_keys(c[2], b3(ik), lp), 1, 2)
        ckv_all = _pad_keys(c[3], b3(ckv), lp)
        kr_all = _pad_keys(c[4], b3(kr), lp)
        ga = _dsa(aq, iq, ikw, z, dst, k_all, v_all, ikt_all, bsz, t, l_valid, past)
        kn, vv = _kvup(ckv_all.reshape(bsz * lp, KV_LORA), w["w_ukv"])
        gb = _mla(qn, qr, z, dst, kn.reshape(bsz, lp, -1), kr_all, vv.reshape(bsz, lp, -1), bsz, t, l_valid, past)
        h = _out(ga, gb, z, dst, h, w["w_o_a"], w["w_o_b"], w["w_out"], final_g, final=(l == depth - 1))
        new_rows.append((ak.reshape(bsz, t, A_KV_HEADS, A_HEAD_DIM), av.reshape(bsz, t, A_KV_HEADS, A_HEAD_DIM),
                         ik.reshape(bsz, t, IDX_DIM), ckv.reshape(bsz, t, KV_LORA), kr.reshape(bsz, t, ROPE_DIM)))
    stacked = [jnp.stack([r[i] for r in new_rows], axis=0) for i in range(5)]
    return h.reshape(bsz, t, d), stacked


def kernel(x_prompt, x_sample, cache_a_k, cache_a_v, cache_idx_k, cache_mla_ckv, cache_mla_krope,
           norm_g, w_in, w_uq, q_norm_g, kv_norm_g, w_uk, w_uv, w_o_a, w_o_b, w_out, final_g):
    depth, d, _ = w_in.shape
    weights = {
        "norm_g": norm_g.reshape(depth, 1, d),
        "w_in": jnp.stack([_regroup_w_in(w_in[l], d) for l in range(depth)]),
        "w_uq": jnp.concatenate([w_uq[..., :NOPE_DIM].reshape(depth, Q_LORA, B_HEADS * NOPE_DIM),
                                 w_uq[..., NOPE_DIM:].reshape(depth, Q_LORA, B_HEADS * ROPE_DIM)], axis=-1).astype(BF16),
        "q_norm_g": q_norm_g.reshape(depth, 1, Q_LORA),
        "kv_norm_g": kv_norm_g.reshape(depth, 1, KV_LORA),
        "w_ukv": jnp.concatenate([w_uk.reshape(depth, KV_LORA, B_HEADS * NOPE_DIM),
                                  w_uv.reshape(depth, KV_LORA, B_WIDTH)], axis=-1).astype(BF16),
        "w_o_a": w_o_a.astype(BF16), "w_o_b": w_o_b.astype(BF16), "w_out": w_out.astype(BF16),
    }
    fg = final_g.reshape(1, d)
    y_p, (pk, pv, pik, pckv, pkr) = _trunk(x_prompt, None, fg, weights)
    caches = (cache_a_k, cache_a_v, cache_idx_k, cache_mla_ckv, cache_mla_krope)
    y_s, (sk, sv, sik, sckv, skr) = _trunk(x_sample, caches, fg, weights)
    return (y_p, y_s, pk, pv, pik, pckv, pkr, sk, sv, sik, sckv, skr)
```

```python
import functools

import jax
import jax.numpy as jnp
from jax import lax
from jax.experimental import pallas as pl
from jax.experimental.pallas import tpu as pltpu

F32 = jnp.float32
BF16 = jnp.bfloat16

CHUNK = 64
ROPE_THETA = 500000.0
EPS = 1e-6

A_HEADS = 8
A_KV_HEADS = 2
A_HEAD_DIM = 128
A_ROT = A_HEAD_DIM // 4
A_WIDTH = A_HEADS * A_HEAD_DIM
A_KV_WIDTH = A_KV_HEADS * A_HEAD_DIM
IDX_HEADS = 16
IDX_DIM = 64
IDX_ROT = IDX_DIM // 4
TOPK_MAX = 256

B_HEADS = 8
Q_LORA = 512
KV_LORA = 512
NOPE_DIM = 128
ROPE_DIM = 64
V_DIM = 128
B_WIDTH = B_HEADS * V_DIM
B_QK_PAD = 256

LANES = 128
KEY_BLOCK = 256
INT_MIN = -(2 ** 31)
NEG = -1e30
VMEM_LIMIT = 56 * 1024 * 1024

_SEGS = (("aq", A_WIDTH), ("ak", A_KV_WIDTH), ("av", A_KV_WIDTH), ("agate", A_WIDTH),
         ("iq", IDX_HEADS * IDX_DIM), ("ik", IDX_DIM), ("iw", IDX_HEADS),
         ("cq", Q_LORA), ("ckv", KV_LORA), ("kr", ROPE_DIM), ("bgate", B_WIDTH), ("mg", None))


def _layout(d_model):
    src, off = {}, 0
    for name, w in _SEGS:
        w = 2 * d_model if w is None else w
        src[name] = (off, w)
        off += w
    order = (("mg", 2 * d_model), ("aq", A_WIDTH), ("agate", A_WIDTH), ("iq", IDX_HEADS * IDX_DIM),
             ("bgate", B_WIDTH), ("cq", Q_LORA), ("ckv", KV_LORA), ("ak", A_KV_WIDTH), ("av", A_KV_WIDTH),
             ("ikw", LANES), ("krp", LANES))
    dst, off = {}, 0
    for name, w in order:
        assert off % w == 0
        dst[name] = (off, w)
        off += w
    return src, dst, off


def _regroup_w_in(w_in, d_model):
    src, dst, total = _layout(d_model)
    def cols(name):
        o, w = src[name]
        return w_in[:, o:o + w]
    zeros = lambda n: jnp.zeros((w_in.shape[0], n), w_in.dtype)
    pieces = [cols("mg"), cols("aq"), cols("agate"), cols("iq"), cols("bgate"), cols("cq"), cols("ckv"),
              cols("ak"), cols("av"),
              cols("ik"), cols("iw"), zeros(LANES - IDX_DIM - IDX_HEADS),
              cols("kr"), zeros(LANES - ROPE_DIM)]
    w = jnp.concatenate(pieces, axis=1).astype(BF16)
    assert w.shape[1] == total
    return w


def _rope_tables(pos, rot, head, heads_per_vreg, extra=None):
    half = rot // 2
    inv_freq = ROPE_THETA ** (-jnp.arange(half, dtype=F32) * (2.0 / rot))
    ang = pos.astype(F32)[:, None] * inv_freq[None, :]
    cos, sin = jnp.cos(ang), jnp.sin(ang)
    t = pos.shape[0]
    one = jnp.ones((t, head - rot), F32)
    zero = lambda n: jnp.zeros((t, n), F32)
    c = jnp.concatenate([cos, cos, one], axis=1)
    sa = jnp.concatenate([-sin, zero(head - half)], axis=1)
    sb = jnp.concatenate([zero(half), sin, zero(head - rot)], axis=1)
    c, sa, sb = (jnp.tile(a, (1, heads_per_vreg)) for a in (c, sa, sb))
    if extra is not None:
        c = jnp.concatenate([c, extra], axis=1)
        sa = jnp.concatenate([sa, jnp.zeros_like(extra)], axis=1)
        sb = jnp.concatenate([sb, jnp.zeros_like(extra)], axis=1)
    assert c.shape[1] == LANES
    return c, sa, sb


def _rope(x, c, sa, sb, half):
    up = pltpu.roll(x, LANES - half, 1)
    dn = pltpu.roll(x, half, 1)
    return x * c + up * sa + dn * sb


def _rms(x, g):
    ms = jnp.mean(x * x, axis=-1, keepdims=True)
    return (x * lax.rsqrt(ms + EPS)) * g


def _lane_tile(x, width):
    return x if width == LANES else jnp.concatenate([x] * (width // LANES), axis=1)


def _proj_kernel(x_ref, g_ref, w_ref, z_ref, xn_ref):
    @pl.when(pl.program_id(1) == 0)
    def _():
        xn_ref[...] = _rms(x_ref[...], g_ref[...]).astype(BF16)
    z_ref[...] = jnp.dot(xn_ref[...], w_ref[...], preferred_element_type=F32)


def _proj(x, g, w):
    m, d = x.shape
    n = w.shape[1]
    tm = min(m, 1024)
    tn = 768
    assert m % tm == 0 and n % tn == 0
    return pl.pallas_call(
        _proj_kernel,
        grid=(m // tm, n // tn),
        in_specs=[pl.BlockSpec((tm, d), lambda i, j: (i, 0)),
                  pl.BlockSpec((1, d), lambda i, j: (0, 0)),
                  pl.BlockSpec((d, tn), lambda i, j: (0, j))],
        out_specs=pl.BlockSpec((tm, tn), lambda i, j: (i, j)),
        out_shape=jax.ShapeDtypeStruct((m, n), F32),
        scratch_shapes=[pltpu.VMEM((tm, d), BF16)],
        compiler_params=pltpu.CompilerParams(dimension_semantics=("arbitrary", "arbitrary"),
                                             vmem_limit_bytes=VMEM_LIMIT),
        name="proj",
    )(x, g, w)


def _prep_kernel(aq_ref, iq_ref, cq_ref, ckv_ref, ak_ref, av_ref, ikw_ref, krp_ref,
                 ta_ref, ti_ref, tk_ref, tr_ref, qg_ref, kvg_ref, wuq_ref,
                 aq_o, iq_o, ak_o, ikw_o, krp_o, ckv_o, qcat_o, kb_o, vb_o, ikb_o, ckvb_o, krb_o):
    ca, saa, sba = ta_ref[0], ta_ref[1], ta_ref[2]
    ci, sai, sbi = ti_ref[0], ti_ref[1], ti_ref[2]
    ck, sak, sbk = tk_ref[0], tk_ref[1], tk_ref[2]
    cr, sar, sbr = tr_ref[0], tr_ref[1], tr_ref[2]
    tm = aq_ref.shape[0]
    for h in range(A_HEADS):
        sl = slice(h * LANES, (h + 1) * LANES)
        aq_o[:, sl] = _rope(aq_ref[:, sl], ca, saa, sba, A_ROT // 2).astype(BF16)
    for h in range(A_KV_HEADS):
        sl = slice(h * LANES, (h + 1) * LANES)
        ak = _rope(ak_ref[:, sl], ca, saa, sba, A_ROT // 2)
        ak_o[:, sl] = ak
        kb_o[:, sl] = ak.astype(BF16)
    vb_o[...] = av_ref[...].astype(BF16)
    for v in range(IDX_HEADS * IDX_DIM // LANES):
        sl = slice(v * LANES, (v + 1) * LANES)
        iq_o[:, sl] = _rope(iq_ref[:, sl], ci, sai, sbi, IDX_ROT // 2).astype(BF16)
    ikw = _rope(ikw_ref[...], ck, sak, sbk, IDX_ROT // 2)
    ikw_o[...] = ikw
    ikb_o[...] = ikw[:, :IDX_DIM].astype(BF16)
    krp = _rope(krp_ref[...], cr, sar, sbr, ROPE_DIM // 2)
    krp_o[...] = krp
    krb_o[...] = krp[:, :ROPE_DIM].astype(BF16)
    ckv = _rms(ckv_ref[...], kvg_ref[...])
    ckv_o[...] = ckv
    ckvb_o[...] = ckv.astype(BF16)
    cqn = _rms(cq_ref[...], qg_ref[...]).astype(BF16)
    qb = jnp.dot(cqn, wuq_ref[...], preferred_element_type=F32)
    nope = B_HEADS * NOPE_DIM
    low = lax.broadcasted_iota(jnp.int32, (tm, LANES), 1) < ROPE_DIM
    for v in range(B_HEADS * ROPE_DIM // LANES):
        r = _rope(qb[:, nope + v * LANES: nope + (v + 1) * LANES], cr, sar, sbr, ROPE_DIM // 2)
        halves = (jnp.where(low, r, 0.0), jnp.where(low, pltpu.roll(r, ROPE_DIM, 1), 0.0))
        for j in range(2):
            h = 2 * v + j
            qcat_o[:, h * B_QK_PAD: h * B_QK_PAD + NOPE_DIM] = qb[:, h * NOPE_DIM:(h + 1) * NOPE_DIM].astype(BF16)
            qcat_o[:, h * B_QK_PAD + NOPE_DIM:(h + 1) * B_QK_PAD] = halves[j].astype(BF16)


def _prep(z, dst, bsz, t, tabs, q_norm_g, kv_norm_g, wuq):
    m = bsz * t
    tm = min(t, 256)
    nt = t // tm
    assert t % tm == 0

    def zspec(name):
        o, w = dst[name]
        return pl.BlockSpec((tm, w), lambda b, i, blk=o // w: (b * nt + i, blk))

    def tspec():
        return pl.BlockSpec((3, tm, LANES), lambda b, i: (0, i, 0))

    def ospec(w):
        return pl.BlockSpec((tm, w), lambda b, i: (b * nt + i, 0))

    def full(a):
        return pl.BlockSpec(a.shape, lambda b, i: (0,) * a.ndim)

    outs = [(A_WIDTH, BF16), (IDX_HEADS * IDX_DIM, BF16), (A_KV_WIDTH, F32), (LANES, F32), (LANES, F32),
            (KV_LORA, F32), (B_HEADS * B_QK_PAD, BF16),
            (A_KV_WIDTH, BF16), (A_KV_WIDTH, BF16), (IDX_DIM, BF16), (KV_LORA, BF16), (ROPE_DIM, BF16)]
    return pl.pallas_call(
        _prep_kernel,
        grid=(bsz, nt),
        in_specs=[zspec("aq"), zspec("iq"), zspec("cq"), zspec("ckv"), zspec("ak"), zspec("av"), zspec("ikw"),
                  zspec("krp"), tspec(), tspec(), tspec(), tspec(), full(q_norm_g), full(kv_norm_g), full(wuq)],
        out_specs=[ospec(w) for w, _ in outs],
        out_shape=[jax.ShapeDtypeStruct((m, w), dt) for w, dt in outs],
        compiler_params=pltpu.CompilerParams(dimension_semantics=("arbitrary", "arbitrary"),
                                             vmem_limit_bytes=VMEM_LIMIT),
        name="prep",
    )(z, z, z, z, z, z, z, z, *tabs, q_norm_g, kv_norm_g, wuq)


def _kvup_kernel(c_ref, kr_ref, w_ref, kcat_ref, v_ref):
    r = jnp.dot(c_ref[...], w_ref[...], preferred_element_type=F32)
    n = B_HEADS * NOPE_DIM
    tail = jnp.concatenate([kr_ref[...], jnp.zeros((kr_ref.shape[0], B_QK_PAD - NOPE_DIM - ROPE_DIM), BF16)], axis=1)
    for h in range(B_HEADS):
        kcat_ref[:, h * B_QK_PAD: h * B_QK_PAD + NOPE_DIM] = r[:, h * NOPE_DIM:(h + 1) * NOPE_DIM].astype(BF16)
        kcat_ref[:, h * B_QK_PAD + NOPE_DIM:(h + 1) * B_QK_PAD] = tail
    v_ref[...] = r[:, n:].astype(BF16)


def _kvup(ckv, kr, w):
    m, c = ckv.shape
    tm = next(t for t in (512, 384, 256, 128) if m % t == 0)
    row = lambda i: (i, 0)
    return pl.pallas_call(
        _kvup_kernel,
        grid=(m // tm,),
        in_specs=[pl.BlockSpec((tm, c), row), pl.BlockSpec((tm, ROPE_DIM), row), pl.BlockSpec(w.shape, lambda i: (0, 0))],
        out_specs=[pl.BlockSpec((tm, B_HEADS * B_QK_PAD), row), pl.BlockSpec((tm, B_WIDTH), row)],
        out_shape=[jax.ShapeDtypeStruct((m, B_HEADS * B_QK_PAD), BF16), jax.ShapeDtypeStruct((m, B_WIDTH), BF16)],
        compiler_params=pltpu.CompilerParams(dimension_semantics=("arbitrary",), vmem_limit_bytes=VMEM_LIMIT),
        name="kv_up",
    )(ckv, kr, w)


_NT = (((1,), (1,)), ((), ()))


def _num_key_blocks(tq, kw, l_valid, qpos0):
    p_last = qpos0 + (pl.program_id(1) + 1) * tq - 1
    n_cols = jnp.minimum(l_valid, (p_last // CHUNK + 1) * CHUNK)
    return (n_cols + kw - 1) // kw


def _allowed(kb, rows0, shape, tq, kw, l_valid, qpos0):
    rows = lax.broadcasted_iota(jnp.int32, shape, 0) + rows0
    cols = lax.broadcasted_iota(jnp.int32, shape, 1) + kb * kw
    qchunk = (qpos0 + pl.program_id(1) * tq + rows) // CHUNK
    return ((cols // CHUNK) <= qchunk) & (cols < l_valid), cols


def _softmax_step(s, h, vblk, m_ref, l_ref, acc_ref):
    kw = s.shape[1]
    m_prev = m_ref[h]
    m_new = jnp.maximum(m_prev, jnp.max(s, axis=-1, keepdims=True))
    alpha = jnp.exp(m_prev - m_new)
    p = jnp.exp(s - _lane_tile(m_new, kw))
    l_ref[h] = alpha * l_ref[h] + jnp.sum(p, axis=-1, keepdims=True)
    acc_ref[h] = alpha * acc_ref[h] + jnp.dot(p.astype(BF16), vblk, preferred_element_type=F32)
    m_ref[h] = m_new


def _softmax_init(heads, tq, m_ref, l_ref, acc_ref):
    for h in range(heads):
        m_ref[h] = jnp.full((tq, LANES), NEG, F32)
        l_ref[h] = jnp.zeros((tq, LANES), F32)
        acc_ref[h] = jnp.zeros((tq, acc_ref.shape[2]), F32)


def _dsa_kernel(aq_ref, iq_ref, ikw_ref, agate_ref, k_ref, v_ref, ik_ref, o_ref,
                key_ref, bias_ref, iwb_ref, acc_ref, m_ref, l_ref, jcut_ref,
                *, tq, kw, l_valid, qpos0, topk):
    nkb = _num_key_blocks(tq, kw, l_valid, qpos0)
    rs = min(tq, 64)
    nc = kw // LANES

    iw = ikw_ref[:, IDX_DIM:IDX_DIM + IDX_HEADS] * (IDX_DIM ** -0.5)
    for h in range(IDX_HEADS):
        iwb_ref[h] = jnp.broadcast_to(iw[:, h:h + 1], (tq, LANES))

    def score_block(kb, carry):
        ikb = ik_ref[0, pl.ds(pl.multiple_of(kb * kw, kw), kw), :]
        for r in range(tq // rs):
            rsl = slice(r * rs, (r + 1) * rs)
            sc = [jnp.zeros((rs, LANES), F32) for _ in range(nc)]
            for h in range(IDX_HEADS):
                rel = lax.dot_general(iq_ref[rsl, h * IDX_DIM:(h + 1) * IDX_DIM], ikb, _NT, preferred_element_type=F32)
                w = iwb_ref[h, rsl, :]
                for c in range(nc):
                    sc[c] = sc[c] + jnp.maximum(rel[:, c * LANES:(c + 1) * LANES], 0.0) * w
            s = jnp.concatenate(sc, axis=1) + 0.0
            bits = lax.bitcast_convert_type(s, jnp.int32)
            key = bits ^ ((bits >> 31) & 0x7FFFFFFF)
            allowed, _ = _allowed(kb, r * rs, (rs, kw), tq, kw, l_valid, qpos0)
            key_ref[kb, rsl, :] = jnp.where(allowed, key, INT_MIN)
        return carry

    lax.fori_loop(0, nkb, score_block, 0)

    kf = float(topk)

    def count(pred):
        def body(kb, acc):
            w = jnp.where(pred(key_ref[kb], kb), 1.0, 0.0)
            for c in range(nc):
                acc = acc + w[:, c * LANES:(c + 1) * LANES]
            return acc
        acc = lax.fori_loop(0, nkb, body, jnp.zeros((tq, LANES), F32))
        return jnp.sum(acc, axis=-1, keepdims=True)

    t0 = jnp.where(count(lambda k, kb: k >= 0) >= kf, 0, INT_MIN).astype(jnp.int32)

    def bit_step(b, t):
        cand = t | jnp.left_shift(jnp.int32(1), 30 - b)
        return jnp.where(count(lambda k, kb: k >= cand) >= kf, cand, t)

    thr = lax.fori_loop(0, 31, bit_step, t0)

    need = kf - count(lambda k, kb: k > thr)
    excess = (count(lambda k, kb: k == thr) > need) & (thr > INT_MIN)
    jcut_ref[...] = jnp.full((tq, 1), 4096, jnp.int32)

    def cols_of(kb):
        return lax.broadcasted_iota(jnp.int32, (tq, kw), 1) + kb * kw

    @pl.when(jnp.max(jnp.where(excess, 1, 0)) > 0)
    def _():
        def col_step(b, j):
            cand = j | jnp.left_shift(jnp.int32(1), 11 - b)
            f = count(lambda k, kb: (k == thr) & (cols_of(kb) < cand))
            return jnp.where(f <= need, cand, j)
        jcut_ref[...] = lax.fori_loop(0, 12, col_step, jnp.zeros((tq, 1), jnp.int32))

    jcut = jcut_ref[...]

    def bias_block(kb, carry):
        k = key_ref[kb]
        sel = ((k > thr) | ((k == thr) & (cols_of(kb) < jcut))) & (k != INT_MIN)
        bias_ref[kb] = jnp.where(sel, 0.0, NEG)
        return carry

    lax.fori_loop(0, nkb, bias_block, 0)

    _softmax_init(A_HEADS, tq, m_ref, l_ref, acc_ref)
    scale = A_HEAD_DIM ** -0.5
    grp = A_HEADS // A_KV_HEADS
    for g in range(A_KV_HEADS):
        gsl = slice(g * A_HEAD_DIM, (g + 1) * A_HEAD_DIM)

        def att_block(kb, carry):
            r0 = pl.multiple_of(kb * kw, kw)
            kblk = k_ref[0, pl.ds(r0, kw), gsl]
            vblk = v_ref[0, pl.ds(r0, kw), gsl]
            bias = bias_ref[kb]
            for hh in range(grp):
                h = g * grp + hh
                q = aq_ref[:, h * A_HEAD_DIM:(h + 1) * A_HEAD_DIM]
                s = lax.dot_general(q, kblk, _NT, preferred_element_type=F32) * scale + bias
                _softmax_step(s, h, vblk, m_ref, l_ref, acc_ref)
            return carry

        lax.fori_loop(0, nkb, att_block, 0)

    for h in range(A_HEADS):
        sl = slice(h * A_HEAD_DIM, (h + 1) * A_HEAD_DIM)
        o = acc_ref[h] / l_ref[h]
        gate = agate_ref[:, sl]
        o_ref[:, sl] = (o * (gate * jax.nn.sigmoid(gate))).astype(BF16)


def _dsa(aq, iq, ikw, z, dst, k, v, ik, bsz, t, l_valid, qpos0):
    lp = k.shape[1]
    kw = KEY_BLOCK
    assert lp <= 4096 and lp % kw == 0
    tq = min(t, 256)
    nt = t // tq
    topk = min(TOPK_MAX, l_valid // 4)
    go, gw = dst["agate"]
    row = lambda b, i: (b * nt + i, 0)
    batch = lambda b, i: (b, 0, 0)
    kern = functools.partial(_dsa_kernel, tq=tq, kw=kw, l_valid=l_valid, qpos0=qpos0, topk=topk)
    return pl.pallas_call(
        kern,
        grid=(bsz, nt),
        in_specs=[pl.BlockSpec((tq, A_WIDTH), row),
                  pl.BlockSpec((tq, IDX_HEADS * IDX_DIM), row),
                  pl.BlockSpec((tq, LANES), row),
                  pl.BlockSpec((tq, gw), lambda b, i: (b * nt + i, go // gw)),
                  pl.BlockSpec((1, lp, A_KV_WIDTH), batch),
                  pl.BlockSpec((1, lp, A_KV_WIDTH), batch),
                  pl.BlockSpec((1, lp, IDX_DIM), batch)],
        out_specs=pl.BlockSpec((tq, A_WIDTH), row),
        out_shape=jax.ShapeDtypeStruct((bsz * t, A_WIDTH), BF16),
        scratch_shapes=[pltpu.VMEM((lp // kw, tq, kw), jnp.int32),
                        pltpu.VMEM((lp // kw, tq, kw), F32),
                        pltpu.VMEM((IDX_HEADS, tq, LANES), F32),
                        pltpu.VMEM((A_HEADS, tq, A_HEAD_DIM), F32),
                        pltpu.VMEM((A_HEADS, tq, LANES), F32),
                        pltpu.VMEM((A_HEADS, tq, LANES), F32),
                        pltpu.VMEM((tq, 1), jnp.int32)],
        compiler_params=pltpu.CompilerParams(dimension_semantics=("arbitrary", "arbitrary"),
                                             vmem_limit_bytes=VMEM_LIMIT),
        name="dsa",
    )(aq, iq, ikw, z, k, v, ik)


def _mla_kernel(q_ref, bgate_ref, k_ref, v_ref, o_ref, acc_ref, m_ref, l_ref, *, tq, kw, l_valid, qpos0):
    nkb = _num_key_blocks(tq, kw, l_valid, qpos0)
    scale = (NOPE_DIM + ROPE_DIM) ** -0.5
    _softmax_init(B_HEADS, tq, m_ref, l_ref, acc_ref)

    def att_block(kb, carry):
        r0 = pl.multiple_of(kb * kw, kw)
        allowed, _ = _allowed(kb, 0, (tq, kw), tq, kw, l_valid, qpos0)
        bias = jnp.where(allowed, 0.0, NEG)
        for h in range(B_HEADS):
            qsl = slice(h * B_QK_PAD, (h + 1) * B_QK_PAD)
            s = lax.dot_general(q_ref[:, qsl], k_ref[0, pl.ds(r0, kw), qsl], _NT, preferred_element_type=F32)
            s = s * scale + bias
            _softmax_step(s, h, v_ref[0, pl.ds(r0, kw), h * V_DIM:(h + 1) * V_DIM], m_ref, l_ref, acc_ref)
        return carry

    lax.fori_loop(0, nkb, att_block, 0)

    for h in range(B_HEADS):
        sl = slice(h * V_DIM, (h + 1) * V_DIM)
        o = acc_ref[h] / l_ref[h]
        gate = bgate_ref[:, sl]
        o_ref[:, sl] = (o * (gate * jax.nn.sigmoid(gate))).astype(BF16)


def _mla(qcat, z, dst, kcat, v, bsz, t, l_valid, qpos0):
    lp = kcat.shape[1]
    kw = KEY_BLOCK
    assert lp % kw == 0
    tq = min(t, 256)
    nt = t // tq
    go, gw = dst["bgate"]
    row = lambda b, i: (b * nt + i, 0)
    batch = lambda b, i: (b, 0, 0)
    kern = functools.partial(_mla_kernel, tq=tq, kw=kw, l_valid=l_valid, qpos0=qpos0)
    return pl.pallas_call(
        kern,
        grid=(bsz, nt),
        in_specs=[pl.BlockSpec((tq, B_HEADS * B_QK_PAD), row),
                  pl.BlockSpec((tq, gw), lambda b, i: (b * nt + i, go // gw)),
                  pl.BlockSpec((1, lp, B_HEADS * B_QK_PAD), batch),
                  pl.BlockSpec((1, lp, B_WIDTH), batch)],
        out_specs=pl.BlockSpec((tq, B_WIDTH), row),
        out_shape=jax.ShapeDtypeStruct((bsz * t, B_WIDTH), BF16),
        scratch_shapes=[pltpu.VMEM((B_HEADS, tq, V_DIM), F32),
                        pltpu.VMEM((B_HEADS, tq, LANES), F32),
                        pltpu.VMEM((B_HEADS, tq, LANES), F32)],
        compiler_params=pltpu.CompilerParams(dimension_semantics=("arbitrary", "arbitrary"),
                                             vmem_limit_bytes=VMEM_LIMIT),
        name="mla",
    )(qcat, z, kcat, v)


def _out_kernel(ga_ref, gb_ref, mg_ref, x_ref, woa_ref, wob_ref, wout_ref, fg_ref, o_ref, *, d, final):
    ya = jnp.dot(ga_ref[...], woa_ref[...], preferred_element_type=F32)
    yb = jnp.dot(gb_ref[...], wob_ref[...], preferred_element_type=F32)
    m = jax.nn.sigmoid(mg_ref[:, :d]) * ya + jax.nn.sigmoid(mg_ref[:, d:]) * yb
    h = x_ref[...] + jnp.dot(m.astype(BF16), wout_ref[...], preferred_element_type=F32)
    o_ref[...] = _rms(h, fg_ref[...]) if final else h


def _out(ga, gb, z, dst, x, woa, wob, wout, fg, final):
    m, d = x.shape
    tm = min(m, 256)
    assert m % tm == 0 and dst["mg"][0] == 0
    row = lambda i: (i, 0)
    const = lambda a: pl.BlockSpec(a.shape, lambda i: (0, 0), pipeline_mode=pl.Buffered(1))
    return pl.pallas_call(
        functools.partial(_out_kernel, d=d, final=final),
        grid=(m // tm,),
        in_specs=[pl.BlockSpec((tm, A_WIDTH), row), pl.BlockSpec((tm, B_WIDTH), row),
                  pl.BlockSpec((tm, 2 * d), row), pl.BlockSpec((tm, d), row),
                  const(woa), const(wob), const(wout), const(fg)],
        out_specs=pl.BlockSpec((tm, d), row),
        out_shape=jax.ShapeDtypeStruct((m, d), F32),
        compiler_params=pltpu.CompilerParams(dimension_semantics=("arbitrary",), vmem_limit_bytes=VMEM_LIMIT),
        name="out",
    )(ga, gb, z, x, woa, wob, wout, fg)


def _all_keys(cache, new, bsz, t, lp):
    new = new.reshape(bsz, t, new.shape[-1])
    if cache is None and lp == t:
        return new
    parts = ([] if cache is None else [cache.reshape(bsz, cache.shape[1], -1).astype(BF16)]) + [new]
    kk = jnp.concatenate(parts, axis=1)
    return jnp.pad(kk, ((0, 0), (0, lp - kk.shape[1]), (0, 0)))


def _trunk(x, caches, final_g, weights):
    bsz, t, d = x.shape
    depth = weights["w_in"].shape[0]
    past = 0 if caches is None else caches[0].shape[2]
    l_valid = past + t
    lp = -(-l_valid // KEY_BLOCK) * KEY_BLOCK
    _, dst, _ = _layout(d)
    pos = past + jnp.arange(t, dtype=jnp.int32)
    t_scale = jnp.concatenate([jnp.full((t, IDX_HEADS), IDX_HEADS ** -0.5, F32),
                               jnp.ones((t, LANES - IDX_DIM - IDX_HEADS), F32)], axis=1)
    tabs = [jnp.stack(_rope_tables(pos, A_ROT, A_HEAD_DIM, 1)),
            jnp.stack(_rope_tables(pos, IDX_ROT, IDX_DIM, 2)),
            jnp.stack(_rope_tables(pos, IDX_ROT, IDX_DIM, 1, extra=t_scale)),
            jnp.stack(_rope_tables(pos, ROPE_DIM, ROPE_DIM, 2))]
    h = x.reshape(bsz * t, d)
    new_rows = []
    for l in range(depth):
        w = {k: v[l] for k, v in weights.items()}
        z = _proj(h, w["norm_g"], w["w_in"])
        aq, iq, ak, ikw, krp, ckv, qcat, kb, vb, ikb, ckvb, krb = _prep(
            z, dst, bsz, t, tabs, w["q_norm_g"], w["kv_norm_g"], w["w_uq"])
        o, wd = dst["av"]
        av = z[:, o:o + wd]
        c = (None,) * 5 if caches is None else tuple(cc[l] for cc in caches)
        k_all = _all_keys(c[0], kb, bsz, t, lp)
        v_all = _all_keys(c[1], vb, bsz, t, lp)
        ik_all = _all_keys(c[2], ikb, bsz, t, lp)
        ckv_all = _all_keys(c[3], ckvb, bsz, t, lp)
        kr_all = _all_keys(c[4], krb, bsz, t, lp)
        ga = _dsa(aq, iq, ikw, z, dst, k_all, v_all, ik_all, bsz, t, l_valid, past)
        kcat, vv = _kvup(ckv_all.reshape(bsz * lp, KV_LORA), kr_all.reshape(bsz * lp, ROPE_DIM), w["w_ukv"])
        gb = _mla(qcat, z, dst, kcat.reshape(bsz, lp, -1), vv.reshape(bsz, lp, -1), bsz, t, l_valid, past)
        h = _out(ga, gb, z, dst, h, w["w_o_a"], w["w_o_b"], w["w_out"], final_g, final=(l == depth - 1))
        new_rows.append((ak.reshape(bsz, t, A_KV_HEADS, A_HEAD_DIM), av.reshape(bsz, t, A_KV_HEADS, A_HEAD_DIM),
                         ikw[:, :IDX_DIM].reshape(bsz, t, IDX_DIM), ckv.reshape(bsz, t, KV_LORA),
                         krp[:, :ROPE_DIM].reshape(bsz, t, ROPE_DIM)))
    stacked = [jnp.stack([r[i] for r in new_rows], axis=0) for i in range(5)]
    return h.reshape(bsz, t, d), stacked


def kernel(x_prompt, x_sample, cache_a_k, cache_a_v, cache_idx_k, cache_mla_ckv, cache_mla_krope,
           norm_g, w_in, w_uq, q_norm_g, kv_norm_g, w_uk, w_uv, w_o_a, w_o_b, w_out, final_g):
    depth, d, _ = w_in.shape
    weights = {
        "norm_g": norm_g.reshape(depth, 1, d),
        "w_in": jnp.stack([_regroup_w_in(w_in[l], d) for l in range(depth)]),
        "w_uq": jnp.concatenate([w_uq[..., :NOPE_DIM].reshape(depth, Q_LORA, B_HEADS * NOPE_DIM),
                                 w_uq[..., NOPE_DIM:].reshape(depth, Q_LORA, B_HEADS * ROPE_DIM)], axis=-1).astype(BF16),
        "q_norm_g": q_norm_g.reshape(depth, 1, Q_LORA),
        "kv_norm_g": kv_norm_g.reshape(depth, 1, KV_LORA),
        "w_ukv": jnp.concatenate([w_uk.reshape(depth, KV_LORA, B_HEADS * NOPE_DIM),
                                  w_uv.reshape(depth, KV_LORA, B_WIDTH)], axis=-1).astype(BF16),
        "w_o_a": w_o_a.astype(BF16), "w_o_b": w_o_b.astype(BF16), "w_out": w_out.astype(BF16),
    }
    fg = final_g.reshape(1, d)
    y_p, (pk, pv, pik, pckv, pkr) = _trunk(x_prompt, None, fg, weights)
    caches = (cache_a_k, cache_a_v, cache_idx_k, cache_mla_ckv, cache_mla_krope)
    y_s, (sk, sv, sik, sckv, skr) = _trunk(x_sample, caches, fg, weights)
    return (y_p, y_s, pk, pv, pik, pckv, pkr, sk, sv, sik, sckv, skr)
```

```python
import functools

import jax
import jax.numpy as jnp
from jax import lax
from jax.experimental import pallas as pl
from jax.experimental.pallas import tpu as pltpu

F32 = jnp.float32
BF16 = jnp.bfloat16

CHUNK = 64
ROPE_THETA = 500000.0
EPS = 1e-6

A_HEADS = 8
A_KV_HEADS = 2
A_HEAD_DIM = 128
A_ROT = A_HEAD_DIM // 4
A_WIDTH = A_HEADS * A_HEAD_DIM
A_KV_WIDTH = A_KV_HEADS * A_HEAD_DIM
IDX_HEADS = 16
IDX_DIM = 64
IDX_ROT = IDX_DIM // 4
TOPK_MAX = 256

B_HEADS = 8
Q_LORA = 512
KV_LORA = 512
NOPE_DIM = 128
ROPE_DIM = 64
V_DIM = 128
B_WIDTH = B_HEADS * V_DIM
B_QK_PAD = 256

LANES = 128
KEY_BLOCK = 256
QUERY_TILE = 256
INT_MIN = -(2 ** 31)
NEG = -1e30
VMEM_LIMIT = 56 * 1024 * 1024

_SEGS = (("aq", A_WIDTH), ("ak", A_KV_WIDTH), ("av", A_KV_WIDTH), ("agate", A_WIDTH),
         ("iq", IDX_HEADS * IDX_DIM), ("ik", IDX_DIM), ("iw", IDX_HEADS),
         ("cq", Q_LORA), ("ckv", KV_LORA), ("kr", ROPE_DIM), ("bgate", B_WIDTH), ("mg", None))


def _layout(d_model):
    src, off = {}, 0
    for name, w in _SEGS:
        w = 2 * d_model if w is None else w
        src[name] = (off, w)
        off += w
    order = (("mg", 2 * d_model), ("aq", A_WIDTH), ("agate", A_WIDTH), ("iq", IDX_HEADS * IDX_DIM),
             ("bgate", B_WIDTH), ("cq", Q_LORA), ("ckv", KV_LORA), ("ak", A_KV_WIDTH), ("av", A_KV_WIDTH),
             ("ikw", LANES), ("krp", LANES))
    dst, off = {}, 0
    for name, w in order:
        assert off % w == 0
        dst[name] = (off, w)
        off += w
    return src, dst, off


def _regroup_w_in(w_in, d_model):
    src, dst, total = _layout(d_model)
    def cols(name):
        o, w = src[name]
        return w_in[:, o:o + w]
    zeros = lambda n: jnp.zeros((w_in.shape[0], n), w_in.dtype)
    pieces = [cols("mg"), cols("aq"), cols("agate"), cols("iq"), cols("bgate"), cols("cq"), cols("ckv"),
              cols("ak"), cols("av"),
              cols("ik"), cols("iw"), zeros(LANES - IDX_DIM - IDX_HEADS),
              cols("kr"), zeros(LANES - ROPE_DIM)]
    w = jnp.concatenate(pieces, axis=1).astype(BF16)
    assert w.shape[1] == total
    return w


def _rope_tables(pos, rot, head, heads_per_vreg, extra=None):
    half = rot // 2
    inv_freq = ROPE_THETA ** (-jnp.arange(half, dtype=F32) * (2.0 / rot))
    ang = pos.astype(F32)[:, None] * inv_freq[None, :]
    cos, sin = jnp.cos(ang), jnp.sin(ang)
    t = pos.shape[0]
    one = jnp.ones((t, head - rot), F32)
    zero = lambda n: jnp.zeros((t, n), F32)
    c = jnp.concatenate([cos, cos, one], axis=1)
    sa = jnp.concatenate([-sin, zero(head - half)], axis=1)
    sb = jnp.concatenate([zero(half), sin, zero(head - rot)], axis=1)
    c, sa, sb = (jnp.tile(a, (1, heads_per_vreg)) for a in (c, sa, sb))
    if extra is not None:
        c = jnp.concatenate([c, extra], axis=1)
        sa = jnp.concatenate([sa, jnp.zeros_like(extra)], axis=1)
        sb = jnp.concatenate([sb, jnp.zeros_like(extra)], axis=1)
    assert c.shape[1] == LANES
    return c, sa, sb


def _rope(x, c, sa, sb, half):
    up = pltpu.roll(x, LANES - half, 1)
    dn = pltpu.roll(x, half, 1)
    return x * c + up * sa + dn * sb


def _rms(x, g):
    ms = jnp.mean(x * x, axis=-1, keepdims=True)
    return (x * lax.rsqrt(ms + EPS)) * g


def _lane_tile(x, width):
    return x if width == LANES else jnp.concatenate([x] * (width // LANES), axis=1)


def _proj_kernel(x_ref, g_ref, w_ref, z_ref, xn_ref):
    @pl.when(pl.program_id(1) == 0)
    def _():
        xn_ref[...] = _rms(x_ref[...], g_ref[...]).astype(BF16)
    z_ref[...] = jnp.dot(xn_ref[...], w_ref[...], preferred_element_type=F32)


def _proj(x, g, w):
    m, d = x.shape
    n = w.shape[1]
    tm = min(m, 1024)
    tn = 768
    assert m % tm == 0 and n % tn == 0
    return pl.pallas_call(
        _proj_kernel,
        grid=(m // tm, n // tn),
        in_specs=[pl.BlockSpec((tm, d), lambda i, j: (i, 0)),
                  pl.BlockSpec((1, d), lambda i, j: (0, 0)),
                  pl.BlockSpec((d, tn), lambda i, j: (0, j))],
        out_specs=pl.BlockSpec((tm, tn), lambda i, j: (i, j)),
        out_shape=jax.ShapeDtypeStruct((m, n), F32),
        scratch_shapes=[pltpu.VMEM((tm, d), BF16)],
        compiler_params=pltpu.CompilerParams(dimension_semantics=("arbitrary", "arbitrary"),
                                             vmem_limit_bytes=VMEM_LIMIT),
        name="proj",
    )(x, g, w)


def _prep_kernel(aq_ref, iq_ref, cq_ref, ckv_ref, ak_ref, av_ref, ikw_ref, krp_ref,
                 ta_ref, ti_ref, tk_ref, tr_ref, qg_ref, kvg_ref, wuq_ref,
                 aq_o, iq_o, ak_o, ikw_o, krp_o, ckv_o, qcat_o, kb_o, vb_o, ikb_o, ckvb_o, krb_o):
    ca, saa, sba = ta_ref[0], ta_ref[1], ta_ref[2]
    ci, sai, sbi = ti_ref[0], ti_ref[1], ti_ref[2]
    ck, sak, sbk = tk_ref[0], tk_ref[1], tk_ref[2]
    cr, sar, sbr = tr_ref[0], tr_ref[1], tr_ref[2]
    tm = aq_ref.shape[0]
    for h in range(A_HEADS):
        sl = slice(h * LANES, (h + 1) * LANES)
        aq_o[:, sl] = _rope(aq_ref[:, sl], ca, saa, sba, A_ROT // 2).astype(BF16)
    for h in range(A_KV_HEADS):
        sl = slice(h * LANES, (h + 1) * LANES)
        ak = _rope(ak_ref[:, sl], ca, saa, sba, A_ROT // 2)
        ak_o[:, sl] = ak
        kb_o[:, sl] = ak.astype(BF16)
    vb_o[...] = av_ref[...].astype(BF16)
    for v in range(IDX_HEADS * IDX_DIM // LANES):
        sl = slice(v * LANES, (v + 1) * LANES)
        iq_o[:, sl] = _rope(iq_ref[:, sl], ci, sai, sbi, IDX_ROT // 2).astype(BF16)
    ikw = _rope(ikw_ref[...], ck, sak, sbk, IDX_ROT // 2)
    ikw_o[...] = ikw
    ikb_o[...] = ikw[:, :IDX_DIM].astype(BF16)
    krp = _rope(krp_ref[...], cr, sar, sbr, ROPE_DIM // 2)
    krp_o[...] = krp
    krb_o[...] = krp[:, :ROPE_DIM].astype(BF16)
    ckv = _rms(ckv_ref[...], kvg_ref[...])
    ckv_o[...] = ckv
    ckvb_o[...] = ckv.astype(BF16)
    cqn = _rms(cq_ref[...], qg_ref[...]).astype(BF16)
    qb = jnp.dot(cqn, wuq_ref[...], preferred_element_type=F32)
    nope = B_HEADS * NOPE_DIM
    low = lax.broadcasted_iota(jnp.int32, (tm, LANES), 1) < ROPE_DIM
    for v in range(B_HEADS * ROPE_DIM // LANES):
        r = _rope(qb[:, nope + v * LANES: nope + (v + 1) * LANES], cr, sar, sbr, ROPE_DIM // 2)
        halves = (jnp.where(low, r, 0.0), jnp.where(low, pltpu.roll(r, ROPE_DIM, 1), 0.0))
        for j in range(2):
            h = 2 * v + j
            qcat_o[:, h * B_QK_PAD: h * B_QK_PAD + NOPE_DIM] = qb[:, h * NOPE_DIM:(h + 1) * NOPE_DIM].astype(BF16)
            qcat_o[:, h * B_QK_PAD + NOPE_DIM:(h + 1) * B_QK_PAD] = halves[j].astype(BF16)


def _prep(z, dst, bsz, t, tabs, q_norm_g, kv_norm_g, wuq):
    m = bsz * t
    tm = min(t, 256)
    nt = t // tm
    assert t % tm == 0

    def zspec(name):
        o, w = dst[name]
        return pl.BlockSpec((tm, w), lambda b, i, blk=o // w: (b * nt + i, blk))

    def tspec():
        return pl.BlockSpec((3, tm, LANES), lambda b, i: (0, i, 0))

    def ospec(w):
        return pl.BlockSpec((tm, w), lambda b, i: (b * nt + i, 0))

    def full(a):
        return pl.BlockSpec(a.shape, lambda b, i: (0,) * a.ndim)

    outs = [(A_WIDTH, BF16), (IDX_HEADS * IDX_DIM, BF16), (A_KV_WIDTH, F32), (LANES, F32), (LANES, F32),
            (KV_LORA, F32), (B_HEADS * B_QK_PAD, BF16),
            (A_KV_WIDTH, BF16), (A_KV_WIDTH, BF16), (IDX_DIM, BF16), (KV_LORA, BF16), (ROPE_DIM, BF16)]
    return pl.pallas_call(
        _prep_kernel,
        grid=(bsz, nt),
        in_specs=[zspec("aq"), zspec("iq"), zspec("cq"), zspec("ckv"), zspec("ak"), zspec("av"), zspec("ikw"),
                  zspec("krp"), tspec(), tspec(), tspec(), tspec(), full(q_norm_g), full(kv_norm_g), full(wuq)],
        out_specs=[ospec(w) for w, _ in outs],
        out_shape=[jax.ShapeDtypeStruct((m, w), dt) for w, dt in outs],
        compiler_params=pltpu.CompilerParams(dimension_semantics=("arbitrary", "arbitrary"),
                                             vmem_limit_bytes=VMEM_LIMIT),
        name="prep",
    )(z, z, z, z, z, z, z, z, *tabs, q_norm_g, kv_norm_g, wuq)


def _kvup_kernel(c_ref, kr_ref, w_ref, kcat_ref, v_ref):
    r = jnp.dot(c_ref[...], w_ref[...], preferred_element_type=F32)
    n = B_HEADS * NOPE_DIM
    tail = jnp.concatenate([kr_ref[...], jnp.zeros((kr_ref.shape[0], B_QK_PAD - NOPE_DIM - ROPE_DIM), BF16)], axis=1)
    for h in range(B_HEADS):
        kcat_ref[:, h * B_QK_PAD: h * B_QK_PAD + NOPE_DIM] = r[:, h * NOPE_DIM:(h + 1) * NOPE_DIM].astype(BF16)
        kcat_ref[:, h * B_QK_PAD + NOPE_DIM:(h + 1) * B_QK_PAD] = tail
    v_ref[...] = r[:, n:].astype(BF16)


def _kvup(ckv, kr, w):
    m, c = ckv.shape
    tm = next(t for t in (512, 384, 256, 128) if m % t == 0)
    row = lambda i: (i, 0)
    return pl.pallas_call(
        _kvup_kernel,
        grid=(m // tm,),
        in_specs=[pl.BlockSpec((tm, c), row), pl.BlockSpec((tm, ROPE_DIM), row), pl.BlockSpec(w.shape, lambda i: (0, 0))],
        out_specs=[pl.BlockSpec((tm, B_HEADS * B_QK_PAD), row), pl.BlockSpec((tm, B_WIDTH), row)],
        out_shape=[jax.ShapeDtypeStruct((m, B_HEADS * B_QK_PAD), BF16), jax.ShapeDtypeStruct((m, B_WIDTH), BF16)],
        compiler_params=pltpu.CompilerParams(dimension_semantics=("arbitrary",), vmem_limit_bytes=VMEM_LIMIT),
        name="kv_up",
    )(ckv, kr, w)


_NT = (((1,), (1,)), ((), ()))


def _visible_cols(tile, tq, l_valid, qpos0):
    p_last = qpos0 + (tile + 1) * tq - 1
    n = (p_last // CHUNK + 1) * CHUNK
    return min(n, l_valid) if isinstance(n, int) else jnp.minimum(n, l_valid)


def _num_key_blocks(tile, tq, kw, l_valid, qpos0):
    return (_visible_cols(tile, tq, l_valid, qpos0) + kw - 1) // kw


def _allowed(tile, kb, rows0, shape, tq, kw, l_valid, qpos0):
    rows = lax.broadcasted_iota(jnp.int32, shape, 0) + rows0
    cols = lax.broadcasted_iota(jnp.int32, shape, 1) + kb * kw
    qchunk = (qpos0 + tile * tq + rows) // CHUNK
    return ((cols // CHUNK) <= qchunk) & (cols < l_valid), cols


def _softmax_step(s, h, vblk, m_ref, l_ref, acc_ref):
    kw = s.shape[1]
    m_prev = m_ref[h]
    m_new = jnp.maximum(m_prev, jnp.max(s, axis=-1, keepdims=True))
    alpha = jnp.exp(m_prev - m_new)
    p = jnp.exp(s - _lane_tile(m_new, kw))
    l_ref[h] = alpha * l_ref[h] + jnp.sum(p, axis=-1, keepdims=True)
    acc_ref[h] = alpha * acc_ref[h] + jnp.dot(p.astype(BF16), vblk, preferred_element_type=F32)
    m_ref[h] = m_new


def _softmax_init(heads, tq, m_ref, l_ref, acc_ref):
    for h in range(heads):
        m_ref[h] = jnp.full((tq, LANES), NEG, F32)
        l_ref[h] = jnp.zeros((tq, LANES), F32)
        acc_ref[h] = jnp.zeros((tq, acc_ref.shape[2]), F32)


def _dsa_kernel(aq_ref, iq_ref, ikw_ref, agate_ref, k_ref, v_ref, ik_ref, o_ref,
                key_ref, thr_ref, jcut_ref, need_ref, bias_ref, iwb_ref, acc_ref, m_ref, l_ref,
                *, nt, tq, kw, l_valid, qpos0, topk):
    nkb_max = key_ref.shape[0] // nt
    nkb_of = [_num_key_blocks(ti, tq, kw, l_valid, qpos0) for ti in range(nt)]
    rs = min(tq, 64)
    ns = tq // rs
    nc = kw // LANES
    kf = float(topk)

    def cols_of(kb, shape):
        return lax.broadcasted_iota(jnp.int32, shape, 1) + kb * kw

    def score_tile(ti, carry):
        r0 = pl.multiple_of(ti * tq, tq)
        iw = ikw_ref[pl.ds(r0, tq), IDX_DIM:IDX_DIM + IDX_HEADS] * (IDX_DIM ** -0.5)
        for h in range(IDX_HEADS):
            iwb_ref[h] = jnp.broadcast_to(iw[:, h:h + 1], (tq, LANES))

        def score_block(kb, c2):
            ikb = ik_ref[0, pl.ds(pl.multiple_of(kb * kw, kw), kw), :]
            for r in range(ns):
                sc = [jnp.zeros((rs, LANES), F32) for _ in range(nc)]
                for h in range(IDX_HEADS):
                    q = iq_ref[pl.ds(r0 + r * rs, rs), h * IDX_DIM:(h + 1) * IDX_DIM]
                    rel = lax.dot_general(q, ikb, _NT, preferred_element_type=F32)
                    w = iwb_ref[h, r * rs:(r + 1) * rs, :]
                    for c in range(nc):
                        sc[c] = sc[c] + jnp.maximum(rel[:, c * LANES:(c + 1) * LANES], 0.0) * w
                s = jnp.concatenate(sc, axis=1) + 0.0
                bits = lax.bitcast_convert_type(s, jnp.int32)
                key = bits ^ ((bits >> 31) & 0x7FFFFFFF)
                allowed, _ = _allowed(ti, kb, r * rs, (rs, kw), tq, kw, l_valid, qpos0)
                key_ref[ti * nkb_max + kb, r * rs:(r + 1) * rs, :] = jnp.where(allowed, key, INT_MIN)
            return c2

        lax.fori_loop(0, _num_key_blocks(ti, tq, kw, l_valid, qpos0), score_block, 0)
        return carry

    def strip_count(ti, s, pred):
        acc = jnp.zeros((rs, LANES), F32)
        for kb in range(nkb_of[ti]):
            k = key_ref[ti * nkb_max + kb, s * rs:(s + 1) * rs, :]
            for c in range(nc):
                acc = acc + jnp.where(pred(k[:, c * LANES:(c + 1) * LANES], kb * kw + c * LANES), 1.0, 0.0)
        return jnp.sum(acc, axis=-1, keepdims=True)

    def bit_step(b, carry):
        bit = jnp.left_shift(jnp.int32(1), 31 - b)
        for ti in range(nt):
            for s in range(ns):
                rows = slice(ti * tq + s * rs, ti * tq + (s + 1) * rs)
                t = thr_ref[rows, :]
                cand = t + bit
                cnt = strip_count(ti, s, lambda k, c0: k >= cand)
                thr_ref[rows, :] = jnp.where(cnt >= kf, cand, t)
        return carry

    @pl.when(pl.program_id(1) == 0)
    def _():
        lax.fori_loop(0, nt, score_tile, 0)
        thr_ref[...] = jnp.full(thr_ref.shape, INT_MIN, jnp.int32)
        lax.fori_loop(0, 32, bit_step, 0)

        jcut_ref[...] = jnp.full(jcut_ref.shape, 4096, jnp.int32)
        flag = jnp.int32(0)
        for ti in range(nt):
            for s in range(ns):
                rows = slice(ti * tq + s * rs, ti * tq + (s + 1) * rs)
                t = thr_ref[rows, :]
                need = kf - strip_count(ti, s, lambda k, c0: k > t)
                n_eq = strip_count(ti, s, lambda k, c0: k == t)
                need_ref[rows, :] = jnp.broadcast_to(need, (rs, LANES))
                excess = (n_eq > need) & (t[:, :1] > INT_MIN)
                flag = jnp.maximum(flag, jnp.max(jnp.where(excess, 1, 0)))

        @pl.when(flag > 0)
        def _():
            lane = lax.broadcasted_iota(jnp.int32, (rs, LANES), 1)

            def col_step(b, carry):
                bit = jnp.left_shift(jnp.int32(1), 11 - b)
                for ti in range(nt):
                    for s in range(ns):
                        rows = slice(ti * tq + s * rs, ti * tq + (s + 1) * rs)
                        t, j = thr_ref[rows, :], jcut_ref[rows, :]
                        cand = j + bit
                        f = strip_count(ti, s, lambda k, c0: (k == t) & (lane + c0 < cand))
                        jcut_ref[rows, :] = jnp.where(f <= need_ref[rows, :], cand, j)
                return carry

            jcut_ref[...] = jnp.zeros(jcut_ref.shape, jnp.int32)
            lax.fori_loop(0, 12, col_step, 0)

    ti = pl.program_id(1)
    nkb = _num_key_blocks(ti, tq, kw, l_valid, qpos0)
    r0 = pl.multiple_of(ti * tq, tq)
    thr = _lane_tile(thr_ref[pl.ds(r0, tq), :], kw)
    jcut = _lane_tile(jcut_ref[pl.ds(r0, tq), :], kw)

    def bias_block(kb, carry):
        k = key_ref[ti * nkb_max + kb]
        sel = ((k > thr) | ((k == thr) & (cols_of(kb, (tq, kw)) < jcut))) & (k != INT_MIN)
        bias_ref[kb] = jnp.where(sel, 0.0, NEG)
        return carry

    lax.fori_loop(0, nkb, bias_block, 0)

    _softmax_init(A_HEADS, tq, m_ref, l_ref, acc_ref)
    scale = A_HEAD_DIM ** -0.5
    grp = A_HEADS // A_KV_HEADS
    for g in range(A_KV_HEADS):
        gsl = slice(g * A_HEAD_DIM, (g + 1) * A_HEAD_DIM)

        def att_block(kb, carry):
            c0 = pl.multiple_of(kb * kw, kw)
            kblk = k_ref[0, pl.ds(c0, kw), gsl]
            vblk = v_ref[0, pl.ds(c0, kw), gsl]
            bias = bias_ref[kb]
            for hh in range(grp):
                h = g * grp + hh
                q = aq_ref[:, h * A_HEAD_DIM:(h + 1) * A_HEAD_DIM]
                s = lax.dot_general(q, kblk, _NT, preferred_element_type=F32) * scale + bias
                _softmax_step(s, h, vblk, m_ref, l_ref, acc_ref)
            return carry

        lax.fori_loop(0, nkb, att_block, 0)

    for h in range(A_HEADS):
        sl = slice(h * A_HEAD_DIM, (h + 1) * A_HEAD_DIM)
        o = acc_ref[h] / l_ref[h]
        gate = agate_ref[:, sl]
        o_ref[:, sl] = (o * (gate * jax.nn.sigmoid(gate))).astype(BF16)


def _dsa(aq, iq, ikw, z, dst, k, v, ik, bsz, t, tq, kw, l_valid, qpos0):
    lp = k.shape[1]
    assert lp < 4096 and lp % kw == 0 and t % tq == 0
    nt = t // tq
    nkb_max = lp // kw
    topk = min(TOPK_MAX, l_valid // 4)
    go, gw = dst["agate"]
    row = lambda b, i: (b * nt + i, 0)
    whole = lambda b, i: (b, 0)
    batch = lambda b, i: (b, 0, 0)
    kern = functools.partial(_dsa_kernel, nt=nt, tq=tq, kw=kw, l_valid=l_valid, qpos0=qpos0, topk=topk)
    return pl.pallas_call(
        kern,
        grid=(bsz, nt),
        in_specs=[pl.BlockSpec((tq, A_WIDTH), row),
                  pl.BlockSpec((t, IDX_HEADS * IDX_DIM), whole),
                  pl.BlockSpec((t, LANES), whole),
                  pl.BlockSpec((tq, gw), lambda b, i: (b * nt + i, go // gw)),
                  pl.BlockSpec((1, lp, A_KV_WIDTH), batch),
                  pl.BlockSpec((1, lp, A_KV_WIDTH), batch),
                  pl.BlockSpec((1, lp, IDX_DIM), batch)],
        out_specs=pl.BlockSpec((tq, A_WIDTH), row),
        out_shape=jax.ShapeDtypeStruct((bsz * t, A_WIDTH), BF16),
        scratch_shapes=[pltpu.VMEM((nt * nkb_max, tq, kw), jnp.int32),
                        pltpu.VMEM((t, LANES), jnp.int32),
                        pltpu.VMEM((t, LANES), jnp.int32),
                        pltpu.VMEM((t, LANES), F32),
                        pltpu.VMEM((nkb_max, tq, kw), F32),
                        pltpu.VMEM((IDX_HEADS, tq, LANES), F32),
                        pltpu.VMEM((A_HEADS, tq, A_HEAD_DIM), F32),
                        pltpu.VMEM((A_HEADS, tq, LANES), F32),
                        pltpu.VMEM((A_HEADS, tq, LANES), F32)],
        compiler_params=pltpu.CompilerParams(dimension_semantics=("arbitrary", "arbitrary"),
                                             vmem_limit_bytes=VMEM_LIMIT),
        name="dsa",
    )(aq, iq, ikw, z, k, v, ik)


def _mla_kernel(q_ref, bgate_ref, k_ref, v_ref, o_ref, acc_ref, m_ref, l_ref, *, tq, kw, l_valid, qpos0):
    ti = pl.program_id(1)
    nkb = _num_key_blocks(ti, tq, kw, l_valid, qpos0)
    scale = (NOPE_DIM + ROPE_DIM) ** -0.5
    _softmax_init(B_HEADS, tq, m_ref, l_ref, acc_ref)

    def att_block(kb, carry):
        r0 = pl.multiple_of(kb * kw, kw)
        allowed, _ = _allowed(ti, kb, 0, (tq, kw), tq, kw, l_valid, qpos0)
        bias = jnp.where(allowed, 0.0, NEG)
        for h in range(B_HEADS):
            qsl = slice(h * B_QK_PAD, (h + 1) * B_QK_PAD)
            s = lax.dot_general(q_ref[:, qsl], k_ref[0, pl.ds(r0, kw), qsl], _NT, preferred_element_type=F32)
            s = s * scale + bias
            _softmax_step(s, h, v_ref[0, pl.ds(r0, kw), h * V_DIM:(h + 1) * V_DIM], m_ref, l_ref, acc_ref)
        return carry

    lax.fori_loop(0, nkb, att_block, 0)

    for h in range(B_HEADS):
        sl = slice(h * V_DIM, (h + 1) * V_DIM)
        o = acc_ref[h] / l_ref[h]
        gate = bgate_ref[:, sl]
        o_ref[:, sl] = (o * (gate * jax.nn.sigmoid(gate))).astype(BF16)


def _mla(qcat, z, dst, kcat, v, bsz, t, tq, kw, l_valid, qpos0):
    lp = kcat.shape[1]
    assert lp % kw == 0 and t % tq == 0
    nt = t // tq
    go, gw = dst["bgate"]
    row = lambda b, i: (b * nt + i, 0)
    batch = lambda b, i: (b, 0, 0)
    kern = functools.partial(_mla_kernel, tq=tq, kw=kw, l_valid=l_valid, qpos0=qpos0)
    return pl.pallas_call(
        kern,
        grid=(bsz, nt),
        in_specs=[pl.BlockSpec((tq, B_HEADS * B_QK_PAD), row),
                  pl.BlockSpec((tq, gw), lambda b, i: (b * nt + i, go // gw)),
                  pl.BlockSpec((1, lp, B_HEADS * B_QK_PAD), batch),
                  pl.BlockSpec((1, lp, B_WIDTH), batch)],
        out_specs=pl.BlockSpec((tq, B_WIDTH), row),
        out_shape=jax.ShapeDtypeStruct((bsz * t, B_WIDTH), BF16),
        scratch_shapes=[pltpu.VMEM((B_HEADS, tq, V_DIM), F32),
                        pltpu.VMEM((B_HEADS, tq, LANES), F32),
                        pltpu.VMEM((B_HEADS, tq, LANES), F32)],
        compiler_params=pltpu.CompilerParams(dimension_semantics=("arbitrary", "arbitrary"),
                                             vmem_limit_bytes=VMEM_LIMIT),
        name="mla",
    )(qcat, z, kcat, v)


def _out_kernel(ga_ref, gb_ref, mg_ref, x_ref, woa_ref, wob_ref, wout_ref, fg_ref, o_ref, *, d, final):
    ya = jnp.dot(ga_ref[...], woa_ref[...], preferred_element_type=F32)
    yb = jnp.dot(gb_ref[...], wob_ref[...], preferred_element_type=F32)
    m = jax.nn.sigmoid(mg_ref[:, :d]) * ya + jax.nn.sigmoid(mg_ref[:, d:]) * yb
    h = x_ref[...] + jnp.dot(m.astype(BF16), wout_ref[...], preferred_element_type=F32)
    o_ref[...] = _rms(h, fg_ref[...]) if final else h


def _out(ga, gb, z, dst, x, woa, wob, wout, fg, final):
    m, d = x.shape
    tm = min(m, 256)
    assert m % tm == 0 and dst["mg"][0] == 0
    row = lambda i: (i, 0)
    const = lambda a: pl.BlockSpec(a.shape, lambda i: (0, 0), pipeline_mode=pl.Buffered(1))
    return pl.pallas_call(
        functools.partial(_out_kernel, d=d, final=final),
        grid=(m // tm,),
        in_specs=[pl.BlockSpec((tm, A_WIDTH), row), pl.BlockSpec((tm, B_WIDTH), row),
                  pl.BlockSpec((tm, 2 * d), row), pl.BlockSpec((tm, d), row),
                  const(woa), const(wob), const(wout), const(fg)],
        out_specs=pl.BlockSpec((tm, d), row),
        out_shape=jax.ShapeDtypeStruct((m, d), F32),
        compiler_params=pltpu.CompilerParams(dimension_semantics=("arbitrary",), vmem_limit_bytes=VMEM_LIMIT),
        name="out",
    )(ga, gb, z, x, woa, wob, wout, fg)


def _all_keys(cache, new, bsz, t, lp):
    new = new.reshape(bsz, t, new.shape[-1])
    if cache is None and lp == t:
        return new
    parts = ([] if cache is None else [cache.reshape(bsz, cache.shape[1], -1).astype(BF16)]) + [new]
    kk = jnp.concatenate(parts, axis=1)
    return jnp.pad(kk, ((0, 0), (0, lp - kk.shape[1]), (0, 0)))


def _trunk(x, caches, final_g, weights):
    bsz, t, d = x.shape
    depth = weights["w_in"].shape[0]
    past = 0 if caches is None else caches[0].shape[2]
    l_valid = past + t
    tq = min(t, QUERY_TILE)
    kw = KEY_BLOCK if t >= QUERY_TILE else -(-l_valid // LANES) * LANES
    lp = -(-l_valid // kw) * kw
    _, dst, _ = _layout(d)
    pos = past + jnp.arange(t, dtype=jnp.int32)
    t_scale = jnp.concatenate([jnp.full((t, IDX_HEADS), IDX_HEADS ** -0.5, F32),
                               jnp.ones((t, LANES - IDX_DIM - IDX_HEADS), F32)], axis=1)
    tabs = [jnp.stack(_rope_tables(pos, A_ROT, A_HEAD_DIM, 1)),
            jnp.stack(_rope_tables(pos, IDX_ROT, IDX_DIM, 2)),
            jnp.stack(_rope_tables(pos, IDX_ROT, IDX_DIM, 1, extra=t_scale)),
            jnp.stack(_rope_tables(pos, ROPE_DIM, ROPE_DIM, 2))]
    h = x.reshape(bsz * t, d)
    new_rows = []
    for l in range(depth):
        w = {k: v[l] for k, v in weights.items()}
        z = _proj(h, w["norm_g"], w["w_in"])
        aq, iq, ak, ikw, krp, ckv, qcat, kb, vb, ikb, ckvb, krb = _prep(
            z, dst, bsz, t, tabs, w["q_norm_g"], w["kv_norm_g"], w["w_uq"])
        o, wd = dst["av"]
        av = z[:, o:o + wd]
        c = (None,) * 5 if caches is None else tuple(cc[l] for cc in caches)
        k_all = _all_keys(c[0], kb, bsz, t, lp)
        v_all = _all_keys(c[1], vb, bsz, t, lp)
        ik_all = _all_keys(c[2], ikb, bsz, t, lp)
        ckv_all = _all_keys(c[3], ckvb, bsz, t, lp)
        kr_all = _all_keys(c[4], krb, bsz, t, lp)
        ga = _dsa(aq, iq, ikw, z, dst, k_all, v_all, ik_all, bsz, t, tq, kw, l_valid, past)
        kcat, vv = _kvup(ckv_all.reshape(bsz * lp, KV_LORA), kr_all.reshape(bsz * lp, ROPE_DIM), w["w_ukv"])
        gb = _mla(qcat, z, dst, kcat.reshape(bsz, lp, -1), vv.reshape(bsz, lp, -1), bsz, t, tq, kw, l_valid, past)
        h = _out(ga, gb, z, dst, h, w["w_o_a"], w["w_o_b"], w["w_out"], final_g, final=(l == depth - 1))
        new_rows.append((ak.reshape(bsz, t, A_KV_HEADS, A_HEAD_DIM), av.reshape(bsz, t, A_KV_HEADS, A_HEAD_DIM),
                         ikw[:, :IDX_DIM].reshape(bsz, t, IDX_DIM), ckv.reshape(bsz, t, KV_LORA),
                         krp[:, :ROPE_DIM].reshape(bsz, t, ROPE_DIM)))
    stacked = [jnp.stack([r[i] for r in new_rows], axis=0) for i in range(5)]
    return h.reshape(bsz, t, d), stacked


def kernel(x_prompt, x_sample, cache_a_k, cache_a_v, cache_idx_k, cache_mla_ckv, cache_mla_krope,
           norm_g, w_in, w_uq, q_norm_g, kv_norm_g, w_uk, w_uv, w_o_a, w_o_b, w_out, final_g):
    depth, d, _ = w_in.shape
    weights = {
        "norm_g": norm_g.reshape(depth, 1, d),
        "w_in": jnp.stack([_regroup_w_in(w_in[l], d) for l in range(depth)]),
        "w_uq": jnp.concatenate([w_uq[..., :NOPE_DIM].reshape(depth, Q_LORA, B_HEADS * NOPE_DIM),
                                 w_uq[..., NOPE_DIM:].reshape(depth, Q_LORA, B_HEADS * ROPE_DIM)], axis=-1).astype(BF16),
        "q_norm_g": q_norm_g.reshape(depth, 1, Q_LORA),
        "kv_norm_g": kv_norm_g.reshape(depth, 1, KV_LORA),
        "w_ukv": jnp.concatenate([w_uk.reshape(depth, KV_LORA, B_HEADS * NOPE_DIM),
                                  w_uv.reshape(depth, KV_LORA, B_WIDTH)], axis=-1).astype(BF16),
        "w_o_a": w_o_a.astype(BF16), "w_o_b": w_o_b.astype(BF16), "w_out": w_out.astype(BF16),
    }
    fg = final_g.reshape(1, d)
    y_p, (pk, pv, pik, pckv, pkr) = _trunk(x_prompt, None, fg, weights)
    caches = (cache_a_k, cache_a_v, cache_idx_k, cache_mla_ckv, cache_mla_krope)
    y_s, (sk, sv, sik, sckv, skr) = _trunk(x_sample, caches, fg, weights)
    return (y_p, y_s, pk, pv, pik, pckv, pkr, sk, sv, sik, sckv, skr)
```

```python
import functools

import jax
import jax.numpy as jnp
from jax import lax
from jax.experimental import pallas as pl
from jax.experimental.pallas import tpu as pltpu

F32 = jnp.float32
BF16 = jnp.bfloat16

CHUNK = 64
ROPE_THETA = 500000.0
EPS = 1e-6

A_HEADS = 8
A_KV_HEADS = 2
A_HEAD_DIM = 128
A_ROT = A_HEAD_DIM // 4
A_WIDTH = A_HEADS * A_HEAD_DIM
A_KV_WIDTH = A_KV_HEADS * A_HEAD_DIM
IDX_HEADS = 16
IDX_DIM = 64
IDX_ROT = IDX_DIM // 4
TOPK_MAX = 256

B_HEADS = 8
Q_LORA = 512
KV_LORA = 512
NOPE_DIM = 128
ROPE_DIM = 64
V_DIM = 128
B_WIDTH = B_HEADS * V_DIM
B_QK_PAD = 256

LANES = 128
KEY_BLOCK = 256
QUERY_TILE = 256
INT_MIN = -(2 ** 31)
NEG = -1e30
VMEM_LIMIT = 56 * 1024 * 1024

_SEGS = (("aq", A_WIDTH), ("ak", A_KV_WIDTH), ("av", A_KV_WIDTH), ("agate", A_WIDTH),
         ("iq", IDX_HEADS * IDX_DIM), ("ik", IDX_DIM), ("iw", IDX_HEADS),
         ("cq", Q_LORA), ("ckv", KV_LORA), ("kr", ROPE_DIM), ("bgate", B_WIDTH), ("mg", None))


def _layout(d_model):
    src, off = {}, 0
    for name, w in _SEGS:
        w = 2 * d_model if w is None else w
        src[name] = (off, w)
        off += w
    order = (("mg", 2 * d_model), ("aq", A_WIDTH), ("agate", A_WIDTH), ("iq", IDX_HEADS * IDX_DIM),
             ("bgate", B_WIDTH), ("cq", Q_LORA), ("ckv", KV_LORA), ("ak", A_KV_WIDTH), ("av", A_KV_WIDTH),
             ("ikw", LANES), ("krp", LANES))
    dst, off = {}, 0
    for name, w in order:
        assert off % w == 0
        dst[name] = (off, w)
        off += w
    return src, dst, off


def _regroup_w_in(w_in, d_model):
    src, dst, total = _layout(d_model)
    def cols(name):
        o, w = src[name]
        return w_in[:, o:o + w].astype(BF16)
    zeros = lambda n: jnp.zeros((w_in.shape[0], n), BF16)
    pieces = [cols("mg"), cols("aq"), cols("agate"), cols("iq"), cols("bgate"), cols("cq"), cols("ckv"),
              cols("ak"), cols("av"),
              cols("ik"), cols("iw"), zeros(LANES - IDX_DIM - IDX_HEADS),
              cols("kr"), zeros(LANES - ROPE_DIM)]
    w = jnp.concatenate(pieces, axis=1)
    assert w.shape[1] == total
    return w


def _rope_tables(pos, rot, head, heads_per_vreg, extra=None):
    half = rot // 2
    inv_freq = ROPE_THETA ** (-jnp.arange(half, dtype=F32) * (2.0 / rot))
    ang = pos.astype(F32)[:, None] * inv_freq[None, :]
    cos, sin = jnp.cos(ang), jnp.sin(ang)
    t = pos.shape[0]
    one = jnp.ones((t, head - rot), F32)
    zero = lambda n: jnp.zeros((t, n), F32)
    c = jnp.concatenate([cos, cos, one], axis=1)
    sa = jnp.concatenate([-sin, zero(head - half)], axis=1)
    sb = jnp.concatenate([zero(half), sin, zero(head - rot)], axis=1)
    c, sa, sb = (jnp.tile(a, (1, heads_per_vreg)) for a in (c, sa, sb))
    if extra is not None:
        c = jnp.concatenate([c, extra], axis=1)
        sa = jnp.concatenate([sa, jnp.zeros_like(extra)], axis=1)
        sb = jnp.concatenate([sb, jnp.zeros_like(extra)], axis=1)
    assert c.shape[1] == LANES
    return c, sa, sb


def _rope(x, c, sa, sb, half):
    up = pltpu.roll(x, LANES - half, 1)
    dn = pltpu.roll(x, half, 1)
    return x * c + up * sa + dn * sb


def _rms(x, g):
    ms = jnp.mean(x * x, axis=-1, keepdims=True)
    return (x * lax.rsqrt(ms + EPS)) * g


def _lane_tile(x, width):
    return x if width == LANES else jnp.concatenate([x] * (width // LANES), axis=1)


def _proj_kernel(x_ref, g_ref, w_ref, z_ref, xn_ref):
    @pl.when(pl.program_id(1) == 0)
    def _():
        xn_ref[...] = _rms(x_ref[...], g_ref[...]).astype(BF16)
    z_ref[...] = jnp.dot(xn_ref[...], w_ref[...], preferred_element_type=F32)


def _proj(x, g, w):
    m, d = x.shape
    n = w.shape[1]
    tm = min(m, 1024)
    tn = 768
    assert m % tm == 0 and n % tn == 0
    return pl.pallas_call(
        _proj_kernel,
        grid=(m // tm, n // tn),
        in_specs=[pl.BlockSpec((tm, d), lambda i, j: (i, 0)),
                  pl.BlockSpec((1, d), lambda i, j: (0, 0)),
                  pl.BlockSpec((d, tn), lambda i, j: (0, j))],
        out_specs=pl.BlockSpec((tm, tn), lambda i, j: (i, j)),
        out_shape=jax.ShapeDtypeStruct((m, n), F32),
        scratch_shapes=[pltpu.VMEM((tm, d), BF16)],
        compiler_params=pltpu.CompilerParams(dimension_semantics=("arbitrary", "arbitrary"),
                                             vmem_limit_bytes=VMEM_LIMIT),
        name="proj",
    )(x, g, w)


def _prep_kernel(aq_ref, iq_ref, cq_ref, ckv_ref, ak_ref, av_ref, ikw_ref, krp_ref,
                 ta_ref, ti_ref, tk_ref, tr_ref, qg_ref, kvg_ref, wuq_ref,
                 aq_o, iq_o, ak_o, ikw_o, krp_o, ckv_o, qcat_o, kb_o, vb_o, ikb_o, ckvb_o, krb_o):
    ca, saa, sba = ta_ref[0], ta_ref[1], ta_ref[2]
    ci, sai, sbi = ti_ref[0], ti_ref[1], ti_ref[2]
    ck, sak, sbk = tk_ref[0], tk_ref[1], tk_ref[2]
    cr, sar, sbr = tr_ref[0], tr_ref[1], tr_ref[2]
    tm = aq_ref.shape[0]
    for h in range(A_HEADS):
        sl = slice(h * LANES, (h + 1) * LANES)
        aq_o[:, sl] = _rope(aq_ref[:, sl], ca, saa, sba, A_ROT // 2).astype(BF16)
    for h in range(A_KV_HEADS):
        sl = slice(h * LANES, (h + 1) * LANES)
        ak = _rope(ak_ref[:, sl], ca, saa, sba, A_ROT // 2)
        ak_o[:, sl] = ak
        kb_o[:, sl] = ak.astype(BF16)
    vb_o[...] = av_ref[...].astype(BF16)
    for v in range(IDX_HEADS * IDX_DIM // LANES):
        sl = slice(v * LANES, (v + 1) * LANES)
        iq_o[:, sl] = _rope(iq_ref[:, sl], ci, sai, sbi, IDX_ROT // 2).astype(BF16)
    ikw = _rope(ikw_ref[...], ck, sak, sbk, IDX_ROT // 2)
    ikw_o[...] = ikw
    ikb_o[...] = ikw[:, :IDX_DIM].astype(BF16)
    krp = _rope(krp_ref[...], cr, sar, sbr, ROPE_DIM // 2)
    krp_o[...] = krp
    krb_o[...] = krp[:, :ROPE_DIM].astype(BF16)
    ckv = _rms(ckv_ref[...], kvg_ref[...])
    ckv_o[...] = ckv
    ckvb_o[...] = ckv.astype(BF16)
    cqn = _rms(cq_ref[...], qg_ref[...]).astype(BF16)
    qb = jnp.dot(cqn, wuq_ref[...], preferred_element_type=F32)
    nope = B_HEADS * NOPE_DIM
    low = lax.broadcasted_iota(jnp.int32, (tm, LANES), 1) < ROPE_DIM
    for v in range(B_HEADS * ROPE_DIM // LANES):
        r = _rope(qb[:, nope + v * LANES: nope + (v + 1) * LANES], cr, sar, sbr, ROPE_DIM // 2)
        halves = (jnp.where(low, r, 0.0), jnp.where(low, pltpu.roll(r, ROPE_DIM, 1), 0.0))
        for j in range(2):
            h = 2 * v + j
            qcat_o[:, h * B_QK_PAD: h * B_QK_PAD + NOPE_DIM] = qb[:, h * NOPE_DIM:(h + 1) * NOPE_DIM].astype(BF16)
            qcat_o[:, h * B_QK_PAD + NOPE_DIM:(h + 1) * B_QK_PAD] = halves[j].astype(BF16)


def _prep(z, dst, bsz, t, tabs, q_norm_g, kv_norm_g, wuq):
    m = bsz * t
    tm = min(t, 256)
    nt = t // tm
    assert t % tm == 0

    def zspec(name):
        o, w = dst[name]
        return pl.BlockSpec((tm, w), lambda b, i, blk=o // w: (b * nt + i, blk))

    def tspec():
        return pl.BlockSpec((3, tm, LANES), lambda b, i: (0, i, 0))

    def ospec(w):
        return pl.BlockSpec((tm, w), lambda b, i: (b * nt + i, 0))

    def full(a):
        return pl.BlockSpec(a.shape, lambda b, i: (0,) * a.ndim)

    outs = [(A_WIDTH, BF16), (IDX_HEADS * IDX_DIM, BF16), (A_KV_WIDTH, F32), (LANES, F32), (LANES, F32),
            (KV_LORA, F32), (B_HEADS * B_QK_PAD, BF16),
            (A_KV_WIDTH, BF16), (A_KV_WIDTH, BF16), (IDX_DIM, BF16), (KV_LORA, BF16), (ROPE_DIM, BF16)]
    return pl.pallas_call(
        _prep_kernel,
        grid=(bsz, nt),
        in_specs=[zspec("aq"), zspec("iq"), zspec("cq"), zspec("ckv"), zspec("ak"), zspec("av"), zspec("ikw"),
                  zspec("krp"), tspec(), tspec(), tspec(), tspec(), full(q_norm_g), full(kv_norm_g), full(wuq)],
        out_specs=[ospec(w) for w, _ in outs],
        out_shape=[jax.ShapeDtypeStruct((m, w), dt) for w, dt in outs],
        compiler_params=pltpu.CompilerParams(dimension_semantics=("arbitrary", "arbitrary"),
                                             vmem_limit_bytes=VMEM_LIMIT),
        name="prep",
    )(z, z, z, z, z, z, z, z, *tabs, q_norm_g, kv_norm_g, wuq)


def _kvup_kernel(c_ref, kr_ref, w_ref, kcat_ref, v_ref):
    r = jnp.dot(c_ref[...], w_ref[...], preferred_element_type=F32)
    n = B_HEADS * NOPE_DIM
    tail = jnp.concatenate([kr_ref[...], jnp.zeros((kr_ref.shape[0], B_QK_PAD - NOPE_DIM - ROPE_DIM), BF16)], axis=1)
    for h in range(B_HEADS):
        kcat_ref[:, h * B_QK_PAD: h * B_QK_PAD + NOPE_DIM] = r[:, h * NOPE_DIM:(h + 1) * NOPE_DIM].astype(BF16)
        kcat_ref[:, h * B_QK_PAD + NOPE_DIM:(h + 1) * B_QK_PAD] = tail
    v_ref[...] = r[:, n:].astype(BF16)


def _kvup(ckv, kr, w):
    m, c = ckv.shape
    tm = next(t for t in (512, 384, 256, 128) if m % t == 0)
    row = lambda i: (i, 0)
    return pl.pallas_call(
        _kvup_kernel,
        grid=(m // tm,),
        in_specs=[pl.BlockSpec((tm, c), row), pl.BlockSpec((tm, ROPE_DIM), row), pl.BlockSpec(w.shape, lambda i: (0, 0))],
        out_specs=[pl.BlockSpec((tm, B_HEADS * B_QK_PAD), row), pl.BlockSpec((tm, B_WIDTH), row)],
        out_shape=[jax.ShapeDtypeStruct((m, B_HEADS * B_QK_PAD), BF16), jax.ShapeDtypeStruct((m, B_WIDTH), BF16)],
        compiler_params=pltpu.CompilerParams(dimension_semantics=("arbitrary",), vmem_limit_bytes=VMEM_LIMIT),
        name="kv_up",
    )(ckv, kr, w)


_NT = (((1,), (1,)), ((), ()))


def _visible_cols(tile, tq, l_valid, qpos0):
    p_last = qpos0 + (tile + 1) * tq - 1
    n = (p_last // CHUNK + 1) * CHUNK
    return min(n, l_valid) if isinstance(n, int) else jnp.minimum(n, l_valid)


def _num_key_blocks(tile, tq, kw, l_valid, qpos0):
    return (_visible_cols(tile, tq, l_valid, qpos0) + kw - 1) // kw


def _allowed(tile, kb, rows0, shape, tq, kw, l_valid, qpos0):
    rows = lax.broadcasted_iota(jnp.int32, shape, 0) + rows0
    cols = lax.broadcasted_iota(jnp.int32, shape, 1) + kb * kw
    qchunk = (qpos0 + tile * tq + rows) // CHUNK
    return ((cols // CHUNK) <= qchunk) & (cols < l_valid), cols


def _softmax_step(s, h, vblk, m_ref, l_ref, acc_ref):
    kw = s.shape[1]
    m_prev = m_ref[h]
    m_new = jnp.maximum(m_prev, jnp.max(s, axis=-1, keepdims=True))
    alpha = jnp.exp(m_prev - m_new)
    p = jnp.exp(s - _lane_tile(m_new, kw))
    l_ref[h] = alpha * l_ref[h] + jnp.sum(p, axis=-1, keepdims=True)
    acc_ref[h] = alpha * acc_ref[h] + jnp.dot(p.astype(BF16), vblk, preferred_element_type=F32)
    m_ref[h] = m_new


def _softmax_init(heads, tq, m_ref, l_ref, acc_ref):
    for h in range(heads):
        m_ref[h] = jnp.full((tq, LANES), NEG, F32)
        l_ref[h] = jnp.zeros((tq, LANES), F32)
        acc_ref[h] = jnp.zeros((tq, acc_ref.shape[2]), F32)


def _dsa_kernel(aq_ref, iq_ref, ikw_ref, agate_ref, k_ref, v_ref, ik_ref, o_ref,
                key_ref, thr_ref, jcut_ref, need_ref, bias_ref, iwb_ref, qs_ref, acc_ref, m_ref, l_ref,
                *, nt, tq, kw, l_valid, qpos0, topk):
    nkb_max = key_ref.shape[0] // nt
    nkb_of = [_num_key_blocks(ti, tq, kw, l_valid, qpos0) for ti in range(nt)]
    rs = min(tq, 64)
    ns = tq // rs
    nc = kw // LANES
    kf = float(topk)

    def cols_of(kb, shape):
        return lax.broadcasted_iota(jnp.int32, shape, 1) + kb * kw

    def score_tile(ti, carry):
        r0 = pl.multiple_of(ti * tq, tq)
        iw = ikw_ref[pl.ds(r0, tq), IDX_DIM:IDX_DIM + IDX_HEADS] * (IDX_DIM ** -0.5)
        for h in range(IDX_HEADS):
            iwb_ref[h] = jnp.broadcast_to(iw[:, h:h + 1], (tq, LANES))

        def score_block(kb, c2):
            ikb = ik_ref[0, pl.ds(pl.multiple_of(kb * kw, kw), kw), :]
            for r in range(ns):
                sc = [jnp.zeros((rs, LANES), F32) for _ in range(nc)]
                for h in range(IDX_HEADS):
                    q = iq_ref[pl.ds(r0 + r * rs, rs), h * IDX_DIM:(h + 1) * IDX_DIM]
                    rel = lax.dot_general(q, ikb, _NT, preferred_element_type=F32)
                    w = iwb_ref[h, r * rs:(r + 1) * rs, :]
                    for c in range(nc):
                        sc[c] = sc[c] + jnp.maximum(rel[:, c * LANES:(c + 1) * LANES], 0.0) * w
                s = jnp.concatenate(sc, axis=1) + 0.0
                bits = lax.bitcast_convert_type(s, jnp.int32)
                key = bits ^ ((bits >> 31) & 0x7FFFFFFF)
                allowed, _ = _allowed(ti, kb, r * rs, (rs, kw), tq, kw, l_valid, qpos0)
                key_ref[ti * nkb_max + kb, r * rs:(r + 1) * rs, :] = jnp.where(allowed, key, INT_MIN)
            return c2

        lax.fori_loop(0, _num_key_blocks(ti, tq, kw, l_valid, qpos0), score_block, 0)
        return carry

    def strip_count(ti, s, pred):
        acc = jnp.zeros((rs, LANES), F32)
        for kb in range(nkb_of[ti]):
            k = key_ref[ti * nkb_max + kb, s * rs:(s + 1) * rs, :]
            for c in range(nc):
                acc = acc + jnp.where(pred(k[:, c * LANES:(c + 1) * LANES], kb * kw + c * LANES), 1.0, 0.0)
        return jnp.sum(acc, axis=-1, keepdims=True)

    def bit_step(b, carry):
        bit = jnp.left_shift(jnp.int32(1), 31 - b)
        for ti in range(nt):
            for s in range(ns):
                rows = slice(ti * tq + s * rs, ti * tq + (s + 1) * rs)
                t = thr_ref[rows, :]
                cand = t + bit
                cnt = strip_count(ti, s, lambda k, c0: k >= cand)
                thr_ref[rows, :] = jnp.where(cnt >= kf, cand, t)
        return carry

    @pl.when(pl.program_id(1) == 0)
    def _():
        lax.fori_loop(0, nt, score_tile, 0)
        thr_ref[...] = jnp.full(thr_ref.shape, INT_MIN, jnp.int32)
        lax.fori_loop(0, 32, bit_step, 0)

        jcut_ref[...] = jnp.full(jcut_ref.shape, 4096, jnp.int32)
        flag = jnp.int32(0)
        for ti in range(nt):
            for s in range(ns):
                rows = slice(ti * tq + s * rs, ti * tq + (s + 1) * rs)
                t = thr_ref[rows, :]
                need = kf - strip_count(ti, s, lambda k, c0: k > t)
                n_eq = strip_count(ti, s, lambda k, c0: k == t)
                need_ref[rows, :] = jnp.broadcast_to(need, (rs, LANES))
                excess = (n_eq > need) & (t[:, :1] > INT_MIN)
                flag = jnp.maximum(flag, jnp.max(jnp.where(excess, 1, 0)))

        @pl.when(flag > 0)
        def _():
            lane = lax.broadcasted_iota(jnp.int32, (rs, LANES), 1)

            def col_step(b, carry):
                bit = jnp.left_shift(jnp.int32(1), 11 - b)
                for ti in range(nt):
                    for s in range(ns):
                        rows = slice(ti * tq + s * rs, ti * tq + (s + 1) * rs)
                        t, j = thr_ref[rows, :], jcut_ref[rows, :]
                        cand = j + bit
                        f = strip_count(ti, s, lambda k, c0: (k == t) & (lane + c0 < cand))
                        jcut_ref[rows, :] = jnp.where(f <= need_ref[rows, :], cand, j)
                return carry

            jcut_ref[...] = jnp.zeros(jcut_ref.shape, jnp.int32)
            lax.fori_loop(0, 12, col_step, 0)

    ti = pl.program_id(1)
    nkb = _num_key_blocks(ti, tq, kw, l_valid, qpos0)
    r0 = pl.multiple_of(ti * tq, tq)
    thr = _lane_tile(thr_ref[pl.ds(r0, tq), :], kw)
    jcut = _lane_tile(jcut_ref[pl.ds(r0, tq), :], kw)

    def bias_block(kb, carry):
        k = key_ref[ti * nkb_max + kb]
        sel = ((k > thr) | ((k == thr) & (cols_of(kb, (tq, kw)) < jcut))) & (k != INT_MIN)
        bias_ref[kb] = jnp.where(sel, 0.0, NEG)
        return carry

    lax.fori_loop(0, nkb, bias_block, 0)

    grp = A_HEADS // A_KV_HEADS
    _softmax_init(A_KV_HEADS, grp * tq, m_ref, l_ref, acc_ref)
    scale = A_HEAD_DIM ** -0.5
    for g in range(A_KV_HEADS):
        gsl = slice(g * A_HEAD_DIM, (g + 1) * A_HEAD_DIM)
        for hh in range(grp):
            h = g * grp + hh
            qs_ref[hh * tq:(hh + 1) * tq, :] = aq_ref[:, h * A_HEAD_DIM:(h + 1) * A_HEAD_DIM]

        def att_block(kb, carry):
            c0 = pl.multiple_of(kb * kw, kw)
            kblk = k_ref[0, pl.ds(c0, kw), gsl]
            vblk = v_ref[0, pl.ds(c0, kw), gsl]
            s = lax.dot_general(qs_ref[...], kblk, _NT, preferred_element_type=F32) * scale
            s = (s.reshape(grp, tq, kw) + bias_ref[kb][None]).reshape(grp * tq, kw)
            _softmax_step(s, g, vblk, m_ref, l_ref, acc_ref)
            return carry

        lax.fori_loop(0, nkb, att_block, 0)

    for h in range(A_HEADS):
        g, hh = divmod(h, grp)
        sl = slice(h * A_HEAD_DIM, (h + 1) * A_HEAD_DIM)
        o = acc_ref[g, hh * tq:(hh + 1) * tq, :] / l_ref[g, hh * tq:(hh + 1) * tq, :]
        gate = agate_ref[:, sl]
        o_ref[:, sl] = (o * (gate * jax.nn.sigmoid(gate))).astype(BF16)


def _dsa(aq, iq, ikw, z, dst, k, v, ik, bsz, t, tq, kw, l_valid, qpos0):
    lp = k.shape[1]
    assert lp < 4096 and lp % kw == 0 and t % tq == 0
    nt = t // tq
    nkb_max = lp // kw
    topk = min(TOPK_MAX, l_valid // 4)
    grp = A_HEADS // A_KV_HEADS
    go, gw = dst["agate"]
    row = lambda b, i: (b * nt + i, 0)
    whole = lambda b, i: (b, 0)
    batch = lambda b, i: (b, 0, 0)
    kern = functools.partial(_dsa_kernel, nt=nt, tq=tq, kw=kw, l_valid=l_valid, qpos0=qpos0, topk=topk)
    return pl.pallas_call(
        kern,
        grid=(bsz, nt),
        in_specs=[pl.BlockSpec((tq, A_WIDTH), row),
                  pl.BlockSpec((t, IDX_HEADS * IDX_DIM), whole),
                  pl.BlockSpec((t, LANES), whole),
                  pl.BlockSpec((tq, gw), lambda b, i: (b * nt + i, go // gw)),
                  pl.BlockSpec((1, lp, A_KV_WIDTH), batch),
                  pl.BlockSpec((1, lp, A_KV_WIDTH), batch),
                  pl.BlockSpec((1, lp, IDX_DIM), batch)],
        out_specs=pl.BlockSpec((tq, A_WIDTH), row),
        out_shape=jax.ShapeDtypeStruct((bsz * t, A_WIDTH), BF16),
        scratch_shapes=[pltpu.VMEM((nt * nkb_max, tq, kw), jnp.int32),
                        pltpu.VMEM((t, LANES), jnp.int32),
                        pltpu.VMEM((t, LANES), jnp.int32),
                        pltpu.VMEM((t, LANES), F32),
                        pltpu.VMEM((nkb_max, tq, kw), F32),
                        pltpu.VMEM((IDX_HEADS, tq, LANES), F32),
                        pltpu.VMEM((grp * tq, A_HEAD_DIM), BF16),
                        pltpu.VMEM((A_KV_HEADS, grp * tq, A_HEAD_DIM), F32),
                        pltpu.VMEM((A_KV_HEADS, grp * tq, LANES), F32),
                        pltpu.VMEM((A_KV_HEADS, grp * tq, LANES), F32)],
        compiler_params=pltpu.CompilerParams(dimension_semantics=("arbitrary", "arbitrary"),
                                             vmem_limit_bytes=VMEM_LIMIT),
        name="dsa",
    )(aq, iq, ikw, z, k, v, ik)


def _mla_kernel(q_ref, bgate_ref, k_ref, v_ref, o_ref, acc_ref, m_ref, l_ref, *, tq, kw, l_valid, qpos0):
    ti = pl.program_id(1)
    nkb = _num_key_blocks(ti, tq, kw, l_valid, qpos0)
    scale = (NOPE_DIM + ROPE_DIM) ** -0.5
    _softmax_init(B_HEADS, tq, m_ref, l_ref, acc_ref)

    def att_block(kb, carry):
        r0 = pl.multiple_of(kb * kw, kw)
        allowed, _ = _allowed(ti, kb, 0, (tq, kw), tq, kw, l_valid, qpos0)
        bias = jnp.where(allowed, 0.0, NEG)
        for h in range(B_HEADS):
            qsl = slice(h * B_QK_PAD, (h + 1) * B_QK_PAD)
            s = lax.dot_general(q_ref[:, qsl], k_ref[0, pl.ds(r0, kw), qsl], _NT, preferred_element_type=F32)
            s = s * scale + bias
            _softmax_step(s, h, v_ref[0, pl.ds(r0, kw), h * V_DIM:(h + 1) * V_DIM], m_ref, l_ref, acc_ref)
        return carry

    lax.fori_loop(0, nkb, att_block, 0)

    for h in range(B_HEADS):
        sl = slice(h * V_DIM, (h + 1) * V_DIM)
        o = acc_ref[h] / l_ref[h]
        gate = bgate_ref[:, sl]
        o_ref[:, sl] = (o * (gate * jax.nn.sigmoid(gate))).astype(BF16)


def _mla(qcat, z, dst, kcat, v, bsz, t, tq, kw, l_valid, qpos0):
    lp = kcat.shape[1]
    assert lp % kw == 0 and t % tq == 0
    nt = t // tq
    go, gw = dst["bgate"]
    row = lambda b, i: (b * nt + i, 0)
    batch = lambda b, i: (b, 0, 0)
    kern = functools.partial(_mla_kernel, tq=tq, kw=kw, l_valid=l_valid, qpos0=qpos0)
    return pl.pallas_call(
        kern,
        grid=(bsz, nt),
        in_specs=[pl.BlockSpec((tq, B_HEADS * B_QK_PAD), row),
                  pl.BlockSpec((tq, gw), lambda b, i: (b * nt + i, go // gw)),
                  pl.BlockSpec((1, lp, B_HEADS * B_QK_PAD), batch),
                  pl.BlockSpec((1, lp, B_WIDTH), batch)],
        out_specs=pl.BlockSpec((tq, B_WIDTH), row),
        out_shape=jax.ShapeDtypeStruct((bsz * t, B_WIDTH), BF16),
        scratch_shapes=[pltpu.VMEM((B_HEADS, tq, V_DIM), F32),
                        pltpu.VMEM((B_HEADS, tq, LANES), F32),
                        pltpu.VMEM((B_HEADS, tq, LANES), F32)],
        compiler_params=pltpu.CompilerParams(dimension_semantics=("arbitrary", "arbitrary"),
                                             vmem_limit_bytes=VMEM_LIMIT),
        name="mla",
    )(qcat, z, kcat, v)


def _mla_lat_kernel(q_ref, bgate_ref, cc_ref, ckr_ref, nc_ref, nkr_ref, wukt_ref, wuv_ref, o_ref,
                    *, t, past, qpos0):
    rows_n = B_HEADS * t
    ql, qr = [], []
    for h in range(B_HEADS):
        qn = q_ref[:, h * B_QK_PAD: h * B_QK_PAD + NOPE_DIM]
        ql.append(jnp.dot(qn, wukt_ref[h], preferred_element_type=F32).astype(BF16))
        qr.append(q_ref[:, h * B_QK_PAD + NOPE_DIM: h * B_QK_PAD + NOPE_DIM + ROPE_DIM])
    ql = jnp.concatenate(ql, axis=0)
    qr = jnp.concatenate(qr, axis=0)
    cc = cc_ref[0].astype(BF16)
    ckr = ckr_ref[0].astype(BF16)
    nc, nkr = nc_ref[...], nkr_ref[...]
    scale = (NOPE_DIM + ROPE_DIM) ** -0.5

    def scores(lat, rope, ncols, col0):
        s = lax.dot_general(ql, lat, _NT, preferred_element_type=F32)
        s = (s + lax.dot_general(qr, rope, _NT, preferred_element_type=F32)) * scale
        rows = lax.broadcasted_iota(jnp.int32, (rows_n, ncols), 0)
        cols = lax.broadcasted_iota(jnp.int32, (rows_n, ncols), 1) + col0
        qchunk = (qpos0 + lax.rem(rows, t)) // CHUNK
        return jnp.where((cols // CHUNK) <= qchunk, s, NEG)

    s1 = scores(cc, ckr, past, 0)
    s2 = scores(nc, nkr, t, past)
    m = jnp.maximum(jnp.max(s1, axis=-1, keepdims=True), jnp.max(s2, axis=-1, keepdims=True))
    e1, e2 = jnp.exp(s1 - m), jnp.exp(s2 - m)
    l = jnp.sum(e1, axis=-1, keepdims=True) + jnp.sum(e2, axis=-1, keepdims=True)
    ol = (jnp.dot(e1.astype(BF16), cc, preferred_element_type=F32)
          + jnp.dot(e2.astype(BF16), nc, preferred_element_type=F32)) / l
    ol = ol.astype(BF16)
    for h in range(B_HEADS):
        sl = slice(h * V_DIM, (h + 1) * V_DIM)
        o = jnp.dot(ol[h * t:(h + 1) * t], wuv_ref[:, sl], preferred_element_type=F32)
        gate = bgate_ref[:, sl]
        o_ref[:, sl] = (o * (gate * jax.nn.sigmoid(gate))).astype(BF16)


def _mla_lat(qcat, z, dst, cache_ckv, cache_kr, new_ckv, new_kr, wukt, wuv, bsz, t, qpos0):
    past = cache_ckv.shape[1]
    go, gw = dst["bgate"]
    row = lambda b: (b, 0)
    batch = lambda b: (b, 0, 0)
    const = lambda a: pl.BlockSpec(a.shape, lambda b: (0,) * a.ndim)
    kern = functools.partial(_mla_lat_kernel, t=t, past=past, qpos0=qpos0)
    return pl.pallas_call(
        kern,
        grid=(bsz,),
        in_specs=[pl.BlockSpec((t, B_HEADS * B_QK_PAD), row),
                  pl.BlockSpec((t, gw), lambda b: (b, go // gw)),
                  pl.BlockSpec((1, past, KV_LORA), batch),
                  pl.BlockSpec((1, past, ROPE_DIM), batch),
                  pl.BlockSpec((t, KV_LORA), row),
                  pl.BlockSpec((t, ROPE_DIM), row),
                  const(wukt), const(wuv)],
        out_specs=pl.BlockSpec((t, B_WIDTH), row),
        out_shape=jax.ShapeDtypeStruct((bsz * t, B_WIDTH), BF16),
        compiler_params=pltpu.CompilerParams(dimension_semantics=("arbitrary",), vmem_limit_bytes=VMEM_LIMIT),
        name="mla_lat",
    )(qcat, z, cache_ckv, cache_kr, new_ckv, new_kr, wukt, wuv)


def _out_kernel(ga_ref, gb_ref, mg_ref, x_ref, woa_ref, wob_ref, wout_ref, fg_ref, o_ref, *, d, final):
    ya = jnp.dot(ga_ref[...], woa_ref[...], preferred_element_type=F32)
    yb = jnp.dot(gb_ref[...], wob_ref[...], preferred_element_type=F32)
    m = jax.nn.sigmoid(mg_ref[:, :d]) * ya + jax.nn.sigmoid(mg_ref[:, d:]) * yb
    h = x_ref[...] + jnp.dot(m.astype(BF16), wout_ref[...], preferred_element_type=F32)
    o_ref[...] = _rms(h, fg_ref[...]) if final else h


def _out(ga, gb, z, dst, x, woa, wob, wout, fg, final):
    m, d = x.shape
    tm = min(m, 256)
    assert m % tm == 0 and dst["mg"][0] == 0
    row = lambda i: (i, 0)
    const = lambda a: pl.BlockSpec(a.shape, lambda i: (0, 0), pipeline_mode=pl.Buffered(1))
    return pl.pallas_call(
        functools.partial(_out_kernel, d=d, final=final),
        grid=(m // tm,),
        in_specs=[pl.BlockSpec((tm, A_WIDTH), row), pl.BlockSpec((tm, B_WIDTH), row),
                  pl.BlockSpec((tm, 2 * d), row), pl.BlockSpec((tm, d), row),
                  const(woa), const(wob), const(wout), const(fg)],
        out_specs=pl.BlockSpec((tm, d), row),
        out_shape=jax.ShapeDtypeStruct((m, d), F32),
        compiler_params=pltpu.CompilerParams(dimension_semantics=("arbitrary",), vmem_limit_bytes=VMEM_LIMIT),
        name="out",
    )(ga, gb, z, x, woa, wob, wout, fg)


def _all_keys(cache, new, bsz, t, lp):
    new = new.reshape(bsz, t, new.shape[-1])
    if cache is None and lp == t:
        return new
    parts = ([] if cache is None else [cache.reshape(bsz, cache.shape[1], -1).astype(BF16)]) + [new]
    kk = jnp.concatenate(parts, axis=1)
    return jnp.pad(kk, ((0, 0), (0, lp - kk.shape[1]), (0, 0)))


def _trunk(x, caches, final_g, weights):
    bsz, t, d = x.shape
    depth = weights["w_in"].shape[0]
    past = 0 if caches is None else caches[0].shape[2]
    l_valid = past + t
    tq = min(t, QUERY_TILE)
    kw = KEY_BLOCK if t >= QUERY_TILE else -(-l_valid // LANES) * LANES
    lp = -(-l_valid // kw) * kw
    _, dst, _ = _layout(d)
    pos = past + jnp.arange(t, dtype=jnp.int32)
    t_scale = jnp.concatenate([jnp.full((t, IDX_HEADS), IDX_HEADS ** -0.5, F32),
                               jnp.ones((t, LANES - IDX_DIM - IDX_HEADS), F32)], axis=1)
    tabs = [jnp.stack(_rope_tables(pos, A_ROT, A_HEAD_DIM, 1)),
            jnp.stack(_rope_tables(pos, IDX_ROT, IDX_DIM, 2)),
            jnp.stack(_rope_tables(pos, IDX_ROT, IDX_DIM, 1, extra=t_scale)),
            jnp.stack(_rope_tables(pos, ROPE_DIM, ROPE_DIM, 2))]
    h = x.reshape(bsz * t, d)
    new_rows = []
    for l in range(depth):
        w = {k: v[l] for k, v in weights.items()}
        z = _proj(h, w["norm_g"], w["w_in"])
        aq, iq, ak, ikw, krp, ckv, qcat, kb, vb, ikb, ckvb, krb = _prep(
            z, dst, bsz, t, tabs, w["q_norm_g"], w["kv_norm_g"], w["w_uq"])
        o, wd = dst["av"]
        av = z[:, o:o + wd]
        c = (None,) * 5 if caches is None else tuple(cc[l] for cc in caches)
        k_all = _all_keys(c[0], kb, bsz, t, lp)
        v_all = _all_keys(c[1], vb, bsz, t, lp)
        ik_all = _all_keys(c[2], ikb, bsz, t, lp)
        ga = _dsa(aq, iq, ikw, z, dst, k_all, v_all, ik_all, bsz, t, tq, kw, l_valid, past)
        if caches is not None and t < QUERY_TILE:
            gb = _mla_lat(qcat, z, dst, c[3], c[4], ckvb, krb, w["w_ukt"], w["w_uv"], bsz, t, past)
        else:
            ckv_all = _all_keys(c[3], ckvb, bsz, t, lp)
            kr_all = _all_keys(c[4], krb, bsz, t, lp)
            kcat, vv = _kvup(ckv_all.reshape(bsz * lp, KV_LORA), kr_all.reshape(bsz * lp, ROPE_DIM), w["w_ukv"])
            gb = _mla(qcat, z, dst, kcat.reshape(bsz, lp, -1), vv.reshape(bsz, lp, -1), bsz, t, tq, kw, l_valid, past)
        h = _out(ga, gb, z, dst, h, w["w_o_a"], w["w_o_b"], w["w_out"], final_g, final=(l == depth - 1))
        new_rows.append((ak.reshape(bsz, t, A_KV_HEADS, A_HEAD_DIM), av.reshape(bsz, t, A_KV_HEADS, A_HEAD_DIM),
                         ikw[:, :IDX_DIM].reshape(bsz, t, IDX_DIM), ckv.reshape(bsz, t, KV_LORA),
                         krp[:, :ROPE_DIM].reshape(bsz, t, ROPE_DIM)))
    stacked = [jnp.stack([r[i] for r in new_rows], axis=0) for i in range(5)]
    return h.reshape(bsz, t, d), stacked


def kernel(x_prompt, x_sample, cache_a_k, cache_a_v, cache_idx_k, cache_mla_ckv, cache_mla_krope,
           norm_g, w_in, w_uq, q_norm_g, kv_norm_g, w_uk, w_uv, w_o_a, w_o_b, w_out, final_g):
    depth, d, _ = w_in.shape
    weights = {
        "norm_g": norm_g.reshape(depth, 1, d),
        "w_in": jnp.stack([_regroup_w_in(w_in[l], d) for l in range(depth)]),
        "w_uq": jnp.concatenate([w_uq[..., :NOPE_DIM].reshape(depth, Q_LORA, B_HEADS * NOPE_DIM),
                                 w_uq[..., NOPE_DIM:].reshape(depth, Q_LORA, B_HEADS * ROPE_DIM)], axis=-1).astype(BF16),
        "q_norm_g": q_norm_g.reshape(depth, 1, Q_LORA),
        "kv_norm_g": kv_norm_g.reshape(depth, 1, KV_LORA),
        "w_ukv": jnp.concatenate([w_uk.reshape(depth, KV_LORA, B_HEADS * NOPE_DIM),
                                  w_uv.reshape(depth, KV_LORA, B_WIDTH)], axis=-1).astype(BF16),
        "w_ukt": jnp.transpose(w_uk, (0, 2, 3, 1)).astype(BF16),
        "w_uv": w_uv.reshape(depth, KV_LORA, B_WIDTH).astype(BF16),
        "w_o_a": w_o_a.astype(BF16), "w_o_b": w_o_b.astype(BF16), "w_out": w_out.astype(BF16),
    }
    fg = final_g.reshape(1, d)
    y_p, (pk, pv, pik, pckv, pkr) = _trunk(x_prompt, None, fg, weights)
    caches = (cache_a_k, cache_a_v, cache_idx_k, cache_mla_ckv, cache_mla_krope)
    y_s, (sk, sv, sik, sckv, skr) = _trunk(x_sample, caches, fg, weights)
    return (y_p, y_s, pk, pv, pik, pckv, pkr, sk, sv, sik, sckv, skr)
```

```python
import functools

import jax
import jax.numpy as jnp
from jax import lax
from jax.experimental import pallas as pl
from jax.experimental.pallas import tpu as pltpu

F32 = jnp.float32
BF16 = jnp.bfloat16

CHUNK = 64
ROPE_THETA = 500000.0
EPS = 1e-6

A_HEADS = 8
A_KV_HEADS = 2
A_HEAD_DIM = 128
A_ROT = A_HEAD_DIM // 4
A_WIDTH = A_HEADS * A_HEAD_DIM
A_KV_WIDTH = A_KV_HEADS * A_HEAD_DIM
IDX_HEADS = 16
IDX_DIM = 64
IDX_ROT = IDX_DIM // 4
TOPK_MAX = 256

B_HEADS = 8
Q_LORA = 512
KV_LORA = 512
NOPE_DIM = 128
ROPE_DIM = 64
V_DIM = 128
B_WIDTH = B_HEADS * V_DIM
B_QK_PAD = 256

LANES = 128
KEY_BLOCK = 256
QUERY_TILE = 256
INT_MIN = -(2 ** 31)
NEG = -1e30
MASKED = -1e38
VMEM_LIMIT = 56 * 1024 * 1024

_SEGS = (("aq", A_WIDTH), ("ak", A_KV_WIDTH), ("av", A_KV_WIDTH), ("agate", A_WIDTH),
         ("iq", IDX_HEADS * IDX_DIM), ("ik", IDX_DIM), ("iw", IDX_HEADS),
         ("cq", Q_LORA), ("ckv", KV_LORA), ("kr", ROPE_DIM), ("bgate", B_WIDTH), ("mg", None))


def _layout(d_model):
    src, off = {}, 0
    for name, w in _SEGS:
        w = 2 * d_model if w is None else w
        src[name] = (off, w)
        off += w
    order = (("mg", 2 * d_model), ("aq", A_WIDTH), ("agate", A_WIDTH), ("iq", IDX_HEADS * IDX_DIM),
             ("bgate", B_WIDTH), ("cq", Q_LORA), ("ckv", KV_LORA), ("ak", A_KV_WIDTH), ("av", A_KV_WIDTH),
             ("ikw", LANES), ("krp", LANES))
    dst, off = {}, 0
    for name, w in order:
        assert off % w == 0
        dst[name] = (off, w)
        off += w
    return src, dst, off


def _regroup_w_in(w_in, d_model):
    src, dst, total = _layout(d_model)
    def cols(name):
        o, w = src[name]
        return w_in[:, o:o + w].astype(BF16)
    zeros = lambda n: jnp.zeros((w_in.shape[0], n), BF16)
    pieces = [cols("mg"), cols("aq"), cols("agate"), cols("iq"), cols("bgate"), cols("cq"), cols("ckv"),
              cols("ak"), cols("av"),
              cols("ik"), cols("iw"), zeros(LANES - IDX_DIM - IDX_HEADS),
              cols("kr"), zeros(LANES - ROPE_DIM)]
    w = jnp.concatenate(pieces, axis=1)
    assert w.shape[1] == total
    return w


def _rope_tables(pos, rot, head, heads_per_vreg, extra=None):
    half = rot // 2
    inv_freq = ROPE_THETA ** (-jnp.arange(half, dtype=F32) * (2.0 / rot))
    ang = pos.astype(F32)[:, None] * inv_freq[None, :]
    cos, sin = jnp.cos(ang), jnp.sin(ang)
    t = pos.shape[0]
    one = jnp.ones((t, head - rot), F32)
    zero = lambda n: jnp.zeros((t, n), F32)
    c = jnp.concatenate([cos, cos, one], axis=1)
    sa = jnp.concatenate([-sin, zero(head - half)], axis=1)
    sb = jnp.concatenate([zero(half), sin, zero(head - rot)], axis=1)
    c, sa, sb = (jnp.tile(a, (1, heads_per_vreg)) for a in (c, sa, sb))
    if extra is not None:
        c = jnp.concatenate([c, extra], axis=1)
        sa = jnp.concatenate([sa, jnp.zeros_like(extra)], axis=1)
        sb = jnp.concatenate([sb, jnp.zeros_like(extra)], axis=1)
    assert c.shape[1] == LANES
    return c, sa, sb


def _rope(x, c, sa, sb, half):
    up = pltpu.roll(x, LANES - half, 1)
    dn = pltpu.roll(x, half, 1)
    return x * c + up * sa + dn * sb


def _rms(x, g):
    ms = jnp.mean(x * x, axis=-1, keepdims=True)
    return (x * lax.rsqrt(ms + EPS)) * g


def _lane_tile(x, width):
    return x if width == LANES else jnp.concatenate([x] * (width // LANES), axis=1)


def _proj_kernel(x_ref, g_ref, w_ref, z_ref, xn_ref):
    @pl.when(pl.program_id(1) == 0)
    def _():
        xn_ref[...] = _rms(x_ref[...], g_ref[...]).astype(BF16)
    z_ref[...] = jnp.dot(xn_ref[...], w_ref[...], preferred_element_type=F32)


def _proj(x, g, w):
    m, d = x.shape
    n = w.shape[1]
    tm = min(m, 1024)
    tn = 768
    assert m % tm == 0 and n % tn == 0
    return pl.pallas_call(
        _proj_kernel,
        grid=(m // tm, n // tn),
        in_specs=[pl.BlockSpec((tm, d), lambda i, j: (i, 0)),
                  pl.BlockSpec((1, d), lambda i, j: (0, 0)),
                  pl.BlockSpec((d, tn), lambda i, j: (0, j))],
        out_specs=pl.BlockSpec((tm, tn), lambda i, j: (i, j)),
        out_shape=jax.ShapeDtypeStruct((m, n), F32),
        scratch_shapes=[pltpu.VMEM((tm, d), BF16)],
        compiler_params=pltpu.CompilerParams(dimension_semantics=("arbitrary", "arbitrary"),
                                             vmem_limit_bytes=VMEM_LIMIT),
        name="proj",
    )(x, g, w)


def _prep_kernel(aq_ref, iq_ref, cq_ref, ckv_ref, ak_ref, av_ref, ikw_ref, krp_ref,
                 ta_ref, ti_ref, tk_ref, tr_ref, qg_ref, kvg_ref, wuq_ref,
                 aq_o, iq_o, ak_o, ikw_o, krp_o, ckv_o, qcat_o, kb_o, vb_o, ikb_o, ckvb_o, krb_o):
    ca, saa, sba = ta_ref[0], ta_ref[1], ta_ref[2]
    ci, sai, sbi = ti_ref[0], ti_ref[1], ti_ref[2]
    ck, sak, sbk = tk_ref[0], tk_ref[1], tk_ref[2]
    cr, sar, sbr = tr_ref[0], tr_ref[1], tr_ref[2]
    tm = aq_ref.shape[0]
    for h in range(A_HEADS):
        sl = slice(h * LANES, (h + 1) * LANES)
        aq_o[:, sl] = _rope(aq_ref[:, sl], ca, saa, sba, A_ROT // 2).astype(BF16)
    for h in range(A_KV_HEADS):
        sl = slice(h * LANES, (h + 1) * LANES)
        ak = _rope(ak_ref[:, sl], ca, saa, sba, A_ROT // 2)
        ak_o[:, sl] = ak
        kb_o[:, sl] = ak.astype(BF16)
    vb_o[...] = av_ref[...].astype(BF16)
    for v in range(IDX_HEADS * IDX_DIM // LANES):
        sl = slice(v * LANES, (v + 1) * LANES)
        iq_o[:, sl] = _rope(iq_ref[:, sl], ci, sai, sbi, IDX_ROT // 2).astype(BF16)
    ikw = _rope(ikw_ref[...], ck, sak, sbk, IDX_ROT // 2)
    ikw_o[...] = ikw
    ikb_o[...] = ikw[:, :IDX_DIM].astype(BF16)
    krp = _rope(krp_ref[...], cr, sar, sbr, ROPE_DIM // 2)
    krp_o[...] = krp
    krb_o[...] = krp[:, :ROPE_DIM].astype(BF16)
    ckv = _rms(ckv_ref[...], kvg_ref[...])
    ckv_o[...] = ckv
    ckvb_o[...] = ckv.astype(BF16)
    cqn = _rms(cq_ref[...], qg_ref[...]).astype(BF16)
    qb = jnp.dot(cqn, wuq_ref[...], preferred_element_type=F32)
    nope = B_HEADS * NOPE_DIM
    low = lax.broadcasted_iota(jnp.int32, (tm, LANES), 1) < ROPE_DIM
    for v in range(B_HEADS * ROPE_DIM // LANES):
        r = _rope(qb[:, nope + v * LANES: nope + (v + 1) * LANES], cr, sar, sbr, ROPE_DIM // 2)
        halves = (jnp.where(low, r, 0.0), jnp.where(low, pltpu.roll(r, ROPE_DIM, 1), 0.0))
        for j in range(2):
            h = 2 * v + j
            qcat_o[:, h * B_QK_PAD: h * B_QK_PAD + NOPE_DIM] = qb[:, h * NOPE_DIM:(h + 1) * NOPE_DIM].astype(BF16)
            qcat_o[:, h * B_QK_PAD + NOPE_DIM:(h + 1) * B_QK_PAD] = halves[j].astype(BF16)


def _prep(z, dst, bsz, t, tabs, q_norm_g, kv_norm_g, wuq):
    m = bsz * t
    tm = min(t, 256)
    nt = t // tm
    assert t % tm == 0

    def zspec(name):
        o, w = dst[name]
        return pl.BlockSpec((tm, w), lambda b, i, blk=o // w: (b * nt + i, blk))

    def tspec():
        return pl.BlockSpec((3, tm, LANES), lambda b, i: (0, i, 0))

    def ospec(w):
        return pl.BlockSpec((tm, w), lambda b, i: (b * nt + i, 0))

    def full(a):
        return pl.BlockSpec(a.shape, lambda b, i: (0,) * a.ndim)

    outs = [(A_WIDTH, BF16), (IDX_HEADS * IDX_DIM, BF16), (A_KV_WIDTH, F32), (LANES, F32), (LANES, F32),
            (KV_LORA, F32), (B_HEADS * B_QK_PAD, BF16),
            (A_KV_WIDTH, BF16), (A_KV_WIDTH, BF16), (IDX_DIM, BF16), (KV_LORA, BF16), (ROPE_DIM, BF16)]
    return pl.pallas_call(
        _prep_kernel,
        grid=(bsz, nt),
        in_specs=[zspec("aq"), zspec("iq"), zspec("cq"), zspec("ckv"), zspec("ak"), zspec("av"), zspec("ikw"),
                  zspec("krp"), tspec(), tspec(), tspec(), tspec(), full(q_norm_g), full(kv_norm_g), full(wuq)],
        out_specs=[ospec(w) for w, _ in outs],
        out_shape=[jax.ShapeDtypeStruct((m, w), dt) for w, dt in outs],
        compiler_params=pltpu.CompilerParams(dimension_semantics=("arbitrary", "arbitrary"),
                                             vmem_limit_bytes=VMEM_LIMIT),
        name="prep",
    )(z, z, z, z, z, z, z, z, *tabs, q_norm_g, kv_norm_g, wuq)


def _kvup_kernel(c_ref, kr_ref, w_ref, kcat_ref, v_ref):
    r = jnp.dot(c_ref[...], w_ref[...], preferred_element_type=F32)
    n = B_HEADS * NOPE_DIM
    tail = jnp.concatenate([kr_ref[...], jnp.zeros((kr_ref.shape[0], B_QK_PAD - NOPE_DIM - ROPE_DIM), BF16)], axis=1)
    for h in range(B_HEADS):
        kcat_ref[:, h * B_QK_PAD: h * B_QK_PAD + NOPE_DIM] = r[:, h * NOPE_DIM:(h + 1) * NOPE_DIM].astype(BF16)
        kcat_ref[:, h * B_QK_PAD + NOPE_DIM:(h + 1) * B_QK_PAD] = tail
    v_ref[...] = r[:, n:].astype(BF16)


def _kvup(ckv, kr, w):
    m, c = ckv.shape
    tm = next(t for t in (512, 384, 256, 128) if m % t == 0)
    row = lambda i: (i, 0)
    return pl.pallas_call(
        _kvup_kernel,
        grid=(m // tm,),
        in_specs=[pl.BlockSpec((tm, c), row), pl.BlockSpec((tm, ROPE_DIM), row), pl.BlockSpec(w.shape, lambda i: (0, 0))],
        out_specs=[pl.BlockSpec((tm, B_HEADS * B_QK_PAD), row), pl.BlockSpec((tm, B_WIDTH), row)],
        out_shape=[jax.ShapeDtypeStruct((m, B_HEADS * B_QK_PAD), BF16), jax.ShapeDtypeStruct((m, B_WIDTH), BF16)],
        compiler_params=pltpu.CompilerParams(dimension_semantics=("arbitrary",), vmem_limit_bytes=VMEM_LIMIT),
        name="kv_up",
    )(ckv, kr, w)


_NT = (((1,), (1,)), ((), ()))


def _visible_cols(tile, tq, l_valid, qpos0):
    p_last = qpos0 + (tile + 1) * tq - 1
    n = (p_last // CHUNK + 1) * CHUNK
    return min(n, l_valid) if isinstance(n, int) else jnp.minimum(n, l_valid)


def _num_key_blocks(tile, tq, kw, l_valid, qpos0):
    return (_visible_cols(tile, tq, l_valid, qpos0) + kw - 1) // kw


def _allowed(tile, kb, rows0, shape, tq, kw, l_valid, qpos0):
    rows = lax.broadcasted_iota(jnp.int32, shape, 0) + rows0
    cols = lax.broadcasted_iota(jnp.int32, shape, 1) + kb * kw
    qchunk = (qpos0 + tile * tq + rows) // CHUNK
    return ((cols // CHUNK) <= qchunk) & (cols < l_valid), cols


def _softmax_step(s, h, vblk, m_ref, l_ref, acc_ref):
    kw = s.shape[1]
    m_prev = m_ref[h]
    m_new = jnp.maximum(m_prev, jnp.max(s, axis=-1, keepdims=True))
    alpha = jnp.exp(m_prev - m_new)
    p = jnp.exp(s - _lane_tile(m_new, kw))
    l_ref[h] = alpha * l_ref[h] + jnp.sum(p, axis=-1, keepdims=True)
    acc_ref[h] = alpha * acc_ref[h] + jnp.dot(p.astype(BF16), vblk, preferred_element_type=F32)
    m_ref[h] = m_new


def _softmax_init(heads, tq, m_ref, l_ref, acc_ref):
    for h in range(heads):
        m_ref[h] = jnp.full((tq, LANES), NEG, F32)
        l_ref[h] = jnp.zeros((tq, LANES), F32)
        acc_ref[h] = jnp.zeros((tq, acc_ref.shape[2]), F32)


def _masked_score(score, allowed):
    return jnp.where(allowed, score, MASKED)


def _key_to_f32(key):
    return lax.bitcast_convert_type(key ^ ((key >> 31) & 0x7FFFFFFF), F32)


def _selection_bias(score, cols, thr, jcut, allowed):
    sel = ((score > thr) | ((score == thr) & (cols < jcut))) & allowed
    return jnp.where(sel, 0.0, NEG)


def _topk_thresholds(strips, rs, kf, thr_ref, jcut_ref, need_ref):
    def count(pieces, pred):
        acc = jnp.zeros((rs, LANES), F32)
        for load, col0 in pieces:
            acc = acc + jnp.where(pred(load(), col0), 1.0, 0.0)
        return jnp.sum(acc, axis=-1, keepdims=True)

    def bit_step(b, carry):
        bit = jnp.left_shift(jnp.int32(1), 31 - b)
        for rows, pieces in strips:
            t = thr_ref[rows, :]
            cand = t + bit
            cand_f = _key_to_f32(cand)
            cnt = count(pieces, lambda s, c0: s >= cand_f)
            thr_ref[rows, :] = jnp.where(cnt >= kf, cand, t)
        return carry

    thr_ref[...] = jnp.full(thr_ref.shape, INT_MIN, jnp.int32)
    lax.fori_loop(0, 32, bit_step, 0)

    jcut_ref[...] = jnp.full(jcut_ref.shape, 4096, jnp.int32)
    flag = jnp.int32(0)
    for rows, pieces in strips:
        t = _key_to_f32(thr_ref[rows, :])
        need = kf - count(pieces, lambda s, c0: s > t)
        n_eq = count(pieces, lambda s, c0: s == t)
        need_ref[rows, :] = jnp.broadcast_to(need, (rs, LANES))
        excess = (n_eq > need) & (t[:, :1] > MASKED)
        flag = jnp.maximum(flag, jnp.max(jnp.where(excess, 1, 0)))

    @pl.when(flag > 0)
    def _():
        lane = lax.broadcasted_iota(jnp.int32, (rs, LANES), 1)

        def col_step(b, carry):
            bit = jnp.left_shift(jnp.int32(1), 11 - b)
            for rows, pieces in strips:
                t, j = _key_to_f32(thr_ref[rows, :]), jcut_ref[rows, :]
                cand = j + bit
                f = count(pieces, lambda s, c0: (s == t) & (lane + c0 < cand))
                jcut_ref[rows, :] = jnp.where(f <= need_ref[rows, :], cand, j)
            return carry

        jcut_ref[...] = jnp.zeros(jcut_ref.shape, jnp.int32)
        lax.fori_loop(0, 12, col_step, 0)


def _dsa_kernel(aq_ref, iq_ref, ikw_ref, agate_ref, k_ref, v_ref, ik_ref, o_ref,
                key_ref, thr_ref, jcut_ref, need_ref, bias_ref, iwb_ref, qs_ref, acc_ref, m_ref, l_ref,
                *, nt, tq, kw, l_valid, qpos0, topk):
    nkb_max = key_ref.shape[0] // nt
    nkb_of = [_num_key_blocks(ti, tq, kw, l_valid, qpos0) for ti in range(nt)]
    rs = min(tq, 64)
    ns = tq // rs
    nc = kw // LANES
    kf = float(topk)

    def score_tile(ti, carry):
        r0 = pl.multiple_of(ti * tq, tq)
        iw = ikw_ref[pl.ds(r0, tq), IDX_DIM:IDX_DIM + IDX_HEADS] * (IDX_DIM ** -0.5)
        for h in range(IDX_HEADS):
            iwb_ref[h] = jnp.broadcast_to(iw[:, h:h + 1], (tq, LANES))

        def score_block(kb, c2):
            ikb = ik_ref[0, pl.ds(pl.multiple_of(kb * kw, kw), kw), :]
            for r in range(ns):
                sc = [jnp.zeros((rs, LANES), F32) for _ in range(nc)]
                for h in range(IDX_HEADS):
                    q = iq_ref[pl.ds(r0 + r * rs, rs), h * IDX_DIM:(h + 1) * IDX_DIM]
                    rel = lax.dot_general(q, ikb, _NT, preferred_element_type=F32)
                    w = iwb_ref[h, r * rs:(r + 1) * rs, :]
                    for c in range(nc):
                        sc[c] = sc[c] + jnp.maximum(rel[:, c * LANES:(c + 1) * LANES], 0.0) * w
                allowed, _ = _allowed(ti, kb, r * rs, (rs, kw), tq, kw, l_valid, qpos0)
                key_ref[ti * nkb_max + kb, r * rs:(r + 1) * rs, :] = _masked_score(jnp.concatenate(sc, axis=1), allowed)
            return c2

        lax.fori_loop(0, _num_key_blocks(ti, tq, kw, l_valid, qpos0), score_block, 0)
        return carry

    def piece(ti, kb, s, c):
        return lambda: key_ref[ti * nkb_max + kb, s * rs:(s + 1) * rs, c * LANES:(c + 1) * LANES]

    strips = [(slice(ti * tq + s * rs, ti * tq + (s + 1) * rs),
               [(piece(ti, kb, s, c), kb * kw + c * LANES) for kb in range(nkb_of[ti]) for c in range(nc)])
              for ti in range(nt) for s in range(ns)]

    @pl.when(pl.program_id(1) == 0)
    def _():
        lax.fori_loop(0, nt, score_tile, 0)
        _topk_thresholds(strips, rs, kf, thr_ref, jcut_ref, need_ref)

    ti = pl.program_id(1)
    nkb = _num_key_blocks(ti, tq, kw, l_valid, qpos0)
    r0 = pl.multiple_of(ti * tq, tq)
    thr = _lane_tile(_key_to_f32(thr_ref[pl.ds(r0, tq), :]), kw)
    jcut = _lane_tile(jcut_ref[pl.ds(r0, tq), :], kw)

    def bias_block(kb, carry):
        allowed, cols = _allowed(ti, kb, 0, (tq, kw), tq, kw, l_valid, qpos0)
        bias_ref[kb] = _selection_bias(key_ref[ti * nkb_max + kb], cols, thr, jcut, allowed)
        return carry

    lax.fori_loop(0, nkb, bias_block, 0)

    grp = A_HEADS // A_KV_HEADS
    _softmax_init(A_KV_HEADS, grp * tq, m_ref, l_ref, acc_ref)
    scale = A_HEAD_DIM ** -0.5
    for g in range(A_KV_HEADS):
        gsl = slice(g * A_HEAD_DIM, (g + 1) * A_HEAD_DIM)
        for hh in range(grp):
            h = g * grp + hh
            qs_ref[hh * tq:(hh + 1) * tq, :] = aq_ref[:, h * A_HEAD_DIM:(h + 1) * A_HEAD_DIM]

        def att_block(kb, carry):
            c0 = pl.multiple_of(kb * kw, kw)
            kblk = k_ref[0, pl.ds(c0, kw), gsl]
            vblk = v_ref[0, pl.ds(c0, kw), gsl]
            s = lax.dot_general(qs_ref[...], kblk, _NT, preferred_element_type=F32) * scale
            s = (s.reshape(grp, tq, kw) + bias_ref[kb][None]).reshape(grp * tq, kw)
            _softmax_step(s, g, vblk, m_ref, l_ref, acc_ref)
            return carry

        lax.fori_loop(0, nkb, att_block, 0)

    for h in range(A_HEADS):
        g, hh = divmod(h, grp)
        sl = slice(h * A_HEAD_DIM, (h + 1) * A_HEAD_DIM)
        o = acc_ref[g, hh * tq:(hh + 1) * tq, :] / l_ref[g, hh * tq:(hh + 1) * tq, :]
        gate = agate_ref[:, sl]
        o_ref[:, sl] = (o * (gate * jax.nn.sigmoid(gate))).astype(BF16)


def _dsa(aq, iq, ikw, z, dst, k, v, ik, bsz, t, tq, kw, l_valid, qpos0):
    lp = k.shape[1]
    assert lp < 4096 and lp % kw == 0 and t % tq == 0
    nt = t // tq
    nkb_max = lp // kw
    topk = min(TOPK_MAX, l_valid // 4)
    grp = A_HEADS // A_KV_HEADS
    go, gw = dst["agate"]
    row = lambda b, i: (b * nt + i, 0)
    whole = lambda b, i: (b, 0)
    batch = lambda b, i: (b, 0, 0)
    kern = functools.partial(_dsa_kernel, nt=nt, tq=tq, kw=kw, l_valid=l_valid, qpos0=qpos0, topk=topk)
    return pl.pallas_call(
        kern,
        grid=(bsz, nt),
        in_specs=[pl.BlockSpec((tq, A_WIDTH), row),
                  pl.BlockSpec((t, IDX_HEADS * IDX_DIM), whole),
                  pl.BlockSpec((t, LANES), whole),
                  pl.BlockSpec((tq, gw), lambda b, i: (b * nt + i, go // gw)),
                  pl.BlockSpec((1, lp, A_KV_WIDTH), batch),
                  pl.BlockSpec((1, lp, A_KV_WIDTH), batch),
                  pl.BlockSpec((1, lp, IDX_DIM), batch)],
        out_specs=pl.BlockSpec((tq, A_WIDTH), row),
        out_shape=jax.ShapeDtypeStruct((bsz * t, A_WIDTH), BF16),
        scratch_shapes=[pltpu.VMEM((nt * nkb_max, tq, kw), F32),
                        pltpu.VMEM((t, LANES), jnp.int32),
                        pltpu.VMEM((t, LANES), jnp.int32),
                        pltpu.VMEM((t, LANES), F32),
                        pltpu.VMEM((nkb_max, tq, kw), F32),
                        pltpu.VMEM((IDX_HEADS, tq, LANES), F32),
                        pltpu.VMEM((grp * tq, A_HEAD_DIM), BF16),
                        pltpu.VMEM((A_KV_HEADS, grp * tq, A_HEAD_DIM), F32),
                        pltpu.VMEM((A_KV_HEADS, grp * tq, LANES), F32),
                        pltpu.VMEM((A_KV_HEADS, grp * tq, LANES), F32)],
        compiler_params=pltpu.CompilerParams(dimension_semantics=("arbitrary", "arbitrary"),
                                             vmem_limit_bytes=VMEM_LIMIT),
        name="dsa",
    )(aq, iq, ikw, z, k, v, ik)


SEL_BATCHES = 8
SEL_CHUNK = 512


def _dsa_sel_kernel(iq_ref, ikw_ref, cik_ref, nik_ref, bias_ref, key_ref, thr_ref, jcut_ref, need_ref,
                    *, gb, t, past, qpos0, topk):
    lp = past + LANES
    nc = lp // LANES

    def allowed_cols(col0, width):
        rows = lax.broadcasted_iota(jnp.int32, (t, width), 0)
        cols = lax.broadcasted_iota(jnp.int32, (t, width), 1) + col0
        return ((cols // CHUNK) <= (qpos0 + rows) // CHUNK) & (cols < past + t)

    def score_batch(j, carry):
        r0 = pl.multiple_of(j * t, t)
        qst = jnp.concatenate([iq_ref[pl.ds(r0, t), h * IDX_DIM:(h + 1) * IDX_DIM] for h in range(IDX_HEADS)], axis=0)
        iw = ikw_ref[pl.ds(r0, t), IDX_DIM:IDX_DIM + IDX_HEADS] * (IDX_DIM ** -0.5)
        iwb = [jnp.broadcast_to(iw[:, h:h + 1], (t, LANES)) for h in range(IDX_HEADS)]

        def head_sum(rel):
            width = rel.shape[1]
            sc = jnp.zeros((t, width), F32)
            for h in range(IDX_HEADS):
                sc = sc + jnp.maximum(rel[h * t:(h + 1) * t], 0.0) * _lane_tile(iwb[h], width)
            return sc

        for c in range(past // SEL_CHUNK):
            ik = cik_ref[j, c * SEL_CHUNK:(c + 1) * SEL_CHUNK, :].astype(BF16)
            sc = head_sum(lax.dot_general(qst, ik, _NT, preferred_element_type=F32))
            key_ref[j, :, c * SEL_CHUNK:(c + 1) * SEL_CHUNK] = _masked_score(sc, allowed_cols(c * SEL_CHUNK, SEL_CHUNK))
        ik = jnp.concatenate([nik_ref[pl.ds(r0, t), :], jnp.zeros((LANES - t, IDX_DIM), BF16)], axis=0)
        sc = head_sum(lax.dot_general(qst, ik, _NT, preferred_element_type=F32))
        key_ref[j, :, past:] = _masked_score(sc, allowed_cols(past, LANES))
        return carry

    lax.fori_loop(0, gb, score_batch, 0)

    def piece(j, c):
        return lambda: key_ref[j, :, c * LANES:(c + 1) * LANES]

    strips = [(slice(j * t, (j + 1) * t), [(piece(j, c), c * LANES) for c in range(nc)]) for j in range(gb)]
    _topk_thresholds(strips, t, float(topk), thr_ref, jcut_ref, need_ref)

    def bias_batch(j, carry):
        r0 = pl.multiple_of(j * t, t)
        cols = lax.broadcasted_iota(jnp.int32, (t, lp), 1)
        bias_ref[j] = _selection_bias(key_ref[j], cols, _lane_tile(_key_to_f32(thr_ref[pl.ds(r0, t), :]), lp),
                                      _lane_tile(jcut_ref[pl.ds(r0, t), :], lp), allowed_cols(0, lp))
        return carry

    lax.fori_loop(0, gb, bias_batch, 0)


def _dsa_sel(iq, ikw, cache_ik, new_ik, bsz, t, qpos0):
    past = cache_ik.shape[1]
    lp = past + LANES
    gb = next(g for g in (SEL_BATCHES, 4, 2, 1) if bsz % g == 0)
    assert past % SEL_CHUNK == 0 and t <= LANES and lp < 4096
    topk = min(TOPK_MAX, (past + t) // 4)
    rows = lambda i: (i, 0)
    kern = functools.partial(_dsa_sel_kernel, gb=gb, t=t, past=past, qpos0=qpos0, topk=topk)
    return pl.pallas_call(
        kern,
        grid=(bsz // gb,),
        in_specs=[pl.BlockSpec((gb * t, IDX_HEADS * IDX_DIM), rows),
                  pl.BlockSpec((gb * t, LANES), rows),
                  pl.BlockSpec((gb, past, IDX_DIM), lambda i: (i, 0, 0)),
                  pl.BlockSpec((gb * t, IDX_DIM), rows)],
        out_specs=pl.BlockSpec((gb, t, lp), lambda i: (i, 0, 0)),
        out_shape=jax.ShapeDtypeStruct((bsz, t, lp), F32),
        scratch_shapes=[pltpu.VMEM((gb, t, lp), F32),
                        pltpu.VMEM((gb * t, LANES), jnp.int32),
                        pltpu.VMEM((gb * t, LANES), jnp.int32),
                        pltpu.VMEM((gb * t, LANES), F32)],
        compiler_params=pltpu.CompilerParams(dimension_semantics=("arbitrary",), vmem_limit_bytes=VMEM_LIMIT),
        name="dsa_sel",
    )(iq, ikw, cache_ik, new_ik)


def _dsa_att_kernel(aq_ref, agate_ref, bias_ref, ck_ref, cv_ref, nk_ref, nv_ref, o_ref, *, t, past):
    grp = A_HEADS // A_KV_HEADS
    scale = A_HEAD_DIM ** -0.5
    pad = jnp.zeros((LANES - t, A_HEAD_DIM), BF16)
    b1, b2 = bias_ref[0, :, :past], bias_ref[0, :, past:]
    for g in range(A_KV_HEADS):
        gsl = slice(g * A_HEAD_DIM, (g + 1) * A_HEAD_DIM)
        kc = ck_ref[0, pl.ds(g, past, stride=A_KV_HEADS), :].astype(BF16)
        vc = cv_ref[0, pl.ds(g, past, stride=A_KV_HEADS), :].astype(BF16)
        kn = jnp.concatenate([nk_ref[:, gsl], pad], axis=0)
        vn = jnp.concatenate([nv_ref[:, gsl], pad], axis=0)
        qs = jnp.concatenate([aq_ref[:, (g * grp + hh) * A_HEAD_DIM:(g * grp + hh + 1) * A_HEAD_DIM]
                              for hh in range(grp)], axis=0)

        def scores(keys, bias):
            s = lax.dot_general(qs, keys, _NT, preferred_element_type=F32) * scale
            return (s.reshape(grp, t, s.shape[1]) + bias[None]).reshape(grp * t, s.shape[1])

        s1, s2 = scores(kc, b1), scores(kn, b2)
        m = jnp.maximum(jnp.max(s1, axis=-1, keepdims=True), jnp.max(s2, axis=-1, keepdims=True))
        e1, e2 = jnp.exp(s1 - m), jnp.exp(s2 - m)
        l = jnp.sum(e1, axis=-1, keepdims=True) + jnp.sum(e2, axis=-1, keepdims=True)
        o = (jnp.dot(e1.astype(BF16), vc, preferred_element_type=F32)
             + jnp.dot(e2.astype(BF16), vn, preferred_element_type=F32)) / l
        for hh in range(grp):
            sl = slice((g * grp + hh) * A_HEAD_DIM, (g * grp + hh + 1) * A_HEAD_DIM)
            gate = agate_ref[:, sl]
            o_ref[:, sl] = (o[hh * t:(hh + 1) * t] * (gate * jax.nn.sigmoid(gate))).astype(BF16)


def _dsa_att(aq, z, dst, bias, cache_k, cache_v, new_k, new_v, bsz, t):
    past = cache_k.shape[1] // A_KV_HEADS
    lp = bias.shape[2]
    go, gw = dst["agate"]
    row = lambda b: (b, 0)
    batch = lambda b: (b, 0, 0)
    return pl.pallas_call(
        functools.partial(_dsa_att_kernel, t=t, past=past),
        grid=(bsz,),
        in_specs=[pl.BlockSpec((t, A_WIDTH), row),
                  pl.BlockSpec((t, gw), lambda b: (b, go // gw)),
                  pl.BlockSpec((1, t, lp), batch),
                  pl.BlockSpec((1, past * A_KV_HEADS, A_HEAD_DIM), batch),
                  pl.BlockSpec((1, past * A_KV_HEADS, A_HEAD_DIM), batch),
                  pl.BlockSpec((t, A_KV_WIDTH), row),
                  pl.BlockSpec((t, A_KV_WIDTH), row)],
        out_specs=pl.BlockSpec((t, A_WIDTH), row),
        out_shape=jax.ShapeDtypeStruct((bsz * t, A_WIDTH), BF16),
        compiler_params=pltpu.CompilerParams(dimension_semantics=("arbitrary",), vmem_limit_bytes=VMEM_LIMIT),
        name="dsa_att",
    )(aq, z, bias, cache_k, cache_v, new_k, new_v)


def _mla_kernel(q_ref, bgate_ref, k_ref, v_ref, o_ref, acc_ref, m_ref, l_ref, *, tq, kw, l_valid, qpos0):
    ti = pl.program_id(1)
    nkb = _num_key_blocks(ti, tq, kw, l_valid, qpos0)
    scale = (NOPE_DIM + ROPE_DIM) ** -0.5
    _softmax_init(B_HEADS, tq, m_ref, l_ref, acc_ref)

    def att_block(kb, carry):
        r0 = pl.multiple_of(kb * kw, kw)
        allowed, _ = _allowed(ti, kb, 0, (tq, kw), tq, kw, l_valid, qpos0)
        bias = jnp.where(allowed, 0.0, NEG)
        for h in range(B_HEADS):
            qsl = slice(h * B_QK_PAD, (h + 1) * B_QK_PAD)
            s = lax.dot_general(q_ref[:, qsl], k_ref[0, pl.ds(r0, kw), qsl], _NT, preferred_element_type=F32)
            s = s * scale + bias
            _softmax_step(s, h, v_ref[0, pl.ds(r0, kw), h * V_DIM:(h + 1) * V_DIM], m_ref, l_ref, acc_ref)
        return carry

    lax.fori_loop(0, nkb, att_block, 0)

    for h in range(B_HEADS):
        sl = slice(h * V_DIM, (h + 1) * V_DIM)
        o = acc_ref[h] / l_ref[h]
        gate = bgate_ref[:, sl]
        o_ref[:, sl] = (o * (gate * jax.nn.sigmoid(gate))).astype(BF16)


def _mla(qcat, z, dst, kcat, v, bsz, t, tq, kw, l_valid, qpos0):
    lp = kcat.shape[1]
    assert lp % kw == 0 and t % tq == 0
    nt = t // tq
    go, gw = dst["bgate"]
    row = lambda b, i: (b * nt + i, 0)
    batch = lambda b, i: (b, 0, 0)
    kern = functools.partial(_mla_kernel, tq=tq, kw=kw, l_valid=l_valid, qpos0=qpos0)
    return pl.pallas_call(
        kern,
        grid=(bsz, nt),
        in_specs=[pl.BlockSpec((tq, B_HEADS * B_QK_PAD), row),
                  pl.BlockSpec((tq, gw), lambda b, i: (b * nt + i, go // gw)),
                  pl.BlockSpec((1, lp, B_HEADS * B_QK_PAD), batch),
                  pl.BlockSpec((1, lp, B_WIDTH), batch)],
        out_specs=pl.BlockSpec((tq, B_WIDTH), row),
        out_shape=jax.ShapeDtypeStruct((bsz * t, B_WIDTH), BF16),
        scratch_shapes=[pltpu.VMEM((B_HEADS, tq, V_DIM), F32),
                        pltpu.VMEM((B_HEADS, tq, LANES), F32),
                        pltpu.VMEM((B_HEADS, tq, LANES), F32)],
        compiler_params=pltpu.CompilerParams(dimension_semantics=("arbitrary", "arbitrary"),
                                             vmem_limit_bytes=VMEM_LIMIT),
        name="mla",
    )(qcat, z, kcat, v)


def _mla_lat_kernel(q_ref, bgate_ref, cc_ref, ckr_ref, nc_ref, nkr_ref, wukt_ref, wuv_ref, o_ref,
                    *, t, past, qpos0):
    rows_n = B_HEADS * t
    ql, qr = [], []
    for h in range(B_HEADS):
        qn = q_ref[:, h * B_QK_PAD: h * B_QK_PAD + NOPE_DIM]
        ql.append(jnp.dot(qn, wukt_ref[h], preferred_element_type=F32).astype(BF16))
        qr.append(q_ref[:, h * B_QK_PAD + NOPE_DIM: h * B_QK_PAD + NOPE_DIM + ROPE_DIM])
    ql = jnp.concatenate(ql, axis=0)
    qr = jnp.concatenate(qr, axis=0)
    cc = cc_ref[0].astype(BF16)
    ckr = ckr_ref[0].astype(BF16)
    nc, nkr = nc_ref[...], nkr_ref[...]
    scale = (NOPE_DIM + ROPE_DIM) ** -0.5

    def scores(lat, rope, ncols, col0):
        s = lax.dot_general(ql, lat, _NT, preferred_element_type=F32)
        s = (s + lax.dot_general(qr, rope, _NT, preferred_element_type=F32)) * scale
        rows = lax.broadcasted_iota(jnp.int32, (rows_n, ncols), 0)
        cols = lax.broadcasted_iota(jnp.int32, (rows_n, ncols), 1) + col0
        qchunk = (qpos0 + lax.rem(rows, t)) // CHUNK
        return jnp.where((cols // CHUNK) <= qchunk, s, NEG)

    s1 = scores(cc, ckr, past, 0)
    s2 = scores(nc, nkr, t, past)
    m = jnp.maximum(jnp.max(s1, axis=-1, keepdims=True), jnp.max(s2, axis=-1, keepdims=True))
    e1, e2 = jnp.exp(s1 - m), jnp.exp(s2 - m)
    l = jnp.sum(e1, axis=-1, keepdims=True) + jnp.sum(e2, axis=-1, keepdims=True)
    ol = (jnp.dot(e1.astype(BF16), cc, preferred_element_type=F32)
          + jnp.dot(e2.astype(BF16), nc, preferred_element_type=F32)) / l
    ol = ol.astype(BF16)
    for h in range(B_HEADS):
        sl = slice(h * V_DIM, (h + 1) * V_DIM)
        o = jnp.dot(ol[h * t:(h + 1) * t], wuv_ref[:, sl], preferred_element_type=F32)
        gate = bgate_ref[:, sl]
        o_ref[:, sl] = (o * (gate * jax.nn.sigmoid(gate))).astype(BF16)


def _mla_lat(qcat, z, dst, cache_ckv, cache_kr, new_ckv, new_kr, wukt, wuv, bsz, t, qpos0):
    past = cache_ckv.shape[1]
    go, gw = dst["bgate"]
    row = lambda b: (b, 0)
    batch = lambda b: (b, 0, 0)
    const = lambda a: pl.BlockSpec(a.shape, lambda b: (0,) * a.ndim)
    kern = functools.partial(_mla_lat_kernel, t=t, past=past, qpos0=qpos0)
    return pl.pallas_call(
        kern,
        grid=(bsz,),
        in_specs=[pl.BlockSpec((t, B_HEADS * B_QK_PAD), row),
                  pl.BlockSpec((t, gw), lambda b: (b, go // gw)),
                  pl.BlockSpec((1, past, KV_LORA), batch),
                  pl.BlockSpec((1, past, ROPE_DIM), batch),
                  pl.BlockSpec((t, KV_LORA), row),
                  pl.BlockSpec((t, ROPE_DIM), row),
                  const(wukt), const(wuv)],
        out_specs=pl.BlockSpec((t, B_WIDTH), row),
        out_shape=jax.ShapeDtypeStruct((bsz * t, B_WIDTH), BF16),
        compiler_params=pltpu.CompilerParams(dimension_semantics=("arbitrary",), vmem_limit_bytes=VMEM_LIMIT),
        name="mla_lat",
    )(qcat, z, cache_ckv, cache_kr, new_ckv, new_kr, wukt, wuv)


def _out_kernel(ga_ref, gb_ref, mg_ref, x_ref, woa_ref, wob_ref, wout_ref, fg_ref, o_ref, *, d, final):
    ya = jnp.dot(ga_ref[...], woa_ref[...], preferred_element_type=F32)
    yb = jnp.dot(gb_ref[...], wob_ref[...], preferred_element_type=F32)
    m = jax.nn.sigmoid(mg_ref[:, :d]) * ya + jax.nn.sigmoid(mg_ref[:, d:]) * yb
    h = x_ref[...] + jnp.dot(m.astype(BF16), wout_ref[...], preferred_element_type=F32)
    o_ref[...] = _rms(h, fg_ref[...]) if final else h


def _out(ga, gb, z, dst, x, woa, wob, wout, fg, final):
    m, d = x.shape
    tm = min(m, 256)
    assert m % tm == 0 and dst["mg"][0] == 0
    row = lambda i: (i, 0)
    const = lambda a: pl.BlockSpec(a.shape, lambda i: (0, 0), pipeline_mode=pl.Buffered(1))
    return pl.pallas_call(
        functools.partial(_out_kernel, d=d, final=final),
        grid=(m // tm,),
        in_specs=[pl.BlockSpec((tm, A_WIDTH), row), pl.BlockSpec((tm, B_WIDTH), row),
                  pl.BlockSpec((tm, 2 * d), row), pl.BlockSpec((tm, d), row),
                  const(woa), const(wob), const(wout), const(fg)],
        out_specs=pl.BlockSpec((tm, d), row),
        out_shape=jax.ShapeDtypeStruct((m, d), F32),
        compiler_params=pltpu.CompilerParams(dimension_semantics=("arbitrary",), vmem_limit_bytes=VMEM_LIMIT),
        name="out",
    )(ga, gb, z, x, woa, wob, wout, fg)


def _all_keys(cache, new, bsz, t, lp):
    new = new.reshape(bsz, t, new.shape[-1])
    if cache is None and lp == t:
        return new
    parts = ([] if cache is None else [cache.reshape(bsz, cache.shape[1], -1).astype(BF16)]) + [new]
    kk = jnp.concatenate(parts, axis=1)
    return jnp.pad(kk, ((0, 0), (0, lp - kk.shape[1]), (0, 0)))


def _trunk(x, caches, final_g, weights):
    bsz, t, d = x.shape
    depth = weights["w_in"].shape[0]
    past = 0 if caches is None else caches[0].shape[2]
    l_valid = past + t
    tq = min(t, QUERY_TILE)
    kw = KEY_BLOCK if t >= QUERY_TILE else -(-l_valid // LANES) * LANES
    lp = -(-l_valid // kw) * kw
    _, dst, _ = _layout(d)
    pos = past + jnp.arange(t, dtype=jnp.int32)
    t_scale = jnp.concatenate([jnp.full((t, IDX_HEADS), IDX_HEADS ** -0.5, F32),
                               jnp.ones((t, LANES - IDX_DIM - IDX_HEADS), F32)], axis=1)
    tabs = [jnp.stack(_rope_tables(pos, A_ROT, A_HEAD_DIM, 1)),
            jnp.stack(_rope_tables(pos, IDX_ROT, IDX_DIM, 2)),
            jnp.stack(_rope_tables(pos, IDX_ROT, IDX_DIM, 1, extra=t_scale)),
            jnp.stack(_rope_tables(pos, ROPE_DIM, ROPE_DIM, 2))]
    h = x.reshape(bsz * t, d)
    new_rows = []
    for l in range(depth):
        w = {k: v[l] for k, v in weights.items()}
        z = _proj(h, w["norm_g"], w["w_in"])
        aq, iq, ak, ikw, krp, ckv, qcat, kb, vb, ikb, ckvb, krb = _prep(
            z, dst, bsz, t, tabs, w["q_norm_g"], w["kv_norm_g"], w["w_uq"])
        o, wd = dst["av"]
        av = z[:, o:o + wd]
        c = (None,) * 5 if caches is None else tuple(cc[l] for cc in caches)
        if caches is not None and t < QUERY_TILE:
            bias = _dsa_sel(iq, ikw, c[2], ikb, bsz, t, past)
            interleaved = lambda a: a.reshape(bsz, past * A_KV_HEADS, A_HEAD_DIM)
            ga = _dsa_att(aq, z, dst, bias, interleaved(c[0]), interleaved(c[1]), kb, vb, bsz, t)
            gb = _mla_lat(qcat, z, dst, c[3], c[4], ckvb, krb, w["w_ukt"], w["w_uv"], bsz, t, past)
        else:
            k_all = _all_keys(c[0], kb, bsz, t, lp)
            v_all = _all_keys(c[1], vb, bsz, t, lp)
            ik_all = _all_keys(c[2], ikb, bsz, t, lp)
            ga = _dsa(aq, iq, ikw, z, dst, k_all, v_all, ik_all, bsz, t, tq, kw, l_valid, past)
            ckv_all = _all_keys(c[3], ckvb, bsz, t, lp)
            kr_all = _all_keys(c[4], krb, bsz, t, lp)
            kcat, vv = _kvup(ckv_all.reshape(bsz * lp, KV_LORA), kr_all.reshape(bsz * lp, ROPE_DIM), w["w_ukv"])
            gb = _mla(qcat, z, dst, kcat.reshape(bsz, lp, -1), vv.reshape(bsz, lp, -1), bsz, t, tq, kw, l_valid, past)
        h = _out(ga, gb, z, dst, h, w["w_o_a"], w["w_o_b"], w["w_out"], final_g, final=(l == depth - 1))
        new_rows.append((ak.reshape(bsz, t, A_KV_HEADS, A_HEAD_DIM), av.reshape(bsz, t, A_KV_HEADS, A_HEAD_DIM),
                         ikw[:, :IDX_DIM].reshape(bsz, t, IDX_DIM), ckv.reshape(bsz, t, KV_LORA),
                         krp[:, :ROPE_DIM].reshape(bsz, t, ROPE_DIM)))
    stacked = [jnp.stack([r[i] for r in new_rows], axis=0) for i in range(5)]
    return h.reshape(bsz, t, d), stacked


def kernel(x_prompt, x_sample, cache_a_k, cache_a_v, cache_idx_k, cache_mla_ckv, cache_mla_krope,
           norm_g, w_in, w_uq, q_norm_g, kv_norm_g, w_uk, w_uv, w_o_a, w_o_b, w_out, final_g):
    depth, d, _ = w_in.shape
    weights = {
        "norm_g": norm_g.reshape(depth, 1, d),
        "w_in": jnp.stack([_regroup_w_in(w_in[l], d) for l in range(depth)]),
        "w_uq": jnp.concatenate([w_uq[..., :NOPE_DIM].reshape(depth, Q_LORA, B_HEADS * NOPE_DIM),
                                 w_uq[..., NOPE_DIM:].reshape(depth, Q_LORA, B_HEADS * ROPE_DIM)], axis=-1).astype(BF16),
        "q_norm_g": q_norm_g.reshape(depth, 1, Q_LORA),
        "kv_norm_g": kv_norm_g.reshape(depth, 1, KV_LORA),
        "w_ukv": jnp.concatenate([w_uk.reshape(depth, KV_LORA, B_HEADS * NOPE_DIM),
                                  w_uv.reshape(depth, KV_LORA, B_WIDTH)], axis=-1).astype(BF16),
        "w_ukt": jnp.transpose(w_uk, (0, 2, 3, 1)).astype(BF16),
        "w_uv": w_uv.reshape(depth, KV_LORA, B_WIDTH).astype(BF16),
        "w_o_a": w_o_a.astype(BF16), "w_o_b": w_o_b.astype(BF16), "w_out": w_out.astype(BF16),
    }
    fg = final_g.reshape(1, d)
    y_p, (pk, pv, pik, pckv, pkr) = _trunk(x_prompt, None, fg, weights)
    caches = (cache_a_k, cache_a_v, cache_idx_k, cache_mla_ckv, cache_mla_krope)
    y_s, (sk, sv, sik, sckv, skr) = _trunk(x_sample, caches, fg, weights)
    return (y_p, y_s, pk, pv, pik, pckv, pkr, sk, sv, sik, sckv, skr)
```

```python
import functools

import jax
import jax.numpy as jnp
from jax import lax
from jax.experimental import pallas as pl
from jax.experimental.pallas import tpu as pltpu

F32 = jnp.float32
BF16 = jnp.bfloat16

CHUNK = 64
ROPE_THETA = 500000.0
EPS = 1e-6

A_HEADS = 8
A_KV_HEADS = 2
A_HEAD_DIM = 128
A_ROT = A_HEAD_DIM // 4
A_WIDTH = A_HEADS * A_HEAD_DIM
A_KV_WIDTH = A_KV_HEADS * A_HEAD_DIM
IDX_HEADS = 16
IDX_DIM = 64
IDX_ROT = IDX_DIM // 4
TOPK_MAX = 256

B_HEADS = 8
Q_LORA = 512
KV_LORA = 512
NOPE_DIM = 128
ROPE_DIM = 64
V_DIM = 128
B_WIDTH = B_HEADS * V_DIM
B_QK_PAD = 256

LANES = 128
KEY_BLOCK = 256
MLA_KEY_BLOCK = 512
QUERY_TILE = 256
ROW_STRIP = 64
INT_MIN = -(2 ** 31)
LOG2_E = 1.4426950408889634
NEG = -1e30
MASKED = -1e38
VMEM_LIMIT = 56 * 1024 * 1024

_SEGS = (("aq", A_WIDTH), ("ak", A_KV_WIDTH), ("av", A_KV_WIDTH), ("agate", A_WIDTH),
         ("iq", IDX_HEADS * IDX_DIM), ("ik", IDX_DIM), ("iw", IDX_HEADS),
         ("cq", Q_LORA), ("ckv", KV_LORA), ("kr", ROPE_DIM), ("bgate", B_WIDTH), ("mg", None))


def _layout(d_model):
    src, off = {}, 0
    for name, w in _SEGS:
        w = 2 * d_model if w is None else w
        src[name] = (off, w)
        off += w
    order = (("mg", 2 * d_model), ("aq", A_WIDTH), ("agate", A_WIDTH), ("iq", IDX_HEADS * IDX_DIM),
             ("bgate", B_WIDTH), ("cq", Q_LORA), ("ckv", KV_LORA), ("ak", A_KV_WIDTH), ("av", A_KV_WIDTH),
             ("ikw", LANES), ("krp", LANES))
    dst, off = {}, 0
    for name, w in order:
        assert off % w == 0
        dst[name] = (off, w)
        off += w
    return src, dst, off


def _regroup_kernel(w_ref, o_ref, *, src, dst):
    rows = w_ref.shape[0]

    def piece(name):
        o, w = src[name]
        return w_ref[:, o:o + w].astype(BF16)

    def zeros(n):
        return jnp.zeros((rows, n), BF16)

    for name, (o, w) in dst.items():
        if name == "ikw":
            val = jnp.concatenate([piece("ik"), piece("iw"), zeros(LANES - IDX_DIM - IDX_HEADS)], axis=1)
        elif name == "krp":
            val = jnp.concatenate([piece("kr"), zeros(LANES - ROPE_DIM)], axis=1)
        else:
            val = piece(name)
        o_ref[:, o:o + w] = val


def _regroup_w_in(w_in, d_model):
    src, dst, total = _layout(d_model)
    tr = 256
    assert w_in.shape[0] % tr == 0
    return pl.pallas_call(
        functools.partial(_regroup_kernel, src=src, dst=dst),
        grid=(w_in.shape[0] // tr,),
        in_specs=[pl.BlockSpec((tr, w_in.shape[1]), lambda i: (i, 0))],
        out_specs=pl.BlockSpec((tr, total), lambda i: (i, 0)),
        out_shape=jax.ShapeDtypeStruct((w_in.shape[0], total), BF16),
        compiler_params=pltpu.CompilerParams(dimension_semantics=("arbitrary",), vmem_limit_bytes=VMEM_LIMIT),
        name="regroup",
    )(w_in)


def _rope_tables(pos, rot, head, heads_per_vreg, extra=None):
    half = rot // 2
    inv_freq = ROPE_THETA ** (-jnp.arange(half, dtype=F32) * (2.0 / rot))
    ang = pos.astype(F32)[:, None] * inv_freq[None, :]
    cos, sin = jnp.cos(ang), jnp.sin(ang)
    t = pos.shape[0]
    one = jnp.ones((t, head - rot), F32)
    zero = lambda n: jnp.zeros((t, n), F32)
    c = jnp.concatenate([cos, cos, one], axis=1)
    sa = jnp.concatenate([-sin, zero(head - half)], axis=1)
    sb = jnp.concatenate([zero(half), sin, zero(head - rot)], axis=1)
    c, sa, sb = (jnp.tile(a, (1, heads_per_vreg)) for a in (c, sa, sb))
    if extra is not None:
        c = jnp.concatenate([c, extra], axis=1)
        sa = jnp.concatenate([sa, jnp.zeros_like(extra)], axis=1)
        sb = jnp.concatenate([sb, jnp.zeros_like(extra)], axis=1)
    assert c.shape[1] == LANES
    return c, sa, sb


def _rope(x, c, sa, sb, half):
    up = pltpu.roll(x, LANES - half, 1)
    dn = pltpu.roll(x, half, 1)
    return x * c + up * sa + dn * sb


def _rms(x, g):
    ms = jnp.mean(x * x, axis=-1, keepdims=True)
    return (x * lax.rsqrt(ms + EPS)) * g


def _lane_tile(x, width):
    return x if width == LANES else jnp.concatenate([x] * (width // LANES), axis=1)


def _proj_kernel(x_ref, g_ref, w_ref, z_ref, xn_ref):
    @pl.when(pl.program_id(1) == 0)
    def _():
        xn_ref[...] = _rms(x_ref[...], g_ref[...]).astype(BF16)
    z_ref[...] = jnp.dot(xn_ref[...], w_ref[...], preferred_element_type=F32)


def _proj(x, g, w):
    m, d = x.shape
    n = w.shape[1]
    tm = min(m, 1024)
    tn = 768
    assert m % tm == 0 and n % tn == 0
    return pl.pallas_call(
        _proj_kernel,
        grid=(m // tm, n // tn),
        in_specs=[pl.BlockSpec((tm, d), lambda i, j: (i, 0)),
                  pl.BlockSpec((1, d), lambda i, j: (0, 0)),
                  pl.BlockSpec((d, tn), lambda i, j: (0, j))],
        out_specs=pl.BlockSpec((tm, tn), lambda i, j: (i, j)),
        out_shape=jax.ShapeDtypeStruct((m, n), F32),
        scratch_shapes=[pltpu.VMEM((tm, d), BF16)],
        compiler_params=pltpu.CompilerParams(dimension_semantics=("arbitrary", "arbitrary"),
                                             vmem_limit_bytes=VMEM_LIMIT),
        name="proj",
    )(x, g, w)


def _prep_kernel(aq_ref, iq_ref, cq_ref, ckv_ref, ak_ref, av_ref, ikw_ref, krp_ref,
                 ta_ref, ti_ref, tk_ref, tr_ref, qg_ref, kvg_ref, wuq_ref,
                 aq_o, iq_o, ak_o, ikw_o, krp_o, ckv_o, qcat_o, kb_o, vb_o, ikb_o, ckvb_o, krb_o):
    ca, saa, sba = ta_ref[0], ta_ref[1], ta_ref[2]
    ci, sai, sbi = ti_ref[0], ti_ref[1], ti_ref[2]
    ck, sak, sbk = tk_ref[0], tk_ref[1], tk_ref[2]
    cr, sar, sbr = tr_ref[0], tr_ref[1], tr_ref[2]
    tm = aq_ref.shape[0]
    for h in range(A_HEADS):
        sl = slice(h * LANES, (h + 1) * LANES)
        aq_o[:, sl] = _rope(aq_ref[:, sl], ca, saa, sba, A_ROT // 2).astype(BF16)
    for h in range(A_KV_HEADS):
        sl = slice(h * LANES, (h + 1) * LANES)
        ak = _rope(ak_ref[:, sl], ca, saa, sba, A_ROT // 2)
        ak_o[:, sl] = ak
        kb_o[:, sl] = ak.astype(BF16)
    vb_o[...] = av_ref[...].astype(BF16)
    for v in range(IDX_HEADS * IDX_DIM // LANES):
        sl = slice(v * LANES, (v + 1) * LANES)
        iq_o[:, sl] = _rope(iq_ref[:, sl], ci, sai, sbi, IDX_ROT // 2).astype(BF16)
    ikw = _rope(ikw_ref[...], ck, sak, sbk, IDX_ROT // 2)
    ikw_o[...] = ikw
    ikb_o[...] = ikw[:, :IDX_DIM].astype(BF16)
    krp = _rope(krp_ref[...], cr, sar, sbr, ROPE_DIM // 2)
    krp_o[...] = krp
    krb_o[...] = krp[:, :ROPE_DIM].astype(BF16)
    ckv = _rms(ckv_ref[...], kvg_ref[...])
    ckv_o[...] = ckv
    ckvb_o[...] = ckv.astype(BF16)
    cqn = _rms(cq_ref[...], qg_ref[...]).astype(BF16)
    qb = jnp.dot(cqn, wuq_ref[...], preferred_element_type=F32)
    nope = B_HEADS * NOPE_DIM
    low = lax.broadcasted_iota(jnp.int32, (tm, LANES), 1) < ROPE_DIM
    for v in range(B_HEADS * ROPE_DIM // LANES):
        r = _rope(qb[:, nope + v * LANES: nope + (v + 1) * LANES], cr, sar, sbr, ROPE_DIM // 2)
        halves = (jnp.where(low, r, 0.0), jnp.where(low, pltpu.roll(r, ROPE_DIM, 1), 0.0))
        for j in range(2):
            h = 2 * v + j
            qcat_o[:, h * B_QK_PAD: h * B_QK_PAD + NOPE_DIM] = qb[:, h * NOPE_DIM:(h + 1) * NOPE_DIM].astype(BF16)
            qcat_o[:, h * B_QK_PAD + NOPE_DIM:(h + 1) * B_QK_PAD] = halves[j].astype(BF16)


def _prep(z, dst, bsz, t, tabs, q_norm_g, kv_norm_g, wuq):
    m = bsz * t
    tm = min(t, 256)
    nt = t // tm
    assert t % tm == 0

    def zspec(name):
        o, w = dst[name]
        return pl.BlockSpec((tm, w), lambda b, i, blk=o // w: (b * nt + i, blk))

    def tspec():
        return pl.BlockSpec((3, tm, LANES), lambda b, i: (0, i, 0))

    def ospec(w):
        return pl.BlockSpec((tm, w), lambda b, i: (b * nt + i, 0))

    def full(a):
        return pl.BlockSpec(a.shape, lambda b, i: (0,) * a.ndim)

    outs = [(A_WIDTH, BF16), (IDX_HEADS * IDX_DIM, BF16), (A_KV_WIDTH, F32), (LANES, F32), (LANES, F32),
            (KV_LORA, F32), (B_HEADS * B_QK_PAD, BF16),
            (A_KV_WIDTH, BF16), (A_KV_WIDTH, BF16), (IDX_DIM, BF16), (KV_LORA, BF16), (ROPE_DIM, BF16)]
    return pl.pallas_call(
        _prep_kernel,
        grid=(bsz, nt),
        in_specs=[zspec("aq"), zspec("iq"), zspec("cq"), zspec("ckv"), zspec("ak"), zspec("av"), zspec("ikw"),
                  zspec("krp"), tspec(), tspec(), tspec(), tspec(), full(q_norm_g), full(kv_norm_g), full(wuq)],
        out_specs=[ospec(w) for w, _ in outs],
        out_shape=[jax.ShapeDtypeStruct((m, w), dt) for w, dt in outs],
        compiler_params=pltpu.CompilerParams(dimension_semantics=("arbitrary", "arbitrary"),
                                             vmem_limit_bytes=VMEM_LIMIT),
        name="prep",
    )(z, z, z, z, z, z, z, z, *tabs, q_norm_g, kv_norm_g, wuq)


def _kvup_kernel(c_ref, kr_ref, w_ref, kcat_ref, v_ref):
    r = jnp.dot(c_ref[...], w_ref[...], preferred_element_type=F32)
    n = B_HEADS * NOPE_DIM
    tail = jnp.concatenate([kr_ref[...], jnp.zeros((kr_ref.shape[0], B_QK_PAD - NOPE_DIM - ROPE_DIM), BF16)], axis=1)
    for h in range(B_HEADS):
        kcat_ref[:, h * B_QK_PAD: h * B_QK_PAD + NOPE_DIM] = r[:, h * NOPE_DIM:(h + 1) * NOPE_DIM].astype(BF16)
        kcat_ref[:, h * B_QK_PAD + NOPE_DIM:(h + 1) * B_QK_PAD] = tail
    v_ref[...] = r[:, n:].astype(BF16)


def _kvup(ckv, kr, w):
    m, c = ckv.shape
    tm = next(t for t in (512, 384, 256, 128) if m % t == 0)
    row = lambda i: (i, 0)
    return pl.pallas_call(
        _kvup_kernel,
        grid=(m // tm,),
        in_specs=[pl.BlockSpec((tm, c), row), pl.BlockSpec((tm, ROPE_DIM), row), pl.BlockSpec(w.shape, lambda i: (0, 0))],
        out_specs=[pl.BlockSpec((tm, B_HEADS * B_QK_PAD), row), pl.BlockSpec((tm, B_WIDTH), row)],
        out_shape=[jax.ShapeDtypeStruct((m, B_HEADS * B_QK_PAD), BF16), jax.ShapeDtypeStruct((m, B_WIDTH), BF16)],
        compiler_params=pltpu.CompilerParams(dimension_semantics=("arbitrary",), vmem_limit_bytes=VMEM_LIMIT),
        name="kv_up",
    )(ckv, kr, w)


_NT = (((1,), (1,)), ((), ()))


def _visible_cols(tile, tq, l_valid, qpos0):
    p_last = qpos0 + (tile + 1) * tq - 1
    n = (p_last // CHUNK + 1) * CHUNK
    return min(n, l_valid) if isinstance(n, int) else jnp.minimum(n, l_valid)


def _num_key_blocks(tile, tq, kw, l_valid, qpos0):
    return (_visible_cols(tile, tq, l_valid, qpos0) + kw - 1) // kw


def _allowed(tile, kb, rows0, shape, tq, kw, l_valid, qpos0):
    rows = lax.broadcasted_iota(jnp.int32, shape, 0) + rows0
    cols = lax.broadcasted_iota(jnp.int32, shape, 1) + kb * kw
    qchunk = (qpos0 + tile * tq + rows) // CHUNK
    return ((cols // CHUNK) <= qchunk) & (cols < l_valid), cols


def _softmax_step(s, scale, h, vblk, m_ref, l_ref, acc_ref):
    kw = s.shape[1]
    c = scale * LOG2_E
    m_prev = m_ref[h]
    m_new = jnp.maximum(m_prev, jnp.max(s, axis=-1, keepdims=True))
    alpha = jnp.exp2((m_prev - m_new) * c)
    p = jnp.exp2((s - _lane_tile(m_new, kw)) * c)
    l_ref[h] = alpha * l_ref[h] + jnp.sum(p, axis=-1, keepdims=True)
    acc_ref[h] = alpha * acc_ref[h] + jnp.dot(p.astype(BF16), vblk, preferred_element_type=F32)
    m_ref[h] = m_new


def _softmax_init(heads, tq, m_ref, l_ref, acc_ref):
    for h in range(heads):
        m_ref[h] = jnp.full((tq, LANES), NEG, F32)
        l_ref[h] = jnp.zeros((tq, LANES), F32)
        acc_ref[h] = jnp.zeros((tq, acc_ref.shape[2]), F32)


def _masked_score(score, allowed):
    return jnp.where(allowed, score, MASKED)


def _key_to_f32(key):
    return lax.bitcast_convert_type(key ^ ((key >> 31) & 0x7FFFFFFF), F32)


def _selection_bias(score, cols, thr, jcut, allowed):
    sel = ((score > thr) | ((score == thr) & (cols < jcut))) & allowed
    return jnp.where(sel, 0.0, NEG)


def _topk_thresholds(strips, rs, kf, thr_ref, jcut_ref, need_ref):
    def count(pieces, pred):
        acc = jnp.zeros((rs, LANES), F32)
        for load, col0 in pieces:
            acc = acc + jnp.where(pred(load(), col0), 1.0, 0.0)
        return jnp.sum(acc, axis=-1, keepdims=True)

    def bit_step(b, carry):
        bit = jnp.left_shift(jnp.int32(1), 31 - b)
        for rows, pieces in strips:
            t = thr_ref[rows, :]
            cand = t + bit
            cand_f = _key_to_f32(cand)
            cnt = count(pieces, lambda s, c0: s >= cand_f)
            thr_ref[rows, :] = jnp.where(cnt >= kf, cand, t)
        return carry

    thr_ref[...] = jnp.full(thr_ref.shape, INT_MIN, jnp.int32)
    lax.fori_loop(0, 32, bit_step, 0)

    jcut_ref[...] = jnp.full(jcut_ref.shape, 4096, jnp.int32)
    flag = jnp.int32(0)
    for rows, pieces in strips:
        t = _key_to_f32(thr_ref[rows, :])
        need = kf - count(pieces, lambda s, c0: s > t)
        n_eq = count(pieces, lambda s, c0: s == t)
        need_ref[rows, :] = jnp.broadcast_to(need, (rs, LANES))
        excess = (n_eq > need) & (t[:, :1] > MASKED)
        flag = jnp.maximum(flag, jnp.max(jnp.where(excess, 1, 0)))

    @pl.when(flag > 0)
    def _():
        lane = lax.broadcasted_iota(jnp.int32, (rs, LANES), 1)

        def col_step(b, carry):
            bit = jnp.left_shift(jnp.int32(1), 11 - b)
            for rows, pieces in strips:
                t, j = _key_to_f32(thr_ref[rows, :]), jcut_ref[rows, :]
                cand = j + bit
                f = count(pieces, lambda s, c0: (s == t) & (lane + c0 < cand))
                jcut_ref[rows, :] = jnp.where(f <= need_ref[rows, :], cand, j)
            return carry

        jcut_ref[...] = jnp.zeros(jcut_ref.shape, jnp.int32)
        lax.fori_loop(0, 12, col_step, 0)


def _dsa_kernel(aq_ref, iq_ref, ikw_ref, agate_ref, k_ref, v_ref, ik_ref, o_ref,
                key_ref, thr_ref, jcut_ref, need_ref, bias_ref, iwb_ref, qst_ref, qs_ref, acc_ref, m_ref, l_ref,
                *, nt, tq, kw, l_valid, qpos0, topk):
    nkb_max = key_ref.shape[0] // nt
    nkb_of = [_num_key_blocks(ti, tq, kw, l_valid, qpos0) for ti in range(nt)]
    rs = qst_ref.shape[1] // IDX_HEADS
    ns = tq // rs
    nc = kw // LANES
    kf = float(topk)

    def score_tile(ti, carry):
        r0 = pl.multiple_of(ti * tq, tq)
        iw = ikw_ref[pl.ds(r0, tq), IDX_DIM:IDX_DIM + IDX_HEADS] * (IDX_DIM ** -0.5)
        for h in range(IDX_HEADS):
            iwb_ref[h] = jnp.broadcast_to(iw[:, h:h + 1], (tq, LANES))
            for r in range(ns):
                qst_ref[r, h * rs:(h + 1) * rs, :] = iq_ref[pl.ds(r0 + r * rs, rs), h * IDX_DIM:(h + 1) * IDX_DIM]

        def score_block(kb, c2):
            ikb = ik_ref[0, pl.ds(pl.multiple_of(kb * kw, kw), kw), :]
            for r in range(ns):
                rel = lax.dot_general(qst_ref[r], ikb, _NT, preferred_element_type=F32)
                sc = jnp.zeros((rs, kw), F32)
                for h in range(IDX_HEADS):
                    w = _lane_tile(iwb_ref[h, r * rs:(r + 1) * rs, :], kw)
                    sc = sc + jnp.maximum(rel[h * rs:(h + 1) * rs], 0.0) * w
                allowed, _ = _allowed(ti, kb, r * rs, (rs, kw), tq, kw, l_valid, qpos0)
                key_ref[ti * nkb_max + kb, r * rs:(r + 1) * rs, :] = _masked_score(sc, allowed)
            return c2

        lax.fori_loop(0, _num_key_blocks(ti, tq, kw, l_valid, qpos0), score_block, 0)
        return carry

    def piece(ti, kb, s, c):
        return lambda: key_ref[ti * nkb_max + kb, s * rs:(s + 1) * rs, c * LANES:(c + 1) * LANES]

    strips = [(slice(ti * tq + s * rs, ti * tq + (s + 1) * rs),
               [(piece(ti, kb, s, c), kb * kw + c * LANES) for kb in range(nkb_of[ti]) for c in range(nc)])
              for ti in range(nt) for s in range(ns)]

    @pl.when(pl.program_id(1) == 0)
    def _():
        lax.fori_loop(0, nt, score_tile, 0)
        _topk_thresholds(strips, rs, kf, thr_ref, jcut_ref, need_ref)

    ti = pl.program_id(1)
    nkb = _num_key_blocks(ti, tq, kw, l_valid, qpos0)
    r0 = pl.multiple_of(ti * tq, tq)
    thr = _lane_tile(_key_to_f32(thr_ref[pl.ds(r0, tq), :]), kw)
    jcut = _lane_tile(jcut_ref[pl.ds(r0, tq), :], kw)

    def bias_block(kb, carry):
        allowed, cols = _allowed(ti, kb, 0, (tq, kw), tq, kw, l_valid, qpos0)
        bias_ref[kb] = _selection_bias(key_ref[ti * nkb_max + kb], cols, thr, jcut, allowed)
        return carry

    lax.fori_loop(0, nkb, bias_block, 0)

    grp = A_HEADS // A_KV_HEADS
    _softmax_init(A_KV_HEADS, grp * tq, m_ref, l_ref, acc_ref)
    scale = A_HEAD_DIM ** -0.5
    for g in range(A_KV_HEADS):
        gsl = slice(g * A_HEAD_DIM, (g + 1) * A_HEAD_DIM)
        for hh in range(grp):
            h = g * grp + hh
            qs_ref[hh * tq:(hh + 1) * tq, :] = aq_ref[:, h * A_HEAD_DIM:(h + 1) * A_HEAD_DIM]

        def att_block(kb, carry):
            c0 = pl.multiple_of(kb * kw, kw)
            kblk = k_ref[0, pl.ds(c0, kw), gsl]
            vblk = v_ref[0, pl.ds(c0, kw), gsl]
            s = lax.dot_general(qs_ref[...], kblk, _NT, preferred_element_type=F32)
            s = (s.reshape(grp, tq, kw) + bias_ref[kb][None]).reshape(grp * tq, kw)
            _softmax_step(s, scale, g, vblk, m_ref, l_ref, acc_ref)
            return carry

        lax.fori_loop(0, nkb, att_block, 0)

    for h in range(A_HEADS):
        g, hh = divmod(h, grp)
        sl = slice(h * A_HEAD_DIM, (h + 1) * A_HEAD_DIM)
        o = acc_ref[g, hh * tq:(hh + 1) * tq, :] / l_ref[g, hh * tq:(hh + 1) * tq, :]
        gate = agate_ref[:, sl]
        o_ref[:, sl] = (o * (gate * jax.nn.sigmoid(gate))).astype(BF16)


def _dsa(aq, iq, ikw, z, dst, k, v, ik, bsz, t, tq, kw, l_valid, qpos0):
    lp = k.shape[1]
    assert lp < 4096 and lp % kw == 0 and t % tq == 0
    nt = t // tq
    nkb_max = lp // kw
    topk = min(TOPK_MAX, l_valid // 4)
    grp = A_HEADS // A_KV_HEADS
    rs = min(tq, ROW_STRIP)
    go, gw = dst["agate"]
    row = lambda b, i: (b * nt + i, 0)
    whole = lambda b, i: (b, 0)
    batch = lambda b, i: (b, 0, 0)
    kern = functools.partial(_dsa_kernel, nt=nt, tq=tq, kw=kw, l_valid=l_valid, qpos0=qpos0, topk=topk)
    return pl.pallas_call(
        kern,
        grid=(bsz, nt),
        in_specs=[pl.BlockSpec((tq, A_WIDTH), row),
                  pl.BlockSpec((t, IDX_HEADS * IDX_DIM), whole),
                  pl.BlockSpec((t, LANES), whole),
                  pl.BlockSpec((tq, gw), lambda b, i: (b * nt + i, go // gw)),
                  pl.BlockSpec((1, lp, A_KV_WIDTH), batch),
                  pl.BlockSpec((1, lp, A_KV_WIDTH), batch),
                  pl.BlockSpec((1, lp, IDX_DIM), batch)],
        out_specs=pl.BlockSpec((tq, A_WIDTH), row),
        out_shape=jax.ShapeDtypeStruct((bsz * t, A_WIDTH), BF16),
        scratch_shapes=[pltpu.VMEM((nt * nkb_max, tq, kw), F32),
                        pltpu.VMEM((t, LANES), jnp.int32),
                        pltpu.VMEM((t, LANES), jnp.int32),
                        pltpu.VMEM((t, LANES), F32),
                        pltpu.VMEM((nkb_max, tq, kw), F32),
                        pltpu.VMEM((IDX_HEADS, tq, LANES), F32),
                        pltpu.VMEM((tq // rs, IDX_HEADS * rs, IDX_DIM), BF16),
                        pltpu.VMEM((grp * tq, A_HEAD_DIM), BF16),
                        pltpu.VMEM((A_KV_HEADS, grp * tq, A_HEAD_DIM), F32),
                        pltpu.VMEM((A_KV_HEADS, grp * tq, LANES), F32),
                        pltpu.VMEM((A_KV_HEADS, grp * tq, LANES), F32)],
        compiler_params=pltpu.CompilerParams(dimension_semantics=("arbitrary", "arbitrary"),
                                             vmem_limit_bytes=VMEM_LIMIT),
        name="dsa",
    )(aq, iq, ikw, z, k, v, ik)


SEL_BATCHES = 8
SEL_CHUNK = 512


def _dsa_sel_kernel(iq_ref, ikw_ref, cik_ref, nik_ref, bias_ref, key_ref, thr_ref, jcut_ref, need_ref,
                    *, gb, t, past, qpos0, topk):
    lp = past + LANES
    nc = lp // LANES

    def allowed_cols(col0, width):
        rows = lax.broadcasted_iota(jnp.int32, (t, width), 0)
        cols = lax.broadcasted_iota(jnp.int32, (t, width), 1) + col0
        return ((cols // CHUNK) <= (qpos0 + rows) // CHUNK) & (cols < past + t)

    def score_batch(j, carry):
        r0 = pl.multiple_of(j * t, t)
        qst = jnp.concatenate([iq_ref[pl.ds(r0, t), h * IDX_DIM:(h + 1) * IDX_DIM] for h in range(IDX_HEADS)], axis=0)
        iw = ikw_ref[pl.ds(r0, t), IDX_DIM:IDX_DIM + IDX_HEADS] * (IDX_DIM ** -0.5)
        iwb = [jnp.broadcast_to(iw[:, h:h + 1], (t, LANES)) for h in range(IDX_HEADS)]

        def head_sum(rel):
            width = rel.shape[1]
            sc = jnp.zeros((t, width), F32)
            for h in range(IDX_HEADS):
                sc = sc + jnp.maximum(rel[h * t:(h + 1) * t], 0.0) * _lane_tile(iwb[h], width)
            return sc

        for c in range(past // SEL_CHUNK):
            ik = cik_ref[j, c * SEL_CHUNK:(c + 1) * SEL_CHUNK, :].astype(BF16)
            sc = head_sum(lax.dot_general(qst, ik, _NT, preferred_element_type=F32))
            key_ref[j, :, c * SEL_CHUNK:(c + 1) * SEL_CHUNK] = _masked_score(sc, allowed_cols(c * SEL_CHUNK, SEL_CHUNK))
        ik = jnp.concatenate([nik_ref[pl.ds(r0, t), :], jnp.zeros((LANES - t, IDX_DIM), BF16)], axis=0)
        sc = head_sum(lax.dot_general(qst, ik, _NT, preferred_element_type=F32))
        key_ref[j, :, past:] = _masked_score(sc, allowed_cols(past, LANES))
        return carry

    lax.fori_loop(0, gb, score_batch, 0)

    def piece(j, c):
        return lambda: key_ref[j, :, c * LANES:(c + 1) * LANES]

    strips = [(slice(j * t, (j + 1) * t), [(piece(j, c), c * LANES) for c in range(nc)]) for j in range(gb)]
    _topk_thresholds(strips, t, float(topk), thr_ref, jcut_ref, need_ref)

    def bias_batch(j, carry):
        r0 = pl.multiple_of(j * t, t)
        cols = lax.broadcasted_iota(jnp.int32, (t, lp), 1)
        bias_ref[j] = _selection_bias(key_ref[j], cols, _lane_tile(_key_to_f32(thr_ref[pl.ds(r0, t), :]), lp),
                                      _lane_tile(jcut_ref[pl.ds(r0, t), :], lp), allowed_cols(0, lp))
        return carry

    lax.fori_loop(0, gb, bias_batch, 0)


def _dsa_sel(iq, ikw, cache_ik, new_ik, bsz, t, qpos0):
    past = cache_ik.shape[1]
    lp = past + LANES
    gb = next(g for g in (SEL_BATCHES, 4, 2, 1) if bsz % g == 0)
    assert past % SEL_CHUNK == 0 and t <= LANES and lp < 4096
    topk = min(TOPK_MAX, (past + t) // 4)
    rows = lambda i: (i, 0)
    kern = functools.partial(_dsa_sel_kernel, gb=gb, t=t, past=past, qpos0=qpos0, topk=topk)
    return pl.pallas_call(
        kern,
        grid=(bsz // gb,),
        in_specs=[pl.BlockSpec((gb * t, IDX_HEADS * IDX_DIM), rows),
                  pl.BlockSpec((gb * t, LANES), rows),
                  pl.BlockSpec((gb, past, IDX_DIM), lambda i: (i, 0, 0)),
                  pl.BlockSpec((gb * t, IDX_DIM), rows)],
        out_specs=pl.BlockSpec((gb, t, lp), lambda i: (i, 0, 0)),
        out_shape=jax.ShapeDtypeStruct((bsz, t, lp), F32),
        scratch_shapes=[pltpu.VMEM((gb, t, lp), F32),
                        pltpu.VMEM((gb * t, LANES), jnp.int32),
                        pltpu.VMEM((gb * t, LANES), jnp.int32),
                        pltpu.VMEM((gb * t, LANES), F32)],
        compiler_params=pltpu.CompilerParams(dimension_semantics=("arbitrary",), vmem_limit_bytes=VMEM_LIMIT),
        name="dsa_sel",
    )(iq, ikw, cache_ik, new_ik)


def _dsa_att_kernel(aq_ref, agate_ref, bias_ref, ck_ref, cv_ref, nk_ref, nv_ref, o_ref, *, t, past):
    grp = A_HEADS // A_KV_HEADS
    scale = A_HEAD_DIM ** -0.5
    pad = jnp.zeros((LANES - t, A_HEAD_DIM), BF16)
    b1, b2 = bias_ref[0, :, :past], bias_ref[0, :, past:]
    for g in range(A_KV_HEADS):
        gsl = slice(g * A_HEAD_DIM, (g + 1) * A_HEAD_DIM)
        kc = ck_ref[0, pl.ds(g, past, stride=A_KV_HEADS), :].astype(BF16)
        vc = cv_ref[0, pl.ds(g, past, stride=A_KV_HEADS), :].astype(BF16)
        kn = jnp.concatenate([nk_ref[:, gsl], pad], axis=0)
        vn = jnp.concatenate([nv_ref[:, gsl], pad], axis=0)
        qs = jnp.concatenate([aq_ref[:, (g * grp + hh) * A_HEAD_DIM:(g * grp + hh + 1) * A_HEAD_DIM]
                              for hh in range(grp)], axis=0)

        def scores(keys, bias):
            s = lax.dot_general(qs, keys, _NT, preferred_element_type=F32) * scale
            return (s.reshape(grp, t, s.shape[1]) + bias[None]).reshape(grp * t, s.shape[1])

        s1, s2 = scores(kc, b1), scores(kn, b2)
        m = jnp.maximum(jnp.max(s1, axis=-1, keepdims=True), jnp.max(s2, axis=-1, keepdims=True))
        e1, e2 = jnp.exp(s1 - m), jnp.exp(s2 - m)
        l = jnp.sum(e1, axis=-1, keepdims=True) + jnp.sum(e2, axis=-1, keepdims=True)
        o = (jnp.dot(e1.astype(BF16), vc, preferred_element_type=F32)
             + jnp.dot(e2.astype(BF16), vn, preferred_element_type=F32)) / l
        for hh in range(grp):
            sl = slice((g * grp + hh) * A_HEAD_DIM, (g * grp + hh + 1) * A_HEAD_DIM)
            gate = agate_ref[:, sl]
            o_ref[:, sl] = (o[hh * t:(hh + 1) * t] * (gate * jax.nn.sigmoid(gate))).astype(BF16)


def _dsa_att(aq, z, dst, bias, cache_k, cache_v, new_k, new_v, bsz, t):
    past = cache_k.shape[1] // A_KV_HEADS
    lp = bias.shape[2]
    go, gw = dst["agate"]
    row = lambda b: (b, 0)
    batch = lambda b: (b, 0, 0)
    return pl.pallas_call(
        functools.partial(_dsa_att_kernel, t=t, past=past),
        grid=(bsz,),
        in_specs=[pl.BlockSpec((t, A_WIDTH), row),
                  pl.BlockSpec((t, gw), lambda b: (b, go // gw)),
                  pl.BlockSpec((1, t, lp), batch),
                  pl.BlockSpec((1, past * A_KV_HEADS, A_HEAD_DIM), batch),
                  pl.BlockSpec((1, past * A_KV_HEADS, A_HEAD_DIM), batch),
                  pl.BlockSpec((t, A_KV_WIDTH), row),
                  pl.BlockSpec((t, A_KV_WIDTH), row)],
        out_specs=pl.BlockSpec((t, A_WIDTH), row),
        out_shape=jax.ShapeDtypeStruct((bsz * t, A_WIDTH), BF16),
        compiler_params=pltpu.CompilerParams(dimension_semantics=("arbitrary",), vmem_limit_bytes=VMEM_LIMIT),
        name="dsa_att",
    )(aq, z, bias, cache_k, cache_v, new_k, new_v)


def _mla_kernel(q_ref, bgate_ref, k_ref, v_ref, o_ref, acc_ref, m_ref, l_ref, *, tq, kw, l_valid, qpos0):
    ti = pl.program_id(1)
    nkb = _num_key_blocks(ti, tq, kw, l_valid, qpos0)
    scale = (NOPE_DIM + ROPE_DIM) ** -0.5
    _softmax_init(B_HEADS, tq, m_ref, l_ref, acc_ref)

    def att_block(kb, masked):
        r0 = pl.multiple_of(kb * kw, kw)
        if masked:
            allowed, _ = _allowed(ti, kb, 0, (tq, kw), tq, kw, l_valid, qpos0)
            bias = jnp.where(allowed, 0.0, NEG)
        for h in range(B_HEADS):
            qsl = slice(h * B_QK_PAD, (h + 1) * B_QK_PAD)
            s = lax.dot_general(q_ref[:, qsl], k_ref[0, pl.ds(r0, kw), qsl], _NT, preferred_element_type=F32)
            if masked:
                s = s + bias
            _softmax_step(s, scale, h, v_ref[0, pl.ds(r0, kw), h * V_DIM:(h + 1) * V_DIM], m_ref, l_ref, acc_ref)

    first_visible = jnp.minimum(((qpos0 + ti * tq) // CHUNK + 1) * CHUNK, l_valid)
    n_full = first_visible // kw
    lax.fori_loop(0, n_full, lambda kb, c: (att_block(kb, False), c)[1], 0)
    lax.fori_loop(n_full, nkb, lambda kb, c: (att_block(kb, True), c)[1], 0)

    for h in range(B_HEADS):
        sl = slice(h * V_DIM, (h + 1) * V_DIM)
        o = acc_ref[h] / l_ref[h]
        gate = bgate_ref[:, sl]
        o_ref[:, sl] = (o * (gate * jax.nn.sigmoid(gate))).astype(BF16)


def _mla(qcat, z, dst, kcat, v, bsz, t, tq, kw, l_valid, qpos0):
    lp = kcat.shape[1]
    assert lp % kw == 0 and t % tq == 0
    nt = t // tq
    go, gw = dst["bgate"]
    row = lambda b, i: (b * nt + i, 0)
    batch = lambda b, i: (b, 0, 0)
    kern = functools.partial(_mla_kernel, tq=tq, kw=kw, l_valid=l_valid, qpos0=qpos0)
    return pl.pallas_call(
        kern,
        grid=(bsz, nt),
        in_specs=[pl.BlockSpec((tq, B_HEADS * B_QK_PAD), row),
                  pl.BlockSpec((tq, gw), lambda b, i: (b * nt + i, go // gw)),
                  pl.BlockSpec((1, lp, B_HEADS * B_QK_PAD), batch),
                  pl.BlockSpec((1, lp, B_WIDTH), batch)],
        out_specs=pl.BlockSpec((tq, B_WIDTH), row),
        out_shape=jax.ShapeDtypeStruct((bsz * t, B_WIDTH), BF16),
        scratch_shapes=[pltpu.VMEM((B_HEADS, tq, V_DIM), F32),
                        pltpu.VMEM((B_HEADS, tq, LANES), F32),
                        pltpu.VMEM((B_HEADS, tq, LANES), F32)],
        compiler_params=pltpu.CompilerParams(dimension_semantics=("arbitrary", "arbitrary"),
                                             vmem_limit_bytes=VMEM_LIMIT),
        name="mla",
    )(qcat, z, kcat, v)


def _mla_lat_kernel(q_ref, bgate_ref, cc_ref, ckr_ref, nc_ref, nkr_ref, wukt_ref, wuv_ref, o_ref,
                    *, t, past, qpos0):
    rows_n = B_HEADS * t
    ql, qr = [], []
    for h in range(B_HEADS):
        qn = q_ref[:, h * B_QK_PAD: h * B_QK_PAD + NOPE_DIM]
        ql.append(jnp.dot(qn, wukt_ref[h], preferred_element_type=F32).astype(BF16))
        qr.append(q_ref[:, h * B_QK_PAD + NOPE_DIM: h * B_QK_PAD + NOPE_DIM + ROPE_DIM])
    ql = jnp.concatenate(ql, axis=0)
    qr = jnp.concatenate(qr, axis=0)
    cc = cc_ref[0].astype(BF16)
    ckr = ckr_ref[0].astype(BF16)
    nc, nkr = nc_ref[...], nkr_ref[...]
    scale = (NOPE_DIM + ROPE_DIM) ** -0.5

    def scores(lat, rope, ncols, col0):
        s = lax.dot_general(ql, lat, _NT, preferred_element_type=F32)
        s = (s + lax.dot_general(qr, rope, _NT, preferred_element_type=F32)) * scale
        rows = lax.broadcasted_iota(jnp.int32, (rows_n, ncols), 0)
        cols = lax.broadcasted_iota(jnp.int32, (rows_n, ncols), 1) + col0
        qchunk = (qpos0 + lax.rem(rows, t)) // CHUNK
        return jnp.where((cols // CHUNK) <= qchunk, s, NEG)

    s1 = scores(cc, ckr, past, 0)
    s2 = scores(nc, nkr, t, past)
    m = jnp.maximum(jnp.max(s1, axis=-1, keepdims=True), jnp.max(s2, axis=-1, keepdims=True))
    e1, e2 = jnp.exp(s1 - m), jnp.exp(s2 - m)
    l = jnp.sum(e1, axis=-1, keepdims=True) + jnp.sum(e2, axis=-1, keepdims=True)
    ol = (jnp.dot(e1.astype(BF16), cc, preferred_element_type=F32)
          + jnp.dot(e2.astype(BF16), nc, preferred_element_type=F32)) / l
    ol = ol.astype(BF16)
    for h in range(B_HEADS):
        sl = slice(h * V_DIM, (h + 1) * V_DIM)
        o = jnp.dot(ol[h * t:(h + 1) * t], wuv_ref[:, sl], preferred_element_type=F32)
        gate = bgate_ref[:, sl]
        o_ref[:, sl] = (o * (gate * jax.nn.sigmoid(gate))).astype(BF16)


def _mla_lat(qcat, z, dst, cache_ckv, cache_kr, new_ckv, new_kr, wukt, wuv, bsz, t, qpos0):
    past = cache_ckv.shape[1]
    go, gw = dst["bgate"]
    row = lambda b: (b, 0)
    batch = lambda b: (b, 0, 0)
    const = lambda a: pl.BlockSpec(a.shape, lambda b: (0,) * a.ndim)
    kern = functools.partial(_mla_lat_kernel, t=t, past=past, qpos0=qpos0)
    return pl.pallas_call(
        kern,
        grid=(bsz,),
        in_specs=[pl.BlockSpec((t, B_HEADS * B_QK_PAD), row),
                  pl.BlockSpec((t, gw), lambda b: (b, go // gw)),
                  pl.BlockSpec((1, past, KV_LORA), batch),
                  pl.BlockSpec((1, past, ROPE_DIM), batch),
                  pl.BlockSpec((t, KV_LORA), row),
                  pl.BlockSpec((t, ROPE_DIM), row),
                  const(wukt), const(wuv)],
        out_specs=pl.BlockSpec((t, B_WIDTH), row),
        out_shape=jax.ShapeDtypeStruct((bsz * t, B_WIDTH), BF16),
        compiler_params=pltpu.CompilerParams(dimension_semantics=("arbitrary",), vmem_limit_bytes=VMEM_LIMIT),
        name="mla_lat",
    )(qcat, z, cache_ckv, cache_kr, new_ckv, new_kr, wukt, wuv)


def _out_kernel(ga_ref, gb_ref, mg_ref, x_ref, woa_ref, wob_ref, wout_ref, fg_ref, o_ref, *, d, final):
    ya = jnp.dot(ga_ref[...], woa_ref[...], preferred_element_type=F32)
    yb = jnp.dot(gb_ref[...], wob_ref[...], preferred_element_type=F32)
    m = jax.nn.sigmoid(mg_ref[:, :d]) * ya + jax.nn.sigmoid(mg_ref[:, d:]) * yb
    h = x_ref[...] + jnp.dot(m.astype(BF16), wout_ref[...], preferred_element_type=F32)
    o_ref[...] = _rms(h, fg_ref[...]) if final else h


def _out(ga, gb, z, dst, x, woa, wob, wout, fg, final):
    m, d = x.shape
    tm = min(m, 256)
    assert m % tm == 0 and dst["mg"][0] == 0
    row = lambda i: (i, 0)
    const = lambda a: pl.BlockSpec(a.shape, lambda i: (0, 0), pipeline_mode=pl.Buffered(1))
    return pl.pallas_call(
        functools.partial(_out_kernel, d=d, final=final),
        grid=(m // tm,),
        in_specs=[pl.BlockSpec((tm, A_WIDTH), row), pl.BlockSpec((tm, B_WIDTH), row),
                  pl.BlockSpec((tm, 2 * d), row), pl.BlockSpec((tm, d), row),
                  const(woa), const(wob), const(wout), const(fg)],
        out_specs=pl.BlockSpec((tm, d), row),
        out_shape=jax.ShapeDtypeStruct((m, d), F32),
        compiler_params=pltpu.CompilerParams(dimension_semantics=("arbitrary",), vmem_limit_bytes=VMEM_LIMIT),
        name="out",
    )(ga, gb, z, x, woa, wob, wout, fg)


def _all_keys(cache, new, bsz, t, lp):
    new = new.reshape(bsz, t, new.shape[-1])
    if cache is None and lp == t:
        return new
    parts = ([] if cache is None else [cache.reshape(bsz, cache.shape[1], -1).astype(BF16)]) + [new]
    kk = jnp.concatenate(parts, axis=1)
    return jnp.pad(kk, ((0, 0), (0, lp - kk.shape[1]), (0, 0)))


def _trunk(x, caches, final_g, weights):
    bsz, t, d = x.shape
    depth = weights["w_in"].shape[0]
    past = 0 if caches is None else caches[0].shape[2]
    l_valid = past + t
    tq = min(t, QUERY_TILE)
    kw = KEY_BLOCK if t >= QUERY_TILE else -(-l_valid // LANES) * LANES
    lp = -(-l_valid // kw) * kw
    _, dst, _ = _layout(d)
    pos = past + jnp.arange(t, dtype=jnp.int32)
    t_scale = jnp.concatenate([jnp.full((t, IDX_HEADS), IDX_HEADS ** -0.5, F32),
                               jnp.ones((t, LANES - IDX_DIM - IDX_HEADS), F32)], axis=1)
    tabs = [jnp.stack(_rope_tables(pos, A_ROT, A_HEAD_DIM, 1)),
            jnp.stack(_rope_tables(pos, IDX_ROT, IDX_DIM, 2)),
            jnp.stack(_rope_tables(pos, IDX_ROT, IDX_DIM, 1, extra=t_scale)),
            jnp.stack(_rope_tables(pos, ROPE_DIM, ROPE_DIM, 2))]
    h = x.reshape(bsz * t, d)
    new_rows = []
    for l in range(depth):
        w = {k: v[l] for k, v in weights.items()}
        z = _proj(h, w["norm_g"], w["w_in"])
        aq, iq, ak, ikw, krp, ckv, qcat, kb, vb, ikb, ckvb, krb = _prep(
            z, dst, bsz, t, tabs, w["q_norm_g"], w["kv_norm_g"], w["w_uq"])
        o, wd = dst["av"]
        av = z[:, o:o + wd]
        c = (None,) * 5 if caches is None else tuple(cc[l] for cc in caches)
        if caches is not None and t < QUERY_TILE:
            bias = _dsa_sel(iq, ikw, c[2], ikb, bsz, t, past)
            interleaved = lambda a: a.reshape(bsz, past * A_KV_HEADS, A_HEAD_DIM)
            ga = _dsa_att(aq, z, dst, bias, interleaved(c[0]), interleaved(c[1]), kb, vb, bsz, t)
            gb = _mla_lat(qcat, z, dst, c[3], c[4], ckvb, krb, w["w_ukt"], w["w_uv"], bsz, t, past)
        else:
            k_all = _all_keys(c[0], kb, bsz, t, lp)
            v_all = _all_keys(c[1], vb, bsz, t, lp)
            ik_all = _all_keys(c[2], ikb, bsz, t, lp)
            ga = _dsa(aq, iq, ikw, z, dst, k_all, v_all, ik_all, bsz, t, tq, kw, l_valid, past)
            ckv_all = _all_keys(c[3], ckvb, bsz, t, lp)
            kr_all = _all_keys(c[4], krb, bsz, t, lp)
            kcat, vv = _kvup(ckv_all.reshape(bsz * lp, KV_LORA), kr_all.reshape(bsz * lp, ROPE_DIM), w["w_ukv"])
            kw_mla = MLA_KEY_BLOCK if lp % MLA_KEY_BLOCK == 0 else kw
            gb = _mla(qcat, z, dst, kcat.reshape(bsz, lp, -1), vv.reshape(bsz, lp, -1), bsz, t, tq, kw_mla, l_valid, past)
        h = _out(ga, gb, z, dst, h, w["w_o_a"], w["w_o_b"], w["w_out"], final_g, final=(l == depth - 1))
        new_rows.append((ak.reshape(bsz, t, A_KV_HEADS, A_HEAD_DIM), av.reshape(bsz, t, A_KV_HEADS, A_HEAD_DIM),
                         ikw[:, :IDX_DIM].reshape(bsz, t, IDX_DIM), ckv.reshape(bsz, t, KV_LORA),
                         krp[:, :ROPE_DIM].reshape(bsz, t, ROPE_DIM)))
    stacked = [jnp.stack([r[i] for r in new_rows], axis=0) for i in range(5)]
    return h.reshape(bsz, t, d), stacked


def kernel(x_prompt, x_sample, cache_a_k, cache_a_v, cache_idx_k, cache_mla_ckv, cache_mla_krope,
           norm_g, w_in, w_uq, q_norm_g, kv_norm_g, w_uk, w_uv, w_o_a, w_o_b, w_out, final_g):
    depth, d, _ = w_in.shape
    weights = {
        "norm_g": norm_g.reshape(depth, 1, d),
        "w_in": jnp.stack([_regroup_w_in(w_in[l], d) for l in range(depth)]),
        "w_uq": jnp.concatenate([w_uq[..., :NOPE_DIM].reshape(depth, Q_LORA, B_HEADS * NOPE_DIM),
                                 w_uq[..., NOPE_DIM:].reshape(depth, Q_LORA, B_HEADS * ROPE_DIM)], axis=-1).astype(BF16),
        "q_norm_g": q_norm_g.reshape(depth, 1, Q_LORA),
        "kv_norm_g": kv_norm_g.reshape(depth, 1, KV_LORA),
        "w_ukv": jnp.concatenate([w_uk.reshape(depth, KV_LORA, B_HEADS * NOPE_DIM),
                                  w_uv.reshape(depth, KV_LORA, B_WIDTH)], axis=-1).astype(BF16),
        "w_ukt": jnp.transpose(w_uk, (0, 2, 3, 1)).astype(BF16),
        "w_uv": w_uv.reshape(depth, KV_LORA, B_WIDTH).astype(BF16),
        "w_o_a": w_o_a.astype(BF16), "w_o_b": w_o_b.astype(BF16), "w_out": w_out.astype(BF16),
    }
    fg = final_g.reshape(1, d)
    y_p, (pk, pv, pik, pckv, pkr) = _trunk(x_prompt, None, fg, weights)
    caches = (cache_a_k, cache_a_v, cache_idx_k, cache_mla_ckv, cache_mla_krope)
    y_s, (sk, sv, sik, sckv, skr) = _trunk(x_sample, caches, fg, weights)
    return (y_p, y_s, pk, pv, pik, pckv, pkr, sk, sv, sik, sckv, skr)
```

```python
import functools

import jax
import jax.numpy as jnp
from jax import lax
from jax.experimental import pallas as pl
from jax.experimental.pallas import tpu as pltpu

F32 = jnp.float32
BF16 = jnp.bfloat16

CHUNK = 64
ROPE_THETA = 500000.0
EPS = 1e-6

A_HEADS = 8
A_KV_HEADS = 2
A_HEAD_DIM = 128
A_ROT = A_HEAD_DIM // 4
A_WIDTH = A_HEADS * A_HEAD_DIM
A_KV_WIDTH = A_KV_HEADS * A_HEAD_DIM
IDX_HEADS = 16
IDX_DIM = 64
IDX_ROT = IDX_DIM // 4
TOPK_MAX = 256

B_HEADS = 8
Q_LORA = 512
KV_LORA = 512
NOPE_DIM = 128
ROPE_DIM = 64
V_DIM = 128
B_WIDTH = B_HEADS * V_DIM
B_QK_PAD = 256

LANES = 128
KEY_BLOCK = 256
QUERY_TILE = 256
ROW_STRIP = 64
INT_MIN = -(2 ** 31)
LOG2_E = 1.4426950408889634
NEG = -1e30
MASKED = -1e38
VMEM_LIMIT = 56 * 1024 * 1024

_SEGS = (("aq", A_WIDTH), ("ak", A_KV_WIDTH), ("av", A_KV_WIDTH), ("agate", A_WIDTH),
         ("iq", IDX_HEADS * IDX_DIM), ("ik", IDX_DIM), ("iw", IDX_HEADS),
         ("cq", Q_LORA), ("ckv", KV_LORA), ("kr", ROPE_DIM), ("bgate", B_WIDTH), ("mg", None))


def _layout(d_model):
    src, off = {}, 0
    for name, w in _SEGS:
        w = 2 * d_model if w is None else w
        src[name] = (off, w)
        off += w
    order = (("mg", 2 * d_model), ("aq", A_WIDTH), ("agate", A_WIDTH), ("iq", IDX_HEADS * IDX_DIM),
             ("bgate", B_WIDTH), ("cq", Q_LORA), ("ckv", KV_LORA), ("ak", A_KV_WIDTH), ("av", A_KV_WIDTH),
             ("ikw", LANES), ("krp", LANES))
    dst, off = {}, 0
    for name, w in order:
        assert off % w == 0
        dst[name] = (off, w)
        off += w
    return src, dst, off


def _regroup_kernel(w_ref, o_ref, *, src, dst):
    cols = w_ref.shape[1]

    def piece(name):
        o, w = src[name]
        return w_ref[o:o + w, :].astype(BF16)

    def zeros(n):
        return jnp.zeros((n, cols), BF16)

    for name, (o, w) in dst.items():
        if name == "ikw":
            val = jnp.concatenate([piece("ik"), piece("iw"), zeros(LANES - IDX_DIM - IDX_HEADS)], axis=0)
        elif name == "krp":
            val = jnp.concatenate([piece("kr"), zeros(LANES - ROPE_DIM)], axis=0)
        else:
            val = piece(name)
        o_ref[o:o + w, :] = val


def _regroup_w_in(w_in_t, d_model):
    src, dst, total = _layout(d_model)
    tc = 256
    assert d_model % tc == 0
    return pl.pallas_call(
        functools.partial(_regroup_kernel, src=src, dst=dst),
        grid=(d_model // tc,),
        in_specs=[pl.BlockSpec((w_in_t.shape[0], tc), lambda i: (0, i))],
        out_specs=pl.BlockSpec((total, tc), lambda i: (0, i)),
        out_shape=jax.ShapeDtypeStruct((total, d_model), BF16),
        compiler_params=pltpu.CompilerParams(dimension_semantics=("arbitrary",), vmem_limit_bytes=VMEM_LIMIT),
        name="regroup",
    )(w_in_t)


def _rope_tables(pos, rot, head, heads_per_vreg, extra=None):
    half = rot // 2
    inv_freq = ROPE_THETA ** (-jnp.arange(half, dtype=F32) * (2.0 / rot))
    ang = pos.astype(F32)[:, None] * inv_freq[None, :]
    cos, sin = jnp.cos(ang), jnp.sin(ang)
    t = pos.shape[0]
    one = jnp.ones((t, head - rot), F32)
    zero = lambda n: jnp.zeros((t, n), F32)
    c = jnp.concatenate([cos, cos, one], axis=1)
    sa = jnp.concatenate([-sin, zero(head - half)], axis=1)
    sb = jnp.concatenate([zero(half), sin, zero(head - rot)], axis=1)
    c, sa, sb = (jnp.tile(a, (1, heads_per_vreg)) for a in (c, sa, sb))
    if extra is not None:
        c = jnp.concatenate([c, extra], axis=1)
        sa = jnp.concatenate([sa, jnp.zeros_like(extra)], axis=1)
        sb = jnp.concatenate([sb, jnp.zeros_like(extra)], axis=1)
    assert c.shape[1] == LANES
    return c, sa, sb


def _rope(x, c, sa, sb, half):
    up = pltpu.roll(x, LANES - half, 1)
    dn = pltpu.roll(x, half, 1)
    return x * c + up * sa + dn * sb


def _rms(x, g):
    ms = jnp.mean(x * x, axis=-1, keepdims=True)
    return (x * lax.rsqrt(ms + EPS)) * g


def _lane_tile(x, width):
    return x if width == LANES else jnp.concatenate([x] * (width // LANES), axis=1)


_NT = (((1,), (1,)), ((), ()))


def _proj_kernel(x_ref, g_ref, w_ref, z_ref, xn_ref):
    @pl.when(pl.program_id(1) == 0)
    def _():
        xn_ref[...] = _rms(x_ref[...], g_ref[...]).astype(BF16)
    z_ref[...] = lax.dot_general(xn_ref[...], w_ref[...], _NT, preferred_element_type=F32)


def _proj(x, g, w):
    m, d = x.shape
    n = w.shape[0]
    tm = min(m, 1024)
    tn = 768
    assert m % tm == 0 and n % tn == 0
    return pl.pallas_call(
        _proj_kernel,
        grid=(m // tm, n // tn),
        in_specs=[pl.BlockSpec((tm, d), lambda i, j: (i, 0)),
                  pl.BlockSpec((1, d), lambda i, j: (0, 0)),
                  pl.BlockSpec((tn, d), lambda i, j: (j, 0))],
        out_specs=pl.BlockSpec((tm, tn), lambda i, j: (i, j)),
        out_shape=jax.ShapeDtypeStruct((m, n), F32),
        scratch_shapes=[pltpu.VMEM((tm, d), BF16)],
        compiler_params=pltpu.CompilerParams(dimension_semantics=("arbitrary", "arbitrary"),
                                             vmem_limit_bytes=VMEM_LIMIT),
        name="proj",
    )(x, g, w)


def _prep_kernel(aq_ref, iq_ref, cq_ref, ckv_ref, ak_ref, av_ref, ikw_ref, krp_ref,
                 ta_ref, ti_ref, tk_ref, tr_ref, qg_ref, kvg_ref, wuq_ref,
                 aq_o, iq_o, ak_o, ikw_o, krp_o, ckv_o, qcat_o, kb_o, vb_o, ikb_o, ckvb_o, krb_o):
    ca, saa, sba = ta_ref[0], ta_ref[1], ta_ref[2]
    ci, sai, sbi = ti_ref[0], ti_ref[1], ti_ref[2]
    ck, sak, sbk = tk_ref[0], tk_ref[1], tk_ref[2]
    cr, sar, sbr = tr_ref[0], tr_ref[1], tr_ref[2]
    tm = aq_ref.shape[0]
    for h in range(A_HEADS):
        sl = slice(h * LANES, (h + 1) * LANES)
        aq_o[:, sl] = _rope(aq_ref[:, sl], ca, saa, sba, A_ROT // 2).astype(BF16)
    for h in range(A_KV_HEADS):
        sl = slice(h * LANES, (h + 1) * LANES)
        ak = _rope(ak_ref[:, sl], ca, saa, sba, A_ROT // 2)
        ak_o[:, sl] = ak
        kb_o[:, sl] = ak.astype(BF16)
    vb_o[...] = av_ref[...].astype(BF16)
    for v in range(IDX_HEADS * IDX_DIM // LANES):
        sl = slice(v * LANES, (v + 1) * LANES)
        iq_o[:, sl] = _rope(iq_ref[:, sl], ci, sai, sbi, IDX_ROT // 2).astype(BF16)
    ikw = _rope(ikw_ref[...], ck, sak, sbk, IDX_ROT // 2)
    ikw_o[...] = ikw
    ikb_o[...] = ikw[:, :IDX_DIM].astype(BF16)
    krp = _rope(krp_ref[...], cr, sar, sbr, ROPE_DIM // 2)
    krp_o[...] = krp
    krb_o[...] = krp[:, :ROPE_DIM].astype(BF16)
    ckv = _rms(ckv_ref[...], kvg_ref[...])
    ckv_o[...] = ckv
    ckvb_o[...] = ckv.astype(BF16)
    cqn = _rms(cq_ref[...], qg_ref[...]).astype(BF16)
    qb = jnp.dot(cqn, wuq_ref[...], preferred_element_type=F32)
    nope = B_HEADS * NOPE_DIM
    low = lax.broadcasted_iota(jnp.int32, (tm, LANES), 1) < ROPE_DIM
    for v in range(B_HEADS * ROPE_DIM // LANES):
        r = _rope(qb[:, nope + v * LANES: nope + (v + 1) * LANES], cr, sar, sbr, ROPE_DIM // 2)
        halves = (jnp.where(low, r, 0.0), jnp.where(low, pltpu.roll(r, ROPE_DIM, 1), 0.0))
        for j in range(2):
            h = 2 * v + j
            qcat_o[:, h * B_QK_PAD: h * B_QK_PAD + NOPE_DIM] = qb[:, h * NOPE_DIM:(h + 1) * NOPE_DIM].astype(BF16)
            qcat_o[:, h * B_QK_PAD + NOPE_DIM:(h + 1) * B_QK_PAD] = halves[j].astype(BF16)


def _prep(z, dst, bsz, t, tabs, q_norm_g, kv_norm_g, wuq):
    m = bsz * t
    tm = min(t, 256)
    nt = t // tm
    assert t % tm == 0

    def zspec(name):
        o, w = dst[name]
        return pl.BlockSpec((tm, w), lambda b, i, blk=o // w: (b * nt + i, blk))

    def tspec():
        return pl.BlockSpec((3, tm, LANES), lambda b, i: (0, i, 0))

    def ospec(w):
        return pl.BlockSpec((tm, w), lambda b, i: (b * nt + i, 0))

    def full(a):
        return pl.BlockSpec(a.shape, lambda b, i: (0,) * a.ndim)

    outs = [(A_WIDTH, BF16), (IDX_HEADS * IDX_DIM, BF16), (A_KV_WIDTH, F32), (LANES, F32), (LANES, F32),
            (KV_LORA, F32), (B_HEADS * B_QK_PAD, BF16),
            (A_KV_WIDTH, BF16), (A_KV_WIDTH, BF16), (IDX_DIM, BF16), (KV_LORA, BF16), (ROPE_DIM, BF16)]
    return pl.pallas_call(
        _prep_kernel,
        grid=(bsz, nt),
        in_specs=[zspec("aq"), zspec("iq"), zspec("cq"), zspec("ckv"), zspec("ak"), zspec("av"), zspec("ikw"),
                  zspec("krp"), tspec(), tspec(), tspec(), tspec(), full(q_norm_g), full(kv_norm_g), full(wuq)],
        out_specs=[ospec(w) for w, _ in outs],
        out_shape=[jax.ShapeDtypeStruct((m, w), dt) for w, dt in outs],
        compiler_params=pltpu.CompilerParams(dimension_semantics=("arbitrary", "arbitrary"),
                                             vmem_limit_bytes=VMEM_LIMIT),
        name="prep",
    )(z, z, z, z, z, z, z, z, *tabs, q_norm_g, kv_norm_g, wuq)


def _kvup_kernel(c_ref, kr_ref, w_ref, kcat_ref, v_ref):
    r = jnp.dot(c_ref[...], w_ref[...], preferred_element_type=F32)
    n = B_HEADS * NOPE_DIM
    tail = jnp.concatenate([kr_ref[...], jnp.zeros((kr_ref.shape[0], B_QK_PAD - NOPE_DIM - ROPE_DIM), BF16)], axis=1)
    for h in range(B_HEADS):
        kcat_ref[:, h * B_QK_PAD: h * B_QK_PAD + NOPE_DIM] = r[:, h * NOPE_DIM:(h + 1) * NOPE_DIM].astype(BF16)
        kcat_ref[:, h * B_QK_PAD + NOPE_DIM:(h + 1) * B_QK_PAD] = tail
    v_ref[...] = r[:, n:].astype(BF16)


def _kvup(ckv, kr, w):
    m, c = ckv.shape
    tm = next(t for t in (512, 384, 256, 128) if m % t == 0)
    row = lambda i: (i, 0)
    return pl.pallas_call(
        _kvup_kernel,
        grid=(m // tm,),
        in_specs=[pl.BlockSpec((tm, c), row), pl.BlockSpec((tm, ROPE_DIM), row), pl.BlockSpec(w.shape, lambda i: (0, 0))],
        out_specs=[pl.BlockSpec((tm, B_HEADS * B_QK_PAD), row), pl.BlockSpec((tm, B_WIDTH), row)],
        out_shape=[jax.ShapeDtypeStruct((m, B_HEADS * B_QK_PAD), BF16), jax.ShapeDtypeStruct((m, B_WIDTH), BF16)],
        compiler_params=pltpu.CompilerParams(dimension_semantics=("arbitrary",), vmem_limit_bytes=VMEM_LIMIT),
        name="kv_up",
    )(ckv, kr, w)


def _visible_cols(tile, tq, l_valid, qpos0):
    p_last = qpos0 + (tile + 1) * tq - 1
    n = (p_last // CHUNK + 1) * CHUNK
    return min(n, l_valid) if isinstance(n, int) else jnp.minimum(n, l_valid)


def _num_key_blocks(tile, tq, kw, l_valid, qpos0):
    return (_visible_cols(tile, tq, l_valid, qpos0) + kw - 1) // kw


def _allowed(tile, kb, rows0, shape, tq, kw, l_valid, qpos0):
    rows = lax.broadcasted_iota(jnp.int32, shape, 0) + rows0
    cols = lax.broadcasted_iota(jnp.int32, shape, 1) + kb * kw
    qchunk = (qpos0 + tile * tq + rows) // CHUNK
    return ((cols // CHUNK) <= qchunk) & (cols < l_valid), cols


def _softmax_step(s, scale, h, vblk, m_ref, l_ref, acc_ref):
    kw = s.shape[1]
    c = scale * LOG2_E
    m_prev = m_ref[h]
    m_new = jnp.maximum(m_prev, jnp.max(s, axis=-1, keepdims=True))
    alpha = jnp.exp2((m_prev - m_new) * c)
    p = jnp.exp2((s - _lane_tile(m_new, kw)) * c)
    l_ref[h] = alpha * l_ref[h] + jnp.sum(p, axis=-1, keepdims=True)
    acc_ref[h] = alpha * acc_ref[h] + jnp.dot(p.astype(BF16), vblk, preferred_element_type=F32)
    m_ref[h] = m_new


def _softmax_init(heads, tq, m_ref, l_ref, acc_ref):
    for h in range(heads):
        m_ref[h] = jnp.full((tq, LANES), NEG, F32)
        l_ref[h] = jnp.zeros((tq, LANES), F32)
        acc_ref[h] = jnp.zeros((tq, acc_ref.shape[2]), F32)


def _masked_score(score, allowed):
    return jnp.where(allowed, score, MASKED)


def _key_to_f32(key):
    return lax.bitcast_convert_type(key ^ ((key >> 31) & 0x7FFFFFFF), F32)


def _selection_bias(score, cols, thr, jcut, allowed):
    sel = ((score > thr) | ((score == thr) & (cols < jcut))) & allowed
    return jnp.where(sel, 0.0, NEG)


def _topk_thresholds(strips, rs, kf, thr_ref, jcut_ref, need_ref):
    def count(pieces, pred):
        acc = jnp.zeros((rs, LANES), F32)
        for load, col0 in pieces:
            acc = acc + jnp.where(pred(load(), col0), 1.0, 0.0)
        return jnp.sum(acc, axis=-1, keepdims=True)

    def bit_step(b, carry):
        bit = jnp.left_shift(jnp.int32(1), 31 - b)
        for rows, pieces in strips:
            t = thr_ref[rows, :]
            cand = t + bit
            cand_f = _key_to_f32(cand)
            cnt = count(pieces, lambda s, c0: s >= cand_f)
            thr_ref[rows, :] = jnp.where(cnt >= kf, cand, t)
        return carry

    thr_ref[...] = jnp.full(thr_ref.shape, INT_MIN, jnp.int32)
    lax.fori_loop(0, 32, bit_step, 0)

    jcut_ref[...] = jnp.full(jcut_ref.shape, 4096, jnp.int32)
    flag = jnp.int32(0)
    for rows, pieces in strips:
        t = _key_to_f32(thr_ref[rows, :])
        need = kf - count(pieces, lambda s, c0: s > t)
        n_eq = count(pieces, lambda s, c0: s == t)
        need_ref[rows, :] = jnp.broadcast_to(need, (rs, LANES))
        excess = (n_eq > need) & (t[:, :1] > MASKED)
        flag = jnp.maximum(flag, jnp.max(jnp.where(excess, 1, 0)))

    @pl.when(flag > 0)
    def _():
        lane = lax.broadcasted_iota(jnp.int32, (rs, LANES), 1)

        def col_step(b, carry):
            bit = jnp.left_shift(jnp.int32(1), 11 - b)
            for rows, pieces in strips:
                t, j = _key_to_f32(thr_ref[rows, :]), jcut_ref[rows, :]
                cand = j + bit
                f = count(pieces, lambda s, c0: (s == t) & (lane + c0 < cand))
                jcut_ref[rows, :] = jnp.where(f <= need_ref[rows, :], cand, j)
            return carry

        jcut_ref[...] = jnp.zeros(jcut_ref.shape, jnp.int32)
        lax.fori_loop(0, 12, col_step, 0)


def _dsa_kernel(aq_ref, iq_ref, ikw_ref, agate_ref, k_ref, v_ref, ik_ref, o_ref,
                key_ref, thr_ref, jcut_ref, need_ref, bias_ref, iwb_ref, qst_ref, qs_ref, acc_ref, m_ref, l_ref,
                *, nt, tq, kw, l_valid, qpos0, topk):
    nkb_max = key_ref.shape[0] // nt
    nkb_of = [_num_key_blocks(ti, tq, kw, l_valid, qpos0) for ti in range(nt)]
    rs = qst_ref.shape[1] // IDX_HEADS
    ns = tq // rs
    nc = kw // LANES
    kf = float(topk)

    def score_tile(ti, carry):
        r0 = pl.multiple_of(ti * tq, tq)
        iw = ikw_ref[pl.ds(r0, tq), IDX_DIM:IDX_DIM + IDX_HEADS] * (IDX_DIM ** -0.5)
        for h in range(IDX_HEADS):
            iwb_ref[h] = jnp.broadcast_to(iw[:, h:h + 1], (tq, LANES))
            for r in range(ns):
                qst_ref[r, h * rs:(h + 1) * rs, :] = iq_ref[pl.ds(r0 + r * rs, rs), h * IDX_DIM:(h + 1) * IDX_DIM]

        def score_block(kb, c2):
            ikb = ik_ref[0, pl.ds(pl.multiple_of(kb * kw, kw), kw), :]
            for r in range(ns):
                rel = lax.dot_general(qst_ref[r], ikb, _NT, preferred_element_type=F32)
                sc = jnp.zeros((rs, kw), F32)
                for h in range(IDX_HEADS):
                    w = _lane_tile(iwb_ref[h, r * rs:(r + 1) * rs, :], kw)
                    sc = sc + jnp.maximum(rel[h * rs:(h + 1) * rs], 0.0) * w
                allowed, _ = _allowed(ti, kb, r * rs, (rs, kw), tq, kw, l_valid, qpos0)
                key_ref[ti * nkb_max + kb, r * rs:(r + 1) * rs, :] = _masked_score(sc, allowed)
            return c2

        lax.fori_loop(0, _num_key_blocks(ti, tq, kw, l_valid, qpos0), score_block, 0)
        return carry

    def piece(ti, kb, s, c):
        return lambda: key_ref[ti * nkb_max + kb, s * rs:(s + 1) * rs, c * LANES:(c + 1) * LANES]

    strips = [(slice(ti * tq + s * rs, ti * tq + (s + 1) * rs),
               [(piece(ti, kb, s, c), kb * kw + c * LANES) for kb in range(nkb_of[ti]) for c in range(nc)])
              for ti in range(nt) for s in range(ns)]

    @pl.when(pl.program_id(1) == 0)
    def _():
        lax.fori_loop(0, nt, score_tile, 0)
        _topk_thresholds(strips, rs, kf, thr_ref, jcut_ref, need_ref)

    ti = pl.program_id(1)
    nkb = _num_key_blocks(ti, tq, kw, l_valid, qpos0)
    r0 = pl.multiple_of(ti * tq, tq)
    thr = _lane_tile(_key_to_f32(thr_ref[pl.ds(r0, tq), :]), kw)
    jcut = _lane_tile(jcut_ref[pl.ds(r0, tq), :], kw)

    def bias_block(kb, carry):
        allowed, cols = _allowed(ti, kb, 0, (tq, kw), tq, kw, l_valid, qpos0)
        bias_ref[kb] = _selection_bias(key_ref[ti * nkb_max + kb], cols, thr, jcut, allowed)
        return carry

    lax.fori_loop(0, nkb, bias_block, 0)

    grp = A_HEADS // A_KV_HEADS
    _softmax_init(A_KV_HEADS, grp * tq, m_ref, l_ref, acc_ref)
    scale = A_HEAD_DIM ** -0.5
    for g in range(A_KV_HEADS):
        gsl = slice(g * A_HEAD_DIM, (g + 1) * A_HEAD_DIM)
        for hh in range(grp):
            h = g * grp + hh
            qs_ref[hh * tq:(hh + 1) * tq, :] = aq_ref[:, h * A_HEAD_DIM:(h + 1) * A_HEAD_DIM]

        def att_block(kb, carry):
            c0 = pl.multiple_of(kb * kw, kw)
            kblk = k_ref[0, pl.ds(c0, kw), gsl]
            vblk = v_ref[0, pl.ds(c0, kw), gsl]
            s = lax.dot_general(qs_ref[...], kblk, _NT, preferred_element_type=F32)
            s = (s.reshape(grp, tq, kw) + bias_ref[kb][None]).reshape(grp * tq, kw)
            _softmax_step(s, scale, g, vblk, m_ref, l_ref, acc_ref)
            return carry

        lax.fori_loop(0, nkb, att_block, 0)

    for h in range(A_HEADS):
        g, hh = divmod(h, grp)
        sl = slice(h * A_HEAD_DIM, (h + 1) * A_HEAD_DIM)
        o = acc_ref[g, hh * tq:(hh + 1) * tq, :] / l_ref[g, hh * tq:(hh + 1) * tq, :]
        gate = agate_ref[:, sl]
        o_ref[:, sl] = (o * (gate * jax.nn.sigmoid(gate))).astype(BF16)


def _dsa(aq, iq, ikw, z, dst, k, v, ik, bsz, t, tq, kw, l_valid, qpos0):
    lp = k.shape[1]
    assert lp < 4096 and lp % kw == 0 and t % tq == 0
    nt = t // tq
    nkb_max = lp // kw
    topk = min(TOPK_MAX, l_valid // 4)
    grp = A_HEADS // A_KV_HEADS
    rs = min(tq, ROW_STRIP)
    go, gw = dst["agate"]
    row = lambda b, i: (b * nt + i, 0)
    whole = lambda b, i: (b, 0)
    batch = lambda b, i: (b, 0, 0)
    kern = functools.partial(_dsa_kernel, nt=nt, tq=tq, kw=kw, l_valid=l_valid, qpos0=qpos0, topk=topk)
    return pl.pallas_call(
        kern,
        grid=(bsz, nt),
        in_specs=[pl.BlockSpec((tq, A_WIDTH), row),
                  pl.BlockSpec((t, IDX_HEADS * IDX_DIM), whole),
                  pl.BlockSpec((t, LANES), whole),
                  pl.BlockSpec((tq, gw), lambda b, i: (b * nt + i, go // gw)),
                  pl.BlockSpec((1, lp, A_KV_WIDTH), batch),
                  pl.BlockSpec((1, lp, A_KV_WIDTH), batch),
                  pl.BlockSpec((1, lp, IDX_DIM), batch)],
        out_specs=pl.BlockSpec((tq, A_WIDTH), row),
        out_shape=jax.ShapeDtypeStruct((bsz * t, A_WIDTH), BF16),
        scratch_shapes=[pltpu.VMEM((nt * nkb_max, tq, kw), F32),
                        pltpu.VMEM((t, LANES), jnp.int32),
                        pltpu.VMEM((t, LANES), jnp.int32),
                        pltpu.VMEM((t, LANES), F32),
                        pltpu.VMEM((nkb_max, tq, kw), F32),
                        pltpu.VMEM((IDX_HEADS, tq, LANES), F32),
                        pltpu.VMEM((tq // rs, IDX_HEADS * rs, IDX_DIM), BF16),
                        pltpu.VMEM((grp * tq, A_HEAD_DIM), BF16),
                        pltpu.VMEM((A_KV_HEADS, grp * tq, A_HEAD_DIM), F32),
                        pltpu.VMEM((A_KV_HEADS, grp * tq, LANES), F32),
                        pltpu.VMEM((A_KV_HEADS, grp * tq, LANES), F32)],
        compiler_params=pltpu.CompilerParams(dimension_semantics=("arbitrary", "arbitrary"),
                                             vmem_limit_bytes=VMEM_LIMIT),
        name="dsa",
    )(aq, iq, ikw, z, k, v, ik)


SEL_BATCHES = 8
SEL_CHUNK = 512


def _dsa_sel_kernel(iq_ref, ikw_ref, cik_ref, nik_ref, bias_ref, key_ref, thr_ref, jcut_ref, need_ref,
                    *, gb, t, past, qpos0, topk):
    lp = past + LANES
    nc = lp // LANES

    def allowed_cols(col0, width):
        rows = lax.broadcasted_iota(jnp.int32, (t, width), 0)
        cols = lax.broadcasted_iota(jnp.int32, (t, width), 1) + col0
        return ((cols // CHUNK) <= (qpos0 + rows) // CHUNK) & (cols < past + t)

    def score_batch(j, carry):
        r0 = pl.multiple_of(j * t, t)
        qst = jnp.concatenate([iq_ref[pl.ds(r0, t), h * IDX_DIM:(h + 1) * IDX_DIM] for h in range(IDX_HEADS)], axis=0)
        iw = ikw_ref[pl.ds(r0, t), IDX_DIM:IDX_DIM + IDX_HEADS] * (IDX_DIM ** -0.5)
        iwb = [jnp.broadcast_to(iw[:, h:h + 1], (t, LANES)) for h in range(IDX_HEADS)]

        def head_sum(rel):
            width = rel.shape[1]
            sc = jnp.zeros((t, width), F32)
            for h in range(IDX_HEADS):
                sc = sc + jnp.maximum(rel[h * t:(h + 1) * t], 0.0) * _lane_tile(iwb[h], width)
            return sc

        for c in range(past // SEL_CHUNK):
            ikt = cik_ref[j, :, c * SEL_CHUNK:(c + 1) * SEL_CHUNK].astype(BF16)
            sc = head_sum(jnp.dot(qst, ikt, preferred_element_type=F32))
            key_ref[j, :, c * SEL_CHUNK:(c + 1) * SEL_CHUNK] = _masked_score(sc, allowed_cols(c * SEL_CHUNK, SEL_CHUNK))
        ik = jnp.concatenate([nik_ref[pl.ds(r0, t), :], jnp.zeros((LANES - t, IDX_DIM), BF16)], axis=0)
        sc = head_sum(lax.dot_general(qst, ik, _NT, preferred_element_type=F32))
        key_ref[j, :, past:] = _masked_score(sc, allowed_cols(past, LANES))
        return carry

    lax.fori_loop(0, gb, score_batch, 0)

    def piece(j, c):
        return lambda: key_ref[j, :, c * LANES:(c + 1) * LANES]

    strips = [(slice(j * t, (j + 1) * t), [(piece(j, c), c * LANES) for c in range(nc)]) for j in range(gb)]
    _topk_thresholds(strips, t, float(topk), thr_ref, jcut_ref, need_ref)

    def bias_batch(j, carry):
        r0 = pl.multiple_of(j * t, t)
        cols = lax.broadcasted_iota(jnp.int32, (t, lp), 1)
        bias_ref[j] = _selection_bias(key_ref[j], cols, _lane_tile(_key_to_f32(thr_ref[pl.ds(r0, t), :]), lp),
                                      _lane_tile(jcut_ref[pl.ds(r0, t), :], lp), allowed_cols(0, lp))
        return carry

    lax.fori_loop(0, gb, bias_batch, 0)


def _dsa_sel(iq, ikw, cache_ik_t, layer, new_ik, bsz, t, qpos0):
    past = cache_ik_t.shape[3]
    lp = past + LANES
    gb = next(g for g in (SEL_BATCHES, 4, 2, 1) if bsz % g == 0)
    assert past % SEL_CHUNK == 0 and t <= LANES and lp < 4096
    topk = min(TOPK_MAX, (past + t) // 4)
    rows = lambda i: (i, 0)
    kern = functools.partial(_dsa_sel_kernel, gb=gb, t=t, past=past, qpos0=qpos0, topk=topk)
    return pl.pallas_call(
        kern,
        grid=(bsz // gb,),
        in_specs=[pl.BlockSpec((gb * t, IDX_HEADS * IDX_DIM), rows),
                  pl.BlockSpec((gb * t, LANES), rows),
                  pl.BlockSpec((None, gb, IDX_DIM, past), lambda i: (layer, i, 0, 0)),
                  pl.BlockSpec((gb * t, IDX_DIM), rows)],
        out_specs=pl.BlockSpec((gb, t, lp), lambda i: (i, 0, 0)),
        out_shape=jax.ShapeDtypeStruct((bsz, t, lp), F32),
        scratch_shapes=[pltpu.VMEM((gb, t, lp), F32),
                        pltpu.VMEM((gb * t, LANES), jnp.int32),
                        pltpu.VMEM((gb * t, LANES), jnp.int32),
                        pltpu.VMEM((gb * t, LANES), F32)],
        compiler_params=pltpu.CompilerParams(dimension_semantics=("arbitrary",), vmem_limit_bytes=VMEM_LIMIT),
        name="dsa_sel",
    )(iq, ikw, cache_ik_t, new_ik)


def _dsa_att_kernel(aq_ref, agate_ref, bias_ref, ck_ref, cv_ref, nk_ref, nv_ref, o_ref, *, t, past):
    grp = A_HEADS // A_KV_HEADS
    scale = A_HEAD_DIM ** -0.5
    pad = jnp.zeros((LANES - t, A_HEAD_DIM), BF16)
    b1, b2 = bias_ref[0, :, :past], bias_ref[0, :, past:]
    for g in range(A_KV_HEADS):
        gsl = slice(g * A_HEAD_DIM, (g + 1) * A_HEAD_DIM)
        kc = ck_ref[0, pl.ds(g, past, stride=A_KV_HEADS), :].astype(BF16)
        vc = cv_ref[0, pl.ds(g, past, stride=A_KV_HEADS), :].astype(BF16)
        kn = jnp.concatenate([nk_ref[:, gsl], pad], axis=0)
        vn = jnp.concatenate([nv_ref[:, gsl], pad], axis=0)
        qs = jnp.concatenate([aq_ref[:, (g * grp + hh) * A_HEAD_DIM:(g * grp + hh + 1) * A_HEAD_DIM]
                              for hh in range(grp)], axis=0)

        def scores(keys, bias):
            s = lax.dot_general(qs, keys, _NT, preferred_element_type=F32) * scale
            return (s.reshape(grp, t, s.shape[1]) + bias[None]).reshape(grp * t, s.shape[1])

        s1, s2 = scores(kc, b1), scores(kn, b2)
        m = jnp.maximum(jnp.max(s1, axis=-1, keepdims=True), jnp.max(s2, axis=-1, keepdims=True))
        e1, e2 = jnp.exp(s1 - m), jnp.exp(s2 - m)
        l = jnp.sum(e1, axis=-1, keepdims=True) + jnp.sum(e2, axis=-1, keepdims=True)
        o = (jnp.dot(e1.astype(BF16), vc, preferred_element_type=F32)
             + jnp.dot(e2.astype(BF16), vn, preferred_element_type=F32)) / l
        for hh in range(grp):
            sl = slice((g * grp + hh) * A_HEAD_DIM, (g * grp + hh + 1) * A_HEAD_DIM)
            gate = agate_ref[:, sl]
            o_ref[:, sl] = (o[hh * t:(hh + 1) * t] * (gate * jax.nn.sigmoid(gate))).astype(BF16)


def _dsa_att(aq, z, dst, bias, cache_k, cache_v, new_k, new_v, bsz, t):
    past = cache_k.shape[1] // A_KV_HEADS
    lp = bias.shape[2]
    go, gw = dst["agate"]
    row = lambda b: (b, 0)
    batch = lambda b: (b, 0, 0)
    return pl.pallas_call(
        functools.partial(_dsa_att_kernel, t=t, past=past),
        grid=(bsz,),
        in_specs=[pl.BlockSpec((t, A_WIDTH), row),
                  pl.BlockSpec((t, gw), lambda b: (b, go // gw)),
                  pl.BlockSpec((1, t, lp), batch),
                  pl.BlockSpec((1, past * A_KV_HEADS, A_HEAD_DIM), batch),
                  pl.BlockSpec((1, past * A_KV_HEADS, A_HEAD_DIM), batch),
                  pl.BlockSpec((t, A_KV_WIDTH), row),
                  pl.BlockSpec((t, A_KV_WIDTH), row)],
        out_specs=pl.BlockSpec((t, A_WIDTH), row),
        out_shape=jax.ShapeDtypeStruct((bsz * t, A_WIDTH), BF16),
        compiler_params=pltpu.CompilerParams(dimension_semantics=("arbitrary",), vmem_limit_bytes=VMEM_LIMIT),
        name="dsa_att",
    )(aq, z, bias, cache_k, cache_v, new_k, new_v)


def _mla_kernel(q_ref, bgate_ref, k_ref, v_ref, o_ref, acc_ref, m_ref, l_ref, *, tq, kw, l_valid, qpos0):
    ti = pl.program_id(1)
    nkb = _num_key_blocks(ti, tq, kw, l_valid, qpos0)
    scale = (NOPE_DIM + ROPE_DIM) ** -0.5
    _softmax_init(B_HEADS, tq, m_ref, l_ref, acc_ref)

    def att_block(kb, masked):
        r0 = pl.multiple_of(kb * kw, kw)
        if masked:
            allowed, _ = _allowed(ti, kb, 0, (tq, kw), tq, kw, l_valid, qpos0)
            bias = jnp.where(allowed, 0.0, NEG)
        for h in range(B_HEADS):
            qsl = slice(h * B_QK_PAD, (h + 1) * B_QK_PAD)
            s = lax.dot_general(q_ref[:, qsl], k_ref[0, pl.ds(r0, kw), qsl], _NT, preferred_element_type=F32)
            if masked:
                s = s + bias
            _softmax_step(s, scale, h, v_ref[0, pl.ds(r0, kw), h * V_DIM:(h + 1) * V_DIM], m_ref, l_ref, acc_ref)

    first_visible = jnp.minimum(((qpos0 + ti * tq) // CHUNK + 1) * CHUNK, l_valid)
    n_full = first_visible // kw
    lax.fori_loop(0, n_full, lambda kb, c: (att_block(kb, False), c)[1], 0)
    lax.fori_loop(n_full, nkb, lambda kb, c: (att_block(kb, True), c)[1], 0)

    for h in range(B_HEADS):
        sl = slice(h * V_DIM, (h + 1) * V_DIM)
        o = acc_ref[h] / l_ref[h]
        gate = bgate_ref[:, sl]
        o_ref[:, sl] = (o * (gate * jax.nn.sigmoid(gate))).astype(BF16)


def _mla(qcat, z, dst, kcat, v, bsz, t, tq, kw, l_valid, qpos0):
    lp = kcat.shape[1]
    assert lp % kw == 0 and t % tq == 0
    nt = t // tq
    go, gw = dst["bgate"]
    row = lambda b, i: (b * nt + i, 0)
    batch = lambda b, i: (b, 0, 0)
    kern = functools.partial(_mla_kernel, tq=tq, kw=kw, l_valid=l_valid, qpos0=qpos0)
    return pl.pallas_call(
        kern,
        grid=(bsz, nt),
        in_specs=[pl.BlockSpec((tq, B_HEADS * B_QK_PAD), row),
                  pl.BlockSpec((tq, gw), lambda b, i: (b * nt + i, go // gw)),
                  pl.BlockSpec((1, lp, B_HEADS * B_QK_PAD), batch),
                  pl.BlockSpec((1, lp, B_WIDTH), batch)],
        out_specs=pl.BlockSpec((tq, B_WIDTH), row),
        out_shape=jax.ShapeDtypeStruct((bsz * t, B_WIDTH), BF16),
        scratch_shapes=[pltpu.VMEM((B_HEADS, tq, V_DIM), F32),
                        pltpu.VMEM((B_HEADS, tq, LANES), F32),
                        pltpu.VMEM((B_HEADS, tq, LANES), F32)],
        compiler_params=pltpu.CompilerParams(dimension_semantics=("arbitrary", "arbitrary"),
                                             vmem_limit_bytes=VMEM_LIMIT),
        name="mla",
    )(qcat, z, kcat, v)


def _mla_lat_kernel(q_ref, bgate_ref, cc_ref, ckr_ref, nc_ref, nkr_ref, wukt_ref, wuv_ref, o_ref,
                    *, t, past, qpos0):
    rows_n = B_HEADS * t
    ql, qr = [], []
    for h in range(B_HEADS):
        qn = q_ref[:, h * B_QK_PAD: h * B_QK_PAD + NOPE_DIM]
        ql.append(jnp.dot(qn, wukt_ref[h], preferred_element_type=F32).astype(BF16))
        qr.append(q_ref[:, h * B_QK_PAD + NOPE_DIM: h * B_QK_PAD + NOPE_DIM + ROPE_DIM])
    ql = jnp.concatenate(ql, axis=0)
    qr = jnp.concatenate(qr, axis=0)
    cc = cc_ref[0].astype(BF16)
    ckr = ckr_ref[0].astype(BF16)
    nc, nkr = nc_ref[...], nkr_ref[...]
    scale = (NOPE_DIM + ROPE_DIM) ** -0.5

    def scores(lat, rope, ncols, col0):
        s = lax.dot_general(ql, lat, _NT, preferred_element_type=F32)
        s = (s + lax.dot_general(qr, rope, _NT, preferred_element_type=F32)) * scale
        rows = lax.broadcasted_iota(jnp.int32, (rows_n, ncols), 0)
        cols = lax.broadcasted_iota(jnp.int32, (rows_n, ncols), 1) + col0
        qchunk = (qpos0 + lax.rem(rows, t)) // CHUNK
        return jnp.where((cols // CHUNK) <= qchunk, s, NEG)

    s1 = scores(cc, ckr, past, 0)
    s2 = scores(nc, nkr, t, past)
    m = jnp.maximum(jnp.max(s1, axis=-1, keepdims=True), jnp.max(s2, axis=-1, keepdims=True))
    e1, e2 = jnp.exp(s1 - m), jnp.exp(s2 - m)
    l = jnp.sum(e1, axis=-1, keepdims=True) + jnp.sum(e2, axis=-1, keepdims=True)
    ol = (jnp.dot(e1.astype(BF16), cc, preferred_element_type=F32)
          + jnp.dot(e2.astype(BF16), nc, preferred_element_type=F32)) / l
    ol = ol.astype(BF16)
    for h in range(B_HEADS):
        sl = slice(h * V_DIM, (h + 1) * V_DIM)
        o = jnp.dot(ol[h * t:(h + 1) * t], wuv_ref[:, sl], preferred_element_type=F32)
        gate = bgate_ref[:, sl]
        o_ref[:, sl] = (o * (gate * jax.nn.sigmoid(gate))).astype(BF16)


def _mla_lat(qcat, z, dst, cache_ckv, cache_kr, layer, new_ckv, new_kr, wukt, wuv, bsz, t, qpos0):
    past = cache_ckv.shape[2]
    go, gw = dst["bgate"]
    row = lambda b: (b, 0)
    batch = lambda b: (b, 0, 0)
    const = lambda a: pl.BlockSpec(a.shape, lambda b: (0,) * a.ndim)
    kern = functools.partial(_mla_lat_kernel, t=t, past=past, qpos0=qpos0)
    return pl.pallas_call(
        kern,
        grid=(bsz,),
        in_specs=[pl.BlockSpec((t, B_HEADS * B_QK_PAD), row),
                  pl.BlockSpec((t, gw), lambda b: (b, go // gw)),
                  pl.BlockSpec((None, 1, past, KV_LORA), lambda b: (layer, b, 0, 0)),
                  pl.BlockSpec((None, 1, past, ROPE_DIM), lambda b: (layer, b, 0, 0)),
                  pl.BlockSpec((t, KV_LORA), row),
                  pl.BlockSpec((t, ROPE_DIM), row),
                  const(wukt), const(wuv)],
        out_specs=pl.BlockSpec((t, B_WIDTH), row),
        out_shape=jax.ShapeDtypeStruct((bsz * t, B_WIDTH), BF16),
        compiler_params=pltpu.CompilerParams(dimension_semantics=("arbitrary",), vmem_limit_bytes=VMEM_LIMIT),
        name="mla_lat",
    )(qcat, z, cache_ckv, cache_kr, new_ckv, new_kr, wukt, wuv)


def _out_kernel(ga_ref, gb_ref, mg_ref, x_ref, woa_ref, wob_ref, wout_ref, fg_ref, o_ref, *, d, final):
    ya = jnp.dot(ga_ref[...], woa_ref[...], preferred_element_type=F32)
    yb = jnp.dot(gb_ref[...], wob_ref[...], preferred_element_type=F32)
    m = jax.nn.sigmoid(mg_ref[:, :d]) * ya + jax.nn.sigmoid(mg_ref[:, d:]) * yb
    h = x_ref[...] + jnp.dot(m.astype(BF16), wout_ref[...], preferred_element_type=F32)
    o_ref[...] = _rms(h, fg_ref[...]) if final else h


def _out(ga, gb, z, dst, x, woa, wob, wout, fg, final):
    m, d = x.shape
    tm = min(m, 256)
    assert m % tm == 0 and dst["mg"][0] == 0
    row = lambda i: (i, 0)
    const = lambda a: pl.BlockSpec(a.shape, lambda i: (0, 0), pipeline_mode=pl.Buffered(1))
    return pl.pallas_call(
        functools.partial(_out_kernel, d=d, final=final),
        grid=(m // tm,),
        in_specs=[pl.BlockSpec((tm, A_WIDTH), row), pl.BlockSpec((tm, B_WIDTH), row),
                  pl.BlockSpec((tm, 2 * d), row), pl.BlockSpec((tm, d), row),
                  const(woa), const(wob), const(wout), const(fg)],
        out_specs=pl.BlockSpec((tm, d), row),
        out_shape=jax.ShapeDtypeStruct((m, d), F32),
        compiler_params=pltpu.CompilerParams(dimension_semantics=("arbitrary",), vmem_limit_bytes=VMEM_LIMIT),
        name="out",
    )(ga, gb, z, x, woa, wob, wout, fg)


def _all_keys(cache, new, bsz, t, lp):
    new = new.reshape(bsz, t, new.shape[-1])
    if cache is None and lp == t:
        return new
    parts = ([] if cache is None else [cache.reshape(bsz, cache.shape[1], -1).astype(BF16)]) + [new]
    kk = jnp.concatenate(parts, axis=1)
    return jnp.pad(kk, ((0, 0), (0, lp - kk.shape[1]), (0, 0)))


def _trunk(x, caches, final_g, weights):
    bsz, t, d = x.shape
    depth = weights["w_in"].shape[0]
    past = 0 if caches is None else caches[0].shape[2]
    l_valid = past + t
    tq = min(t, QUERY_TILE)
    kw = KEY_BLOCK if t >= QUERY_TILE else -(-l_valid // LANES) * LANES
    lp = -(-l_valid // kw) * kw
    _, dst, _ = _layout(d)
    pos = past + jnp.arange(t, dtype=jnp.int32)
    t_scale = jnp.concatenate([jnp.full((t, IDX_HEADS), IDX_HEADS ** -0.5, F32),
                               jnp.ones((t, LANES - IDX_DIM - IDX_HEADS), F32)], axis=1)
    tabs = [jnp.stack(_rope_tables(pos, A_ROT, A_HEAD_DIM, 1)),
            jnp.stack(_rope_tables(pos, IDX_ROT, IDX_DIM, 2)),
            jnp.stack(_rope_tables(pos, IDX_ROT, IDX_DIM, 1, extra=t_scale)),
            jnp.stack(_rope_tables(pos, ROPE_DIM, ROPE_DIM, 2))]
    h = x.reshape(bsz * t, d)
    new_rows = []
    for l in range(depth):
        w = {k: v[l] for k, v in weights.items()}
        z = _proj(h, w["norm_g"], w["w_in"])
        aq, iq, ak, ikw, krp, ckv, qcat, kb, vb, ikb, ckvb, krb = _prep(
            z, dst, bsz, t, tabs, w["q_norm_g"], w["kv_norm_g"], w["w_uq"])
        o, wd = dst["av"]
        av = z[:, o:o + wd]
        c = (None,) * 5 if caches is None else tuple(cc[l] for cc in caches)
        if caches is not None and t < QUERY_TILE:
            bias = _dsa_sel(iq, ikw, jnp.swapaxes(caches[2], 2, 3), l, ikb, bsz, t, past)
            interleaved = lambda a: a.reshape(bsz, past * A_KV_HEADS, A_HEAD_DIM)
            ga = _dsa_att(aq, z, dst, bias, interleaved(c[0]), interleaved(c[1]), kb, vb, bsz, t)
            gb = _mla_lat(qcat, z, dst, caches[3], caches[4], l, ckvb, krb, w["w_ukt"], w["w_uv"], bsz, t, past)
        else:
            k_all = _all_keys(c[0], kb, bsz, t, lp)
            v_all = _all_keys(c[1], vb, bsz, t, lp)
            ik_all = _all_keys(c[2], ikb, bsz, t, lp)
            ga = _dsa(aq, iq, ikw, z, dst, k_all, v_all, ik_all, bsz, t, tq, kw, l_valid, past)
            ckv_all = _all_keys(c[3], ckvb, bsz, t, lp)
            kr_all = _all_keys(c[4], krb, bsz, t, lp)
            kcat, vv = _kvup(ckv_all.reshape(bsz * lp, KV_LORA), kr_all.reshape(bsz * lp, ROPE_DIM), w["w_ukv"])
            gb = _mla(qcat, z, dst, kcat.reshape(bsz, lp, -1), vv.reshape(bsz, lp, -1), bsz, t, tq, kw, l_valid, past)
        h = _out(ga, gb, z, dst, h, w["w_o_a"], w["w_o_b"], w["w_out"], final_g, final=(l == depth - 1))
        new_rows.append((ak.reshape(bsz, t, A_KV_HEADS, A_HEAD_DIM), av.reshape(bsz, t, A_KV_HEADS, A_HEAD_DIM),
                         ikw[:, :IDX_DIM].reshape(bsz, t, IDX_DIM), ckv.reshape(bsz, t, KV_LORA),
                         krp[:, :ROPE_DIM].reshape(bsz, t, ROPE_DIM)))
    stacked = [jnp.stack([r[i] for r in new_rows], axis=0) for i in range(5)]
    return h.reshape(bsz, t, d), stacked


def kernel(x_prompt, x_sample, cache_a_k, cache_a_v, cache_idx_k, cache_mla_ckv, cache_mla_krope,
           norm_g, w_in, w_uq, q_norm_g, kv_norm_g, w_uk, w_uv, w_o_a, w_o_b, w_out, final_g):
    depth, d, _ = w_in.shape
    weights = {
        "norm_g": norm_g.reshape(depth, 1, d),
        "w_in": jnp.stack([_regroup_w_in(jnp.swapaxes(w_in[l], 0, 1), d) for l in range(depth)]),
        "w_uq": jnp.concatenate([w_uq[..., :NOPE_DIM].reshape(depth, Q_LORA, B_HEADS * NOPE_DIM),
                                 w_uq[..., NOPE_DIM:].reshape(depth, Q_LORA, B_HEADS * ROPE_DIM)], axis=-1).astype(BF16),
        "q_norm_g": q_norm_g.reshape(depth, 1, Q_LORA),
        "kv_norm_g": kv_norm_g.reshape(depth, 1, KV_LORA),
        "w_ukv": jnp.concatenate([w_uk.reshape(depth, KV_LORA, B_HEADS * NOPE_DIM),
                                  w_uv.reshape(depth, KV_LORA, B_WIDTH)], axis=-1).astype(BF16),
        "w_ukt": jnp.transpose(w_uk, (0, 2, 3, 1)).astype(BF16),
        "w_uv": w_uv.reshape(depth, KV_LORA, B_WIDTH).astype(BF16),
        "w_o_a": w_o_a.astype(BF16), "w_o_b": w_o_b.astype(BF16), "w_out": w_out.astype(BF16),
    }
    fg = final_g.reshape(1, d)
    y_p, (pk, pv, pik, pckv, pkr) = _trunk(x_prompt, None, fg, weights)
    caches = (cache_a_k, cache_a_v, cache_idx_k, cache_mla_ckv, cache_mla_krope)
    y_s, (sk, sv, sik, sckv, skr) = _trunk(x_sample, caches, fg, weights)
    return (y_p, y_s, pk, pv, pik, pckv, pkr, sk, sv, sik, sckv, skr)
```

```python
import functools

import jax
import jax.numpy as jnp
from jax import lax
from jax.experimental import pallas as pl
from jax.experimental.pallas import tpu as pltpu

F32 = jnp.float32
BF16 = jnp.bfloat16

CHUNK = 64
ROPE_THETA = 500000.0
EPS = 1e-6

A_HEADS = 8
A_KV_HEADS = 2
A_HEAD_DIM = 128
A_ROT = A_HEAD_DIM // 4
A_WIDTH = A_HEADS * A_HEAD_DIM
A_KV_WIDTH = A_KV_HEADS * A_HEAD_DIM
IDX_HEADS = 16
IDX_DIM = 64
IDX_ROT = IDX_DIM // 4
TOPK_MAX = 256

B_HEADS = 8
Q_LORA = 512
KV_LORA = 512
NOPE_DIM = 128
ROPE_DIM = 64
V_DIM = 128
B_WIDTH = B_HEADS * V_DIM
B_QK_PAD = 256

LANES = 128
KEY_BLOCK = 256
QUERY_TILE = 256
MLA_QUERY_TILE = 512
ROW_STRIP = 64
INT_MIN = -(2 ** 31)
LOG2_E = 1.4426950408889634
NEG = -1e30
MASKED = -1e38
VMEM_LIMIT = 56 * 1024 * 1024

_SEGS = (("aq", A_WIDTH), ("ak", A_KV_WIDTH), ("av", A_KV_WIDTH), ("agate", A_WIDTH),
         ("iq", IDX_HEADS * IDX_DIM), ("ik", IDX_DIM), ("iw", IDX_HEADS),
         ("cq", Q_LORA), ("ckv", KV_LORA), ("kr", ROPE_DIM), ("bgate", B_WIDTH), ("mg", None))


def _layout(d_model):
    src, off = {}, 0
    for name, w in _SEGS:
        w = 2 * d_model if w is None else w
        src[name] = (off, w)
        off += w
    order = (("mg", 2 * d_model), ("aq", A_WIDTH), ("agate", A_WIDTH), ("iq", IDX_HEADS * IDX_DIM),
             ("bgate", B_WIDTH), ("cq", Q_LORA), ("ckv", KV_LORA), ("ak", A_KV_WIDTH), ("av", A_KV_WIDTH),
             ("ikw", LANES), ("krp", LANES))
    dst, off = {}, 0
    for name, w in order:
        assert off % w == 0
        dst[name] = (off, w)
        off += w
    return src, dst, off


def _regroup_kernel(w_ref, o_ref, *, src, dst):
    cols = w_ref.shape[1]

    def piece(name):
        o, w = src[name]
        return w_ref[o:o + w, :].astype(BF16)

    def zeros(n):
        return jnp.zeros((n, cols), BF16)

    for name, (o, w) in dst.items():
        if name == "ikw":
            val = jnp.concatenate([piece("ik"), piece("iw"), zeros(LANES - IDX_DIM - IDX_HEADS)], axis=0)
        elif name == "krp":
            val = jnp.concatenate([piece("kr"), zeros(LANES - ROPE_DIM)], axis=0)
        else:
            val = piece(name)
        o_ref[o:o + w, :] = val


def _regroup_w_in(w_in_t, d_model):
    src, dst, total = _layout(d_model)
    tc = 256
    assert d_model % tc == 0
    return pl.pallas_call(
        functools.partial(_regroup_kernel, src=src, dst=dst),
        grid=(d_model // tc,),
        in_specs=[pl.BlockSpec((w_in_t.shape[0], tc), lambda i: (0, i))],
        out_specs=pl.BlockSpec((total, tc), lambda i: (0, i)),
        out_shape=jax.ShapeDtypeStruct((total, d_model), BF16),
        compiler_params=pltpu.CompilerParams(dimension_semantics=("arbitrary",), vmem_limit_bytes=VMEM_LIMIT),
        name="regroup",
    )(w_in_t)


def _rope_tables(pos, rot, head, heads_per_vreg, extra=None):
    half = rot // 2
    inv_freq = ROPE_THETA ** (-jnp.arange(half, dtype=F32) * (2.0 / rot))
    ang = pos.astype(F32)[:, None] * inv_freq[None, :]
    cos, sin = jnp.cos(ang), jnp.sin(ang)
    t = pos.shape[0]
    one = jnp.ones((t, head - rot), F32)
    zero = lambda n: jnp.zeros((t, n), F32)
    c = jnp.concatenate([cos, cos, one], axis=1)
    sa = jnp.concatenate([-sin, zero(head - half)], axis=1)
    sb = jnp.concatenate([zero(half), sin, zero(head - rot)], axis=1)
    c, sa, sb = (jnp.tile(a, (1, heads_per_vreg)) for a in (c, sa, sb))
    if extra is not None:
        c = jnp.concatenate([c, extra], axis=1)
        sa = jnp.concatenate([sa, jnp.zeros_like(extra)], axis=1)
        sb = jnp.concatenate([sb, jnp.zeros_like(extra)], axis=1)
    assert c.shape[1] == LANES
    return c, sa, sb


def _rope(x, c, sa, sb, half):
    up = pltpu.roll(x, LANES - half, 1)
    dn = pltpu.roll(x, half, 1)
    return x * c + up * sa + dn * sb


def _rms(x, g):
    ms = jnp.mean(x * x, axis=-1, keepdims=True)
    return (x * lax.rsqrt(ms + EPS)) * g


def _lane_tile(x, width):
    return x if width == LANES else jnp.concatenate([x] * (width // LANES), axis=1)


_NT = (((1,), (1,)), ((), ()))


def _proj_kernel(x_ref, g_ref, w_ref, z_ref, xn_ref):
    @pl.when(pl.program_id(1) == 0)
    def _():
        xn_ref[...] = _rms(x_ref[...], g_ref[...]).astype(BF16)
    z_ref[...] = lax.dot_general(xn_ref[...], w_ref[...], _NT, preferred_element_type=F32)


def _proj(x, g, w):
    m, d = x.shape
    n = w.shape[0]
    tm = min(m, 1024)
    tn = 768
    assert m % tm == 0 and n % tn == 0
    return pl.pallas_call(
        _proj_kernel,
        grid=(m // tm, n // tn),
        in_specs=[pl.BlockSpec((tm, d), lambda i, j: (i, 0)),
                  pl.BlockSpec((1, d), lambda i, j: (0, 0)),
                  pl.BlockSpec((tn, d), lambda i, j: (j, 0))],
        out_specs=pl.BlockSpec((tm, tn), lambda i, j: (i, j)),
        out_shape=jax.ShapeDtypeStruct((m, n), F32),
        scratch_shapes=[pltpu.VMEM((tm, d), BF16)],
        compiler_params=pltpu.CompilerParams(dimension_semantics=("arbitrary", "arbitrary"),
                                             vmem_limit_bytes=VMEM_LIMIT),
        name="proj",
    )(x, g, w)


def _prep_kernel(aq_ref, iq_ref, cq_ref, ckv_ref, ak_ref, av_ref, ikw_ref, krp_ref,
                 ta_ref, ti_ref, tk_ref, tr_ref, qg_ref, kvg_ref, wuq_ref,
                 aq_o, iq_o, ak_o, ikw_o, krp_o, ckv_o, qcat_o, kb_o, vb_o, ikb_o, ckvb_o, krb_o):
    ca, saa, sba = ta_ref[0], ta_ref[1], ta_ref[2]
    ci, sai, sbi = ti_ref[0], ti_ref[1], ti_ref[2]
    ck, sak, sbk = tk_ref[0], tk_ref[1], tk_ref[2]
    cr, sar, sbr = tr_ref[0], tr_ref[1], tr_ref[2]
    tm = aq_ref.shape[0]
    for h in range(A_HEADS):
        sl = slice(h * LANES, (h + 1) * LANES)
        aq_o[:, sl] = _rope(aq_ref[:, sl], ca, saa, sba, A_ROT // 2).astype(BF16)
    for h in range(A_KV_HEADS):
        sl = slice(h * LANES, (h + 1) * LANES)
        ak = _rope(ak_ref[:, sl], ca, saa, sba, A_ROT // 2)
        ak_o[:, sl] = ak
        kb_o[:, sl] = ak.astype(BF16)
    vb_o[...] = av_ref[...].astype(BF16)
    for v in range(IDX_HEADS * IDX_DIM // LANES):
        sl = slice(v * LANES, (v + 1) * LANES)
        iq_o[:, sl] = _rope(iq_ref[:, sl], ci, sai, sbi, IDX_ROT // 2).astype(BF16)
    ikw = _rope(ikw_ref[...], ck, sak, sbk, IDX_ROT // 2)
    ikw_o[...] = ikw
    ikb_o[...] = ikw[:, :IDX_DIM].astype(BF16)
    krp = _rope(krp_ref[...], cr, sar, sbr, ROPE_DIM // 2)
    krp_o[...] = krp
    krb_o[...] = krp[:, :ROPE_DIM].astype(BF16)
    ckv = _rms(ckv_ref[...], kvg_ref[...])
    ckv_o[...] = ckv
    ckvb_o[...] = ckv.astype(BF16)
    cqn = _rms(cq_ref[...], qg_ref[...]).astype(BF16)
    qb = jnp.dot(cqn, wuq_ref[...], preferred_element_type=F32)
    nope = B_HEADS * NOPE_DIM
    low = lax.broadcasted_iota(jnp.int32, (tm, LANES), 1) < ROPE_DIM
    for v in range(B_HEADS * ROPE_DIM // LANES):
        r = _rope(qb[:, nope + v * LANES: nope + (v + 1) * LANES], cr, sar, sbr, ROPE_DIM // 2)
        halves = (jnp.where(low, r, 0.0), jnp.where(low, pltpu.roll(r, ROPE_DIM, 1), 0.0))
        for j in range(2):
            h = 2 * v + j
            qcat_o[:, h * B_QK_PAD: h * B_QK_PAD + NOPE_DIM] = qb[:, h * NOPE_DIM:(h + 1) * NOPE_DIM].astype(BF16)
            qcat_o[:, h * B_QK_PAD + NOPE_DIM:(h + 1) * B_QK_PAD] = halves[j].astype(BF16)


def _prep(z, dst, bsz, t, tabs, q_norm_g, kv_norm_g, wuq):
    m = bsz * t
    tm = min(t, 256)
    nt = t // tm
    assert t % tm == 0

    def zspec(name):
        o, w = dst[name]
        return pl.BlockSpec((tm, w), lambda b, i, blk=o // w: (b * nt + i, blk))

    def tspec():
        return pl.BlockSpec((3, tm, LANES), lambda b, i: (0, i, 0))

    def ospec(w):
        return pl.BlockSpec((tm, w), lambda b, i: (b * nt + i, 0))

    def full(a):
        return pl.BlockSpec(a.shape, lambda b, i: (0,) * a.ndim)

    outs = [(A_WIDTH, BF16), (IDX_HEADS * IDX_DIM, BF16), (A_KV_WIDTH, F32), (LANES, F32), (LANES, F32),
            (KV_LORA, F32), (B_HEADS * B_QK_PAD, BF16),
            (A_KV_WIDTH, BF16), (A_KV_WIDTH, BF16), (IDX_DIM, BF16), (KV_LORA, BF16), (ROPE_DIM, BF16)]
    return pl.pallas_call(
        _prep_kernel,
        grid=(bsz, nt),
        in_specs=[zspec("aq"), zspec("iq"), zspec("cq"), zspec("ckv"), zspec("ak"), zspec("av"), zspec("ikw"),
                  zspec("krp"), tspec(), tspec(), tspec(), tspec(), full(q_norm_g), full(kv_norm_g), full(wuq)],
        out_specs=[ospec(w) for w, _ in outs],
        out_shape=[jax.ShapeDtypeStruct((m, w), dt) for w, dt in outs],
        compiler_params=pltpu.CompilerParams(dimension_semantics=("arbitrary", "arbitrary"),
                                             vmem_limit_bytes=VMEM_LIMIT),
        name="prep",
    )(z, z, z, z, z, z, z, z, *tabs, q_norm_g, kv_norm_g, wuq)


def _kvup_kernel(c_ref, kr_ref, w_ref, kcat_ref, v_ref):
    r = jnp.dot(c_ref[...], w_ref[...], preferred_element_type=F32)
    n = B_HEADS * NOPE_DIM
    tail = jnp.concatenate([kr_ref[...], jnp.zeros((kr_ref.shape[0], B_QK_PAD - NOPE_DIM - ROPE_DIM), BF16)], axis=1)
    for h in range(B_HEADS):
        kcat_ref[:, h * B_QK_PAD: h * B_QK_PAD + NOPE_DIM] = r[:, h * NOPE_DIM:(h + 1) * NOPE_DIM].astype(BF16)
        kcat_ref[:, h * B_QK_PAD + NOPE_DIM:(h + 1) * B_QK_PAD] = tail
    v_ref[...] = r[:, n:].astype(BF16)


def _kvup(ckv, kr, w):
    m, c = ckv.shape
    tm = next(t for t in (512, 384, 256, 128) if m % t == 0)
    row = lambda i: (i, 0)
    return pl.pallas_call(
        _kvup_kernel,
        grid=(m // tm,),
        in_specs=[pl.BlockSpec((tm, c), row), pl.BlockSpec((tm, ROPE_DIM), row), pl.BlockSpec(w.shape, lambda i: (0, 0))],
        out_specs=[pl.BlockSpec((tm, B_HEADS * B_QK_PAD), row), pl.BlockSpec((tm, B_WIDTH), row)],
        out_shape=[jax.ShapeDtypeStruct((m, B_HEADS * B_QK_PAD), BF16), jax.ShapeDtypeStruct((m, B_WIDTH), BF16)],
        compiler_params=pltpu.CompilerParams(dimension_semantics=("arbitrary",), vmem_limit_bytes=VMEM_LIMIT),
        name="kv_up",
    )(ckv, kr, w)


def _visible_cols(tile, tq, l_valid, qpos0):
    p_last = qpos0 + (tile + 1) * tq - 1
    n = (p_last // CHUNK + 1) * CHUNK
    return min(n, l_valid) if isinstance(n, int) else jnp.minimum(n, l_valid)


def _num_key_blocks(tile, tq, kw, l_valid, qpos0):
    return (_visible_cols(tile, tq, l_valid, qpos0) + kw - 1) // kw


def _fully_visible_blocks(tile, tq, kw, l_valid, qpos0):
    first_row_cols = jnp.minimum(((qpos0 + tile * tq) // CHUNK + 1) * CHUNK, l_valid)
    return first_row_cols // kw


def _allowed(tile, kb, rows0, shape, tq, kw, l_valid, qpos0):
    rows = lax.broadcasted_iota(jnp.int32, shape, 0) + rows0
    cols = lax.broadcasted_iota(jnp.int32, shape, 1) + kb * kw
    qchunk = (qpos0 + tile * tq + rows) // CHUNK
    return ((cols // CHUNK) <= qchunk) & (cols < l_valid), cols


def _softmax_step(s, scale, h, vblk, m_ref, l_ref, acc_ref):
    kw = s.shape[1]
    c = scale * LOG2_E
    m_prev = m_ref[h]
    m_new = jnp.maximum(m_prev, jnp.max(s, axis=-1, keepdims=True))
    alpha = jnp.exp2((m_prev - m_new) * c)
    p = jnp.exp2((s - _lane_tile(m_new, kw)) * c)
    l_ref[h] = alpha * l_ref[h] + jnp.sum(p, axis=-1, keepdims=True)
    acc_ref[h] = alpha * acc_ref[h] + jnp.dot(p.astype(BF16), vblk, preferred_element_type=F32)
    m_ref[h] = m_new


def _softmax_init(heads, tq, m_ref, l_ref, acc_ref):
    for h in range(heads):
        m_ref[h] = jnp.full((tq, LANES), NEG, F32)
        l_ref[h] = jnp.zeros((tq, LANES), F32)
        acc_ref[h] = jnp.zeros((tq, acc_ref.shape[2]), F32)


def _masked_score(score, allowed):
    return jnp.where(allowed, score, MASKED)


def _key_to_f32(key):
    return lax.bitcast_convert_type(key ^ ((key >> 31) & 0x7FFFFFFF), F32)


def _selection_bias(score, cols, thr, jcut, allowed):
    sel = ((score > thr) | ((score == thr) & (cols < jcut))) & allowed
    return jnp.where(sel, 0.0, NEG)


def _topk_thresholds(strips, rs, kf, thr_ref, jcut_ref, need_ref):
    def count(pieces, pred):
        acc = jnp.zeros((rs, LANES), F32)
        for load, col0 in pieces:
            acc = acc + jnp.where(pred(load(), col0), 1.0, 0.0)
        return jnp.sum(acc, axis=-1, keepdims=True)

    def bit_step(b, carry):
        bit = jnp.left_shift(jnp.int32(1), 31 - b)
        for rows, pieces in strips:
            t = thr_ref[rows, :]
            cand = t + bit
            cand_f = _key_to_f32(cand)
            cnt = count(pieces, lambda s, c0: s >= cand_f)
            thr_ref[rows, :] = jnp.where(cnt >= kf, cand, t)
        return carry

    thr_ref[...] = jnp.full(thr_ref.shape, INT_MIN, jnp.int32)
    lax.fori_loop(0, 32, bit_step, 0)

    jcut_ref[...] = jnp.full(jcut_ref.shape, 4096, jnp.int32)
    flag = jnp.int32(0)
    for rows, pieces in strips:
        t = _key_to_f32(thr_ref[rows, :])
        excess = (count(pieces, lambda s, c0: s >= t) > kf) & (t[:, :1] > MASKED)
        flag = jnp.maximum(flag, jnp.max(jnp.where(excess, 1, 0)))

    @pl.when(flag > 0)
    def _():
        lane = lax.broadcasted_iota(jnp.int32, (rs, LANES), 1)
        for rows, pieces in strips:
            t = _key_to_f32(thr_ref[rows, :])
            need = kf - count(pieces, lambda s, c0: s > t)
            need_ref[rows, :] = jnp.broadcast_to(need, (rs, LANES))

        def col_step(b, carry):
            bit = jnp.left_shift(jnp.int32(1), 11 - b)
            for rows, pieces in strips:
                t, j = _key_to_f32(thr_ref[rows, :]), jcut_ref[rows, :]
                cand = j + bit
                f = count(pieces, lambda s, c0: (s == t) & (lane + c0 < cand))
                jcut_ref[rows, :] = jnp.where(f <= need_ref[rows, :], cand, j)
            return carry

        jcut_ref[...] = jnp.zeros(jcut_ref.shape, jnp.int32)
        lax.fori_loop(0, 12, col_step, 0)

    return flag


def _dsa_kernel(aq_ref, iq_ref, ikw_ref, agate_ref, k_ref, v_ref, ik_ref, o_ref,
                key_ref, thr_ref, jcut_ref, need_ref, ties_ref, bias_ref, iwb_ref, qst_ref, qs_ref, acc_ref, m_ref, l_ref,
                *, nt, tq, kw, l_valid, qpos0, topk):
    nkb_max = key_ref.shape[0] // nt
    nkb_of = [_num_key_blocks(ti, tq, kw, l_valid, qpos0) for ti in range(nt)]
    rs = qst_ref.shape[1] // IDX_HEADS
    ns = tq // rs
    nc = kw // LANES
    kf = float(topk)

    def score_tile(ti, carry):
        r0 = pl.multiple_of(ti * tq, tq)
        iw = ikw_ref[pl.ds(r0, tq), IDX_DIM:IDX_DIM + IDX_HEADS] * (IDX_DIM ** -0.5)
        for h in range(IDX_HEADS):
            iwb_ref[h] = jnp.broadcast_to(iw[:, h:h + 1], (tq, LANES))
            for r in range(ns):
                qst_ref[r, h * rs:(h + 1) * rs, :] = iq_ref[pl.ds(r0 + r * rs, rs), h * IDX_DIM:(h + 1) * IDX_DIM]

        def score_block(kb, c2):
            ikb = ik_ref[0, pl.ds(pl.multiple_of(kb * kw, kw), kw), :]
            for r in range(ns):
                rel = lax.dot_general(qst_ref[r], ikb, _NT, preferred_element_type=F32)
                sc = jnp.zeros((rs, kw), F32)
                for h in range(IDX_HEADS):
                    w = _lane_tile(iwb_ref[h, r * rs:(r + 1) * rs, :], kw)
                    sc = sc + jnp.maximum(rel[h * rs:(h + 1) * rs], 0.0) * w
                allowed, _ = _allowed(ti, kb, r * rs, (rs, kw), tq, kw, l_valid, qpos0)
                key_ref[ti * nkb_max + kb, r * rs:(r + 1) * rs, :] = _masked_score(sc, allowed)
            return c2

        lax.fori_loop(0, _num_key_blocks(ti, tq, kw, l_valid, qpos0), score_block, 0)
        return carry

    def piece(ti, kb, s, c):
        return lambda: key_ref[ti * nkb_max + kb, s * rs:(s + 1) * rs, c * LANES:(c + 1) * LANES]

    strips = [(slice(ti * tq + s * rs, ti * tq + (s + 1) * rs),
               [(piece(ti, kb, s, c), kb * kw + c * LANES) for kb in range(nkb_of[ti]) for c in range(nc)])
              for ti in range(nt) for s in range(ns)]

    @pl.when(pl.program_id(1) == 0)
    def _():
        lax.fori_loop(0, nt, score_tile, 0)
        ties_ref[0] = _topk_thresholds(strips, rs, kf, thr_ref, jcut_ref, need_ref)

    ti = pl.program_id(1)
    nkb = _num_key_blocks(ti, tq, kw, l_valid, qpos0)
    r0 = pl.multiple_of(ti * tq, tq)
    thr = _lane_tile(_key_to_f32(thr_ref[pl.ds(r0, tq), :]), kw)
    jcut = _lane_tile(jcut_ref[pl.ds(r0, tq), :], kw)

    def bias_block(kb, masked, ties):
        score = key_ref[ti * nkb_max + kb]
        if ties:
            allowed, cols = _allowed(ti, kb, 0, (tq, kw), tq, kw, l_valid, qpos0)
            bias_ref[kb] = _selection_bias(score, cols, thr, jcut, allowed)
        else:
            sel = score >= thr
            if masked:
                sel = sel & _allowed(ti, kb, 0, (tq, kw), tq, kw, l_valid, qpos0)[0]
            bias_ref[kb] = jnp.where(sel, 0.0, NEG)

    def bias_loop(lo, hi, masked, ties):
        lax.fori_loop(lo, hi, lambda kb, c: (bias_block(kb, masked, ties), c)[1], 0)

    n_full = _fully_visible_blocks(ti, tq, kw, l_valid, qpos0)

    @pl.when(ties_ref[0] == 0)
    def _():
        bias_loop(0, n_full, False, False)
        bias_loop(n_full, nkb, True, False)

    @pl.when(ties_ref[0] != 0)
    def _():
        bias_loop(0, nkb, True, True)

    grp = A_HEADS // A_KV_HEADS
    _softmax_init(A_KV_HEADS, grp * tq, m_ref, l_ref, acc_ref)
    scale = A_HEAD_DIM ** -0.5
    for h in range(A_HEADS):
        g, hh = divmod(h, grp)
        qs_ref[g, hh * tq:(hh + 1) * tq, :] = aq_ref[:, h * A_HEAD_DIM:(h + 1) * A_HEAD_DIM]

    def att_block(kb, carry):
        c0 = pl.multiple_of(kb * kw, kw)
        bias = bias_ref[kb][None]
        for g in range(A_KV_HEADS):
            gsl = slice(g * A_HEAD_DIM, (g + 1) * A_HEAD_DIM)
            s = lax.dot_general(qs_ref[g], k_ref[0, pl.ds(c0, kw), gsl], _NT, preferred_element_type=F32)
            s = (s.reshape(grp, tq, kw) + bias).reshape(grp * tq, kw)
            _softmax_step(s, scale, g, v_ref[0, pl.ds(c0, kw), gsl], m_ref, l_ref, acc_ref)
        return carry

    lax.fori_loop(0, nkb, att_block, 0)

    for h in range(A_HEADS):
        g, hh = divmod(h, grp)
        sl = slice(h * A_HEAD_DIM, (h + 1) * A_HEAD_DIM)
        o = acc_ref[g, hh * tq:(hh + 1) * tq, :] / l_ref[g, hh * tq:(hh + 1) * tq, :]
        gate = agate_ref[:, sl]
        o_ref[:, sl] = (o * (gate * jax.nn.sigmoid(gate))).astype(BF16)


def _dsa(aq, iq, ikw, z, dst, k, v, ik, bsz, t, tq, kw, l_valid, qpos0):
    lp = k.shape[1]
    assert lp < 4096 and lp % kw == 0 and t % tq == 0
    nt = t // tq
    nkb_max = lp // kw
    topk = min(TOPK_MAX, l_valid // 4)
    grp = A_HEADS // A_KV_HEADS
    rs = min(tq, ROW_STRIP)
    go, gw = dst["agate"]
    row = lambda b, i: (b * nt + i, 0)
    whole = lambda b, i: (b, 0)
    batch = lambda b, i: (b, 0, 0)
    kern = functools.partial(_dsa_kernel, nt=nt, tq=tq, kw=kw, l_valid=l_valid, qpos0=qpos0, topk=topk)
    return pl.pallas_call(
        kern,
        grid=(bsz, nt),
        in_specs=[pl.BlockSpec((tq, A_WIDTH), row),
                  pl.BlockSpec((t, IDX_HEADS * IDX_DIM), whole),
                  pl.BlockSpec((t, LANES), whole),
                  pl.BlockSpec((tq, gw), lambda b, i: (b * nt + i, go // gw)),
                  pl.BlockSpec((1, lp, A_KV_WIDTH), batch),
                  pl.BlockSpec((1, lp, A_KV_WIDTH), batch),
                  pl.BlockSpec((1, lp, IDX_DIM), batch)],
        out_specs=pl.BlockSpec((tq, A_WIDTH), row),
        out_shape=jax.ShapeDtypeStruct((bsz * t, A_WIDTH), BF16),
        scratch_shapes=[pltpu.VMEM((nt * nkb_max, tq, kw), F32),
                        pltpu.VMEM((t, LANES), jnp.int32),
                        pltpu.VMEM((t, LANES), jnp.int32),
                        pltpu.VMEM((t, LANES), F32),
                        pltpu.SMEM((1,), jnp.int32),
                        pltpu.VMEM((nkb_max, tq, kw), F32),
                        pltpu.VMEM((IDX_HEADS, tq, LANES), F32),
                        pltpu.VMEM((tq // rs, IDX_HEADS * rs, IDX_DIM), BF16),
                        pltpu.VMEM((A_KV_HEADS, grp * tq, A_HEAD_DIM), BF16),
                        pltpu.VMEM((A_KV_HEADS, grp * tq, A_HEAD_DIM), F32),
                        pltpu.VMEM((A_KV_HEADS, grp * tq, LANES), F32),
                        pltpu.VMEM((A_KV_HEADS, grp * tq, LANES), F32)],
        compiler_params=pltpu.CompilerParams(dimension_semantics=("arbitrary", "arbitrary"),
                                             vmem_limit_bytes=VMEM_LIMIT),
        name="dsa",
    )(aq, iq, ikw, z, k, v, ik)


SEL_BATCHES = 8
SEL_CHUNK = 512


def _dsa_sel_kernel(iq_ref, ikw_ref, cik_ref, nik_ref, bias_ref, key_ref, thr_ref, jcut_ref, need_ref,
                    *, gb, t, past, qpos0, topk):
    lp = past + LANES
    nc = lp // LANES

    def allowed_cols(col0, width):
        rows = lax.broadcasted_iota(jnp.int32, (t, width), 0)
        cols = lax.broadcasted_iota(jnp.int32, (t, width), 1) + col0
        return ((cols // CHUNK) <= (qpos0 + rows) // CHUNK) & (cols < past + t)

    def score_batch(j, carry):
        r0 = pl.multiple_of(j * t, t)
        qst = jnp.concatenate([iq_ref[pl.ds(r0, t), h * IDX_DIM:(h + 1) * IDX_DIM] for h in range(IDX_HEADS)], axis=0)
        iw = ikw_ref[pl.ds(r0, t), IDX_DIM:IDX_DIM + IDX_HEADS] * (IDX_DIM ** -0.5)
        iwb = [jnp.broadcast_to(iw[:, h:h + 1], (t, LANES)) for h in range(IDX_HEADS)]

        def head_sum(rel):
            width = rel.shape[1]
            sc = jnp.zeros((t, width), F32)
            for h in range(IDX_HEADS):
                sc = sc + jnp.maximum(rel[h * t:(h + 1) * t], 0.0) * _lane_tile(iwb[h], width)
            return sc

        for c in range(past // SEL_CHUNK):
            ikt = cik_ref[j, :, c * SEL_CHUNK:(c + 1) * SEL_CHUNK].astype(BF16)
            sc = head_sum(jnp.dot(qst, ikt, preferred_element_type=F32))
            key_ref[j, :, c * SEL_CHUNK:(c + 1) * SEL_CHUNK] = _masked_score(sc, allowed_cols(c * SEL_CHUNK, SEL_CHUNK))
        ik = jnp.concatenate([nik_ref[pl.ds(r0, t), :], jnp.zeros((LANES - t, IDX_DIM), BF16)], axis=0)
        sc = head_sum(lax.dot_general(qst, ik, _NT, preferred_element_type=F32))
        key_ref[j, :, past:] = _masked_score(sc, allowed_cols(past, LANES))
        return carry

    lax.fori_loop(0, gb, score_batch, 0)

    def piece(j, c):
        return lambda: key_ref[j, :, c * LANES:(c + 1) * LANES]

    strips = [(slice(j * t, (j + 1) * t), [(piece(j, c), c * LANES) for c in range(nc)]) for j in range(gb)]
    _topk_thresholds(strips, t, float(topk), thr_ref, jcut_ref, need_ref)

    def bias_batch(j, carry):
        r0 = pl.multiple_of(j * t, t)
        cols = lax.broadcasted_iota(jnp.int32, (t, lp), 1)
        bias_ref[j] = _selection_bias(key_ref[j], cols, _lane_tile(_key_to_f32(thr_ref[pl.ds(r0, t), :]), lp),
                                      _lane_tile(jcut_ref[pl.ds(r0, t), :], lp), allowed_cols(0, lp))
        return carry

    lax.fori_loop(0, gb, bias_batch, 0)


def _dsa_sel(iq, ikw, cache_ik_t, layer, new_ik, bsz, t, qpos0):
    past = cache_ik_t.shape[3]
    lp = past + LANES
    gb = next(g for g in (SEL_BATCHES, 4, 2, 1) if bsz % g == 0)
    assert past % SEL_CHUNK == 0 and t <= LANES and lp < 4096
    topk = min(TOPK_MAX, (past + t) // 4)
    rows = lambda i: (i, 0)
    kern = functools.partial(_dsa_sel_kernel, gb=gb, t=t, past=past, qpos0=qpos0, topk=topk)
    return pl.pallas_call(
        kern,
        grid=(bsz // gb,),
        in_specs=[pl.BlockSpec((gb * t, IDX_HEADS * IDX_DIM), rows),
                  pl.BlockSpec((gb * t, LANES), rows),
                  pl.BlockSpec((None, gb, IDX_DIM, past), lambda i: (layer, i, 0, 0)),
                  pl.BlockSpec((gb * t, IDX_DIM), rows)],
        out_specs=pl.BlockSpec((gb, t, lp), lambda i: (i, 0, 0)),
        out_shape=jax.ShapeDtypeStruct((bsz, t, lp), F32),
        scratch_shapes=[pltpu.VMEM((gb, t, lp), F32),
                        pltpu.VMEM((gb * t, LANES), jnp.int32),
                        pltpu.VMEM((gb * t, LANES), jnp.int32),
                        pltpu.VMEM((gb * t, LANES), F32)],
        compiler_params=pltpu.CompilerParams(dimension_semantics=("arbitrary",), vmem_limit_bytes=VMEM_LIMIT),
        name="dsa_sel",
    )(iq, ikw, cache_ik_t, new_ik)


def _dsa_att_kernel(aq_ref, agate_ref, bias_ref, ck_ref, cv_ref, nk_ref, nv_ref, o_ref, *, t, past):
    grp = A_HEADS // A_KV_HEADS
    scale = A_HEAD_DIM ** -0.5
    pad = jnp.zeros((LANES - t, A_HEAD_DIM), BF16)
    b1, b2 = bias_ref[0, :, :past], bias_ref[0, :, past:]
    for g in range(A_KV_HEADS):
        gsl = slice(g * A_HEAD_DIM, (g + 1) * A_HEAD_DIM)
        kc = ck_ref[0, pl.ds(g, past, stride=A_KV_HEADS), :].astype(BF16)
        vc = cv_ref[0, pl.ds(g, past, stride=A_KV_HEADS), :].astype(BF16)
        kn = jnp.concatenate([nk_ref[:, gsl], pad], axis=0)
        vn = jnp.concatenate([nv_ref[:, gsl], pad], axis=0)
        qs = jnp.concatenate([aq_ref[:, (g * grp + hh) * A_HEAD_DIM:(g * grp + hh + 1) * A_HEAD_DIM]
                              for hh in range(grp)], axis=0)

        def scores(keys, bias):
            s = lax.dot_general(qs, keys, _NT, preferred_element_type=F32) * scale
            return (s.reshape(grp, t, s.shape[1]) + bias[None]).reshape(grp * t, s.shape[1])

        s1, s2 = scores(kc, b1), scores(kn, b2)
        m = jnp.maximum(jnp.max(s1, axis=-1, keepdims=True), jnp.max(s2, axis=-1, keepdims=True))
        e1, e2 = jnp.exp(s1 - m), jnp.exp(s2 - m)
        l = jnp.sum(e1, axis=-1, keepdims=True) + jnp.sum(e2, axis=-1, keepdims=True)
        o = (jnp.dot(e1.astype(BF16), vc, preferred_element_type=F32)
             + jnp.dot(e2.astype(BF16), vn, preferred_element_type=F32)) / l
        for hh in range(grp):
            sl = slice((g * grp + hh) * A_HEAD_DIM, (g * grp + hh + 1) * A_HEAD_DIM)
            gate = agate_ref[:, sl]
            o_ref[:, sl] = (o[hh * t:(hh + 1) * t] * (gate * jax.nn.sigmoid(gate))).astype(BF16)


def _dsa_att(aq, z, dst, bias, cache_k, cache_v, new_k, new_v, bsz, t):
    past = cache_k.shape[1] // A_KV_HEADS
    lp = bias.shape[2]
    go, gw = dst["agate"]
    row = lambda b: (b, 0)
    batch = lambda b: (b, 0, 0)
    return pl.pallas_call(
        functools.partial(_dsa_att_kernel, t=t, past=past),
        grid=(bsz,),
        in_specs=[pl.BlockSpec((t, A_WIDTH), row),
                  pl.BlockSpec((t, gw), lambda b: (b, go // gw)),
                  pl.BlockSpec((1, t, lp), batch),
                  pl.BlockSpec((1, past * A_KV_HEADS, A_HEAD_DIM), batch),
                  pl.BlockSpec((1, past * A_KV_HEADS, A_HEAD_DIM), batch),
                  pl.BlockSpec((t, A_KV_WIDTH), row),
                  pl.BlockSpec((t, A_KV_WIDTH), row)],
        out_specs=pl.BlockSpec((t, A_WIDTH), row),
        out_shape=jax.ShapeDtypeStruct((bsz * t, A_WIDTH), BF16),
        compiler_params=pltpu.CompilerParams(dimension_semantics=("arbitrary",), vmem_limit_bytes=VMEM_LIMIT),
        name="dsa_att",
    )(aq, z, bias, cache_k, cache_v, new_k, new_v)


def _mla_kernel(q_ref, bgate_ref, k_ref, v_ref, o_ref, acc_ref, m_ref, l_ref, *, tq, kw, l_valid, qpos0):
    ti = pl.program_id(1)
    nkb = _num_key_blocks(ti, tq, kw, l_valid, qpos0)
    scale = (NOPE_DIM + ROPE_DIM) ** -0.5
    _softmax_init(B_HEADS, tq, m_ref, l_ref, acc_ref)

    def att_block(kb, masked):
        r0 = pl.multiple_of(kb * kw, kw)
        if masked:
            allowed, _ = _allowed(ti, kb, 0, (tq, kw), tq, kw, l_valid, qpos0)
            bias = jnp.where(allowed, 0.0, NEG)
        for h in range(B_HEADS):
            qsl = slice(h * B_QK_PAD, (h + 1) * B_QK_PAD)
            s = lax.dot_general(q_ref[:, qsl], k_ref[0, pl.ds(r0, kw), qsl], _NT, preferred_element_type=F32)
            if masked:
                s = s + bias
            _softmax_step(s, scale, h, v_ref[0, pl.ds(r0, kw), h * V_DIM:(h + 1) * V_DIM], m_ref, l_ref, acc_ref)

    n_full = _fully_visible_blocks(ti, tq, kw, l_valid, qpos0)
    lax.fori_loop(0, n_full, lambda kb, c: (att_block(kb, False), c)[1], 0)
    lax.fori_loop(n_full, nkb, lambda kb, c: (att_block(kb, True), c)[1], 0)

    for h in range(B_HEADS):
        sl = slice(h * V_DIM, (h + 1) * V_DIM)
        o = acc_ref[h] / l_ref[h]
        gate = bgate_ref[:, sl]
        o_ref[:, sl] = (o * (gate * jax.nn.sigmoid(gate))).astype(BF16)


def _mla(qcat, z, dst, kcat, v, bsz, t, tq, kw, l_valid, qpos0):
    lp = kcat.shape[1]
    assert lp % kw == 0 and t % tq == 0
    nt = t // tq
    go, gw = dst["bgate"]
    row = lambda b, i: (b * nt + i, 0)
    batch = lambda b, i: (b, 0, 0)
    kern = functools.partial(_mla_kernel, tq=tq, kw=kw, l_valid=l_valid, qpos0=qpos0)
    return pl.pallas_call(
        kern,
        grid=(bsz, nt),
        in_specs=[pl.BlockSpec((tq, B_HEADS * B_QK_PAD), row),
                  pl.BlockSpec((tq, gw), lambda b, i: (b * nt + i, go // gw)),
                  pl.BlockSpec((1, lp, B_HEADS * B_QK_PAD), batch),
                  pl.BlockSpec((1, lp, B_WIDTH), batch)],
        out_specs=pl.BlockSpec((tq, B_WIDTH), row),
        out_shape=jax.ShapeDtypeStruct((bsz * t, B_WIDTH), BF16),
        scratch_shapes=[pltpu.VMEM((B_HEADS, tq, V_DIM), F32),
                        pltpu.VMEM((B_HEADS, tq, LANES), F32),
                        pltpu.VMEM((B_HEADS, tq, LANES), F32)],
        compiler_params=pltpu.CompilerParams(dimension_semantics=("arbitrary", "arbitrary"),
                                             vmem_limit_bytes=VMEM_LIMIT),
        name="mla",
    )(qcat, z, kcat, v)


def _mla_lat_kernel(q_ref, bgate_ref, cc_ref, ckr_ref, nc_ref, nkr_ref, wukt_ref, wuv_ref, o_ref,
                    *, t, past, qpos0):
    rows_n = B_HEADS * t
    ql, qr = [], []
    for h in range(B_HEADS):
        qn = q_ref[:, h * B_QK_PAD: h * B_QK_PAD + NOPE_DIM]
        ql.append(jnp.dot(qn, wukt_ref[h], preferred_element_type=F32).astype(BF16))
        qr.append(q_ref[:, h * B_QK_PAD + NOPE_DIM: h * B_QK_PAD + NOPE_DIM + ROPE_DIM])
    ql = jnp.concatenate(ql, axis=0)
    qr = jnp.concatenate(qr, axis=0)
    cc = cc_ref[0].astype(BF16)
    ckr = ckr_ref[0].astype(BF16)
    nc, nkr = nc_ref[...], nkr_ref[...]
    scale = (NOPE_DIM + ROPE_DIM) ** -0.5

    def scores(lat, rope, ncols, col0):
        s = lax.dot_general(ql, lat, _NT, preferred_element_type=F32)
        s = (s + lax.dot_general(qr, rope, _NT, preferred_element_type=F32)) * scale
        rows = lax.broadcasted_iota(jnp.int32, (rows_n, ncols), 0)
        cols = lax.broadcasted_iota(jnp.int32, (rows_n, ncols), 1) + col0
        qchunk = (qpos0 + lax.rem(rows, t)) // CHUNK
        return jnp.where((cols // CHUNK) <= qchunk, s, NEG)

    s1 = scores(cc, ckr, past, 0)
    s2 = scores(nc, nkr, t, past)
    m = jnp.maximum(jnp.max(s1, axis=-1, keepdims=True), jnp.max(s2, axis=-1, keepdims=True))
    e1, e2 = jnp.exp(s1 - m), jnp.exp(s2 - m)
    l = jnp.sum(e1, axis=-1, keepdims=True) + jnp.sum(e2, axis=-1, keepdims=True)
    ol = (jnp.dot(e1.astype(BF16), cc, preferred_element_type=F32)
          + jnp.dot(e2.astype(BF16), nc, preferred_element_type=F32)) / l
    ol = ol.astype(BF16)
    for h in range(B_HEADS):
        sl = slice(h * V_DIM, (h + 1) * V_DIM)
        o = jnp.dot(ol[h * t:(h + 1) * t], wuv_ref[:, sl], preferred_element_type=F32)
        gate = bgate_ref[:, sl]
        o_ref[:, sl] = (o * (gate * jax.nn.sigmoid(gate))).astype(BF16)


def _mla_lat(qcat, z, dst, cache_ckv, cache_kr, layer, new_ckv, new_kr, wukt, wuv, bsz, t, qpos0):
    past = cache_ckv.shape[2]
    go, gw = dst["bgate"]
    row = lambda b: (b, 0)
    batch = lambda b: (b, 0, 0)
    const = lambda a: pl.BlockSpec(a.shape, lambda b: (0,) * a.ndim)
    kern = functools.partial(_mla_lat_kernel, t=t, past=past, qpos0=qpos0)
    return pl.pallas_call(
        kern,
        grid=(bsz,),
        in_specs=[pl.BlockSpec((t, B_HEADS * B_QK_PAD), row),
                  pl.BlockSpec((t, gw), lambda b: (b, go // gw)),
                  pl.BlockSpec((None, 1, past, KV_LORA), lambda b: (layer, b, 0, 0)),
                  pl.BlockSpec((None, 1, past, ROPE_DIM), lambda b: (layer, b, 0, 0)),
                  pl.BlockSpec((t, KV_LORA), row),
                  pl.BlockSpec((t, ROPE_DIM), row),
                  const(wukt), const(wuv)],
        out_specs=pl.BlockSpec((t, B_WIDTH), row),
        out_shape=jax.ShapeDtypeStruct((bsz * t, B_WIDTH), BF16),
        compiler_params=pltpu.CompilerParams(dimension_semantics=("arbitrary",), vmem_limit_bytes=VMEM_LIMIT),
        name="mla_lat",
    )(qcat, z, cache_ckv, cache_kr, new_ckv, new_kr, wukt, wuv)


def _out_kernel(ga_ref, gb_ref, mg_ref, x_ref, woa_ref, wob_ref, wout_ref, fg_ref, o_ref, *, d, final):
    ya = jnp.dot(ga_ref[...], woa_ref[...], preferred_element_type=F32)
    yb = jnp.dot(gb_ref[...], wob_ref[...], preferred_element_type=F32)
    m = jax.nn.sigmoid(mg_ref[:, :d]) * ya + jax.nn.sigmoid(mg_ref[:, d:]) * yb
    h = x_ref[...] + jnp.dot(m.astype(BF16), wout_ref[...], preferred_element_type=F32)
    o_ref[...] = _rms(h, fg_ref[...]) if final else h


def _out(ga, gb, z, dst, x, woa, wob, wout, fg, final):
    m, d = x.shape
    tm = min(m, 256)
    assert m % tm == 0 and dst["mg"][0] == 0
    row = lambda i: (i, 0)
    const = lambda a: pl.BlockSpec(a.shape, lambda i: (0, 0), pipeline_mode=pl.Buffered(1))
    return pl.pallas_call(
        functools.partial(_out_kernel, d=d, final=final),
        grid=(m // tm,),
        in_specs=[pl.BlockSpec((tm, A_WIDTH), row), pl.BlockSpec((tm, B_WIDTH), row),
                  pl.BlockSpec((tm, 2 * d), row), pl.BlockSpec((tm, d), row),
                  const(woa), const(wob), const(wout), const(fg)],
        out_specs=pl.BlockSpec((tm, d), row),
        out_shape=jax.ShapeDtypeStruct((m, d), F32),
        compiler_params=pltpu.CompilerParams(dimension_semantics=("arbitrary",), vmem_limit_bytes=VMEM_LIMIT),
        name="out",
    )(ga, gb, z, x, woa, wob, wout, fg)


def _all_keys(cache, new, bsz, t, lp):
    new = new.reshape(bsz, t, new.shape[-1])
    if cache is None and lp == t:
        return new
    parts = ([] if cache is None else [cache.reshape(bsz, cache.shape[1], -1).astype(BF16)]) + [new]
    kk = jnp.concatenate(parts, axis=1)
    return jnp.pad(kk, ((0, 0), (0, lp - kk.shape[1]), (0, 0)))


def _trunk(x, caches, final_g, weights):
    bsz, t, d = x.shape
    depth = weights["w_in"].shape[0]
    past = 0 if caches is None else caches[0].shape[2]
    l_valid = past + t
    tq = min(t, QUERY_TILE)
    kw = KEY_BLOCK if t >= QUERY_TILE else -(-l_valid // LANES) * LANES
    lp = -(-l_valid // kw) * kw
    _, dst, _ = _layout(d)
    pos = past + jnp.arange(t, dtype=jnp.int32)
    t_scale = jnp.concatenate([jnp.full((t, IDX_HEADS), IDX_HEADS ** -0.5, F32),
                               jnp.ones((t, LANES - IDX_DIM - IDX_HEADS), F32)], axis=1)
    tabs = [jnp.stack(_rope_tables(pos, A_ROT, A_HEAD_DIM, 1)),
            jnp.stack(_rope_tables(pos, IDX_ROT, IDX_DIM, 2)),
            jnp.stack(_rope_tables(pos, IDX_ROT, IDX_DIM, 1, extra=t_scale)),
            jnp.stack(_rope_tables(pos, ROPE_DIM, ROPE_DIM, 2))]
    h = x.reshape(bsz * t, d)
    new_rows = []
    for l in range(depth):
        w = {k: v[l] for k, v in weights.items()}
        z = _proj(h, w["norm_g"], w["w_in"])
        aq, iq, ak, ikw, krp, ckv, qcat, kb, vb, ikb, ckvb, krb = _prep(
            z, dst, bsz, t, tabs, w["q_norm_g"], w["kv_norm_g"], w["w_uq"])
        o, wd = dst["av"]
        av = z[:, o:o + wd]
        c = (None,) * 5 if caches is None else tuple(cc[l] for cc in caches)
        if caches is not None and t < QUERY_TILE:
            bias = _dsa_sel(iq, ikw, jnp.swapaxes(caches[2], 2, 3), l, ikb, bsz, t, past)
            interleaved = lambda a: a.reshape(bsz, past * A_KV_HEADS, A_HEAD_DIM)
            ga = _dsa_att(aq, z, dst, bias, interleaved(c[0]), interleaved(c[1]), kb, vb, bsz, t)
            gb = _mla_lat(qcat, z, dst, caches[3], caches[4], l, ckvb, krb, w["w_ukt"], w["w_uv"], bsz, t, past)
        else:
            k_all = _all_keys(c[0], kb, bsz, t, lp)
            v_all = _all_keys(c[1], vb, bsz, t, lp)
            ik_all = _all_keys(c[2], ikb, bsz, t, lp)
            ga = _dsa(aq, iq, ikw, z, dst, k_all, v_all, ik_all, bsz, t, tq, kw, l_valid, past)
            ckv_all = _all_keys(c[3], ckvb, bsz, t, lp)
            kr_all = _all_keys(c[4], krb, bsz, t, lp)
            kcat, vv = _kvup(ckv_all.reshape(bsz * lp, KV_LORA), kr_all.reshape(bsz * lp, ROPE_DIM), w["w_ukv"])
            tq_mla = MLA_QUERY_TILE if t % MLA_QUERY_TILE == 0 else tq
            gb = _mla(qcat, z, dst, kcat.reshape(bsz, lp, -1), vv.reshape(bsz, lp, -1), bsz, t, tq_mla, kw, l_valid, past)
        h = _out(ga, gb, z, dst, h, w["w_o_a"], w["w_o_b"], w["w_out"], final_g, final=(l == depth - 1))
        new_rows.append((ak.reshape(bsz, t, A_KV_HEADS, A_HEAD_DIM), av.reshape(bsz, t, A_KV_HEADS, A_HEAD_DIM),
                         ikw[:, :IDX_DIM].reshape(bsz, t, IDX_DIM), ckv.reshape(bsz, t, KV_LORA),
                         krp[:, :ROPE_DIM].reshape(bsz, t, ROPE_DIM)))
    stacked = [jnp.stack([r[i] for r in new_rows], axis=0) for i in range(5)]
    return h.reshape(bsz, t, d), stacked


def kernel(x_prompt, x_sample, cache_a_k, cache_a_v, cache_idx_k, cache_mla_ckv, cache_mla_krope,
           norm_g, w_in, w_uq, q_norm_g, kv_norm_g, w_uk, w_uv, w_o_a, w_o_b, w_out, final_g):
    depth, d, _ = w_in.shape
    weights = {
        "norm_g": norm_g.reshape(depth, 1, d),
        "w_in": jnp.stack([_regroup_w_in(jnp.swapaxes(w_in[l], 0, 1), d) for l in range(depth)]),
        "w_uq": jnp.concatenate([w_uq[..., :NOPE_DIM].reshape(depth, Q_LORA, B_HEADS * NOPE_DIM),
                                 w_uq[..., NOPE_DIM:].reshape(depth, Q_LORA, B_HEADS * ROPE_DIM)], axis=-1).astype(BF16),
        "q_norm_g": q_norm_g.reshape(depth, 1, Q_LORA),
        "kv_norm_g": kv_norm_g.reshape(depth, 1, KV_LORA),
        "w_ukv": jnp.concatenate([w_uk.reshape(depth, KV_LORA, B_HEADS * NOPE_DIM),
                                  w_uv.reshape(depth, KV_LORA, B_WIDTH)], axis=-1).astype(BF16),
        "w_ukt": jnp.transpose(w_uk, (0, 2, 3, 1)).astype(BF16),
        "w_uv": w_uv.reshape(depth, KV_LORA, B_WIDTH).astype(BF16),
        "w_o_a": w_o_a.astype(BF16), "w_o_b": w_o_b.astype(BF16), "w_out": w_out.astype(BF16),
    }
    fg = final_g.reshape(1, d)
    y_p, (pk, pv, pik, pckv, pkr) = _trunk(x_prompt, None, fg, weights)
    caches = (cache_a_k, cache_a_v, cache_idx_k, cache_mla_ckv, cache_mla_krope)
    y_s, (sk, sv, sik, sckv, skr) = _trunk(x_sample, caches, fg, weights)
    return (y_p, y_s, pk, pv, pik, pckv, pkr, sk, sv, sik, sckv, skr)
```

```python
import functools

import jax
import jax.numpy as jnp
from jax import lax
from jax.experimental import pallas as pl
from jax.experimental.pallas import tpu as pltpu

F32 = jnp.float32
BF16 = jnp.bfloat16

CHUNK = 64
ROPE_THETA = 500000.0
EPS = 1e-6

A_HEADS = 8
A_KV_HEADS = 2
A_HEAD_DIM = 128
A_ROT = A_HEAD_DIM // 4
A_WIDTH = A_HEADS * A_HEAD_DIM
A_KV_WIDTH = A_KV_HEADS * A_HEAD_DIM
IDX_HEADS = 16
IDX_DIM = 64
IDX_ROT = IDX_DIM // 4
TOPK_MAX = 256

B_HEADS = 8
Q_LORA = 512
KV_LORA = 512
NOPE_DIM = 128
ROPE_DIM = 64
V_DIM = 128
B_WIDTH = B_HEADS * V_DIM
B_QK_PAD = 256

LANES = 128
KEY_BLOCK = 256
QUERY_TILE = 256
MLA_QUERY_TILE = 512
ROW_STRIP = 64
INT_MIN = -(2 ** 31)
LOG2_E = 1.4426950408889634
NEG = -1e30
MASKED = -1e38
VMEM_LIMIT = 56 * 1024 * 1024

_SEGS = (("aq", A_WIDTH), ("ak", A_KV_WIDTH), ("av", A_KV_WIDTH), ("agate", A_WIDTH),
         ("iq", IDX_HEADS * IDX_DIM), ("ik", IDX_DIM), ("iw", IDX_HEADS),
         ("cq", Q_LORA), ("ckv", KV_LORA), ("kr", ROPE_DIM), ("bgate", B_WIDTH), ("mg", None))


def _layout(d_model):
    src, off = {}, 0
    for name, w in _SEGS:
        w = 2 * d_model if w is None else w
        src[name] = (off, w)
        off += w
    order = (("mg", 2 * d_model), ("aq", A_WIDTH), ("agate", A_WIDTH), ("iq", IDX_HEADS * IDX_DIM),
             ("bgate", B_WIDTH), ("cq", Q_LORA), ("ckv", KV_LORA), ("ak", A_KV_WIDTH), ("av", A_KV_WIDTH),
             ("ikw", LANES), ("krp", LANES))
    dst, off = {}, 0
    for name, w in order:
        assert off % w == 0
        dst[name] = (off, w)
        off += w
    return src, dst, off


def _regroup_kernel(w_ref, o_ref, *, src, dst):
    cols = w_ref.shape[1]

    def piece(name):
        o, w = src[name]
        return w_ref[o:o + w, :].astype(BF16)

    def zeros(n):
        return jnp.zeros((n, cols), BF16)

    for name, (o, w) in dst.items():
        if name == "ikw":
            val = jnp.concatenate([piece("ik"), piece("iw"), zeros(LANES - IDX_DIM - IDX_HEADS)], axis=0)
        elif name == "krp":
            val = jnp.concatenate([piece("kr"), zeros(LANES - ROPE_DIM)], axis=0)
        else:
            val = piece(name)
        o_ref[o:o + w, :] = val


def _regroup_w_in(w_in_t, d_model):
    src, dst, total = _layout(d_model)
    tc = 256
    assert d_model % tc == 0
    return pl.pallas_call(
        functools.partial(_regroup_kernel, src=src, dst=dst),
        grid=(d_model // tc,),
        in_specs=[pl.BlockSpec((w_in_t.shape[0], tc), lambda i: (0, i))],
        out_specs=pl.BlockSpec((total, tc), lambda i: (0, i)),
        out_shape=jax.ShapeDtypeStruct((total, d_model), BF16),
        compiler_params=pltpu.CompilerParams(dimension_semantics=("arbitrary",), vmem_limit_bytes=VMEM_LIMIT),
        name="regroup",
    )(w_in_t)


def _rope_tables(pos, rot, head, heads_per_vreg, extra=None):
    half = rot // 2
    inv_freq = ROPE_THETA ** (-jnp.arange(half, dtype=F32) * (2.0 / rot))
    ang = pos.astype(F32)[:, None] * inv_freq[None, :]
    cos, sin = jnp.cos(ang), jnp.sin(ang)
    t = pos.shape[0]
    one = jnp.ones((t, head - rot), F32)
    zero = lambda n: jnp.zeros((t, n), F32)
    c = jnp.concatenate([cos, cos, one], axis=1)
    sa = jnp.concatenate([-sin, zero(head - half)], axis=1)
    sb = jnp.concatenate([zero(half), sin, zero(head - rot)], axis=1)
    c, sa, sb = (jnp.tile(a, (1, heads_per_vreg)) for a in (c, sa, sb))
    if extra is not None:
        c = jnp.concatenate([c, extra], axis=1)
        sa = jnp.concatenate([sa, jnp.zeros_like(extra)], axis=1)
        sb = jnp.concatenate([sb, jnp.zeros_like(extra)], axis=1)
    assert c.shape[1] == LANES
    return c, sa, sb


def _rope(x, c, sa, sb, half):
    up = pltpu.roll(x, LANES - half, 1)
    dn = pltpu.roll(x, half, 1)
    return x * c + up * sa + dn * sb


def _rms(x, g):
    ms = jnp.mean(x * x, axis=-1, keepdims=True)
    return (x * lax.rsqrt(ms + EPS)) * g


def _lane_tile(x, width):
    return x if width == LANES else jnp.concatenate([x] * (width // LANES), axis=1)


_NT = (((1,), (1,)), ((), ()))


def _proj_kernel(x_ref, g_ref, w_ref, z_ref, xn_ref):
    @pl.when(pl.program_id(1) == 0)
    def _():
        xn_ref[...] = _rms(x_ref[...], g_ref[...]).astype(BF16)
    z_ref[...] = lax.dot_general(xn_ref[...], w_ref[...], _NT, preferred_element_type=F32)


def _proj(x, g, w):
    m, d = x.shape
    n = w.shape[0]
    tm = min(m, 1024)
    tn = 768
    assert m % tm == 0 and n % tn == 0
    return pl.pallas_call(
        _proj_kernel,
        grid=(m // tm, n // tn),
        in_specs=[pl.BlockSpec((tm, d), lambda i, j: (i, 0)),
                  pl.BlockSpec((1, d), lambda i, j: (0, 0)),
                  pl.BlockSpec((tn, d), lambda i, j: (j, 0))],
        out_specs=pl.BlockSpec((tm, tn), lambda i, j: (i, j)),
        out_shape=jax.ShapeDtypeStruct((m, n), F32),
        scratch_shapes=[pltpu.VMEM((tm, d), BF16)],
        compiler_params=pltpu.CompilerParams(dimension_semantics=("arbitrary", "arbitrary"),
                                             vmem_limit_bytes=VMEM_LIMIT),
        name="proj",
    )(x, g, w)


def _prep_kernel(aq_ref, iq_ref, cq_ref, ckv_ref, ak_ref, av_ref, ikw_ref, krp_ref,
                 ta_ref, ti_ref, tk_ref, tr_ref, qg_ref, kvg_ref, wuq_ref,
                 aq_o, iq_o, ak_o, av_o, ikw_o, krp_o, ckv_o, qcat_o, kb_o, vb_o, ikb_o, ckvb_o, krb_o):
    ca, saa, sba = ta_ref[0], ta_ref[1], ta_ref[2]
    ci, sai, sbi = ti_ref[0], ti_ref[1], ti_ref[2]
    ck, sak, sbk = tk_ref[0], tk_ref[1], tk_ref[2]
    cr, sar, sbr = tr_ref[0], tr_ref[1], tr_ref[2]
    tm = aq_ref.shape[0]
    for h in range(A_HEADS):
        sl = slice(h * LANES, (h + 1) * LANES)
        aq_o[:, sl] = _rope(aq_ref[:, sl], ca, saa, sba, A_ROT // 2).astype(BF16)
    for h in range(A_KV_HEADS):
        sl = slice(h * LANES, (h + 1) * LANES)
        ak = _rope(ak_ref[:, sl], ca, saa, sba, A_ROT // 2)
        ak_o[pl.ds(h, tm, stride=A_KV_HEADS), :] = ak
        av_o[pl.ds(h, tm, stride=A_KV_HEADS), :] = av_ref[:, sl]
        kb_o[:, sl] = ak.astype(BF16)
    vb_o[...] = av_ref[...].astype(BF16)
    for v in range(IDX_HEADS * IDX_DIM // LANES):
        sl = slice(v * LANES, (v + 1) * LANES)
        iq_o[:, sl] = _rope(iq_ref[:, sl], ci, sai, sbi, IDX_ROT // 2).astype(BF16)
    ikw = _rope(ikw_ref[...], ck, sak, sbk, IDX_ROT // 2)
    ikw_o[...] = ikw
    ikb_o[...] = ikw[:, :IDX_DIM].astype(BF16)
    krp = _rope(krp_ref[...], cr, sar, sbr, ROPE_DIM // 2)
    krp_o[...] = krp
    krb_o[...] = krp[:, :ROPE_DIM].astype(BF16)
    ckv = _rms(ckv_ref[...], kvg_ref[...])
    ckv_o[...] = ckv
    ckvb_o[...] = ckv.astype(BF16)
    cqn = _rms(cq_ref[...], qg_ref[...]).astype(BF16)
    qb = jnp.dot(cqn, wuq_ref[...], preferred_element_type=F32)
    nope = B_HEADS * NOPE_DIM
    low = lax.broadcasted_iota(jnp.int32, (tm, LANES), 1) < ROPE_DIM
    for v in range(B_HEADS * ROPE_DIM // LANES):
        r = _rope(qb[:, nope + v * LANES: nope + (v + 1) * LANES], cr, sar, sbr, ROPE_DIM // 2)
        halves = (jnp.where(low, r, 0.0), jnp.where(low, pltpu.roll(r, ROPE_DIM, 1), 0.0))
        for j in range(2):
            h = 2 * v + j
            qcat_o[:, h * B_QK_PAD: h * B_QK_PAD + NOPE_DIM] = qb[:, h * NOPE_DIM:(h + 1) * NOPE_DIM].astype(BF16)
            qcat_o[:, h * B_QK_PAD + NOPE_DIM:(h + 1) * B_QK_PAD] = halves[j].astype(BF16)


def _prep(z, dst, bsz, t, tabs, q_norm_g, kv_norm_g, wuq):
    m = bsz * t
    tm = min(t, 256)
    nt = t // tm
    assert t % tm == 0

    def zspec(name):
        o, w = dst[name]
        return pl.BlockSpec((tm, w), lambda b, i, blk=o // w: (b * nt + i, blk))

    def tspec():
        return pl.BlockSpec((3, tm, LANES), lambda b, i: (0, i, 0))

    def ospec(w):
        return pl.BlockSpec((tm, w), lambda b, i: (b * nt + i, 0))

    def full(a):
        return pl.BlockSpec(a.shape, lambda b, i: (0,) * a.ndim)

    outs = [(A_WIDTH, BF16), (IDX_HEADS * IDX_DIM, BF16), ("interleaved", F32), ("interleaved", F32), (LANES, F32), (LANES, F32),
            (KV_LORA, F32), (B_HEADS * B_QK_PAD, BF16),
            (A_KV_WIDTH, BF16), (A_KV_WIDTH, BF16), (IDX_DIM, BF16), (KV_LORA, BF16), (ROPE_DIM, BF16)]
    return pl.pallas_call(
        _prep_kernel,
        grid=(bsz, nt),
        in_specs=[zspec("aq"), zspec("iq"), zspec("cq"), zspec("ckv"), zspec("ak"), zspec("av"), zspec("ikw"),
                  zspec("krp"), tspec(), tspec(), tspec(), tspec(), full(q_norm_g), full(kv_norm_g), full(wuq)],
        out_specs=[pl.BlockSpec((tm * A_KV_HEADS, A_HEAD_DIM), lambda b, i: (b * nt + i, 0)) if w == "interleaved" else ospec(w)
                   for w, _ in outs],
        out_shape=[jax.ShapeDtypeStruct((m * A_KV_HEADS, A_HEAD_DIM) if w == "interleaved" else (m, w), dt) for w, dt in outs],
        compiler_params=pltpu.CompilerParams(dimension_semantics=("arbitrary", "arbitrary"),
                                             vmem_limit_bytes=VMEM_LIMIT),
        name="prep",
    )(z, z, z, z, z, z, z, z, *tabs, q_norm_g, kv_norm_g, wuq)


def _kvup_kernel(c_ref, kr_ref, w_ref, kcat_ref, v_ref):
    r = jnp.dot(c_ref[...], w_ref[...], preferred_element_type=F32)
    n = B_HEADS * NOPE_DIM
    tail = jnp.concatenate([kr_ref[...], jnp.zeros((kr_ref.shape[0], B_QK_PAD - NOPE_DIM - ROPE_DIM), BF16)], axis=1)
    for h in range(B_HEADS):
        kcat_ref[:, h * B_QK_PAD: h * B_QK_PAD + NOPE_DIM] = r[:, h * NOPE_DIM:(h + 1) * NOPE_DIM].astype(BF16)
        kcat_ref[:, h * B_QK_PAD + NOPE_DIM:(h + 1) * B_QK_PAD] = tail
    v_ref[...] = r[:, n:].astype(BF16)


def _kvup(ckv, kr, w):
    m, c = ckv.shape
    tm = next(t for t in (512, 384, 256, 128) if m % t == 0)
    row = lambda i: (i, 0)
    return pl.pallas_call(
        _kvup_kernel,
        grid=(m // tm,),
        in_specs=[pl.BlockSpec((tm, c), row), pl.BlockSpec((tm, ROPE_DIM), row), pl.BlockSpec(w.shape, lambda i: (0, 0))],
        out_specs=[pl.BlockSpec((tm, B_HEADS * B_QK_PAD), row), pl.BlockSpec((tm, B_WIDTH), row)],
        out_shape=[jax.ShapeDtypeStruct((m, B_HEADS * B_QK_PAD), BF16), jax.ShapeDtypeStruct((m, B_WIDTH), BF16)],
        compiler_params=pltpu.CompilerParams(dimension_semantics=("arbitrary",), vmem_limit_bytes=VMEM_LIMIT),
        name="kv_up",
    )(ckv, kr, w)


def _visible_cols(tile, tq, l_valid, qpos0):
    p_last = qpos0 + (tile + 1) * tq - 1
    n = (p_last // CHUNK + 1) * CHUNK
    return min(n, l_valid) if isinstance(n, int) else jnp.minimum(n, l_valid)


def _num_key_blocks(tile, tq, kw, l_valid, qpos0):
    return (_visible_cols(tile, tq, l_valid, qpos0) + kw - 1) // kw


def _fully_visible_blocks(tile, tq, kw, l_valid, qpos0):
    first_row_cols = jnp.minimum(((qpos0 + tile * tq) // CHUNK + 1) * CHUNK, l_valid)
    return first_row_cols // kw


def _allowed(tile, kb, rows0, shape, tq, kw, l_valid, qpos0):
    rows = lax.broadcasted_iota(jnp.int32, shape, 0) + rows0
    cols = lax.broadcasted_iota(jnp.int32, shape, 1) + kb * kw
    qchunk = (qpos0 + tile * tq + rows) // CHUNK
    return ((cols // CHUNK) <= qchunk) & (cols < l_valid), cols


def _softmax_step(s, scale, h, vblk, m_ref, l_ref, acc_ref):
    kw = s.shape[1]
    c = scale * LOG2_E
    m_prev = m_ref[h]
    m_new = jnp.maximum(m_prev, jnp.max(s, axis=-1, keepdims=True))
    alpha = jnp.exp2((m_prev - m_new) * c)
    p = jnp.exp2((s - _lane_tile(m_new, kw)) * c)
    l_ref[h] = alpha * l_ref[h] + jnp.sum(p, axis=-1, keepdims=True)
    acc_ref[h] = alpha * acc_ref[h] + jnp.dot(p.astype(BF16), vblk, preferred_element_type=F32)
    m_ref[h] = m_new


def _softmax_init(heads, tq, m_ref, l_ref, acc_ref):
    for h in range(heads):
        m_ref[h] = jnp.full((tq, LANES), NEG, F32)
        l_ref[h] = jnp.zeros((tq, LANES), F32)
        acc_ref[h] = jnp.zeros((tq, acc_ref.shape[2]), F32)


def _masked_score(score, allowed):
    return jnp.where(allowed, score, MASKED)


def _key_to_f32(key):
    return lax.bitcast_convert_type(key ^ ((key >> 31) & 0x7FFFFFFF), F32)


def _selection_bias(score, cols, thr, jcut, allowed):
    sel = ((score > thr) | ((score == thr) & (cols < jcut))) & allowed
    return jnp.where(sel, 0.0, NEG)


def _topk_thresholds(strips, rs, kf, thr_ref, jcut_ref, need_ref):
    def count(pieces, pred):
        acc = jnp.zeros((rs, LANES), F32)
        for load, col0 in pieces:
            acc = acc + jnp.where(pred(load(), col0), 1.0, 0.0)
        return jnp.sum(acc, axis=-1, keepdims=True)

    def bit_step(b, carry):
        bit = jnp.left_shift(jnp.int32(1), 31 - b)
        for rows, pieces in strips:
            t = thr_ref[rows, :]
            cand = t + bit
            cand_f = _key_to_f32(cand)
            cnt = count(pieces, lambda s, c0: s >= cand_f)
            thr_ref[rows, :] = jnp.where(cnt >= kf, cand, t)
        return carry

    thr_ref[...] = jnp.full(thr_ref.shape, INT_MIN, jnp.int32)
    lax.fori_loop(0, 32, bit_step, 0)

    jcut_ref[...] = jnp.full(jcut_ref.shape, 4096, jnp.int32)
    flag = jnp.int32(0)
    for rows, pieces in strips:
        t = _key_to_f32(thr_ref[rows, :])
        excess = (count(pieces, lambda s, c0: s >= t) > kf) & (t[:, :1] > MASKED)
        flag = jnp.maximum(flag, jnp.max(jnp.where(excess, 1, 0)))

    @pl.when(flag > 0)
    def _():
        lane = lax.broadcasted_iota(jnp.int32, (rs, LANES), 1)
        for rows, pieces in strips:
            t = _key_to_f32(thr_ref[rows, :])
            need = kf - count(pieces, lambda s, c0: s > t)
            need_ref[rows, :] = jnp.broadcast_to(need, (rs, LANES))

        def col_step(b, carry):
            bit = jnp.left_shift(jnp.int32(1), 11 - b)
            for rows, pieces in strips:
                t, j = _key_to_f32(thr_ref[rows, :]), jcut_ref[rows, :]
                cand = j + bit
                f = count(pieces, lambda s, c0: (s == t) & (lane + c0 < cand))
                jcut_ref[rows, :] = jnp.where(f <= need_ref[rows, :], cand, j)
            return carry

        jcut_ref[...] = jnp.zeros(jcut_ref.shape, jnp.int32)
        lax.fori_loop(0, 12, col_step, 0)

    return flag


def _dsa_kernel(aq_ref, iq_ref, ikw_ref, agate_ref, k_ref, v_ref, ik_ref, o_ref,
                key_ref, thr_ref, jcut_ref, need_ref, ties_ref, bias_ref, iwb_ref, qst_ref, qs_ref, acc_ref, m_ref, l_ref,
                *, nt, tq, kw, l_valid, qpos0, topk):
    nkb_max = key_ref.shape[0] // nt
    nkb_of = [_num_key_blocks(ti, tq, kw, l_valid, qpos0) for ti in range(nt)]
    rs = qst_ref.shape[1] // IDX_HEADS
    ns = tq // rs
    nc = kw // LANES
    kf = float(topk)

    def score_tile(ti, carry):
        r0 = pl.multiple_of(ti * tq, tq)
        iw = ikw_ref[pl.ds(r0, tq), IDX_DIM:IDX_DIM + IDX_HEADS] * (IDX_DIM ** -0.5)
        for h in range(IDX_HEADS):
            iwb_ref[h] = jnp.broadcast_to(iw[:, h:h + 1], (tq, LANES))
            for r in range(ns):
                qst_ref[r, h * rs:(h + 1) * rs, :] = iq_ref[pl.ds(r0 + r * rs, rs), h * IDX_DIM:(h + 1) * IDX_DIM]

        def score_block(kb, c2):
            ikb = ik_ref[0, pl.ds(pl.multiple_of(kb * kw, kw), kw), :]
            for r in range(ns):
                rel = lax.dot_general(qst_ref[r], ikb, _NT, preferred_element_type=F32)
                sc = jnp.zeros((rs, kw), F32)
                for h in range(IDX_HEADS):
                    w = _lane_tile(iwb_ref[h, r * rs:(r + 1) * rs, :], kw)
                    sc = sc + jnp.maximum(rel[h * rs:(h + 1) * rs], 0.0) * w
                allowed, _ = _allowed(ti, kb, r * rs, (rs, kw), tq, kw, l_valid, qpos0)
                key_ref[ti * nkb_max + kb, r * rs:(r + 1) * rs, :] = _masked_score(sc, allowed)
            return c2

        lax.fori_loop(0, _num_key_blocks(ti, tq, kw, l_valid, qpos0), score_block, 0)
        return carry

    def piece(ti, kb, s, c):
        return lambda: key_ref[ti * nkb_max + kb, s * rs:(s + 1) * rs, c * LANES:(c + 1) * LANES]

    strips = [(slice(ti * tq + s * rs, ti * tq + (s + 1) * rs),
               [(piece(ti, kb, s, c), kb * kw + c * LANES) for kb in range(nkb_of[ti]) for c in range(nc)])
              for ti in range(nt) for s in range(ns)]

    @pl.when(pl.program_id(1) == 0)
    def _():
        lax.fori_loop(0, nt, score_tile, 0)
        ties_ref[0] = _topk_thresholds(strips, rs, kf, thr_ref, jcut_ref, need_ref)

    ti = pl.program_id(1)
    nkb = _num_key_blocks(ti, tq, kw, l_valid, qpos0)
    r0 = pl.multiple_of(ti * tq, tq)
    thr = _lane_tile(_key_to_f32(thr_ref[pl.ds(r0, tq), :]), kw)
    jcut = _lane_tile(jcut_ref[pl.ds(r0, tq), :], kw)

    def bias_block(kb, masked, ties):
        score = key_ref[ti * nkb_max + kb]
        if ties:
            allowed, cols = _allowed(ti, kb, 0, (tq, kw), tq, kw, l_valid, qpos0)
            bias_ref[kb] = _selection_bias(score, cols, thr, jcut, allowed)
        else:
            sel = score >= thr
            if masked:
                sel = sel & _allowed(ti, kb, 0, (tq, kw), tq, kw, l_valid, qpos0)[0]
            bias_ref[kb] = jnp.where(sel, 0.0, NEG)

    def bias_loop(lo, hi, masked, ties):
        lax.fori_loop(lo, hi, lambda kb, c: (bias_block(kb, masked, ties), c)[1], 0)

    n_full = _fully_visible_blocks(ti, tq, kw, l_valid, qpos0)

    @pl.when(ties_ref[0] == 0)
    def _():
        bias_loop(0, n_full, False, False)
        bias_loop(n_full, nkb, True, False)

    @pl.when(ties_ref[0] != 0)
    def _():
        bias_loop(0, nkb, True, True)

    grp = A_HEADS // A_KV_HEADS
    _softmax_init(A_KV_HEADS, grp * tq, m_ref, l_ref, acc_ref)
    scale = A_HEAD_DIM ** -0.5
    for h in range(A_HEADS):
        g, hh = divmod(h, grp)
        qs_ref[g, hh * tq:(hh + 1) * tq, :] = aq_ref[:, h * A_HEAD_DIM:(h + 1) * A_HEAD_DIM]

    def att_block(kb, carry):
        c0 = pl.multiple_of(kb * kw, kw)
        bias = bias_ref[kb][None]
        for g in range(A_KV_HEADS):
            gsl = slice(g * A_HEAD_DIM, (g + 1) * A_HEAD_DIM)
            s = lax.dot_general(qs_ref[g], k_ref[0, pl.ds(c0, kw), gsl], _NT, preferred_element_type=F32)
            s = (s.reshape(grp, tq, kw) + bias).reshape(grp * tq, kw)
            _softmax_step(s, scale, g, v_ref[0, pl.ds(c0, kw), gsl], m_ref, l_ref, acc_ref)
        return carry

    lax.fori_loop(0, nkb, att_block, 0)

    for h in range(A_HEADS):
        g, hh = divmod(h, grp)
        sl = slice(h * A_HEAD_DIM, (h + 1) * A_HEAD_DIM)
        o = acc_ref[g, hh * tq:(hh + 1) * tq, :] / l_ref[g, hh * tq:(hh + 1) * tq, :]
        gate = agate_ref[:, sl]
        o_ref[:, sl] = (o * (gate * jax.nn.sigmoid(gate))).astype(BF16)


def _dsa(aq, iq, ikw, z, dst, k, v, ik, bsz, t, tq, kw, l_valid, qpos0):
    lp = k.shape[1]
    assert lp < 4096 and lp % kw == 0 and t % tq == 0
    nt = t // tq
    nkb_max = lp // kw
    topk = min(TOPK_MAX, l_valid // 4)
    grp = A_HEADS // A_KV_HEADS
    rs = min(tq, ROW_STRIP)
    go, gw = dst["agate"]
    row = lambda b, i: (b * nt + i, 0)
    whole = lambda b, i: (b, 0)
    batch = lambda b, i: (b, 0, 0)
    kern = functools.partial(_dsa_kernel, nt=nt, tq=tq, kw=kw, l_valid=l_valid, qpos0=qpos0, topk=topk)
    return pl.pallas_call(
        kern,
        grid=(bsz, nt),
        in_specs=[pl.BlockSpec((tq, A_WIDTH), row),
                  pl.BlockSpec((t, IDX_HEADS * IDX_DIM), whole),
                  pl.BlockSpec((t, LANES), whole),
                  pl.BlockSpec((tq, gw), lambda b, i: (b * nt + i, go // gw)),
                  pl.BlockSpec((1, lp, A_KV_WIDTH), batch),
                  pl.BlockSpec((1, lp, A_KV_WIDTH), batch),
                  pl.BlockSpec((1, lp, IDX_DIM), batch)],
        out_specs=pl.BlockSpec((tq, A_WIDTH), row),
        out_shape=jax.ShapeDtypeStruct((bsz * t, A_WIDTH), BF16),
        scratch_shapes=[pltpu.VMEM((nt * nkb_max, tq, kw), F32),
                        pltpu.VMEM((t, LANES), jnp.int32),
                        pltpu.VMEM((t, LANES), jnp.int32),
                        pltpu.VMEM((t, LANES), F32),
                        pltpu.SMEM((1,), jnp.int32),
                        pltpu.VMEM((nkb_max, tq, kw), F32),
                        pltpu.VMEM((IDX_HEADS, tq, LANES), F32),
                        pltpu.VMEM((tq // rs, IDX_HEADS * rs, IDX_DIM), BF16),
                        pltpu.VMEM((A_KV_HEADS, grp * tq, A_HEAD_DIM), BF16),
                        pltpu.VMEM((A_KV_HEADS, grp * tq, A_HEAD_DIM), F32),
                        pltpu.VMEM((A_KV_HEADS, grp * tq, LANES), F32),
                        pltpu.VMEM((A_KV_HEADS, grp * tq, LANES), F32)],
        compiler_params=pltpu.CompilerParams(dimension_semantics=("arbitrary", "arbitrary"),
                                             vmem_limit_bytes=VMEM_LIMIT),
        name="dsa",
    )(aq, iq, ikw, z, k, v, ik)


SEL_BATCHES = 8
SEL_CHUNK = 512


def _dsa_sel_kernel(iq_ref, ikw_ref, cik_ref, nik_ref, bias_ref, key_ref, thr_ref, jcut_ref, need_ref,
                    *, gb, t, past, qpos0, topk):
    lp = past + LANES
    nc = lp // LANES

    def allowed_cols(col0, width):
        rows = lax.broadcasted_iota(jnp.int32, (t, width), 0)
        cols = lax.broadcasted_iota(jnp.int32, (t, width), 1) + col0
        return ((cols // CHUNK) <= (qpos0 + rows) // CHUNK) & (cols < past + t)

    def score_batch(j, carry):
        r0 = pl.multiple_of(j * t, t)
        qst = jnp.concatenate([iq_ref[pl.ds(r0, t), h * IDX_DIM:(h + 1) * IDX_DIM] for h in range(IDX_HEADS)], axis=0)
        iw = ikw_ref[pl.ds(r0, t), IDX_DIM:IDX_DIM + IDX_HEADS] * (IDX_DIM ** -0.5)
        iwb = [jnp.broadcast_to(iw[:, h:h + 1], (t, LANES)) for h in range(IDX_HEADS)]

        def head_sum(rel):
            width = rel.shape[1]
            sc = jnp.zeros((t, width), F32)
            for h in range(IDX_HEADS):
                sc = sc + jnp.maximum(rel[h * t:(h + 1) * t], 0.0) * _lane_tile(iwb[h], width)
            return sc

        for c in range(past // SEL_CHUNK):
            ikt = cik_ref[j, :, c * SEL_CHUNK:(c + 1) * SEL_CHUNK].astype(BF16)
            sc = head_sum(jnp.dot(qst, ikt, preferred_element_type=F32))
            key_ref[j, :, c * SEL_CHUNK:(c + 1) * SEL_CHUNK] = _masked_score(sc, allowed_cols(c * SEL_CHUNK, SEL_CHUNK))
        ik = jnp.concatenate([nik_ref[pl.ds(r0, t), :], jnp.zeros((LANES - t, IDX_DIM), BF16)], axis=0)
        sc = head_sum(lax.dot_general(qst, ik, _NT, preferred_element_type=F32))
        key_ref[j, :, past:] = _masked_score(sc, allowed_cols(past, LANES))
        return carry

    lax.fori_loop(0, gb, score_batch, 0)

    def piece(j, c):
        return lambda: key_ref[j, :, c * LANES:(c + 1) * LANES]

    strips = [(slice(j * t, (j + 1) * t), [(piece(j, c), c * LANES) for c in range(nc)]) for j in range(gb)]
    _topk_thresholds(strips, t, float(topk), thr_ref, jcut_ref, need_ref)

    def bias_batch(j, carry):
        r0 = pl.multiple_of(j * t, t)
        cols = lax.broadcasted_iota(jnp.int32, (t, lp), 1)
        bias_ref[j] = _selection_bias(key_ref[j], cols, _lane_tile(_key_to_f32(thr_ref[pl.ds(r0, t), :]), lp),
                                      _lane_tile(jcut_ref[pl.ds(r0, t), :], lp), allowed_cols(0, lp))
        return carry

    lax.fori_loop(0, gb, bias_batch, 0)


def _dsa_sel(iq, ikw, cache_ik_t, layer, new_ik, bsz, t, qpos0):
    past = cache_ik_t.shape[3]
    lp = past + LANES
    gb = next(g for g in (SEL_BATCHES, 4, 2, 1) if bsz % g == 0)
    assert past % SEL_CHUNK == 0 and t <= LANES and lp < 4096
    topk = min(TOPK_MAX, (past + t) // 4)
    rows = lambda i: (i, 0)
    kern = functools.partial(_dsa_sel_kernel, gb=gb, t=t, past=past, qpos0=qpos0, topk=topk)
    return pl.pallas_call(
        kern,
        grid=(bsz // gb,),
        in_specs=[pl.BlockSpec((gb * t, IDX_HEADS * IDX_DIM), rows),
                  pl.BlockSpec((gb * t, LANES), rows),
                  pl.BlockSpec((None, gb, IDX_DIM, past), lambda i: (layer, i, 0, 0)),
                  pl.BlockSpec((gb * t, IDX_DIM), rows)],
        out_specs=pl.BlockSpec((gb, t, lp), lambda i: (i, 0, 0)),
        out_shape=jax.ShapeDtypeStruct((bsz, t, lp), F32),
        scratch_shapes=[pltpu.VMEM((gb, t, lp), F32),
                        pltpu.VMEM((gb * t, LANES), jnp.int32),
                        pltpu.VMEM((gb * t, LANES), jnp.int32),
                        pltpu.VMEM((gb * t, LANES), F32)],
        compiler_params=pltpu.CompilerParams(dimension_semantics=("arbitrary",), vmem_limit_bytes=VMEM_LIMIT),
        name="dsa_sel",
    )(iq, ikw, cache_ik_t, new_ik)


def _dsa_att_kernel(aq_ref, agate_ref, bias_ref, ck_ref, cv_ref, nk_ref, nv_ref, o_ref, *, t, past):
    grp = A_HEADS // A_KV_HEADS
    scale = A_HEAD_DIM ** -0.5
    pad = jnp.zeros((LANES - t, A_HEAD_DIM), BF16)
    b1, b2 = bias_ref[0, :, :past], bias_ref[0, :, past:]
    for g in range(A_KV_HEADS):
        gsl = slice(g * A_HEAD_DIM, (g + 1) * A_HEAD_DIM)
        kc = ck_ref[0, pl.ds(g, past, stride=A_KV_HEADS), :].astype(BF16)
        vc = cv_ref[0, pl.ds(g, past, stride=A_KV_HEADS), :].astype(BF16)
        kn = jnp.concatenate([nk_ref[:, gsl], pad], axis=0)
        vn = jnp.concatenate([nv_ref[:, gsl], pad], axis=0)
        qs = jnp.concatenate([aq_ref[:, (g * grp + hh) * A_HEAD_DIM:(g * grp + hh + 1) * A_HEAD_DIM]
                              for hh in range(grp)], axis=0)

        def scores(keys, bias):
            s = lax.dot_general(qs, keys, _NT, preferred_element_type=F32) * scale
            return (s.reshape(grp, t, s.shape[1]) + bias[None]).reshape(grp * t, s.shape[1])

        s1, s2 = scores(kc, b1), scores(kn, b2)
        m = jnp.maximum(jnp.max(s1, axis=-1, keepdims=True), jnp.max(s2, axis=-1, keepdims=True))
        e1, e2 = jnp.exp(s1 - m), jnp.exp(s2 - m)
        l = jnp.sum(e1, axis=-1, keepdims=True) + jnp.sum(e2, axis=-1, keepdims=True)
        o = (jnp.dot(e1.astype(BF16), vc, preferred_element_type=F32)
             + jnp.dot(e2.astype(BF16), vn, preferred_element_type=F32)) / l
        for hh in range(grp):
            sl = slice((g * grp + hh) * A_HEAD_DIM, (g * grp + hh + 1) * A_HEAD_DIM)
            gate = agate_ref[:, sl]
            o_ref[:, sl] = (o[hh * t:(hh + 1) * t] * (gate * jax.nn.sigmoid(gate))).astype(BF16)


def _dsa_att(aq, z, dst, bias, cache_k, cache_v, new_k, new_v, bsz, t):
    past = cache_k.shape[1] // A_KV_HEADS
    lp = bias.shape[2]
    go, gw = dst["agate"]
    row = lambda b: (b, 0)
    batch = lambda b: (b, 0, 0)
    return pl.pallas_call(
        functools.partial(_dsa_att_kernel, t=t, past=past),
        grid=(bsz,),
        in_specs=[pl.BlockSpec((t, A_WIDTH), row),
                  pl.BlockSpec((t, gw), lambda b: (b, go // gw)),
                  pl.BlockSpec((1, t, lp), batch),
                  pl.BlockSpec((1, past * A_KV_HEADS, A_HEAD_DIM), batch),
                  pl.BlockSpec((1, past * A_KV_HEADS, A_HEAD_DIM), batch),
                  pl.BlockSpec((t, A_KV_WIDTH), row),
                  pl.BlockSpec((t, A_KV_WIDTH), row)],
        out_specs=pl.BlockSpec((t, A_WIDTH), row),
        out_shape=jax.ShapeDtypeStruct((bsz * t, A_WIDTH), BF16),
        compiler_params=pltpu.CompilerParams(dimension_semantics=("arbitrary",), vmem_limit_bytes=VMEM_LIMIT),
        name="dsa_att",
    )(aq, z, bias, cache_k, cache_v, new_k, new_v)


def _mla_kernel(q_ref, bgate_ref, k_ref, v_ref, o_ref, acc_ref, m_ref, l_ref, *, tq, kw, l_valid, qpos0):
    ti = pl.program_id(1)
    nkb = _num_key_blocks(ti, tq, kw, l_valid, qpos0)
    scale = (NOPE_DIM + ROPE_DIM) ** -0.5
    _softmax_init(B_HEADS, tq, m_ref, l_ref, acc_ref)

    def att_block(kb, masked):
        r0 = pl.multiple_of(kb * kw, kw)
        if masked:
            allowed, _ = _allowed(ti, kb, 0, (tq, kw), tq, kw, l_valid, qpos0)
            bias = jnp.where(allowed, 0.0, NEG)
        for h in range(B_HEADS):
            qsl = slice(h * B_QK_PAD, (h + 1) * B_QK_PAD)
            s = lax.dot_general(q_ref[:, qsl], k_ref[0, pl.ds(r0, kw), qsl], _NT, preferred_element_type=F32)
            if masked:
                s = s + bias
            _softmax_step(s, scale, h, v_ref[0, pl.ds(r0, kw), h * V_DIM:(h + 1) * V_DIM], m_ref, l_ref, acc_ref)

    n_full = _fully_visible_blocks(ti, tq, kw, l_valid, qpos0)
    lax.fori_loop(0, n_full, lambda kb, c: (att_block(kb, False), c)[1], 0)
    lax.fori_loop(n_full, nkb, lambda kb, c: (att_block(kb, True), c)[1], 0)

    for h in range(B_HEADS):
        sl = slice(h * V_DIM, (h + 1) * V_DIM)
        o = acc_ref[h] / l_ref[h]
        gate = bgate_ref[:, sl]
        o_ref[:, sl] = (o * (gate * jax.nn.sigmoid(gate))).astype(BF16)


def _mla(qcat, z, dst, kcat, v, bsz, t, tq, kw, l_valid, qpos0):
    lp = kcat.shape[1]
    assert lp % kw == 0 and t % tq == 0
    nt = t // tq
    go, gw = dst["bgate"]
    row = lambda b, i: (b * nt + i, 0)
    batch = lambda b, i: (b, 0, 0)
    kern = functools.partial(_mla_kernel, tq=tq, kw=kw, l_valid=l_valid, qpos0=qpos0)
    return pl.pallas_call(
        kern,
        grid=(bsz, nt),
        in_specs=[pl.BlockSpec((tq, B_HEADS * B_QK_PAD), row),
                  pl.BlockSpec((tq, gw), lambda b, i: (b * nt + i, go // gw)),
                  pl.BlockSpec((1, lp, B_HEADS * B_QK_PAD), batch),
                  pl.BlockSpec((1, lp, B_WIDTH), batch)],
        out_specs=pl.BlockSpec((tq, B_WIDTH), row),
        out_shape=jax.ShapeDtypeStruct((bsz * t, B_WIDTH), BF16),
        scratch_shapes=[pltpu.VMEM((B_HEADS, tq, V_DIM), F32),
                        pltpu.VMEM((B_HEADS, tq, LANES), F32),
                        pltpu.VMEM((B_HEADS, tq, LANES), F32)],
        compiler_params=pltpu.CompilerParams(dimension_semantics=("arbitrary", "arbitrary"),
                                             vmem_limit_bytes=VMEM_LIMIT),
        name="mla",
    )(qcat, z, kcat, v)


def _mla_lat_kernel(q_ref, bgate_ref, cc_ref, ckr_ref, nc_ref, nkr_ref, wukt_ref, wuv_ref, o_ref,
                    *, t, past, qpos0):
    rows_n = B_HEADS * t
    ql, qr = [], []
    for h in range(B_HEADS):
        qn = q_ref[:, h * B_QK_PAD: h * B_QK_PAD + NOPE_DIM]
        ql.append(jnp.dot(qn, wukt_ref[h], preferred_element_type=F32).astype(BF16))
        qr.append(q_ref[:, h * B_QK_PAD + NOPE_DIM: h * B_QK_PAD + NOPE_DIM + ROPE_DIM])
    ql = jnp.concatenate(ql, axis=0)
    qr = jnp.concatenate(qr, axis=0)
    cc = cc_ref[0].astype(BF16)
    ckr_t = ckr_ref[0].astype(BF16)
    nc, nkr = nc_ref[...], nkr_ref[...]
    scale = (NOPE_DIM + ROPE_DIM) ** -0.5

    def scores(lat, rope_part, ncols, col0):
        s = (lax.dot_general(ql, lat, _NT, preferred_element_type=F32) + rope_part) * scale
        rows = lax.broadcasted_iota(jnp.int32, (rows_n, ncols), 0)
        cols = lax.broadcasted_iota(jnp.int32, (rows_n, ncols), 1) + col0
        qchunk = (qpos0 + lax.rem(rows, t)) // CHUNK
        return jnp.where((cols // CHUNK) <= qchunk, s, NEG)

    s1 = scores(cc, jnp.dot(qr, ckr_t, preferred_element_type=F32), past, 0)
    s2 = scores(nc, lax.dot_general(qr, nkr, _NT, preferred_element_type=F32), t, past)
    m = jnp.maximum(jnp.max(s1, axis=-1, keepdims=True), jnp.max(s2, axis=-1, keepdims=True))
    e1, e2 = jnp.exp(s1 - m), jnp.exp(s2 - m)
    l = jnp.sum(e1, axis=-1, keepdims=True) + jnp.sum(e2, axis=-1, keepdims=True)
    ol = (jnp.dot(e1.astype(BF16), cc, preferred_element_type=F32)
          + jnp.dot(e2.astype(BF16), nc, preferred_element_type=F32)) / l
    ol = ol.astype(BF16)
    for h in range(B_HEADS):
        sl = slice(h * V_DIM, (h + 1) * V_DIM)
        o = jnp.dot(ol[h * t:(h + 1) * t], wuv_ref[:, sl], preferred_element_type=F32)
        gate = bgate_ref[:, sl]
        o_ref[:, sl] = (o * (gate * jax.nn.sigmoid(gate))).astype(BF16)


def _mla_lat(qcat, z, dst, cache_ckv, cache_kr, layer, new_ckv, new_kr, wukt, wuv, bsz, t, qpos0):
    past = cache_ckv.shape[2]
    go, gw = dst["bgate"]
    row = lambda b: (b, 0)
    batch = lambda b: (b, 0, 0)
    const = lambda a: pl.BlockSpec(a.shape, lambda b: (0,) * a.ndim)
    kern = functools.partial(_mla_lat_kernel, t=t, past=past, qpos0=qpos0)
    return pl.pallas_call(
        kern,
        grid=(bsz,),
        in_specs=[pl.BlockSpec((t, B_HEADS * B_QK_PAD), row),
                  pl.BlockSpec((t, gw), lambda b: (b, go // gw)),
                  pl.BlockSpec((None, 1, past, KV_LORA), lambda b: (layer, b, 0, 0)),
                  pl.BlockSpec((None, 1, ROPE_DIM, past), lambda b: (layer, b, 0, 0)),
                  pl.BlockSpec((t, KV_LORA), row),
                  pl.BlockSpec((t, ROPE_DIM), row),
                  const(wukt), const(wuv)],
        out_specs=pl.BlockSpec((t, B_WIDTH), row),
        out_shape=jax.ShapeDtypeStruct((bsz * t, B_WIDTH), BF16),
        compiler_params=pltpu.CompilerParams(dimension_semantics=("arbitrary",), vmem_limit_bytes=VMEM_LIMIT),
        name="mla_lat",
    )(qcat, z, cache_ckv, cache_kr, new_ckv, new_kr, wukt, wuv)


def _out_kernel(ga_ref, gb_ref, mg_ref, x_ref, woa_ref, wob_ref, wout_ref, fg_ref, o_ref, *, d, final):
    ya = jnp.dot(ga_ref[...], woa_ref[...], preferred_element_type=F32)
    yb = jnp.dot(gb_ref[...], wob_ref[...], preferred_element_type=F32)
    m = jax.nn.sigmoid(mg_ref[:, :d]) * ya + jax.nn.sigmoid(mg_ref[:, d:]) * yb
    h = x_ref[...] + jnp.dot(m.astype(BF16), wout_ref[...], preferred_element_type=F32)
    o_ref[...] = _rms(h, fg_ref[...]) if final else h


def _out(ga, gb, z, dst, x, woa, wob, wout, fg, final):
    m, d = x.shape
    tm = min(m, 256)
    assert m % tm == 0 and dst["mg"][0] == 0
    row = lambda i: (i, 0)
    const = lambda a: pl.BlockSpec(a.shape, lambda i: (0, 0), pipeline_mode=pl.Buffered(1))
    return pl.pallas_call(
        functools.partial(_out_kernel, d=d, final=final),
        grid=(m // tm,),
        in_specs=[pl.BlockSpec((tm, A_WIDTH), row), pl.BlockSpec((tm, B_WIDTH), row),
                  pl.BlockSpec((tm, 2 * d), row), pl.BlockSpec((tm, d), row),
                  const(woa), const(wob), const(wout), const(fg)],
        out_specs=pl.BlockSpec((tm, d), row),
        out_shape=jax.ShapeDtypeStruct((m, d), F32),
        compiler_params=pltpu.CompilerParams(dimension_semantics=("arbitrary",), vmem_limit_bytes=VMEM_LIMIT),
        name="out",
    )(ga, gb, z, x, woa, wob, wout, fg)


def _all_keys(cache, new, bsz, t, lp):
    new = new.reshape(bsz, t, new.shape[-1])
    if cache is None and lp == t:
        return new
    parts = ([] if cache is None else [cache.reshape(bsz, cache.shape[1], -1).astype(BF16)]) + [new]
    kk = jnp.concatenate(parts, axis=1)
    return jnp.pad(kk, ((0, 0), (0, lp - kk.shape[1]), (0, 0)))


def _trunk(x, caches, final_g, weights):
    bsz, t, d = x.shape
    depth = weights["w_in"].shape[0]
    past = 0 if caches is None else caches[0].shape[2]
    l_valid = past + t
    tq = min(t, QUERY_TILE)
    kw = KEY_BLOCK if t >= QUERY_TILE else -(-l_valid // LANES) * LANES
    lp = -(-l_valid // kw) * kw
    _, dst, _ = _layout(d)
    pos = past + jnp.arange(t, dtype=jnp.int32)
    t_scale = jnp.concatenate([jnp.full((t, IDX_HEADS), IDX_HEADS ** -0.5, F32),
                               jnp.ones((t, LANES - IDX_DIM - IDX_HEADS), F32)], axis=1)
    tabs = [jnp.stack(_rope_tables(pos, A_ROT, A_HEAD_DIM, 1)),
            jnp.stack(_rope_tables(pos, IDX_ROT, IDX_DIM, 2)),
            jnp.stack(_rope_tables(pos, IDX_ROT, IDX_DIM, 1, extra=t_scale)),
            jnp.stack(_rope_tables(pos, ROPE_DIM, ROPE_DIM, 2))]
    h = x.reshape(bsz * t, d)
    new_rows = []
    for l in range(depth):
        w = {k: v[l] for k, v in weights.items()}
        z = _proj(h, w["norm_g"], w["w_in"])
        aq, iq, ak, av, ikw, krp, ckv, qcat, kb, vb, ikb, ckvb, krb = _prep(
            z, dst, bsz, t, tabs, w["q_norm_g"], w["kv_norm_g"], w["w_uq"])
        c = (None,) * 5 if caches is None else tuple(cc[l] for cc in caches)
        if caches is not None and t < QUERY_TILE:
            bias = _dsa_sel(iq, ikw, jnp.swapaxes(caches[2], 2, 3), l, ikb, bsz, t, past)
            interleaved = lambda a: a.reshape(bsz, past * A_KV_HEADS, A_HEAD_DIM)
            ga = _dsa_att(aq, z, dst, bias, interleaved(c[0]), interleaved(c[1]), kb, vb, bsz, t)
            gb = _mla_lat(qcat, z, dst, caches[3], jnp.swapaxes(caches[4], 2, 3), l, ckvb, krb, w["w_ukt"], w["w_uv"], bsz, t, past)
        else:
            k_all = _all_keys(c[0], kb, bsz, t, lp)
            v_all = _all_keys(c[1], vb, bsz, t, lp)
            ik_all = _all_keys(c[2], ikb, bsz, t, lp)
            ga = _dsa(aq, iq, ikw, z, dst, k_all, v_all, ik_all, bsz, t, tq, kw, l_valid, past)
            ckv_all = _all_keys(c[3], ckvb, bsz, t, lp)
            kr_all = _all_keys(c[4], krb, bsz, t, lp)
            kcat, vv = _kvup(ckv_all.reshape(bsz * lp, KV_LORA), kr_all.reshape(bsz * lp, ROPE_DIM), w["w_ukv"])
            tq_mla = MLA_QUERY_TILE if t % MLA_QUERY_TILE == 0 else tq
            gb = _mla(qcat, z, dst, kcat.reshape(bsz, lp, -1), vv.reshape(bsz, lp, -1), bsz, t, tq_mla, kw, l_valid, past)
        h = _out(ga, gb, z, dst, h, w["w_o_a"], w["w_o_b"], w["w_out"], final_g, final=(l == depth - 1))
        new_rows.append((ak.reshape(bsz, t, A_KV_HEADS, A_HEAD_DIM), av.reshape(bsz, t, A_KV_HEADS, A_HEAD_DIM),
                         ikw[:, :IDX_DIM].reshape(bsz, t, IDX_DIM), ckv.reshape(bsz, t, KV_LORA),
                         krp[:, :ROPE_DIM].reshape(bsz, t, ROPE_DIM)))
    stacked = [jnp.stack([r[i] for r in new_rows], axis=0) for i in range(5)]
    return h.reshape(bsz, t, d), stacked


def kernel(x_prompt, x_sample, cache_a_k, cache_a_v, cache_idx_k, cache_mla_ckv, cache_mla_krope,
           norm_g, w_in, w_uq, q_norm_g, kv_norm_g, w_uk, w_uv, w_o_a, w_o_b, w_out, final_g):
    depth, d, _ = w_in.shape
    weights = {
        "norm_g": norm_g.reshape(depth, 1, d),
        "w_in": jnp.stack([_regroup_w_in(jnp.swapaxes(w_in[l], 0, 1), d) for l in range(depth)]),
        "w_uq": jnp.concatenate([w_uq[..., :NOPE_DIM].reshape(depth, Q_LORA, B_HEADS * NOPE_DIM),
                                 w_uq[..., NOPE_DIM:].reshape(depth, Q_LORA, B_HEADS * ROPE_DIM)], axis=-1).astype(BF16),
        "q_norm_g": q_norm_g.reshape(depth, 1, Q_LORA),
        "kv_norm_g": kv_norm_g.reshape(depth, 1, KV_LORA),
        "w_ukv": jnp.concatenate([w_uk.reshape(depth, KV_LORA, B_HEADS * NOPE_DIM),
                                  w_uv.reshape(depth, KV_LORA, B_WIDTH)], axis=-1).astype(BF16),
        "w_ukt": jnp.transpose(w_uk, (0, 2, 3, 1)).astype(BF16),
        "w_uv": w_uv.reshape(depth, KV_LORA, B_WIDTH).astype(BF16),
        "w_o_a": w_o_a.astype(BF16), "w_o_b": w_o_b.astype(BF16), "w_out": w_out.astype(BF16),
    }
    fg = final_g.reshape(1, d)
    y_p, (pk, pv, pik, pckv, pkr) = _trunk(x_prompt, None, fg, weights)
    caches = (cache_a_k, cache_a_v, cache_idx_k, cache_mla_ckv, cache_mla_krope)
    y_s, (sk, sv, sik, sckv, skr) = _trunk(x_sample, caches, fg, weights)
    return (y_p, y_s, pk, pv, pik, pckv, pkr, sk, sv, sik, sckv, skr)
```

```python
import functools

import jax
import jax.numpy as jnp
from jax import lax
from jax.experimental import pallas as pl
from jax.experimental.pallas import tpu as pltpu

F32 = jnp.float32
BF16 = jnp.bfloat16

CHUNK = 64
ROPE_THETA = 500000.0
EPS = 1e-6

A_HEADS = 8
A_KV_HEADS = 2
A_HEAD_DIM = 128
A_ROT = A_HEAD_DIM // 4
A_WIDTH = A_HEADS * A_HEAD_DIM
A_KV_WIDTH = A_KV_HEADS * A_HEAD_DIM
IDX_HEADS = 16
IDX_DIM = 64
IDX_ROT = IDX_DIM // 4
TOPK_MAX = 256

B_HEADS = 8
Q_LORA = 512
KV_LORA = 512
NOPE_DIM = 128
ROPE_DIM = 64
V_DIM = 128
B_WIDTH = B_HEADS * V_DIM
B_QK_PAD = 256

LANES = 128
KEY_BLOCK = 256
QUERY_TILE = 256
PROJ_ROWS = 512
PROJ_TILE = 1024
MLA_QUERY_TILE = 512
ROW_STRIP = 64
INT_MIN = -(2 ** 31)
LOG2_E = 1.4426950408889634
NEG = -1e30
MASKED = -1e38
VMEM_LIMIT = 56 * 1024 * 1024

_SEGS = (("aq", A_WIDTH), ("ak", A_KV_WIDTH), ("av", A_KV_WIDTH), ("agate", A_WIDTH),
         ("iq", IDX_HEADS * IDX_DIM), ("ik", IDX_DIM), ("iw", IDX_HEADS),
         ("cq", Q_LORA), ("ckv", KV_LORA), ("kr", ROPE_DIM), ("bgate", B_WIDTH), ("mg", None))


def _layout(d_model):
    src, off = {}, 0
    for name, w in _SEGS:
        w = 2 * d_model if w is None else w
        src[name] = (off, w)
        off += w
    order = (("mg", 2 * d_model), ("aq", A_WIDTH), ("agate", A_WIDTH), ("iq", IDX_HEADS * IDX_DIM),
             ("bgate", B_WIDTH), ("cq", Q_LORA), ("ckv", KV_LORA), ("ak", A_KV_WIDTH), ("av", A_KV_WIDTH),
             ("ikw", LANES), ("krp", LANES))
    dst, off = {}, 0
    for name, w in order:
        assert off % w == 0
        dst[name] = (off, w)
        off += w
    return src, dst, off


def _regroup_kernel(w_ref, o_ref, *, src, dst):
    cols = w_ref.shape[1]

    def piece(name):
        o, w = src[name]
        return w_ref[o:o + w, :].astype(BF16)

    def zeros(n):
        return jnp.zeros((n, cols), BF16)

    for name, (o, w) in dst.items():
        if name == "ikw":
            val = jnp.concatenate([piece("ik"), piece("iw"), zeros(LANES - IDX_DIM - IDX_HEADS)], axis=0)
        elif name == "krp":
            val = jnp.concatenate([piece("kr"), zeros(LANES - ROPE_DIM)], axis=0)
        else:
            val = piece(name)
        o_ref[o:o + w, :] = val
    total = max(o + w for o, w in dst.values())
    if o_ref.shape[0] > total:
        o_ref[total:, :] = zeros(o_ref.shape[0] - total)


def _regroup_w_in(w_in_t, d_model):
    src, dst, total = _layout(d_model)
    tc = 256
    padded = -(-total // PROJ_TILE) * PROJ_TILE
    assert d_model % tc == 0
    return pl.pallas_call(
        functools.partial(_regroup_kernel, src=src, dst=dst),
        grid=(d_model // tc,),
        in_specs=[pl.BlockSpec((w_in_t.shape[0], tc), lambda i: (0, i))],
        out_specs=pl.BlockSpec((padded, tc), lambda i: (0, i)),
        out_shape=jax.ShapeDtypeStruct((padded, d_model), BF16),
        compiler_params=pltpu.CompilerParams(dimension_semantics=("arbitrary",), vmem_limit_bytes=VMEM_LIMIT),
        name="regroup",
    )(w_in_t)


def _rope_tables(pos, rot, head, heads_per_vreg, extra=None):
    half = rot // 2
    inv_freq = ROPE_THETA ** (-jnp.arange(half, dtype=F32) * (2.0 / rot))
    ang = pos.astype(F32)[:, None] * inv_freq[None, :]
    cos, sin = jnp.cos(ang), jnp.sin(ang)
    t = pos.shape[0]
    one = jnp.ones((t, head - rot), F32)
    zero = lambda n: jnp.zeros((t, n), F32)
    c = jnp.concatenate([cos, cos, one], axis=1)
    sa = jnp.concatenate([-sin, zero(head - half)], axis=1)
    sb = jnp.concatenate([zero(half), sin, zero(head - rot)], axis=1)
    c, sa, sb = (jnp.tile(a, (1, heads_per_vreg)) for a in (c, sa, sb))
    if extra is not None:
        c = jnp.concatenate([c, extra], axis=1)
        sa = jnp.concatenate([sa, jnp.zeros_like(extra)], axis=1)
        sb = jnp.concatenate([sb, jnp.zeros_like(extra)], axis=1)
    assert c.shape[1] == LANES
    return c, sa, sb


def _rope(x, c, sa, sb, half):
    up = pltpu.roll(x, LANES - half, 1)
    dn = pltpu.roll(x, half, 1)
    return x * c + up * sa + dn * sb


def _rms(x, g):
    ms = jnp.mean(x * x, axis=-1, keepdims=True)
    return (x * lax.rsqrt(ms + EPS)) * g


def _lane_tile(x, width):
    return x if width == LANES else jnp.concatenate([x] * (width // LANES), axis=1)


_NT = (((1,), (1,)), ((), ()))


def _proj_kernel(x_ref, g_ref, w_ref, ta_ref, ti_ref, z_ref, aq_ref, iq_ref, xn_ref, hold_ref, *, j_aq, j_iq):
    j = pl.program_id(1)

    @pl.when(j == 0)
    def _():
        xn_ref[...] = _rms(x_ref[...], g_ref[...]).astype(BF16)

    def tile():
        return lax.dot_general(xn_ref[...], w_ref[...], _NT, preferred_element_type=F32)

    def rope_held(tab_ref, out_ref, half):
        c, sa, sb = tab_ref[0], tab_ref[1], tab_ref[2]
        for v in range(hold_ref.shape[1] // LANES):
            sl = slice(v * LANES, (v + 1) * LANES)
            out_ref[:, sl] = _rope(hold_ref[:, sl], c, sa, sb, half).astype(BF16)

    held = (j == j_aq) | (j == j_iq)
    after_aq = j == j_aq + 1
    after_iq = j == j_iq + 1

    @pl.when(held)
    def _():
        hold_ref[...] = tile()

    @pl.when(after_aq)
    def _():
        z_ref[...] = tile()
        rope_held(ta_ref, aq_ref, A_ROT // 2)

    @pl.when(after_iq)
    def _():
        z_ref[...] = tile()
        rope_held(ti_ref, iq_ref, IDX_ROT // 2)

    @pl.when(jnp.logical_not(held | after_aq | after_iq))
    def _():
        z_ref[...] = tile()


def _proj(x, g, w, dst, n, tab_a, tab_i):
    m, d = x.shape
    tm = min(m, PROJ_ROWS)
    tn = PROJ_TILE
    assert m % tm == 0 and w.shape[0] % tn == 0 and dst["aq"][1] == tn and dst["iq"][1] == tn
    j_aq, j_iq = dst["aq"][0] // tn, dst["iq"][0] // tn
    assert dst["aq"][0] % tn == 0 and dst["iq"][0] % tn == 0 and j_iq > j_aq + 1 and (j_iq + 1) * tn < n
    ntt = tab_a.shape[1] // tm
    tspec = pl.BlockSpec((3, tm, LANES), lambda i, j: (0, i % ntt, 0))
    zcol = lambda i, j: (i, jnp.where((j == j_aq) | (j == j_iq), j - 1, j))
    return pl.pallas_call(
        functools.partial(_proj_kernel, j_aq=j_aq, j_iq=j_iq),
        grid=(m // tm, w.shape[0] // tn),
        in_specs=[pl.BlockSpec((tm, d), lambda i, j: (i, 0)),
                  pl.BlockSpec((1, d), lambda i, j: (0, 0)),
                  pl.BlockSpec((tn, d), lambda i, j: (j, 0)),
                  tspec, tspec],
        out_specs=[pl.BlockSpec((tm, tn), zcol),
                   pl.BlockSpec((tm, tn), lambda i, j: (i, 0)),
                   pl.BlockSpec((tm, tn), lambda i, j: (i, 0))],
        out_shape=[jax.ShapeDtypeStruct((m, n), F32), jax.ShapeDtypeStruct((m, tn), BF16),
                   jax.ShapeDtypeStruct((m, tn), BF16)],
        scratch_shapes=[pltpu.VMEM((tm, d), BF16), pltpu.VMEM((tm, tn), F32)],
        compiler_params=pltpu.CompilerParams(dimension_semantics=("arbitrary", "arbitrary"),
                                             vmem_limit_bytes=VMEM_LIMIT),
        name="proj",
    )(x, g, w, tab_a, tab_i)


def _prep_kernel(cq_ref, ckv_ref, ak_ref, av_ref, ikw_ref, krp_ref,
                 ta_ref, tk_ref, tr_ref, qg_ref, kvg_ref, wuq_ref,
                 ak_o, av_o, ikw_o, krp_o, ckv_o, qcat_o, kb_o, vb_o, ikb_o, ckvb_o, krb_o):
    ca, saa, sba = ta_ref[0], ta_ref[1], ta_ref[2]
    ck, sak, sbk = tk_ref[0], tk_ref[1], tk_ref[2]
    cr, sar, sbr = tr_ref[0], tr_ref[1], tr_ref[2]
    tm = cq_ref.shape[0]
    for h in range(A_KV_HEADS):
        sl = slice(h * LANES, (h + 1) * LANES)
        ak = _rope(ak_ref[:, sl], ca, saa, sba, A_ROT // 2)
        ak_o[pl.ds(h, tm, stride=A_KV_HEADS), :] = ak
        av_o[pl.ds(h, tm, stride=A_KV_HEADS), :] = av_ref[:, sl]
        kb_o[:, sl] = ak.astype(BF16)
    vb_o[...] = av_ref[...].astype(BF16)
    ikw = _rope(ikw_ref[...], ck, sak, sbk, IDX_ROT // 2)
    ikw_o[...] = ikw
    ikb_o[...] = ikw[:, :IDX_DIM].astype(BF16)
    krp = _rope(krp_ref[...], cr, sar, sbr, ROPE_DIM // 2)
    krp_o[...] = krp
    krb_o[...] = krp[:, :ROPE_DIM].astype(BF16)
    ckv = _rms(ckv_ref[...], kvg_ref[...])
    ckv_o[...] = ckv
    ckvb_o[...] = ckv.astype(BF16)
    cqn = _rms(cq_ref[...], qg_ref[...]).astype(BF16)
    qb = jnp.dot(cqn, wuq_ref[...], preferred_element_type=F32)
    nope = B_HEADS * NOPE_DIM
    low = lax.broadcasted_iota(jnp.int32, (tm, LANES), 1) < ROPE_DIM
    for v in range(B_HEADS * ROPE_DIM // LANES):
        r = _rope(qb[:, nope + v * LANES: nope + (v + 1) * LANES], cr, sar, sbr, ROPE_DIM // 2)
        halves = (jnp.where(low, r, 0.0), jnp.where(low, pltpu.roll(r, ROPE_DIM, 1), 0.0))
        for j in range(2):
            h = 2 * v + j
            qcat_o[:, h * B_QK_PAD: h * B_QK_PAD + NOPE_DIM] = qb[:, h * NOPE_DIM:(h + 1) * NOPE_DIM].astype(BF16)
            qcat_o[:, h * B_QK_PAD + NOPE_DIM:(h + 1) * B_QK_PAD] = halves[j].astype(BF16)


def _prep(z, dst, bsz, t, tabs, q_norm_g, kv_norm_g, wuq):
    m = bsz * t
    tm = min(t, 256)
    nt = t // tm
    assert t % tm == 0

    def zspec(name):
        o, w = dst[name]
        return pl.BlockSpec((tm, w), lambda b, i, blk=o // w: (b * nt + i, blk))

    def tspec():
        return pl.BlockSpec((3, tm, LANES), lambda b, i: (0, i, 0))

    def ospec(w):
        return pl.BlockSpec((tm, w), lambda b, i: (b * nt + i, 0))

    def full(a):
        return pl.BlockSpec(a.shape, lambda b, i: (0,) * a.ndim)

    outs = [("interleaved", F32), ("interleaved", F32), (LANES, F32), (LANES, F32),
            (KV_LORA, F32), (B_HEADS * B_QK_PAD, BF16),
            (A_KV_WIDTH, BF16), (A_KV_WIDTH, BF16), (IDX_DIM, BF16), (KV_LORA, BF16), (ROPE_DIM, BF16)]
    return pl.pallas_call(
        _prep_kernel,
        grid=(bsz, nt),
        in_specs=[zspec("cq"), zspec("ckv"), zspec("ak"), zspec("av"), zspec("ikw"),
                  zspec("krp"), tspec(), tspec(), tspec(), full(q_norm_g), full(kv_norm_g), full(wuq)],
        out_specs=[pl.BlockSpec((tm * A_KV_HEADS, A_HEAD_DIM), lambda b, i: (b * nt + i, 0)) if w == "interleaved" else ospec(w)
                   for w, _ in outs],
        out_shape=[jax.ShapeDtypeStruct((m * A_KV_HEADS, A_HEAD_DIM) if w == "interleaved" else (m, w), dt) for w, dt in outs],
        compiler_params=pltpu.CompilerParams(dimension_semantics=("arbitrary", "arbitrary"),
                                             vmem_limit_bytes=VMEM_LIMIT),
        name="prep",
    )(z, z, z, z, z, z, *tabs, q_norm_g, kv_norm_g, wuq)


def _kvup_kernel(c_ref, kr_ref, w_ref, kcat_ref, v_ref):
    r = jnp.dot(c_ref[...], w_ref[...], preferred_element_type=F32)
    n = B_HEADS * NOPE_DIM
    tail = jnp.concatenate([kr_ref[...], jnp.zeros((kr_ref.shape[0], B_QK_PAD - NOPE_DIM - ROPE_DIM), BF16)], axis=1)
    for h in range(B_HEADS):
        kcat_ref[:, h * B_QK_PAD: h * B_QK_PAD + NOPE_DIM] = r[:, h * NOPE_DIM:(h + 1) * NOPE_DIM].astype(BF16)
        kcat_ref[:, h * B_QK_PAD + NOPE_DIM:(h + 1) * B_QK_PAD] = tail
    v_ref[...] = r[:, n:].astype(BF16)


def _kvup(ckv, kr, w):
    m, c = ckv.shape
    tm = next(t for t in (512, 384, 256, 128) if m % t == 0)
    row = lambda i: (i, 0)
    return pl.pallas_call(
        _kvup_kernel,
        grid=(m // tm,),
        in_specs=[pl.BlockSpec((tm, c), row), pl.BlockSpec((tm, ROPE_DIM), row), pl.BlockSpec(w.shape, lambda i: (0, 0))],
        out_specs=[pl.BlockSpec((tm, B_HEADS * B_QK_PAD), row), pl.BlockSpec((tm, B_WIDTH), row)],
        out_shape=[jax.ShapeDtypeStruct((m, B_HEADS * B_QK_PAD), BF16), jax.ShapeDtypeStruct((m, B_WIDTH), BF16)],
        compiler_params=pltpu.CompilerParams(dimension_semantics=("arbitrary",), vmem_limit_bytes=VMEM_LIMIT),
        name="kv_up",
    )(ckv, kr, w)


def _visible_cols(tile, tq, l_valid, qpos0):
    p_last = qpos0 + (tile + 1) * tq - 1
    n = (p_last // CHUNK + 1) * CHUNK
    return min(n, l_valid) if isinstance(n, int) else jnp.minimum(n, l_valid)


def _num_key_blocks(tile, tq, kw, l_valid, qpos0):
    return (_visible_cols(tile, tq, l_valid, qpos0) + kw - 1) // kw


def _fully_visible_blocks(tile, tq, kw, l_valid, qpos0):
    first_row_cols = jnp.minimum(((qpos0 + tile * tq) // CHUNK + 1) * CHUNK, l_valid)
    return first_row_cols // kw


def _allowed(tile, kb, rows0, shape, tq, kw, l_valid, qpos0):
    rows = lax.broadcasted_iota(jnp.int32, shape, 0) + rows0
    cols = lax.broadcasted_iota(jnp.int32, shape, 1) + kb * kw
    qchunk = (qpos0 + tile * tq + rows) // CHUNK
    return ((cols // CHUNK) <= qchunk) & (cols < l_valid), cols


def _softmax_step(s, scale, h, vblk, m_ref, l_ref, acc_ref):
    kw = s.shape[1]
    c = scale * LOG2_E
    m_prev = m_ref[h]
    m_new = jnp.maximum(m_prev, jnp.max(s, axis=-1, keepdims=True))
    alpha = jnp.exp2((m_prev - m_new) * c)
    p = jnp.exp2((s - _lane_tile(m_new, kw)) * c)
    l_ref[h] = alpha * l_ref[h] + jnp.sum(p, axis=-1, keepdims=True)
    acc_ref[h] = alpha * acc_ref[h] + jnp.dot(p.astype(BF16), vblk, preferred_element_type=F32)
    m_ref[h] = m_new


def _softmax_init(heads, tq, m_ref, l_ref, acc_ref):
    for h in range(heads):
        m_ref[h] = jnp.full((tq, LANES), NEG, F32)
        l_ref[h] = jnp.zeros((tq, LANES), F32)
        acc_ref[h] = jnp.zeros((tq, acc_ref.shape[2]), F32)


def _masked_score(score, allowed):
    return jnp.where(allowed, score, MASKED)


def _key_to_f32(key):
    return lax.bitcast_convert_type(key ^ ((key >> 31) & 0x7FFFFFFF), F32)


def _selection_bias(score, cols, thr, jcut, allowed):
    sel = ((score > thr) | ((score == thr) & (cols < jcut))) & allowed
    return jnp.where(sel, 0.0, NEG)


def _topk_thresholds(strips, rs, kf, thr_ref, jcut_ref, need_ref):
    def count(pieces, pred):
        acc = jnp.zeros((rs, LANES), F32)
        for load, col0 in pieces:
            acc = acc + jnp.where(pred(load(), col0), 1.0, 0.0)
        return jnp.sum(acc, axis=-1, keepdims=True)

    def bit_step(b, carry):
        bit = jnp.left_shift(jnp.int32(1), 31 - b)
        for rows, pieces in strips:
            t = thr_ref[rows, :]
            cand = t + bit
            cand_f = _key_to_f32(cand)
            cnt = count(pieces, lambda s, c0: s >= cand_f)
            thr_ref[rows, :] = jnp.where(cnt >= kf, cand, t)
        return carry

    thr_ref[...] = jnp.full(thr_ref.shape, INT_MIN, jnp.int32)
    lax.fori_loop(0, 32, bit_step, 0)

    jcut_ref[...] = jnp.full(jcut_ref.shape, 4096, jnp.int32)
    flag = jnp.int32(0)
    for rows, pieces in strips:
        t = _key_to_f32(thr_ref[rows, :])
        excess = (count(pieces, lambda s, c0: s >= t) > kf) & (t[:, :1] > MASKED)
        flag = jnp.maximum(flag, jnp.max(jnp.where(excess, 1, 0)))

    @pl.when(flag > 0)
    def _():
        lane = lax.broadcasted_iota(jnp.int32, (rs, LANES), 1)
        for rows, pieces in strips:
            t = _key_to_f32(thr_ref[rows, :])
            need = kf - count(pieces, lambda s, c0: s > t)
            need_ref[rows, :] = jnp.broadcast_to(need, (rs, LANES))

        def col_step(b, carry):
            bit = jnp.left_shift(jnp.int32(1), 11 - b)
            for rows, pieces in strips:
                t, j = _key_to_f32(thr_ref[rows, :]), jcut_ref[rows, :]
                cand = j + bit
                f = count(pieces, lambda s, c0: (s == t) & (lane + c0 < cand))
                jcut_ref[rows, :] = jnp.where(f <= need_ref[rows, :], cand, j)
            return carry

        jcut_ref[...] = jnp.zeros(jcut_ref.shape, jnp.int32)
        lax.fori_loop(0, 12, col_step, 0)

    return flag


def _dsa_kernel(aq_ref, iq_ref, ikw_ref, agate_ref, k_ref, v_ref, ik_ref, o_ref,
                key_ref, thr_ref, jcut_ref, need_ref, ties_ref, bias_ref, iwb_ref, qst_ref, qs_ref, acc_ref, m_ref, l_ref,
                *, nt, tq, kw, l_valid, qpos0, topk):
    nkb_max = key_ref.shape[0] // nt
    nkb_of = [_num_key_blocks(ti, tq, kw, l_valid, qpos0) for ti in range(nt)]
    rs = qst_ref.shape[1] // IDX_HEADS
    ns = tq // rs
    nc = kw // LANES
    kf = float(topk)

    def score_tile(ti, carry):
        r0 = pl.multiple_of(ti * tq, tq)
        iw = ikw_ref[pl.ds(r0, tq), IDX_DIM:IDX_DIM + IDX_HEADS] * (IDX_DIM ** -0.5)
        for h in range(IDX_HEADS):
            iwb_ref[h] = jnp.broadcast_to(iw[:, h:h + 1], (tq, LANES))
            for r in range(ns):
                qst_ref[r, h * rs:(h + 1) * rs, :] = iq_ref[pl.ds(r0 + r * rs, rs), h * IDX_DIM:(h + 1) * IDX_DIM]

        def score_block(kb, c2):
            ikb = ik_ref[0, pl.ds(pl.multiple_of(kb * kw, kw), kw), :]
            for r in range(ns):
                rel = lax.dot_general(qst_ref[r], ikb, _NT, preferred_element_type=F32)
                sc = jnp.zeros((rs, kw), F32)
                for h in range(IDX_HEADS):
                    w = _lane_tile(iwb_ref[h, r * rs:(r + 1) * rs, :], kw)
                    sc = sc + jnp.maximum(rel[h * rs:(h + 1) * rs], 0.0) * w
                allowed, _ = _allowed(ti, kb, r * rs, (rs, kw), tq, kw, l_valid, qpos0)
                key_ref[ti * nkb_max + kb, r * rs:(r + 1) * rs, :] = _masked_score(sc, allowed)
            return c2

        lax.fori_loop(0, _num_key_blocks(ti, tq, kw, l_valid, qpos0), score_block, 0)
        return carry

    def piece(ti, kb, s, c):
        return lambda: key_ref[ti * nkb_max + kb, s * rs:(s + 1) * rs, c * LANES:(c + 1) * LANES]

    strips = [(slice(ti * tq + s * rs, ti * tq + (s + 1) * rs),
               [(piece(ti, kb, s, c), kb * kw + c * LANES) for kb in range(nkb_of[ti]) for c in range(nc)])
              for ti in range(nt) for s in range(ns)]

    @pl.when(pl.program_id(1) == 0)
    def _():
        lax.fori_loop(0, nt, score_tile, 0)
        ties_ref[0] = _topk_thresholds(strips, rs, kf, thr_ref, jcut_ref, need_ref)

    ti = pl.program_id(1)
    nkb = _num_key_blocks(ti, tq, kw, l_valid, qpos0)
    r0 = pl.multiple_of(ti * tq, tq)
    thr = _lane_tile(_key_to_f32(thr_ref[pl.ds(r0, tq), :]), kw)
    jcut = _lane_tile(jcut_ref[pl.ds(r0, tq), :], kw)

    def bias_block(kb, masked, ties):
        score = key_ref[ti * nkb_max + kb]
        if ties:
            allowed, cols = _allowed(ti, kb, 0, (tq, kw), tq, kw, l_valid, qpos0)
            bias_ref[kb] = _selection_bias(score, cols, thr, jcut, allowed)
        else:
            sel = score >= thr
            if masked:
                sel = sel & _allowed(ti, kb, 0, (tq, kw), tq, kw, l_valid, qpos0)[0]
            bias_ref[kb] = jnp.where(sel, 0.0, NEG)

    def bias_loop(lo, hi, masked, ties):
        lax.fori_loop(lo, hi, lambda kb, c: (bias_block(kb, masked, ties), c)[1], 0)

    n_full = _fully_visible_blocks(ti, tq, kw, l_valid, qpos0)

    @pl.when(ties_ref[0] == 0)
    def _():
        bias_loop(0, n_full, False, False)
        bias_loop(n_full, nkb, True, False)

    @pl.when(ties_ref[0] != 0)
    def _():
        bias_loop(0, nkb, True, True)

    grp = A_HEADS // A_KV_HEADS
    _softmax_init(A_KV_HEADS, grp * tq, m_ref, l_ref, acc_ref)
    scale = A_HEAD_DIM ** -0.5
    for h in range(A_HEADS):
        g, hh = divmod(h, grp)
        qs_ref[g, hh * tq:(hh + 1) * tq, :] = aq_ref[:, h * A_HEAD_DIM:(h + 1) * A_HEAD_DIM]

    def att_block(kb, carry):
        c0 = pl.multiple_of(kb * kw, kw)
        bias = bias_ref[kb][None]
        for g in range(A_KV_HEADS):
            gsl = slice(g * A_HEAD_DIM, (g + 1) * A_HEAD_DIM)
            s = lax.dot_general(qs_ref[g], k_ref[0, pl.ds(c0, kw), gsl], _NT, preferred_element_type=F32)
            s = (s.reshape(grp, tq, kw) + bias).reshape(grp * tq, kw)
            _softmax_step(s, scale, g, v_ref[0, pl.ds(c0, kw), gsl], m_ref, l_ref, acc_ref)
        return carry

    lax.fori_loop(0, nkb, att_block, 0)

    for h in range(A_HEADS):
        g, hh = divmod(h, grp)
        sl = slice(h * A_HEAD_DIM, (h + 1) * A_HEAD_DIM)
        o = acc_ref[g, hh * tq:(hh + 1) * tq, :] / l_ref[g, hh * tq:(hh + 1) * tq, :]
        gate = agate_ref[:, sl]
        o_ref[:, sl] = (o * (gate * jax.nn.sigmoid(gate))).astype(BF16)


def _dsa(aq, iq, ikw, z, dst, k, v, ik, bsz, t, tq, kw, l_valid, qpos0):
    lp = k.shape[1]
    assert lp < 4096 and lp % kw == 0 and t % tq == 0
    nt = t // tq
    nkb_max = lp // kw
    topk = min(TOPK_MAX, l_valid // 4)
    grp = A_HEADS // A_KV_HEADS
    rs = min(tq, ROW_STRIP)
    go, gw = dst["agate"]
    row = lambda b, i: (b * nt + i, 0)
    whole = lambda b, i: (b, 0)
    batch = lambda b, i: (b, 0, 0)
    kern = functools.partial(_dsa_kernel, nt=nt, tq=tq, kw=kw, l_valid=l_valid, qpos0=qpos0, topk=topk)
    return pl.pallas_call(
        kern,
        grid=(bsz, nt),
        in_specs=[pl.BlockSpec((tq, A_WIDTH), row),
                  pl.BlockSpec((t, IDX_HEADS * IDX_DIM), whole),
                  pl.BlockSpec((t, LANES), whole),
                  pl.BlockSpec((tq, gw), lambda b, i: (b * nt + i, go // gw)),
                  pl.BlockSpec((1, lp, A_KV_WIDTH), batch),
                  pl.BlockSpec((1, lp, A_KV_WIDTH), batch),
                  pl.BlockSpec((1, lp, IDX_DIM), batch)],
        out_specs=pl.BlockSpec((tq, A_WIDTH), row),
        out_shape=jax.ShapeDtypeStruct((bsz * t, A_WIDTH), BF16),
        scratch_shapes=[pltpu.VMEM((nt * nkb_max, tq, kw), F32),
                        pltpu.VMEM((t, LANES), jnp.int32),
                        pltpu.VMEM((t, LANES), jnp.int32),
                        pltpu.VMEM((t, LANES), F32),
                        pltpu.SMEM((1,), jnp.int32),
                        pltpu.VMEM((nkb_max, tq, kw), F32),
                        pltpu.VMEM((IDX_HEADS, tq, LANES), F32),
                        pltpu.VMEM((tq // rs, IDX_HEADS * rs, IDX_DIM), BF16),
                        pltpu.VMEM((A_KV_HEADS, grp * tq, A_HEAD_DIM), BF16),
                        pltpu.VMEM((A_KV_HEADS, grp * tq, A_HEAD_DIM), F32),
                        pltpu.VMEM((A_KV_HEADS, grp * tq, LANES), F32),
                        pltpu.VMEM((A_KV_HEADS, grp * tq, LANES), F32)],
        compiler_params=pltpu.CompilerParams(dimension_semantics=("arbitrary", "arbitrary"),
                                             vmem_limit_bytes=VMEM_LIMIT),
        name="dsa",
    )(aq, iq, ikw, z, k, v, ik)


SEL_BATCHES = 8
SEL_CHUNK = 512


def _dsa_sel_kernel(iq_ref, ikw_ref, cik_ref, nik_ref, bias_ref, key_ref, thr_ref, jcut_ref, need_ref,
                    *, gb, t, past, qpos0, topk):
    lp = past + LANES
    nc = lp // LANES

    def allowed_cols(col0, width):
        rows = lax.broadcasted_iota(jnp.int32, (t, width), 0)
        cols = lax.broadcasted_iota(jnp.int32, (t, width), 1) + col0
        return ((cols // CHUNK) <= (qpos0 + rows) // CHUNK) & (cols < past + t)

    def score_batch(j, carry):
        r0 = pl.multiple_of(j * t, t)
        qst = jnp.concatenate([iq_ref[pl.ds(r0, t), h * IDX_DIM:(h + 1) * IDX_DIM] for h in range(IDX_HEADS)], axis=0)
        iw = ikw_ref[pl.ds(r0, t), IDX_DIM:IDX_DIM + IDX_HEADS] * (IDX_DIM ** -0.5)
        iwb = [jnp.broadcast_to(iw[:, h:h + 1], (t, LANES)) for h in range(IDX_HEADS)]

        def head_sum(rel):
            width = rel.shape[1]
            sc = jnp.zeros((t, width), F32)
            for h in range(IDX_HEADS):
                sc = sc + jnp.maximum(rel[h * t:(h + 1) * t], 0.0) * _lane_tile(iwb[h], width)
            return sc

        for c in range(past // SEL_CHUNK):
            ikt = cik_ref[j, :, c * SEL_CHUNK:(c + 1) * SEL_CHUNK].astype(BF16)
            sc = head_sum(jnp.dot(qst, ikt, preferred_element_type=F32))
            key_ref[j, :, c * SEL_CHUNK:(c + 1) * SEL_CHUNK] = _masked_score(sc, allowed_cols(c * SEL_CHUNK, SEL_CHUNK))
        ik = jnp.concatenate([nik_ref[pl.ds(r0, t), :], jnp.zeros((LANES - t, IDX_DIM), BF16)], axis=0)
        sc = head_sum(lax.dot_general(qst, ik, _NT, preferred_element_type=F32))
        key_ref[j, :, past:] = _masked_score(sc, allowed_cols(past, LANES))
        return carry

    lax.fori_loop(0, gb, score_batch, 0)

    def piece(j, c):
        return lambda: key_ref[j, :, c * LANES:(c + 1) * LANES]

    strips = [(slice(j * t, (j + 1) * t), [(piece(j, c), c * LANES) for c in range(nc)]) for j in range(gb)]
    _topk_thresholds(strips, t, float(topk), thr_ref, jcut_ref, need_ref)

    def bias_batch(j, carry):
        r0 = pl.multiple_of(j * t, t)
        cols = lax.broadcasted_iota(jnp.int32, (t, lp), 1)
        bias_ref[j] = _selection_bias(key_ref[j], cols, _lane_tile(_key_to_f32(thr_ref[pl.ds(r0, t), :]), lp),
                                      _lane_tile(jcut_ref[pl.ds(r0, t), :], lp), allowed_cols(0, lp))
        return carry

    lax.fori_loop(0, gb, bias_batch, 0)


def _dsa_sel(iq, ikw, cache_ik_t, layer, new_ik, bsz, t, qpos0):
    past = cache_ik_t.shape[3]
    lp = past + LANES
    gb = next(g for g in (SEL_BATCHES, 4, 2, 1) if bsz % g == 0)
    assert past % SEL_CHUNK == 0 and t <= LANES and lp < 4096
    topk = min(TOPK_MAX, (past + t) // 4)
    rows = lambda i: (i, 0)
    kern = functools.partial(_dsa_sel_kernel, gb=gb, t=t, past=past, qpos0=qpos0, topk=topk)
    return pl.pallas_call(
        kern,
        grid=(bsz // gb,),
        in_specs=[pl.BlockSpec((gb * t, IDX_HEADS * IDX_DIM), rows),
                  pl.BlockSpec((gb * t, LANES), rows),
                  pl.BlockSpec((None, gb, IDX_DIM, past), lambda i: (layer, i, 0, 0)),
                  pl.BlockSpec((gb * t, IDX_DIM), rows)],
        out_specs=pl.BlockSpec((gb, t, lp), lambda i: (i, 0, 0)),
        out_shape=jax.ShapeDtypeStruct((bsz, t, lp), F32),
        scratch_shapes=[pltpu.VMEM((gb, t, lp), F32),
                        pltpu.VMEM((gb * t, LANES), jnp.int32),
                        pltpu.VMEM((gb * t, LANES), jnp.int32),
                        pltpu.VMEM((gb * t, LANES), F32)],
        compiler_params=pltpu.CompilerParams(dimension_semantics=("arbitrary",), vmem_limit_bytes=VMEM_LIMIT),
        name="dsa_sel",
    )(iq, ikw, cache_ik_t, new_ik)


def _dsa_att_kernel(aq_ref, agate_ref, bias_ref, ck_ref, cv_ref, nk_ref, nv_ref, o_ref, *, t, past):
    grp = A_HEADS // A_KV_HEADS
    scale = A_HEAD_DIM ** -0.5
    pad = jnp.zeros((LANES - t, A_HEAD_DIM), BF16)
    b1, b2 = bias_ref[0, :, :past], bias_ref[0, :, past:]
    for g in range(A_KV_HEADS):
        gsl = slice(g * A_HEAD_DIM, (g + 1) * A_HEAD_DIM)
        kc = ck_ref[0, pl.ds(g, past, stride=A_KV_HEADS), :].astype(BF16)
        vc = cv_ref[0, pl.ds(g, past, stride=A_KV_HEADS), :].astype(BF16)
        kn = jnp.concatenate([nk_ref[:, gsl], pad], axis=0)
        vn = jnp.concatenate([nv_ref[:, gsl], pad], axis=0)
        qs = jnp.concatenate([aq_ref[:, (g * grp + hh) * A_HEAD_DIM:(g * grp + hh + 1) * A_HEAD_DIM]
                              for hh in range(grp)], axis=0)

        def scores(keys, bias):
            s = lax.dot_general(qs, keys, _NT, preferred_element_type=F32) * scale
            return (s.reshape(grp, t, s.shape[1]) + bias[None]).reshape(grp * t, s.shape[1])

        s1, s2 = scores(kc, b1), scores(kn, b2)
        m = jnp.maximum(jnp.max(s1, axis=-1, keepdims=True), jnp.max(s2, axis=-1, keepdims=True))
        e1, e2 = jnp.exp(s1 - m), jnp.exp(s2 - m)
        l = jnp.sum(e1, axis=-1, keepdims=True) + jnp.sum(e2, axis=-1, keepdims=True)
        o = (jnp.dot(e1.astype(BF16), vc, preferred_element_type=F32)
             + jnp.dot(e2.astype(BF16), vn, preferred_element_type=F32)) / l
        for hh in range(grp):
            sl = slice((g * grp + hh) * A_HEAD_DIM, (g * grp + hh + 1) * A_HEAD_DIM)
            gate = agate_ref[:, sl]
            o_ref[:, sl] = (o[hh * t:(hh + 1) * t] * (gate * jax.nn.sigmoid(gate))).astype(BF16)


def _dsa_att(aq, z, dst, bias, cache_k, cache_v, new_k, new_v, bsz, t):
    past = cache_k.shape[1] // A_KV_HEADS
    lp = bias.shape[2]
    go, gw = dst["agate"]
    row = lambda b: (b, 0)
    batch = lambda b: (b, 0, 0)
    return pl.pallas_call(
        functools.partial(_dsa_att_kernel, t=t, past=past),
        grid=(bsz,),
        in_specs=[pl.BlockSpec((t, A_WIDTH), row),
                  pl.BlockSpec((t, gw), lambda b: (b, go // gw)),
                  pl.BlockSpec((1, t, lp), batch),
                  pl.BlockSpec((1, past * A_KV_HEADS, A_HEAD_DIM), batch),
                  pl.BlockSpec((1, past * A_KV_HEADS, A_HEAD_DIM), batch),
                  pl.BlockSpec((t, A_KV_WIDTH), row),
                  pl.BlockSpec((t, A_KV_WIDTH), row)],
        out_specs=pl.BlockSpec((t, A_WIDTH), row),
        out_shape=jax.ShapeDtypeStruct((bsz * t, A_WIDTH), BF16),
        compiler_params=pltpu.CompilerParams(dimension_semantics=("arbitrary",), vmem_limit_bytes=VMEM_LIMIT),
        name="dsa_att",
    )(aq, z, bias, cache_k, cache_v, new_k, new_v)


def _mla_kernel(q_ref, bgate_ref, k_ref, v_ref, o_ref, acc_ref, m_ref, l_ref, *, tq, kw, l_valid, qpos0):
    ti = pl.program_id(1)
    nkb = _num_key_blocks(ti, tq, kw, l_valid, qpos0)
    scale = (NOPE_DIM + ROPE_DIM) ** -0.5
    _softmax_init(B_HEADS, tq, m_ref, l_ref, acc_ref)

    def att_block(kb, masked):
        r0 = pl.multiple_of(kb * kw, kw)
        if masked:
            allowed, _ = _allowed(ti, kb, 0, (tq, kw), tq, kw, l_valid, qpos0)
            bias = jnp.where(allowed, 0.0, NEG)
        for h in range(B_HEADS):
            qsl = slice(h * B_QK_PAD, (h + 1) * B_QK_PAD)
            s = lax.dot_general(q_ref[:, qsl], k_ref[0, pl.ds(r0, kw), qsl], _NT, preferred_element_type=F32)
            if masked:
                s = s + bias
            _softmax_step(s, scale, h, v_ref[0, pl.ds(r0, kw), h * V_DIM:(h + 1) * V_DIM], m_ref, l_ref, acc_ref)

    n_full = _fully_visible_blocks(ti, tq, kw, l_valid, qpos0)
    lax.fori_loop(0, n_full, lambda kb, c: (att_block(kb, False), c)[1], 0)
    lax.fori_loop(n_full, nkb, lambda kb, c: (att_block(kb, True), c)[1], 0)

    for h in range(B_HEADS):
        sl = slice(h * V_DIM, (h + 1) * V_DIM)
        o = acc_ref[h] / l_ref[h]
        gate = bgate_ref[:, sl]
        o_ref[:, sl] = (o * (gate * jax.nn.sigmoid(gate))).astype(BF16)


def _mla(qcat, z, dst, kcat, v, bsz, t, tq, kw, l_valid, qpos0):
    lp = kcat.shape[1]
    assert lp % kw == 0 and t % tq == 0
    nt = t // tq
    go, gw = dst["bgate"]
    row = lambda b, i: (b * nt + i, 0)
    batch = lambda b, i: (b, 0, 0)
    kern = functools.partial(_mla_kernel, tq=tq, kw=kw, l_valid=l_valid, qpos0=qpos0)
    return pl.pallas_call(
        kern,
        grid=(bsz, nt),
        in_specs=[pl.BlockSpec((tq, B_HEADS * B_QK_PAD), row),
                  pl.BlockSpec((tq, gw), lambda b, i: (b * nt + i, go // gw)),
                  pl.BlockSpec((1, lp, B_HEADS * B_QK_PAD), batch),
                  pl.BlockSpec((1, lp, B_WIDTH), batch)],
        out_specs=pl.BlockSpec((tq, B_WIDTH), row),
        out_shape=jax.ShapeDtypeStruct((bsz * t, B_WIDTH), BF16),
        scratch_shapes=[pltpu.VMEM((B_HEADS, tq, V_DIM), F32),
                        pltpu.VMEM((B_HEADS, tq, LANES), F32),
                        pltpu.VMEM((B_HEADS, tq, LANES), F32)],
        compiler_params=pltpu.CompilerParams(dimension_semantics=("arbitrary", "arbitrary"),
                                             vmem_limit_bytes=VMEM_LIMIT),
        name="mla",
    )(qcat, z, kcat, v)


def _mla_lat_kernel(q_ref, bgate_ref, cc_ref, ckr_ref, nc_ref, nkr_ref, wukt_ref, wuv_ref, o_ref,
                    *, t, past, qpos0):
    rows_n = B_HEADS * t
    ql, qr = [], []
    for h in range(B_HEADS):
        qn = q_ref[:, h * B_QK_PAD: h * B_QK_PAD + NOPE_DIM]
        ql.append(jnp.dot(qn, wukt_ref[h], preferred_element_type=F32).astype(BF16))
        qr.append(q_ref[:, h * B_QK_PAD + NOPE_DIM: h * B_QK_PAD + NOPE_DIM + ROPE_DIM])
    ql = jnp.concatenate(ql, axis=0)
    qr = jnp.concatenate(qr, axis=0)
    cc = cc_ref[0].astype(BF16)
    ckr_t = ckr_ref[0].astype(BF16)
    nc, nkr = nc_ref[...], nkr_ref[...]
    scale = (NOPE_DIM + ROPE_DIM) ** -0.5

    def scores(lat, rope_part, ncols, col0):
        s = (lax.dot_general(ql, lat, _NT, preferred_element_type=F32) + rope_part) * scale
        rows = lax.broadcasted_iota(jnp.int32, (rows_n, ncols), 0)
        cols = lax.broadcasted_iota(jnp.int32, (rows_n, ncols), 1) + col0
        qchunk = (qpos0 + lax.rem(rows, t)) // CHUNK
        return jnp.where((cols // CHUNK) <= qchunk, s, NEG)

    s1 = scores(cc, jnp.dot(qr, ckr_t, preferred_element_type=F32), past, 0)
    s2 = scores(nc, lax.dot_general(qr, nkr, _NT, preferred_element_type=F32), t, past)
    m = jnp.maximum(jnp.max(s1, axis=-1, keepdims=True), jnp.max(s2, axis=-1, keepdims=True))
    e1, e2 = jnp.exp(s1 - m), jnp.exp(s2 - m)
    l = jnp.sum(e1, axis=-1, keepdims=True) + jnp.sum(e2, axis=-1, keepdims=True)
    ol = (jnp.dot(e1.astype(BF16), cc, preferred_element_type=F32)
          + jnp.dot(e2.astype(BF16), nc, preferred_element_type=F32)) / l
    ol = ol.astype(BF16)
    for h in range(B_HEADS):
        sl = slice(h * V_DIM, (h + 1) * V_DIM)
        o = jnp.dot(ol[h * t:(h + 1) * t], wuv_ref[:, sl], preferred_element_type=F32)
        gate = bgate_ref[:, sl]
        o_ref[:, sl] = (o * (gate * jax.nn.sigmoid(gate))).astype(BF16)


def _mla_lat(qcat, z, dst, cache_ckv, cache_kr, layer, new_ckv, new_kr, wukt, wuv, bsz, t, qpos0):
    past = cache_ckv.shape[2]
    go, gw = dst["bgate"]
    row = lambda b: (b, 0)
    batch = lambda b: (b, 0, 0)
    const = lambda a: pl.BlockSpec(a.shape, lambda b: (0,) * a.ndim)
    kern = functools.partial(_mla_lat_kernel, t=t, past=past, qpos0=qpos0)
    return pl.pallas_call(
        kern,
        grid=(bsz,),
        in_specs=[pl.BlockSpec((t, B_HEADS * B_QK_PAD), row),
                  pl.BlockSpec((t, gw), lambda b: (b, go // gw)),
                  pl.BlockSpec((None, 1, past, KV_LORA), lambda b: (layer, b, 0, 0)),
                  pl.BlockSpec((None, 1, ROPE_DIM, past), lambda b: (layer, b, 0, 0)),
                  pl.BlockSpec((t, KV_LORA), row),
                  pl.BlockSpec((t, ROPE_DIM), row),
                  const(wukt), const(wuv)],
        out_specs=pl.BlockSpec((t, B_WIDTH), row),
        out_shape=jax.ShapeDtypeStruct((bsz * t, B_WIDTH), BF16),
        compiler_params=pltpu.CompilerParams(dimension_semantics=("arbitrary",), vmem_limit_bytes=VMEM_LIMIT),
        name="mla_lat",
    )(qcat, z, cache_ckv, cache_kr, new_ckv, new_kr, wukt, wuv)


def _out_kernel(ga_ref, gb_ref, mg_ref, x_ref, woa_ref, wob_ref, wout_ref, fg_ref, o_ref, *, d, final):
    ya = jnp.dot(ga_ref[...], woa_ref[...], preferred_element_type=F32)
    yb = jnp.dot(gb_ref[...], wob_ref[...], preferred_element_type=F32)
    m = jax.nn.sigmoid(mg_ref[:, :d]) * ya + jax.nn.sigmoid(mg_ref[:, d:]) * yb
    h = x_ref[...] + jnp.dot(m.astype(BF16), wout_ref[...], preferred_element_type=F32)
    o_ref[...] = _rms(h, fg_ref[...]) if final else h


def _out(ga, gb, z, dst, x, woa, wob, wout, fg, final):
    m, d = x.shape
    tm = min(m, 256)
    assert m % tm == 0 and dst["mg"][0] == 0
    row = lambda i: (i, 0)
    const = lambda a: pl.BlockSpec(a.shape, lambda i: (0, 0), pipeline_mode=pl.Buffered(1))
    return pl.pallas_call(
        functools.partial(_out_kernel, d=d, final=final),
        grid=(m // tm,),
        in_specs=[pl.BlockSpec((tm, A_WIDTH), row), pl.BlockSpec((tm, B_WIDTH), row),
                  pl.BlockSpec((tm, 2 * d), row), pl.BlockSpec((tm, d), row),
                  const(woa), const(wob), const(wout), const(fg)],
        out_specs=pl.BlockSpec((tm, d), row),
        out_shape=jax.ShapeDtypeStruct((m, d), F32),
        compiler_params=pltpu.CompilerParams(dimension_semantics=("arbitrary",), vmem_limit_bytes=VMEM_LIMIT),
        name="out",
    )(ga, gb, z, x, woa, wob, wout, fg)


def _all_keys(cache, new, bsz, t, lp):
    new = new.reshape(bsz, t, new.shape[-1])
    if cache is None and lp == t:
        return new
    parts = ([] if cache is None else [cache.reshape(bsz, cache.shape[1], -1).astype(BF16)]) + [new]
    kk = jnp.concatenate(parts, axis=1)
    return jnp.pad(kk, ((0, 0), (0, lp - kk.shape[1]), (0, 0)))


def _trunk(x, caches, final_g, weights):
    bsz, t, d = x.shape
    depth = weights["w_in"].shape[0]
    past = 0 if caches is None else caches[0].shape[2]
    l_valid = past + t
    tq = min(t, QUERY_TILE)
    kw = KEY_BLOCK if t >= QUERY_TILE else -(-l_valid // LANES) * LANES
    lp = -(-l_valid // kw) * kw
    _, dst, n_proj = _layout(d)
    pos = past + jnp.arange(t, dtype=jnp.int32)
    t_scale = jnp.concatenate([jnp.full((t, IDX_HEADS), IDX_HEADS ** -0.5, F32),
                               jnp.ones((t, LANES - IDX_DIM - IDX_HEADS), F32)], axis=1)
    tab_a = jnp.stack(_rope_tables(pos, A_ROT, A_HEAD_DIM, 1))
    tab_i = jnp.stack(_rope_tables(pos, IDX_ROT, IDX_DIM, 2))
    tabs = [tab_a, jnp.stack(_rope_tables(pos, IDX_ROT, IDX_DIM, 1, extra=t_scale)),
            jnp.stack(_rope_tables(pos, ROPE_DIM, ROPE_DIM, 2))]
    reps = max(1, min(bsz * t, PROJ_ROWS) // t)
    proj_tabs = [jnp.tile(a, (1, reps, 1)) for a in (tab_a, tab_i)]
    h = x.reshape(bsz * t, d)
    new_rows = []
    for l in range(depth):
        w = {k: v[l] for k, v in weights.items()}
        z, aq, iq = _proj(h, w["norm_g"], w["w_in"], dst, n_proj, *proj_tabs)
        ak, av, ikw, krp, ckv, qcat, kb, vb, ikb, ckvb, krb = _prep(
            z, dst, bsz, t, tabs, w["q_norm_g"], w["kv_norm_g"], w["w_uq"])
        c = (None,) * 5 if caches is None else tuple(cc[l] for cc in caches)
        if caches is not None and t < QUERY_TILE:
            bias = _dsa_sel(iq, ikw, jnp.swapaxes(caches[2], 2, 3), l, ikb, bsz, t, past)
            interleaved = lambda a: a.reshape(bsz, past * A_KV_HEADS, A_HEAD_DIM)
            ga = _dsa_att(aq, z, dst, bias, interleaved(c[0]), interleaved(c[1]), kb, vb, bsz, t)
            gb = _mla_lat(qcat, z, dst, caches[3], jnp.swapaxes(caches[4], 2, 3), l, ckvb, krb, w["w_ukt"], w["w_uv"], bsz, t, past)
        else:
            k_all = _all_keys(c[0], kb, bsz, t, lp)
            v_all = _all_keys(c[1], vb, bsz, t, lp)
            ik_all = _all_keys(c[2], ikb, bsz, t, lp)
            ga = _dsa(aq, iq, ikw, z, dst, k_all, v_all, ik_all, bsz, t, tq, kw, l_valid, past)
            ckv_all = _all_keys(c[3], ckvb, bsz, t, lp)
            kr_all = _all_keys(c[4], krb, bsz, t, lp)
            kcat, vv = _kvup(ckv_all.reshape(bsz * lp, KV_LORA), kr_all.reshape(bsz * lp, ROPE_DIM), w["w_ukv"])
            tq_mla = MLA_QUERY_TILE if t % MLA_QUERY_TILE == 0 else tq
            gb = _mla(qcat, z, dst, kcat.reshape(bsz, lp, -1), vv.reshape(bsz, lp, -1), bsz, t, tq_mla, kw, l_valid, past)
        h = _out(ga, gb, z, dst, h, w["w_o_a"], w["w_o_b"], w["w_out"], final_g, final=(l == depth - 1))
        new_rows.append((ak.reshape(bsz, t, A_KV_HEADS, A_HEAD_DIM), av.reshape(bsz, t, A_KV_HEADS, A_HEAD_DIM),
                         ikw[:, :IDX_DIM].reshape(bsz, t, IDX_DIM), ckv.reshape(bsz, t, KV_LORA),
                         krp[:, :ROPE_DIM].reshape(bsz, t, ROPE_DIM)))
    stacked = [jnp.stack([r[i] for r in new_rows], axis=0) for i in range(5)]
    return h.reshape(bsz, t, d), stacked


def kernel(x_prompt, x_sample, cache_a_k, cache_a_v, cache_idx_k, cache_mla_ckv, cache_mla_krope,
           norm_g, w_in, w_uq, q_norm_g, kv_norm_g, w_uk, w_uv, w_o_a, w_o_b, w_out, final_g):
    depth, d, _ = w_in.shape
    weights = {
        "norm_g": norm_g.reshape(depth, 1, d),
        "w_in": jnp.stack([_regroup_w_in(jnp.swapaxes(w_in[l], 0, 1), d) for l in range(depth)]),
        "w_uq": jnp.concatenate([w_uq[..., :NOPE_DIM].reshape(depth, Q_LORA, B_HEADS * NOPE_DIM),
                                 w_uq[..., NOPE_DIM:].reshape(depth, Q_LORA, B_HEADS * ROPE_DIM)], axis=-1).astype(BF16),
        "q_norm_g": q_norm_g.reshape(depth, 1, Q_LORA),
        "kv_norm_g": kv_norm_g.reshape(depth, 1, KV_LORA),
        "w_ukv": jnp.concatenate([w_uk.reshape(depth, KV_LORA, B_HEADS * NOPE_DIM),
                                  w_uv.reshape(depth, KV_LORA, B_WIDTH)], axis=-1).astype(BF16),
        "w_ukt": jnp.transpose(w_uk, (0, 2, 3, 1)).astype(BF16),
        "w_uv": w_uv.reshape(depth, KV_LORA, B_WIDTH).astype(BF16),
        "w_o_a": w_o_a.astype(BF16), "w_o_b": w_o_b.astype(BF16), "w_out": w_out.astype(BF16),
    }
    fg = final_g.reshape(1, d)
    y_p, (pk, pv, pik, pckv, pkr) = _trunk(x_prompt, None, fg, weights)
    caches = (cache_a_k, cache_a_v, cache_idx_k, cache_mla_ckv, cache_mla_krope)
    y_s, (sk, sv, sik, sckv, skr) = _trunk(x_sample, caches, fg, weights)
    return (y_p, y_s, pk, pv, pik, pckv, pkr, sk, sv, sik, sckv, skr)
```

```python
import functools

import jax
import jax.numpy as jnp
from jax import lax
from jax.experimental import pallas as pl
from jax.experimental.pallas import tpu as pltpu

F32 = jnp.float32
BF16 = jnp.bfloat16

CHUNK = 64
ROPE_THETA = 500000.0
EPS = 1e-6

A_HEADS = 8
A_KV_HEADS = 2
A_HEAD_DIM = 128
A_ROT = A_HEAD_DIM // 4
A_WIDTH = A_HEADS * A_HEAD_DIM
A_KV_WIDTH = A_KV_HEADS * A_HEAD_DIM
IDX_HEADS = 16
IDX_DIM = 64
IDX_ROT = IDX_DIM // 4
TOPK_MAX = 256

B_HEADS = 8
Q_LORA = 512
KV_LORA = 512
NOPE_DIM = 128
ROPE_DIM = 64
V_DIM = 128
B_WIDTH = B_HEADS * V_DIM
B_QK_PAD = 256

LANES = 128
KEY_BLOCK = 256
QUERY_TILE = 256
PROJ_ROWS = 1024
PROJ_TILE = 1024
MLA_QUERY_TILE = 512
ROW_STRIP = 64
INT_MIN = -(2 ** 31)
LOG2_E = 1.4426950408889634
NEG = -1e30
MASKED = -1e38
VMEM_LIMIT = 56 * 1024 * 1024

_SEGS = (("aq", A_WIDTH), ("ak", A_KV_WIDTH), ("av", A_KV_WIDTH), ("agate", A_WIDTH),
         ("iq", IDX_HEADS * IDX_DIM), ("ik", IDX_DIM), ("iw", IDX_HEADS),
         ("cq", Q_LORA), ("ckv", KV_LORA), ("kr", ROPE_DIM), ("bgate", B_WIDTH), ("mg", None))


def _layout(d_model):
    src, off = {}, 0
    for name, w in _SEGS:
        w = 2 * d_model if w is None else w
        src[name] = (off, w)
        off += w
    order = (("mg", 2 * d_model), ("aq", A_WIDTH), ("agate", A_WIDTH), ("iq", IDX_HEADS * IDX_DIM),
             ("bgate", B_WIDTH), ("cq", Q_LORA), ("ckv", KV_LORA), ("ak", A_KV_WIDTH), ("av", A_KV_WIDTH),
             ("ikw", LANES), ("krp", LANES))
    dst, off = {}, 0
    for name, w in order:
        assert off % w == 0
        dst[name] = (off, w)
        off += w
    return src, dst, off


def _regroup_kernel(w_ref, o_ref, *, src, dst):
    cols = w_ref.shape[1]

    def piece(name):
        o, w = src[name]
        return w_ref[o:o + w, :].astype(BF16)

    def zeros(n):
        return jnp.zeros((n, cols), BF16)

    for name, (o, w) in dst.items():
        if name == "ikw":
            val = jnp.concatenate([piece("ik"), piece("iw"), zeros(LANES - IDX_DIM - IDX_HEADS)], axis=0)
        elif name == "krp":
            val = jnp.concatenate([piece("kr"), zeros(LANES - ROPE_DIM)], axis=0)
        else:
            val = piece(name)
        o_ref[o:o + w, :] = val
    total = max(o + w for o, w in dst.values())
    if o_ref.shape[0] > total:
        o_ref[total:, :] = zeros(o_ref.shape[0] - total)


def _regroup_w_in(w_in_t, d_model):
    src, dst, total = _layout(d_model)
    tc = 256
    padded = -(-total // PROJ_TILE) * PROJ_TILE
    assert d_model % tc == 0
    return pl.pallas_call(
        functools.partial(_regroup_kernel, src=src, dst=dst),
        grid=(d_model // tc,),
        in_specs=[pl.BlockSpec((w_in_t.shape[0], tc), lambda i: (0, i))],
        out_specs=pl.BlockSpec((padded, tc), lambda i: (0, i)),
        out_shape=jax.ShapeDtypeStruct((padded, d_model), BF16),
        compiler_params=pltpu.CompilerParams(dimension_semantics=("arbitrary",), vmem_limit_bytes=VMEM_LIMIT),
        name="regroup",
    )(w_in_t)


def _rope_tables(pos, rot, head, heads_per_vreg, extra=None):
    half = rot // 2
    inv_freq = ROPE_THETA ** (-jnp.arange(half, dtype=F32) * (2.0 / rot))
    ang = pos.astype(F32)[:, None] * inv_freq[None, :]
    cos, sin = jnp.cos(ang), jnp.sin(ang)
    t = pos.shape[0]
    one = jnp.ones((t, head - rot), F32)
    zero = lambda n: jnp.zeros((t, n), F32)
    c = jnp.concatenate([cos, cos, one], axis=1)
    sa = jnp.concatenate([-sin, zero(head - half)], axis=1)
    sb = jnp.concatenate([zero(half), sin, zero(head - rot)], axis=1)
    c, sa, sb = (jnp.tile(a, (1, heads_per_vreg)) for a in (c, sa, sb))
    if extra is not None:
        c = jnp.concatenate([c, extra], axis=1)
        sa = jnp.concatenate([sa, jnp.zeros_like(extra)], axis=1)
        sb = jnp.concatenate([sb, jnp.zeros_like(extra)], axis=1)
    assert c.shape[1] == LANES
    return c, sa, sb


def _rope(x, c, sa, sb, half):
    up = pltpu.roll(x, LANES - half, 1)
    dn = pltpu.roll(x, half, 1)
    return x * c + up * sa + dn * sb


def _rms(x, g):
    ms = jnp.mean(x * x, axis=-1, keepdims=True)
    return (x * lax.rsqrt(ms + EPS)) * g


def _lane_tile(x, width):
    return x if width == LANES else jnp.concatenate([x] * (width // LANES), axis=1)


_NT = (((1,), (1,)), ((), ()))


def _proj_kernel(x_ref, g_ref, w_ref, ta_ref, ti_ref, z_ref, aq_ref, iq_ref, xn_ref, hold_ref, *, j_aq, j_iq):
    j = pl.program_id(1)

    @pl.when(j == 0)
    def _():
        xn_ref[...] = _rms(x_ref[...], g_ref[...]).astype(BF16)

    def tile():
        return lax.dot_general(xn_ref[...], w_ref[...], _NT, preferred_element_type=F32)

    def rope_held(tab_ref, out_ref, half):
        c, sa, sb = tab_ref[0], tab_ref[1], tab_ref[2]
        for v in range(hold_ref.shape[1] // LANES):
            sl = slice(v * LANES, (v + 1) * LANES)
            out_ref[:, sl] = _rope(hold_ref[:, sl], c, sa, sb, half).astype(BF16)

    held = (j == j_aq) | (j == j_iq)
    after_aq = j == j_aq + 1
    after_iq = j == j_iq + 1

    @pl.when(held)
    def _():
        hold_ref[...] = tile()

    @pl.when(after_aq)
    def _():
        z_ref[...] = tile()
        rope_held(ta_ref, aq_ref, A_ROT // 2)

    @pl.when(after_iq)
    def _():
        z_ref[...] = tile()
        rope_held(ti_ref, iq_ref, IDX_ROT // 2)

    @pl.when(jnp.logical_not(held | after_aq | after_iq))
    def _():
        z_ref[...] = tile()


def _proj(x, g, w, dst, n, tab_a, tab_i):
    m, d = x.shape
    tm = min(m, PROJ_ROWS)
    tn = PROJ_TILE
    assert m % tm == 0 and w.shape[0] % tn == 0 and dst["aq"][1] == tn and dst["iq"][1] == tn
    j_aq, j_iq = dst["aq"][0] // tn, dst["iq"][0] // tn
    assert dst["aq"][0] % tn == 0 and dst["iq"][0] % tn == 0 and j_iq > j_aq + 1 and (j_iq + 1) * tn < n
    ntt = tab_a.shape[1] // tm
    tspec = pl.BlockSpec((3, tm, LANES), lambda i, j: (0, i % ntt, 0), pipeline_mode=pl.Buffered(1))
    zcol = lambda i, j: (i, jnp.where((j == j_aq) | (j == j_iq), j - 1, j))
    return pl.pallas_call(
        functools.partial(_proj_kernel, j_aq=j_aq, j_iq=j_iq),
        grid=(m // tm, w.shape[0] // tn),
        in_specs=[pl.BlockSpec((tm, d), lambda i, j: (i, 0)),
                  pl.BlockSpec((1, d), lambda i, j: (0, 0)),
                  pl.BlockSpec((tn, d), lambda i, j: (j, 0)),
                  tspec, tspec],
        out_specs=[pl.BlockSpec((tm, tn), zcol),
                   pl.BlockSpec((tm, tn), lambda i, j: (i, 0)),
                   pl.BlockSpec((tm, tn), lambda i, j: (i, 0))],
        out_shape=[jax.ShapeDtypeStruct((m, n), F32), jax.ShapeDtypeStruct((m, tn), BF16),
                   jax.ShapeDtypeStruct((m, tn), BF16)],
        scratch_shapes=[pltpu.VMEM((tm, d), BF16), pltpu.VMEM((tm, tn), F32)],
        compiler_params=pltpu.CompilerParams(dimension_semantics=("arbitrary", "arbitrary"),
                                             vmem_limit_bytes=VMEM_LIMIT),
        name="proj",
    )(x, g, w, tab_a, tab_i)


def _prep_kernel(cq_ref, ckv_ref, ak_ref, av_ref, ikw_ref, krp_ref,
                 ta_ref, tk_ref, tr_ref, qg_ref, kvg_ref, wuq_ref,
                 ak_o, av_o, ikw_o, krp_o, ckv_o, qcat_o, kb_o, vb_o, ikb_o, ckvb_o, krb_o):
    ca, saa, sba = ta_ref[0], ta_ref[1], ta_ref[2]
    ck, sak, sbk = tk_ref[0], tk_ref[1], tk_ref[2]
    cr, sar, sbr = tr_ref[0], tr_ref[1], tr_ref[2]
    tm = cq_ref.shape[0]
    for h in range(A_KV_HEADS):
        sl = slice(h * LANES, (h + 1) * LANES)
        ak = _rope(ak_ref[:, sl], ca, saa, sba, A_ROT // 2)
        ak_o[pl.ds(h, tm, stride=A_KV_HEADS), :] = ak
        av_o[pl.ds(h, tm, stride=A_KV_HEADS), :] = av_ref[:, sl]
        kb_o[:, sl] = ak.astype(BF16)
    vb_o[...] = av_ref[...].astype(BF16)
    ikw = _rope(ikw_ref[...], ck, sak, sbk, IDX_ROT // 2)
    ikw_o[...] = ikw
    ikb_o[...] = ikw[:, :IDX_DIM].astype(BF16)
    krp = _rope(krp_ref[...], cr, sar, sbr, ROPE_DIM // 2)
    krp_o[...] = krp
    krb_o[...] = krp[:, :ROPE_DIM].astype(BF16)
    ckv = _rms(ckv_ref[...], kvg_ref[...])
    ckv_o[...] = ckv
    ckvb_o[...] = ckv.astype(BF16)
    cqn = _rms(cq_ref[...], qg_ref[...]).astype(BF16)
    qb = jnp.dot(cqn, wuq_ref[...], preferred_element_type=F32)
    nope = B_HEADS * NOPE_DIM
    low = lax.broadcasted_iota(jnp.int32, (tm, LANES), 1) < ROPE_DIM
    for v in range(B_HEADS * ROPE_DIM // LANES):
        r = _rope(qb[:, nope + v * LANES: nope + (v + 1) * LANES], cr, sar, sbr, ROPE_DIM // 2)
        halves = (jnp.where(low, r, 0.0), jnp.where(low, pltpu.roll(r, ROPE_DIM, 1), 0.0))
        for j in range(2):
            h = 2 * v + j
            qcat_o[:, h * B_QK_PAD: h * B_QK_PAD + NOPE_DIM] = qb[:, h * NOPE_DIM:(h + 1) * NOPE_DIM].astype(BF16)
            qcat_o[:, h * B_QK_PAD + NOPE_DIM:(h + 1) * B_QK_PAD] = halves[j].astype(BF16)


def _prep(z, dst, bsz, t, tabs, q_norm_g, kv_norm_g, wuq):
    m = bsz * t
    tm = min(t, 256)
    nt = t // tm
    assert t % tm == 0

    def zspec(name):
        o, w = dst[name]
        return pl.BlockSpec((tm, w), lambda b, i, blk=o // w: (b * nt + i, blk))

    def tspec():
        return pl.BlockSpec((3, tm, LANES), lambda b, i: (0, i, 0))

    def ospec(w):
        return pl.BlockSpec((tm, w), lambda b, i: (b * nt + i, 0))

    def full(a):
        return pl.BlockSpec(a.shape, lambda b, i: (0,) * a.ndim)

    outs = [("interleaved", F32), ("interleaved", F32), (LANES, F32), (LANES, F32),
            (KV_LORA, F32), (B_HEADS * B_QK_PAD, BF16),
            (A_KV_WIDTH, BF16), (A_KV_WIDTH, BF16), (IDX_DIM, BF16), (KV_LORA, BF16), (ROPE_DIM, BF16)]
    return pl.pallas_call(
        _prep_kernel,
        grid=(bsz, nt),
        in_specs=[zspec("cq"), zspec("ckv"), zspec("ak"), zspec("av"), zspec("ikw"),
                  zspec("krp"), tspec(), tspec(), tspec(), full(q_norm_g), full(kv_norm_g), full(wuq)],
        out_specs=[pl.BlockSpec((tm * A_KV_HEADS, A_HEAD_DIM), lambda b, i: (b * nt + i, 0)) if w == "interleaved" else ospec(w)
                   for w, _ in outs],
        out_shape=[jax.ShapeDtypeStruct((m * A_KV_HEADS, A_HEAD_DIM) if w == "interleaved" else (m, w), dt) for w, dt in outs],
        compiler_params=pltpu.CompilerParams(dimension_semantics=("arbitrary", "arbitrary"),
                                             vmem_limit_bytes=VMEM_LIMIT),
        name="prep",
    )(z, z, z, z, z, z, *tabs, q_norm_g, kv_norm_g, wuq)


def _kvup_kernel(c_ref, kr_ref, w_ref, kcat_ref, v_ref):
    r = jnp.dot(c_ref[...], w_ref[...], preferred_element_type=F32)
    n = B_HEADS * NOPE_DIM
    tail = jnp.concatenate([kr_ref[...], jnp.zeros((kr_ref.shape[0], B_QK_PAD - NOPE_DIM - ROPE_DIM), BF16)], axis=1)
    for h in range(B_HEADS):
        kcat_ref[:, h * B_QK_PAD: h * B_QK_PAD + NOPE_DIM] = r[:, h * NOPE_DIM:(h + 1) * NOPE_DIM].astype(BF16)
        kcat_ref[:, h * B_QK_PAD + NOPE_DIM:(h + 1) * B_QK_PAD] = tail
    v_ref[...] = r[:, n:].astype(BF16)


def _kvup(ckv, kr, w):
    m, c = ckv.shape
    tm = next(t for t in (512, 384, 256, 128) if m % t == 0)
    row = lambda i: (i, 0)
    return pl.pallas_call(
        _kvup_kernel,
        grid=(m // tm,),
        in_specs=[pl.BlockSpec((tm, c), row), pl.BlockSpec((tm, ROPE_DIM), row), pl.BlockSpec(w.shape, lambda i: (0, 0))],
        out_specs=[pl.BlockSpec((tm, B_HEADS * B_QK_PAD), row), pl.BlockSpec((tm, B_WIDTH), row)],
        out_shape=[jax.ShapeDtypeStruct((m, B_HEADS * B_QK_PAD), BF16), jax.ShapeDtypeStruct((m, B_WIDTH), BF16)],
        compiler_params=pltpu.CompilerParams(dimension_semantics=("arbitrary",), vmem_limit_bytes=VMEM_LIMIT),
        name="kv_up",
    )(ckv, kr, w)


def _visible_cols(tile, tq, l_valid, qpos0):
    p_last = qpos0 + (tile + 1) * tq - 1
    n = (p_last // CHUNK + 1) * CHUNK
    return min(n, l_valid) if isinstance(n, int) else jnp.minimum(n, l_valid)


def _num_key_blocks(tile, tq, kw, l_valid, qpos0):
    return (_visible_cols(tile, tq, l_valid, qpos0) + kw - 1) // kw


def _fully_visible_blocks(tile, tq, kw, l_valid, qpos0):
    first_row_cols = jnp.minimum(((qpos0 + tile * tq) // CHUNK + 1) * CHUNK, l_valid)
    return first_row_cols // kw


def _allowed(tile, kb, rows0, shape, tq, kw, l_valid, qpos0):
    rows = lax.broadcasted_iota(jnp.int32, shape, 0) + rows0
    cols = lax.broadcasted_iota(jnp.int32, shape, 1) + kb * kw
    qchunk = (qpos0 + tile * tq + rows) // CHUNK
    return ((cols // CHUNK) <= qchunk) & (cols < l_valid), cols


def _softmax_step(s, scale, h, vblk, m_ref, l_ref, acc_ref):
    kw = s.shape[1]
    c = scale * LOG2_E
    m_prev = m_ref[h]
    m_new = jnp.maximum(m_prev, jnp.max(s, axis=-1, keepdims=True))
    alpha = jnp.exp2((m_prev - m_new) * c)
    p = jnp.exp2((s - _lane_tile(m_new, kw)) * c)
    l_ref[h] = alpha * l_ref[h] + jnp.sum(p, axis=-1, keepdims=True)
    acc_ref[h] = alpha * acc_ref[h] + jnp.dot(p.astype(BF16), vblk, preferred_element_type=F32)
    m_ref[h] = m_new


def _softmax_init(heads, tq, m_ref, l_ref, acc_ref):
    for h in range(heads):
        m_ref[h] = jnp.full((tq, LANES), NEG, F32)
        l_ref[h] = jnp.zeros((tq, LANES), F32)
        acc_ref[h] = jnp.zeros((tq, acc_ref.shape[2]), F32)


def _masked_score(score, allowed):
    return jnp.where(allowed, score, MASKED)


def _key_to_f32(key):
    return lax.bitcast_convert_type(key ^ ((key >> 31) & 0x7FFFFFFF), F32)


def _selection_bias(score, cols, thr, jcut, allowed):
    sel = ((score > thr) | ((score == thr) & (cols < jcut))) & allowed
    return jnp.where(sel, 0.0, NEG)


def _topk_thresholds(strips, rs, kf, thr_ref, jcut_ref, need_ref):
    def count(pieces, pred):
        acc = jnp.zeros((rs, LANES), F32)
        for load, col0 in pieces:
            acc = acc + jnp.where(pred(load(), col0), 1.0, 0.0)
        return jnp.sum(acc, axis=-1, keepdims=True)

    def bit_step(b, carry):
        bit = jnp.left_shift(jnp.int32(1), 31 - b)
        for rows, pieces in strips:
            t = thr_ref[rows, :]
            cand = t + bit
            cand_f = _key_to_f32(cand)
            cnt = count(pieces, lambda s, c0: s >= cand_f)
            thr_ref[rows, :] = jnp.where(cnt >= kf, cand, t)
        return carry

    thr_ref[...] = jnp.full(thr_ref.shape, INT_MIN, jnp.int32)
    lax.fori_loop(0, 32, bit_step, 0)

    jcut_ref[...] = jnp.full(jcut_ref.shape, 4096, jnp.int32)
    flag = jnp.int32(0)
    for rows, pieces in strips:
        t = _key_to_f32(thr_ref[rows, :])
        excess = (count(pieces, lambda s, c0: s >= t) > kf) & (t[:, :1] > MASKED)
        flag = jnp.maximum(flag, jnp.max(jnp.where(excess, 1, 0)))

    @pl.when(flag > 0)
    def _():
        lane = lax.broadcasted_iota(jnp.int32, (rs, LANES), 1)
        for rows, pieces in strips:
            t = _key_to_f32(thr_ref[rows, :])
            need = kf - count(pieces, lambda s, c0: s > t)
            need_ref[rows, :] = jnp.broadcast_to(need, (rs, LANES))

        def col_step(b, carry):
            bit = jnp.left_shift(jnp.int32(1), 11 - b)
            for rows, pieces in strips:
                t, j = _key_to_f32(thr_ref[rows, :]), jcut_ref[rows, :]
                cand = j + bit
                f = count(pieces, lambda s, c0: (s == t) & (lane + c0 < cand))
                jcut_ref[rows, :] = jnp.where(f <= need_ref[rows, :], cand, j)
            return carry

        jcut_ref[...] = jnp.zeros(jcut_ref.shape, jnp.int32)
        lax.fori_loop(0, 12, col_step, 0)

    return flag


def _dsa_kernel(aq_ref, iq_ref, ikw_ref, agate_ref, k_ref, v_ref, ik_ref, o_ref,
                key_ref, thr_ref, jcut_ref, need_ref, ties_ref, bias_ref, iwb_ref, qst_ref, qs_ref, acc_ref, m_ref, l_ref,
                *, nt, tq, kw, l_valid, qpos0, topk):
    nkb_max = key_ref.shape[0] // nt
    nkb_of = [_num_key_blocks(ti, tq, kw, l_valid, qpos0) for ti in range(nt)]
    rs = qst_ref.shape[1] // IDX_HEADS
    ns = tq // rs
    nc = kw // LANES
    kf = float(topk)

    def score_tile(ti, carry):
        r0 = pl.multiple_of(ti * tq, tq)
        iw = ikw_ref[pl.ds(r0, tq), IDX_DIM:IDX_DIM + IDX_HEADS] * (IDX_DIM ** -0.5)
        for h in range(IDX_HEADS):
            iwb_ref[h] = jnp.broadcast_to(iw[:, h:h + 1], (tq, LANES))
            for r in range(ns):
                qst_ref[r, h * rs:(h + 1) * rs, :] = iq_ref[pl.ds(r0 + r * rs, rs), h * IDX_DIM:(h + 1) * IDX_DIM]

        def score_block(kb, c2):
            ikb = ik_ref[0, pl.ds(pl.multiple_of(kb * kw, kw), kw), :]
            for r in range(ns):
                rel = lax.dot_general(qst_ref[r], ikb, _NT, preferred_element_type=F32)
                sc = jnp.zeros((rs, kw), F32)
                for h in range(IDX_HEADS):
                    w = _lane_tile(iwb_ref[h, r * rs:(r + 1) * rs, :], kw)
                    sc = sc + jnp.maximum(rel[h * rs:(h + 1) * rs], 0.0) * w
                allowed, _ = _allowed(ti, kb, r * rs, (rs, kw), tq, kw, l_valid, qpos0)
                key_ref[ti * nkb_max + kb, r * rs:(r + 1) * rs, :] = _masked_score(sc, allowed)
            return c2

        lax.fori_loop(0, _num_key_blocks(ti, tq, kw, l_valid, qpos0), score_block, 0)
        return carry

    def piece(ti, kb, s, c):
        return lambda: key_ref[ti * nkb_max + kb, s * rs:(s + 1) * rs, c * LANES:(c + 1) * LANES]

    strips = [(slice(ti * tq + s * rs, ti * tq + (s + 1) * rs),
               [(piece(ti, kb, s, c), kb * kw + c * LANES) for kb in range(nkb_of[ti]) for c in range(nc)])
              for ti in range(nt) for s in range(ns)]

    @pl.when(pl.program_id(1) == 0)
    def _():
        lax.fori_loop(0, nt, score_tile, 0)
        ties_ref[0] = _topk_thresholds(strips, rs, kf, thr_ref, jcut_ref, need_ref)

    ti = pl.program_id(1)
    nkb = _num_key_blocks(ti, tq, kw, l_valid, qpos0)
    r0 = pl.multiple_of(ti * tq, tq)
    thr = _lane_tile(_key_to_f32(thr_ref[pl.ds(r0, tq), :]), kw)
    jcut = _lane_tile(jcut_ref[pl.ds(r0, tq), :], kw)

    def bias_block(kb, masked, ties):
        score = key_ref[ti * nkb_max + kb]
        if ties:
            allowed, cols = _allowed(ti, kb, 0, (tq, kw), tq, kw, l_valid, qpos0)
            bias_ref[kb] = _selection_bias(score, cols, thr, jcut, allowed)
        else:
            sel = score >= thr
            if masked:
                sel = sel & _allowed(ti, kb, 0, (tq, kw), tq, kw, l_valid, qpos0)[0]
            bias_ref[kb] = jnp.where(sel, 0.0, NEG)

    def bias_loop(lo, hi, masked, ties):
        lax.fori_loop(lo, hi, lambda kb, c: (bias_block(kb, masked, ties), c)[1], 0)

    n_full = _fully_visible_blocks(ti, tq, kw, l_valid, qpos0)

    @pl.when(ties_ref[0] == 0)
    def _():
        bias_loop(0, n_full, False, False)
        bias_loop(n_full, nkb, True, False)

    @pl.when(ties_ref[0] != 0)
    def _():
        bias_loop(0, nkb, True, True)

    grp = A_HEADS // A_KV_HEADS
    _softmax_init(A_KV_HEADS, grp * tq, m_ref, l_ref, acc_ref)
    scale = A_HEAD_DIM ** -0.5
    for h in range(A_HEADS):
        g, hh = divmod(h, grp)
        qs_ref[g, hh * tq:(hh + 1) * tq, :] = aq_ref[:, h * A_HEAD_DIM:(h + 1) * A_HEAD_DIM]

    def att_block(kb, carry):
        c0 = pl.multiple_of(kb * kw, kw)
        bias = bias_ref[kb][None]
        for g in range(A_KV_HEADS):
            gsl = slice(g * A_HEAD_DIM, (g + 1) * A_HEAD_DIM)
            s = lax.dot_general(qs_ref[g], k_ref[0, pl.ds(c0, kw), gsl], _NT, preferred_element_type=F32)
            s = (s.reshape(grp, tq, kw) + bias).reshape(grp * tq, kw)
            _softmax_step(s, scale, g, v_ref[0, pl.ds(c0, kw), gsl], m_ref, l_ref, acc_ref)
        return carry

    lax.fori_loop(0, nkb, att_block, 0)

    for h in range(A_HEADS):
        g, hh = divmod(h, grp)
        sl = slice(h * A_HEAD_DIM, (h + 1) * A_HEAD_DIM)
        o = acc_ref[g, hh * tq:(hh + 1) * tq, :] / l_ref[g, hh * tq:(hh + 1) * tq, :]
        gate = agate_ref[:, sl]
        o_ref[:, sl] = (o * (gate * jax.nn.sigmoid(gate))).astype(BF16)


def _dsa(aq, iq, ikw, z, dst, k, v, ik, bsz, t, tq, kw, l_valid, qpos0):
    lp = k.shape[1]
    assert lp < 4096 and lp % kw == 0 and t % tq == 0
    nt = t // tq
    nkb_max = lp // kw
    topk = min(TOPK_MAX, l_valid // 4)
    grp = A_HEADS // A_KV_HEADS
    rs = min(tq, ROW_STRIP)
    go, gw = dst["agate"]
    row = lambda b, i: (b * nt + i, 0)
    whole = lambda b, i: (b, 0)
    batch = lambda b, i: (b, 0, 0)
    kern = functools.partial(_dsa_kernel, nt=nt, tq=tq, kw=kw, l_valid=l_valid, qpos0=qpos0, topk=topk)
    return pl.pallas_call(
        kern,
        grid=(bsz, nt),
        in_specs=[pl.BlockSpec((tq, A_WIDTH), row),
                  pl.BlockSpec((t, IDX_HEADS * IDX_DIM), whole),
                  pl.BlockSpec((t, LANES), whole),
                  pl.BlockSpec((tq, gw), lambda b, i: (b * nt + i, go // gw)),
                  pl.BlockSpec((1, lp, A_KV_WIDTH), batch),
                  pl.BlockSpec((1, lp, A_KV_WIDTH), batch),
                  pl.BlockSpec((1, lp, IDX_DIM), batch)],
        out_specs=pl.BlockSpec((tq, A_WIDTH), row),
        out_shape=jax.ShapeDtypeStruct((bsz * t, A_WIDTH), BF16),
        scratch_shapes=[pltpu.VMEM((nt * nkb_max, tq, kw), F32),
                        pltpu.VMEM((t, LANES), jnp.int32),
                        pltpu.VMEM((t, LANES), jnp.int32),
                        pltpu.VMEM((t, LANES), F32),
                        pltpu.SMEM((1,), jnp.int32),
                        pltpu.VMEM((nkb_max, tq, kw), F32),
                        pltpu.VMEM((IDX_HEADS, tq, LANES), F32),
                        pltpu.VMEM((tq // rs, IDX_HEADS * rs, IDX_DIM), BF16),
                        pltpu.VMEM((A_KV_HEADS, grp * tq, A_HEAD_DIM), BF16),
                        pltpu.VMEM((A_KV_HEADS, grp * tq, A_HEAD_DIM), F32),
                        pltpu.VMEM((A_KV_HEADS, grp * tq, LANES), F32),
                        pltpu.VMEM((A_KV_HEADS, grp * tq, LANES), F32)],
        compiler_params=pltpu.CompilerParams(dimension_semantics=("arbitrary", "arbitrary"),
                                             vmem_limit_bytes=VMEM_LIMIT),
        name="dsa",
    )(aq, iq, ikw, z, k, v, ik)


SEL_BATCHES = 8
SEL_CHUNK = 512


def _dsa_sel_kernel(iq_ref, ikw_ref, cik_ref, nik_ref, bias_ref, key_ref, thr_ref, jcut_ref, need_ref,
                    *, gb, t, past, qpos0, topk):
    lp = past + LANES
    nc = lp // LANES

    def allowed_cols(col0, width):
        rows = lax.broadcasted_iota(jnp.int32, (t, width), 0)
        cols = lax.broadcasted_iota(jnp.int32, (t, width), 1) + col0
        return ((cols // CHUNK) <= (qpos0 + rows) // CHUNK) & (cols < past + t)

    def score_batch(j, carry):
        r0 = pl.multiple_of(j * t, t)
        qst = jnp.concatenate([iq_ref[pl.ds(r0, t), h * IDX_DIM:(h + 1) * IDX_DIM] for h in range(IDX_HEADS)], axis=0)
        iw = ikw_ref[pl.ds(r0, t), IDX_DIM:IDX_DIM + IDX_HEADS] * (IDX_DIM ** -0.5)
        iwb = [jnp.broadcast_to(iw[:, h:h + 1], (t, LANES)) for h in range(IDX_HEADS)]

        def head_sum(rel):
            width = rel.shape[1]
            sc = jnp.zeros((t, width), F32)
            for h in range(IDX_HEADS):
                sc = sc + jnp.maximum(rel[h * t:(h + 1) * t], 0.0) * _lane_tile(iwb[h], width)
            return sc

        for c in range(past // SEL_CHUNK):
            ikt = cik_ref[j, :, c * SEL_CHUNK:(c + 1) * SEL_CHUNK].astype(BF16)
            sc = head_sum(jnp.dot(qst, ikt, preferred_element_type=F32))
            key_ref[j, :, c * SEL_CHUNK:(c + 1) * SEL_CHUNK] = _masked_score(sc, allowed_cols(c * SEL_CHUNK, SEL_CHUNK))
        ik = jnp.concatenate([nik_ref[pl.ds(r0, t), :], jnp.zeros((LANES - t, IDX_DIM), BF16)], axis=0)
        sc = head_sum(lax.dot_general(qst, ik, _NT, preferred_element_type=F32))
        key_ref[j, :, past:] = _masked_score(sc, allowed_cols(past, LANES))
        return carry

    lax.fori_loop(0, gb, score_batch, 0)

    def piece(j, c):
        return lambda: key_ref[j, :, c * LANES:(c + 1) * LANES]

    strips = [(slice(j * t, (j + 1) * t), [(piece(j, c), c * LANES) for c in range(nc)]) for j in range(gb)]
    _topk_thresholds(strips, t, float(topk), thr_ref, jcut_ref, need_ref)

    def bias_batch(j, carry):
        r0 = pl.multiple_of(j * t, t)
        cols = lax.broadcasted_iota(jnp.int32, (t, lp), 1)
        bias_ref[j] = _selection_bias(key_ref[j], cols, _lane_tile(_key_to_f32(thr_ref[pl.ds(r0, t), :]), lp),
                                      _lane_tile(jcut_ref[pl.ds(r0, t), :], lp), allowed_cols(0, lp))
        return carry

    lax.fori_loop(0, gb, bias_batch, 0)


def _dsa_sel(iq, ikw, cache_ik_t, layer, new_ik, bsz, t, qpos0):
    past = cache_ik_t.shape[3]
    lp = past + LANES
    gb = next(g for g in (SEL_BATCHES, 4, 2, 1) if bsz % g == 0)
    assert past % SEL_CHUNK == 0 and t <= LANES and lp < 4096
    topk = min(TOPK_MAX, (past + t) // 4)
    rows = lambda i: (i, 0)
    kern = functools.partial(_dsa_sel_kernel, gb=gb, t=t, past=past, qpos0=qpos0, topk=topk)
    return pl.pallas_call(
        kern,
        grid=(bsz // gb,),
        in_specs=[pl.BlockSpec((gb * t, IDX_HEADS * IDX_DIM), rows),
                  pl.BlockSpec((gb * t, LANES), rows),
                  pl.BlockSpec((None, gb, IDX_DIM, past), lambda i: (layer, i, 0, 0)),
                  pl.BlockSpec((gb * t, IDX_DIM), rows)],
        out_specs=pl.BlockSpec((gb, t, lp), lambda i: (i, 0, 0)),
        out_shape=jax.ShapeDtypeStruct((bsz, t, lp), F32),
        scratch_shapes=[pltpu.VMEM((gb, t, lp), F32),
                        pltpu.VMEM((gb * t, LANES), jnp.int32),
                        pltpu.VMEM((gb * t, LANES), jnp.int32),
                        pltpu.VMEM((gb * t, LANES), F32)],
        compiler_params=pltpu.CompilerParams(dimension_semantics=("arbitrary",), vmem_limit_bytes=VMEM_LIMIT),
        name="dsa_sel",
    )(iq, ikw, cache_ik_t, new_ik)


def _dsa_att_kernel(aq_ref, agate_ref, bias_ref, ck_ref, cv_ref, nk_ref, nv_ref, o_ref, *, t, past):
    grp = A_HEADS // A_KV_HEADS
    scale = A_HEAD_DIM ** -0.5
    pad = jnp.zeros((LANES - t, A_HEAD_DIM), BF16)
    b1, b2 = bias_ref[0, :, :past], bias_ref[0, :, past:]
    for g in range(A_KV_HEADS):
        gsl = slice(g * A_HEAD_DIM, (g + 1) * A_HEAD_DIM)
        kc = ck_ref[0, pl.ds(g, past, stride=A_KV_HEADS), :].astype(BF16)
        vc = cv_ref[0, pl.ds(g, past, stride=A_KV_HEADS), :].astype(BF16)
        kn = jnp.concatenate([nk_ref[:, gsl], pad], axis=0)
        vn = jnp.concatenate([nv_ref[:, gsl], pad], axis=0)
        qs = jnp.concatenate([aq_ref[:, (g * grp + hh) * A_HEAD_DIM:(g * grp + hh + 1) * A_HEAD_DIM]
                              for hh in range(grp)], axis=0)

        def scores(keys, bias):
            s = lax.dot_general(qs, keys, _NT, preferred_element_type=F32) * scale
            return (s.reshape(grp, t, s.shape[1]) + bias[None]).reshape(grp * t, s.shape[1])

        s1, s2 = scores(kc, b1), scores(kn, b2)
        m = jnp.maximum(jnp.max(s1, axis=-1, keepdims=True), jnp.max(s2, axis=-1, keepdims=True))
        e1, e2 = jnp.exp(s1 - m), jnp.exp(s2 - m)
        l = jnp.sum(e1, axis=-1, keepdims=True) + jnp.sum(e2, axis=-1, keepdims=True)
        o = (jnp.dot(e1.astype(BF16), vc, preferred_element_type=F32)
             + jnp.dot(e2.astype(BF16), vn, preferred_element_type=F32)) / l
        for hh in range(grp):
            sl = slice((g * grp + hh) * A_HEAD_DIM, (g * grp + hh + 1) * A_HEAD_DIM)
            gate = agate_ref[:, sl]
            o_ref[:, sl] = (o[hh * t:(hh + 1) * t] * (gate * jax.nn.sigmoid(gate))).astype(BF16)


def _dsa_att(aq, z, dst, bias, cache_k, cache_v, new_k, new_v, bsz, t):
    past = cache_k.shape[1] // A_KV_HEADS
    lp = bias.shape[2]
    go, gw = dst["agate"]
    row = lambda b: (b, 0)
    batch = lambda b: (b, 0, 0)
    return pl.pallas_call(
        functools.partial(_dsa_att_kernel, t=t, past=past),
        grid=(bsz,),
        in_specs=[pl.BlockSpec((t, A_WIDTH), row),
                  pl.BlockSpec((t, gw), lambda b: (b, go // gw)),
                  pl.BlockSpec((1, t, lp), batch),
                  pl.BlockSpec((1, past * A_KV_HEADS, A_HEAD_DIM), batch),
                  pl.BlockSpec((1, past * A_KV_HEADS, A_HEAD_DIM), batch),
                  pl.BlockSpec((t, A_KV_WIDTH), row),
                  pl.BlockSpec((t, A_KV_WIDTH), row)],
        out_specs=pl.BlockSpec((t, A_WIDTH), row),
        out_shape=jax.ShapeDtypeStruct((bsz * t, A_WIDTH), BF16),
        compiler_params=pltpu.CompilerParams(dimension_semantics=("arbitrary",), vmem_limit_bytes=VMEM_LIMIT),
        name="dsa_att",
    )(aq, z, bias, cache_k, cache_v, new_k, new_v)


def _mla_kernel(q_ref, bgate_ref, k_ref, v_ref, o_ref, acc_ref, m_ref, l_ref, *, tq, kw, l_valid, qpos0):
    ti = pl.program_id(1)
    nkb = _num_key_blocks(ti, tq, kw, l_valid, qpos0)
    scale = (NOPE_DIM + ROPE_DIM) ** -0.5
    _softmax_init(B_HEADS, tq, m_ref, l_ref, acc_ref)

    def att_block(kb, masked):
        r0 = pl.multiple_of(kb * kw, kw)
        if masked:
            allowed, _ = _allowed(ti, kb, 0, (tq, kw), tq, kw, l_valid, qpos0)
            bias = jnp.where(allowed, 0.0, NEG)
        for h in range(B_HEADS):
            qsl = slice(h * B_QK_PAD, (h + 1) * B_QK_PAD)
            s = lax.dot_general(q_ref[:, qsl], k_ref[0, pl.ds(r0, kw), qsl], _NT, preferred_element_type=F32)
            if masked:
                s = s + bias
            _softmax_step(s, scale, h, v_ref[0, pl.ds(r0, kw), h * V_DIM:(h + 1) * V_DIM], m_ref, l_ref, acc_ref)

    n_full = _fully_visible_blocks(ti, tq, kw, l_valid, qpos0)
    lax.fori_loop(0, n_full, lambda kb, c: (att_block(kb, False), c)[1], 0)
    lax.fori_loop(n_full, nkb, lambda kb, c: (att_block(kb, True), c)[1], 0)

    for h in range(B_HEADS):
        sl = slice(h * V_DIM, (h + 1) * V_DIM)
        o = acc_ref[h] / l_ref[h]
        gate = bgate_ref[:, sl]
        o_ref[:, sl] = (o * (gate * jax.nn.sigmoid(gate))).astype(BF16)


def _mla(qcat, z, dst, kcat, v, bsz, t, tq, kw, l_valid, qpos0):
    lp = kcat.shape[1]
    assert lp % kw == 0 and t % tq == 0
    nt = t // tq
    go, gw = dst["bgate"]
    row = lambda b, i: (b * nt + i, 0)
    batch = lambda b, i: (b, 0, 0)
    kern = functools.partial(_mla_kernel, tq=tq, kw=kw, l_valid=l_valid, qpos0=qpos0)
    return pl.pallas_call(
        kern,
        grid=(bsz, nt),
        in_specs=[pl.BlockSpec((tq, B_HEADS * B_QK_PAD), row),
                  pl.BlockSpec((tq, gw), lambda b, i: (b * nt + i, go // gw)),
                  pl.BlockSpec((1, lp, B_HEADS * B_QK_PAD), batch),
                  pl.BlockSpec((1, lp, B_WIDTH), batch)],
        out_specs=pl.BlockSpec((tq, B_WIDTH), row),
        out_shape=jax.ShapeDtypeStruct((bsz * t, B_WIDTH), BF16),
        scratch_shapes=[pltpu.VMEM((B_HEADS, tq, V_DIM), F32),
                        pltpu.VMEM((B_HEADS, tq, LANES), F32),
                        pltpu.VMEM((B_HEADS, tq, LANES), F32)],
        compiler_params=pltpu.CompilerParams(dimension_semantics=("arbitrary", "arbitrary"),
                                             vmem_limit_bytes=VMEM_LIMIT),
        name="mla",
    )(qcat, z, kcat, v)


def _mla_lat_kernel(q_ref, bgate_ref, cc_ref, ckr_ref, nc_ref, nkr_ref, wukt_ref, wuv_ref, o_ref,
                    *, t, past, qpos0):
    rows_n = B_HEADS * t
    ql, qr = [], []
    for h in range(B_HEADS):
        qn = q_ref[:, h * B_QK_PAD: h * B_QK_PAD + NOPE_DIM]
        ql.append(jnp.dot(qn, wukt_ref[h], preferred_element_type=F32).astype(BF16))
        qr.append(q_ref[:, h * B_QK_PAD + NOPE_DIM: h * B_QK_PAD + NOPE_DIM + ROPE_DIM])
    ql = jnp.concatenate(ql, axis=0)
    qr = jnp.concatenate(qr, axis=0)
    cc = cc_ref[0].astype(BF16)
    ckr_t = ckr_ref[0].astype(BF16)
    nc, nkr = nc_ref[...], nkr_ref[...]
    scale = (NOPE_DIM + ROPE_DIM) ** -0.5

    def scores(lat, rope_part, ncols, col0):
        s = (lax.dot_general(ql, lat, _NT, preferred_element_type=F32) + rope_part) * scale
        rows = lax.broadcasted_iota(jnp.int32, (rows_n, ncols), 0)
        cols = lax.broadcasted_iota(jnp.int32, (rows_n, ncols), 1) + col0
        qchunk = (qpos0 + lax.rem(rows, t)) // CHUNK
        return jnp.where((cols // CHUNK) <= qchunk, s, NEG)

    s1 = scores(cc, jnp.dot(qr, ckr_t, preferred_element_type=F32), past, 0)
    s2 = scores(nc, lax.dot_general(qr, nkr, _NT, preferred_element_type=F32), t, past)
    m = jnp.maximum(jnp.max(s1, axis=-1, keepdims=True), jnp.max(s2, axis=-1, keepdims=True))
    e1, e2 = jnp.exp(s1 - m), jnp.exp(s2 - m)
    l = jnp.sum(e1, axis=-1, keepdims=True) + jnp.sum(e2, axis=-1, keepdims=True)
    ol = (jnp.dot(e1.astype(BF16), cc, preferred_element_type=F32)
          + jnp.dot(e2.astype(BF16), nc, preferred_element_type=F32)) / l
    ol = ol.astype(BF16)
    for h in range(B_HEADS):
        sl = slice(h * V_DIM, (h + 1) * V_DIM)
        o = jnp.dot(ol[h * t:(h + 1) * t], wuv_ref[:, sl], preferred_element_type=F32)
        gate = bgate_ref[:, sl]
        o_ref[:, sl] = (o * (gate * jax.nn.sigmoid(gate))).astype(BF16)


def _mla_lat(qcat, z, dst, cache_ckv, cache_kr, layer, new_ckv, new_kr, wukt, wuv, bsz, t, qpos0):
    past = cache_ckv.shape[2]
    go, gw = dst["bgate"]
    row = lambda b: (b, 0)
    batch = lambda b: (b, 0, 0)
    const = lambda a: pl.BlockSpec(a.shape, lambda b: (0,) * a.ndim)
    kern = functools.partial(_mla_lat_kernel, t=t, past=past, qpos0=qpos0)
    return pl.pallas_call(
        kern,
        grid=(bsz,),
        in_specs=[pl.BlockSpec((t, B_HEADS * B_QK_PAD), row),
                  pl.BlockSpec((t, gw), lambda b: (b, go // gw)),
                  pl.BlockSpec((None, 1, past, KV_LORA), lambda b: (layer, b, 0, 0)),
                  pl.BlockSpec((None, 1, ROPE_DIM, past), lambda b: (layer, b, 0, 0)),
                  pl.BlockSpec((t, KV_LORA), row),
                  pl.BlockSpec((t, ROPE_DIM), row),
                  const(wukt), const(wuv)],
        out_specs=pl.BlockSpec((t, B_WIDTH), row),
        out_shape=jax.ShapeDtypeStruct((bsz * t, B_WIDTH), BF16),
        compiler_params=pltpu.CompilerParams(dimension_semantics=("arbitrary",), vmem_limit_bytes=VMEM_LIMIT),
        name="mla_lat",
    )(qcat, z, cache_ckv, cache_kr, new_ckv, new_kr, wukt, wuv)


def _out_kernel(ga_ref, gb_ref, mg_ref, x_ref, woa_ref, wob_ref, wout_ref, fg_ref, o_ref, *, d, final):
    ya = jnp.dot(ga_ref[...], woa_ref[...], preferred_element_type=F32)
    yb = jnp.dot(gb_ref[...], wob_ref[...], preferred_element_type=F32)
    m = jax.nn.sigmoid(mg_ref[:, :d]) * ya + jax.nn.sigmoid(mg_ref[:, d:]) * yb
    h = x_ref[...] + jnp.dot(m.astype(BF16), wout_ref[...], preferred_element_type=F32)
    o_ref[...] = _rms(h, fg_ref[...]) if final else h


def _out(ga, gb, z, dst, x, woa, wob, wout, fg, final):
    m, d = x.shape
    tm = min(m, 256)
    assert m % tm == 0 and dst["mg"][0] == 0
    row = lambda i: (i, 0)
    const = lambda a: pl.BlockSpec(a.shape, lambda i: (0, 0), pipeline_mode=pl.Buffered(1))
    return pl.pallas_call(
        functools.partial(_out_kernel, d=d, final=final),
        grid=(m // tm,),
        in_specs=[pl.BlockSpec((tm, A_WIDTH), row), pl.BlockSpec((tm, B_WIDTH), row),
                  pl.BlockSpec((tm, 2 * d), row), pl.BlockSpec((tm, d), row),
                  const(woa), const(wob), const(wout), const(fg)],
        out_specs=pl.BlockSpec((tm, d), row),
        out_shape=jax.ShapeDtypeStruct((m, d), F32),
        compiler_params=pltpu.CompilerParams(dimension_semantics=("arbitrary",), vmem_limit_bytes=VMEM_LIMIT),
        name="out",
    )(ga, gb, z, x, woa, wob, wout, fg)


def _all_keys(cache, new, bsz, t, lp):
    new = new.reshape(bsz, t, new.shape[-1])
    if cache is None and lp == t:
        return new
    parts = ([] if cache is None else [cache.reshape(bsz, cache.shape[1], -1).astype(BF16)]) + [new]
    kk = jnp.concatenate(parts, axis=1)
    return jnp.pad(kk, ((0, 0), (0, lp - kk.shape[1]), (0, 0)))


def _trunk(x, caches, final_g, weights):
    bsz, t, d = x.shape
    depth = weights["w_in"].shape[0]
    past = 0 if caches is None else caches[0].shape[2]
    l_valid = past + t
    tq = min(t, QUERY_TILE)
    kw = KEY_BLOCK if t >= QUERY_TILE else -(-l_valid // LANES) * LANES
    lp = -(-l_valid // kw) * kw
    _, dst, n_proj = _layout(d)
    pos = past + jnp.arange(t, dtype=jnp.int32)
    t_scale = jnp.concatenate([jnp.full((t, IDX_HEADS), IDX_HEADS ** -0.5, F32),
                               jnp.ones((t, LANES - IDX_DIM - IDX_HEADS), F32)], axis=1)
    tab_a = jnp.stack(_rope_tables(pos, A_ROT, A_HEAD_DIM, 1))
    tab_i = jnp.stack(_rope_tables(pos, IDX_ROT, IDX_DIM, 2))
    tabs = [tab_a, jnp.stack(_rope_tables(pos, IDX_ROT, IDX_DIM, 1, extra=t_scale)),
            jnp.stack(_rope_tables(pos, ROPE_DIM, ROPE_DIM, 2))]
    reps = max(1, min(bsz * t, PROJ_ROWS) // t)
    proj_tabs = [jnp.tile(a, (1, reps, 1)) for a in (tab_a, tab_i)]
    h = x.reshape(bsz * t, d)
    new_rows = []
    for l in range(depth):
        w = {k: v[l] for k, v in weights.items()}
        z, aq, iq = _proj(h, w["norm_g"], w["w_in"], dst, n_proj, *proj_tabs)
        ak, av, ikw, krp, ckv, qcat, kb, vb, ikb, ckvb, krb = _prep(
            z, dst, bsz, t, tabs, w["q_norm_g"], w["kv_norm_g"], w["w_uq"])
        c = (None,) * 5 if caches is None else tuple(cc[l] for cc in caches)
        if caches is not None and t < QUERY_TILE:
            bias = _dsa_sel(iq, ikw, jnp.swapaxes(caches[2], 2, 3), l, ikb, bsz, t, past)
            interleaved = lambda a: a.reshape(bsz, past * A_KV_HEADS, A_HEAD_DIM)
            ga = _dsa_att(aq, z, dst, bias, interleaved(c[0]), interleaved(c[1]), kb, vb, bsz, t)
            gb = _mla_lat(qcat, z, dst, caches[3], jnp.swapaxes(caches[4], 2, 3), l, ckvb, krb, w["w_ukt"], w["w_uv"], bsz, t, past)
        else:
            k_all = _all_keys(c[0], kb, bsz, t, lp)
            v_all = _all_keys(c[1], vb, bsz, t, lp)
            ik_all = _all_keys(c[2], ikb, bsz, t, lp)
            ga = _dsa(aq, iq, ikw, z, dst, k_all, v_all, ik_all, bsz, t, tq, kw, l_valid, past)
            ckv_all = _all_keys(c[3], ckvb, bsz, t, lp)
            kr_all = _all_keys(c[4], krb, bsz, t, lp)
            kcat, vv = _kvup(ckv_all.reshape(bsz * lp, KV_LORA), kr_all.reshape(bsz * lp, ROPE_DIM), w["w_ukv"])
            tq_mla = MLA_QUERY_TILE if t % MLA_QUERY_TILE == 0 else tq
            gb = _mla(qcat, z, dst, kcat.reshape(bsz, lp, -1), vv.reshape(bsz, lp, -1), bsz, t, tq_mla, kw, l_valid, past)
        h = _out(ga, gb, z, dst, h, w["w_o_a"], w["w_o_b"], w["w_out"], final_g, final=(l == depth - 1))
        new_rows.append((ak.reshape(bsz, t, A_KV_HEADS, A_HEAD_DIM), av.reshape(bsz, t, A_KV_HEADS, A_HEAD_DIM),
                         ikw[:, :IDX_DIM].reshape(bsz, t, IDX_DIM), ckv.reshape(bsz, t, KV_LORA),
                         krp[:, :ROPE_DIM].reshape(bsz, t, ROPE_DIM)))
    stacked = [jnp.stack([r[i] for r in new_rows], axis=0) for i in range(5)]
    return h.reshape(bsz, t, d), stacked


def kernel(x_prompt, x_sample, cache_a_k, cache_a_v, cache_idx_k, cache_mla_ckv, cache_mla_krope,
           norm_g, w_in, w_uq, q_norm_g, kv_norm_g, w_uk, w_uv, w_o_a, w_o_b, w_out, final_g):
    depth, d, _ = w_in.shape
    weights = {
        "norm_g": norm_g.reshape(depth, 1, d),
        "w_in": jnp.stack([_regroup_w_in(jnp.swapaxes(w_in[l], 0, 1), d) for l in range(depth)]),
        "w_uq": jnp.concatenate([w_uq[..., :NOPE_DIM].reshape(depth, Q_LORA, B_HEADS * NOPE_DIM),
                                 w_uq[..., NOPE_DIM:].reshape(depth, Q_LORA, B_HEADS * ROPE_DIM)], axis=-1).astype(BF16),
        "q_norm_g": q_norm_g.reshape(depth, 1, Q_LORA),
        "kv_norm_g": kv_norm_g.reshape(depth, 1, KV_LORA),
        "w_ukv": jnp.concatenate([w_uk.reshape(depth, KV_LORA, B_HEADS * NOPE_DIM),
                                  w_uv.reshape(depth, KV_LORA, B_WIDTH)], axis=-1).astype(BF16),
        "w_ukt": jnp.transpose(w_uk, (0, 2, 3, 1)).astype(BF16),
        "w_uv": w_uv.reshape(depth, KV_LORA, B_WIDTH).astype(BF16),
        "w_o_a": w_o_a.astype(BF16), "w_o_b": w_o_b.astype(BF16), "w_out": w_out.astype(BF16),
    }
    fg = final_g.reshape(1, d)
    y_p, (pk, pv, pik, pckv, pkr) = _trunk(x_prompt, None, fg, weights)
    caches = (cache_a_k, cache_a_v, cache_idx_k, cache_mla_ckv, cache_mla_krope)
    y_s, (sk, sv, sik, sckv, skr) = _trunk(x_sample, caches, fg, weights)
    return (y_p, y_s, pk, pv, pik, pckv, pkr, sk, sv, sik, sckv, skr)
```

```python
import functools

import jax
import jax.numpy as jnp
from jax import lax
from jax.experimental import pallas as pl
from jax.experimental.pallas import tpu as pltpu

F32 = jnp.float32
BF16 = jnp.bfloat16

CHUNK = 64
ROPE_THETA = 500000.0
EPS = 1e-6

A_HEADS = 8
A_KV_HEADS = 2
A_HEAD_DIM = 128
A_ROT = A_HEAD_DIM // 4
A_WIDTH = A_HEADS * A_HEAD_DIM
A_KV_WIDTH = A_KV_HEADS * A_HEAD_DIM
IDX_HEADS = 16
IDX_DIM = 64
IDX_ROT = IDX_DIM // 4
TOPK_MAX = 256

B_HEADS = 8
Q_LORA = 512
KV_LORA = 512
NOPE_DIM = 128
ROPE_DIM = 64
V_DIM = 128
B_WIDTH = B_HEADS * V_DIM
B_QK_PAD = 256

LANES = 128
KEY_BLOCK = 256
QUERY_TILE = 256
PROJ_ROWS = 1024
PROJ_TILE = 1024
MLA_QUERY_TILE = 512
ROW_STRIP = 64
INT_MIN = -(2 ** 31)
LOG2_E = 1.4426950408889634
NEG = -1e30
MASKED = -1e38
VMEM_LIMIT = 56 * 1024 * 1024

_SEGS = (("aq", A_WIDTH), ("ak", A_KV_WIDTH), ("av", A_KV_WIDTH), ("agate", A_WIDTH),
         ("iq", IDX_HEADS * IDX_DIM), ("ik", IDX_DIM), ("iw", IDX_HEADS),
         ("cq", Q_LORA), ("ckv", KV_LORA), ("kr", ROPE_DIM), ("bgate", B_WIDTH), ("mg", None))


def _layout(d_model):
    src, off = {}, 0
    for name, w in _SEGS:
        w = 2 * d_model if w is None else w
        src[name] = (off, w)
        off += w
    order = (("mg", 2 * d_model), ("aq", A_WIDTH), ("agate", A_WIDTH), ("iq", IDX_HEADS * IDX_DIM),
             ("bgate", B_WIDTH), ("cq", Q_LORA), ("ckv", KV_LORA), ("ak", A_KV_WIDTH), ("av", A_KV_WIDTH),
             ("ikw", LANES), ("krp", LANES))
    dst, off = {}, 0
    for name, w in order:
        assert off % w == 0
        dst[name] = (off, w)
        off += w
    return src, dst, off


def _regroup_kernel(w_ref, o_ref, *, src, dst):
    cols = w_ref.shape[1]

    def piece(name):
        o, w = src[name]
        return w_ref[o:o + w, :].astype(BF16)

    def zeros(n):
        return jnp.zeros((n, cols), BF16)

    for name, (o, w) in dst.items():
        if name == "ikw":
            val = jnp.concatenate([piece("ik"), piece("iw"), zeros(LANES - IDX_DIM - IDX_HEADS)], axis=0)
        elif name == "krp":
            val = jnp.concatenate([piece("kr"), zeros(LANES - ROPE_DIM)], axis=0)
        else:
            val = piece(name)
        o_ref[o:o + w, :] = val
    total = max(o + w for o, w in dst.values())
    if o_ref.shape[0] > total:
        o_ref[total:, :] = zeros(o_ref.shape[0] - total)


def _regroup_w_in(w_in_t, d_model):
    src, dst, total = _layout(d_model)
    tc = 256
    padded = -(-total // PROJ_TILE) * PROJ_TILE
    assert d_model % tc == 0
    return pl.pallas_call(
        functools.partial(_regroup_kernel, src=src, dst=dst),
        grid=(d_model // tc,),
        in_specs=[pl.BlockSpec((w_in_t.shape[0], tc), lambda i: (0, i))],
        out_specs=pl.BlockSpec((padded, tc), lambda i: (0, i)),
        out_shape=jax.ShapeDtypeStruct((padded, d_model), BF16),
        compiler_params=pltpu.CompilerParams(dimension_semantics=("arbitrary",), vmem_limit_bytes=VMEM_LIMIT),
        name="regroup",
    )(w_in_t)


def _rope_tables(pos, rot, head, heads_per_vreg, extra=None):
    half = rot // 2
    inv_freq = ROPE_THETA ** (-jnp.arange(half, dtype=F32) * (2.0 / rot))
    ang = pos.astype(F32)[:, None] * inv_freq[None, :]
    cos, sin = jnp.cos(ang), jnp.sin(ang)
    t = pos.shape[0]
    one = jnp.ones((t, head - rot), F32)
    zero = lambda n: jnp.zeros((t, n), F32)
    c = jnp.concatenate([cos, cos, one], axis=1)
    sa = jnp.concatenate([-sin, zero(head - half)], axis=1)
    sb = jnp.concatenate([zero(half), sin, zero(head - rot)], axis=1)
    c, sa, sb = (jnp.tile(a, (1, heads_per_vreg)) for a in (c, sa, sb))
    if extra is not None:
        c = jnp.concatenate([c, extra], axis=1)
        sa = jnp.concatenate([sa, jnp.zeros_like(extra)], axis=1)
        sb = jnp.concatenate([sb, jnp.zeros_like(extra)], axis=1)
    assert c.shape[1] == LANES
    return c, sa, sb


def _rope(x, c, sa, sb, half):
    up = pltpu.roll(x, LANES - half, 1)
    dn = pltpu.roll(x, half, 1)
    return x * c + up * sa + dn * sb


def _rms(x, g):
    ms = jnp.mean(x * x, axis=-1, keepdims=True)
    return (x * lax.rsqrt(ms + EPS)) * g


def _lane_tile(x, width):
    return x if width == LANES else jnp.concatenate([x] * (width // LANES), axis=1)


_NT = (((1,), (1,)), ((), ()))


def _proj_kernel(x_ref, g_ref, w_ref, ta_ref, ti_ref, z_ref, aq_ref, iq_ref, xn_ref, hold_ref, *, j_aq, j_iq):
    j = pl.program_id(1)

    @pl.when(j == 0)
    def _():
        xn_ref[...] = _rms(x_ref[...], g_ref[...]).astype(BF16)

    def tile():
        return lax.dot_general(xn_ref[...], w_ref[...], _NT, preferred_element_type=F32)

    def rope_held(tab_ref, out_ref, half):
        c, sa, sb = tab_ref[0], tab_ref[1], tab_ref[2]
        for v in range(hold_ref.shape[1] // LANES):
            sl = slice(v * LANES, (v + 1) * LANES)
            out_ref[:, sl] = _rope(hold_ref[:, sl], c, sa, sb, half).astype(BF16)

    held = (j == j_aq) | (j == j_iq)
    after_aq = j == j_aq + 1
    after_iq = j == j_iq + 1

    @pl.when(held)
    def _():
        r = tile()
        z_ref[...] = r
        hold_ref[...] = r

    @pl.when(after_aq)
    def _():
        z_ref[...] = tile()
        rope_held(ta_ref, aq_ref, A_ROT // 2)

    @pl.when(after_iq)
    def _():
        z_ref[...] = tile()
        rope_held(ti_ref, iq_ref, IDX_ROT // 2)

    @pl.when(jnp.logical_not(held | after_aq | after_iq))
    def _():
        z_ref[...] = tile()


def _proj(x, g, w, dst, n, tab_a, tab_i):
    m, d = x.shape
    tm = min(m, PROJ_ROWS)
    tn = PROJ_TILE
    assert m % tm == 0 and w.shape[0] % tn == 0 and dst["aq"][1] == tn and dst["iq"][1] == tn
    j_aq, j_iq = dst["aq"][0] // tn, dst["iq"][0] // tn
    assert dst["aq"][0] % tn == 0 and dst["iq"][0] % tn == 0 and j_iq > j_aq + 1 and (j_iq + 1) * tn < n
    ntt = tab_a.shape[1] // tm
    tspec = pl.BlockSpec((3, tm, LANES), lambda i, j: (0, i % ntt, 0), pipeline_mode=pl.Buffered(1))
    return pl.pallas_call(
        functools.partial(_proj_kernel, j_aq=j_aq, j_iq=j_iq),
        grid=(m // tm, w.shape[0] // tn),
        in_specs=[pl.BlockSpec((tm, d), lambda i, j: (i, 0)),
                  pl.BlockSpec((1, d), lambda i, j: (0, 0)),
                  pl.BlockSpec((tn, d), lambda i, j: (j, 0)),
                  tspec, tspec],
        out_specs=[pl.BlockSpec((tm, tn), lambda i, j: (i, j)),
                   pl.BlockSpec((tm, tn), lambda i, j: (i, 0)),
                   pl.BlockSpec((tm, tn), lambda i, j: (i, 0))],
        out_shape=[jax.ShapeDtypeStruct((m, n), F32), jax.ShapeDtypeStruct((m, tn), BF16),
                   jax.ShapeDtypeStruct((m, tn), BF16)],
        scratch_shapes=[pltpu.VMEM((tm, d), BF16), pltpu.VMEM((tm, tn), F32)],
        compiler_params=pltpu.CompilerParams(dimension_semantics=("arbitrary", "arbitrary"),
                                             vmem_limit_bytes=VMEM_LIMIT),
        name="proj",
    )(x, g, w, tab_a, tab_i)


def _prep_kernel(cq_ref, ckv_ref, ak_ref, av_ref, ikw_ref, krp_ref,
                 ta_ref, tk_ref, tr_ref, qg_ref, kvg_ref, wuq_ref,
                 ak_o, av_o, ikw_o, krp_o, ckv_o, qcat_o, kb_o, vb_o, ikb_o, ckvb_o, krb_o):
    ca, saa, sba = ta_ref[0], ta_ref[1], ta_ref[2]
    ck, sak, sbk = tk_ref[0], tk_ref[1], tk_ref[2]
    cr, sar, sbr = tr_ref[0], tr_ref[1], tr_ref[2]
    tm = cq_ref.shape[0]
    for h in range(A_KV_HEADS):
        sl = slice(h * LANES, (h + 1) * LANES)
        ak = _rope(ak_ref[:, sl], ca, saa, sba, A_ROT // 2)
        ak_o[pl.ds(h, tm, stride=A_KV_HEADS), :] = ak
        av_o[pl.ds(h, tm, stride=A_KV_HEADS), :] = av_ref[:, sl]
        kb_o[:, sl] = ak.astype(BF16)
    vb_o[...] = av_ref[...].astype(BF16)
    ikw = _rope(ikw_ref[...], ck, sak, sbk, IDX_ROT // 2)
    ikw_o[...] = ikw
    ikb_o[...] = ikw[:, :IDX_DIM].astype(BF16)
    krp = _rope(krp_ref[...], cr, sar, sbr, ROPE_DIM // 2)
    krp_o[...] = krp
    krb_o[...] = krp[:, :ROPE_DIM].astype(BF16)
    ckv = _rms(ckv_ref[...], kvg_ref[...])
    ckv_o[...] = ckv
    ckvb_o[...] = ckv.astype(BF16)
    cqn = _rms(cq_ref[...], qg_ref[...]).astype(BF16)
    qb = jnp.dot(cqn, wuq_ref[...], preferred_element_type=F32)
    nope = B_HEADS * NOPE_DIM
    low = lax.broadcasted_iota(jnp.int32, (tm, LANES), 1) < ROPE_DIM
    for v in range(B_HEADS * ROPE_DIM // LANES):
        r = _rope(qb[:, nope + v * LANES: nope + (v + 1) * LANES], cr, sar, sbr, ROPE_DIM // 2)
        halves = (jnp.where(low, r, 0.0), jnp.where(low, pltpu.roll(r, ROPE_DIM, 1), 0.0))
        for j in range(2):
            h = 2 * v + j
            qcat_o[:, h * B_QK_PAD: h * B_QK_PAD + NOPE_DIM] = qb[:, h * NOPE_DIM:(h + 1) * NOPE_DIM].astype(BF16)
            qcat_o[:, h * B_QK_PAD + NOPE_DIM:(h + 1) * B_QK_PAD] = halves[j].astype(BF16)


def _prep(z, dst, bsz, t, tabs, q_norm_g, kv_norm_g, wuq):
    m = bsz * t
    tm = min(t, 256)
    nt = t // tm
    assert t % tm == 0

    def zspec(name):
        o, w = dst[name]
        return pl.BlockSpec((tm, w), lambda b, i, blk=o // w: (b * nt + i, blk))

    def tspec():
        return pl.BlockSpec((3, tm, LANES), lambda b, i: (0, i, 0))

    def ospec(w):
        return pl.BlockSpec((tm, w), lambda b, i: (b * nt + i, 0))

    def full(a):
        return pl.BlockSpec(a.shape, lambda b, i: (0,) * a.ndim)

    outs = [("interleaved", F32), ("interleaved", F32), (LANES, F32), (LANES, F32),
            (KV_LORA, F32), (B_HEADS * B_QK_PAD, BF16),
            (A_KV_WIDTH, BF16), (A_KV_WIDTH, BF16), (IDX_DIM, BF16), (KV_LORA, BF16), (ROPE_DIM, BF16)]
    return pl.pallas_call(
        _prep_kernel,
        grid=(bsz, nt),
        in_specs=[zspec("cq"), zspec("ckv"), zspec("ak"), zspec("av"), zspec("ikw"),
                  zspec("krp"), tspec(), tspec(), tspec(), full(q_norm_g), full(kv_norm_g), full(wuq)],
        out_specs=[pl.BlockSpec((tm * A_KV_HEADS, A_HEAD_DIM), lambda b, i: (b * nt + i, 0)) if w == "interleaved" else ospec(w)
                   for w, _ in outs],
        out_shape=[jax.ShapeDtypeStruct((m * A_KV_HEADS, A_HEAD_DIM) if w == "interleaved" else (m, w), dt) for w, dt in outs],
        compiler_params=pltpu.CompilerParams(dimension_semantics=("arbitrary", "arbitrary"),
                                             vmem_limit_bytes=VMEM_LIMIT),
        name="prep",
    )(z, z, z, z, z, z, *tabs, q_norm_g, kv_norm_g, wuq)


def _kvup_kernel(c_ref, kr_ref, w_ref, kcat_ref, v_ref):
    r = jnp.dot(c_ref[...], w_ref[...], preferred_element_type=F32)
    n = B_HEADS * NOPE_DIM
    tail = jnp.concatenate([kr_ref[...], jnp.zeros((kr_ref.shape[0], B_QK_PAD - NOPE_DIM - ROPE_DIM), BF16)], axis=1)
    for h in range(B_HEADS):
        kcat_ref[:, h * B_QK_PAD: h * B_QK_PAD + NOPE_DIM] = r[:, h * NOPE_DIM:(h + 1) * NOPE_DIM].astype(BF16)
        kcat_ref[:, h * B_QK_PAD + NOPE_DIM:(h + 1) * B_QK_PAD] = tail
    v_ref[...] = r[:, n:].astype(BF16)


def _kvup(ckv, kr, w):
    m, c = ckv.shape
    tm = next(t for t in (512, 384, 256, 128) if m % t == 0)
    row = lambda i: (i, 0)
    return pl.pallas_call(
        _kvup_kernel,
        grid=(m // tm,),
        in_specs=[pl.BlockSpec((tm, c), row), pl.BlockSpec((tm, ROPE_DIM), row), pl.BlockSpec(w.shape, lambda i: (0, 0))],
        out_specs=[pl.BlockSpec((tm, B_HEADS * B_QK_PAD), row), pl.BlockSpec((tm, B_WIDTH), row)],
        out_shape=[jax.ShapeDtypeStruct((m, B_HEADS * B_QK_PAD), BF16), jax.ShapeDtypeStruct((m, B_WIDTH), BF16)],
        compiler_params=pltpu.CompilerParams(dimension_semantics=("arbitrary",), vmem_limit_bytes=VMEM_LIMIT),
        name="kv_up",
    )(ckv, kr, w)


def _visible_cols(tile, tq, l_valid, qpos0):
    p_last = qpos0 + (tile + 1) * tq - 1
    n = (p_last // CHUNK + 1) * CHUNK
    return min(n, l_valid) if isinstance(n, int) else jnp.minimum(n, l_valid)


def _num_key_blocks(tile, tq, kw, l_valid, qpos0):
    return (_visible_cols(tile, tq, l_valid, qpos0) + kw - 1) // kw


def _fully_visible_blocks(tile, tq, kw, l_valid, qpos0):
    first_row_cols = jnp.minimum(((qpos0 + tile * tq) // CHUNK + 1) * CHUNK, l_valid)
    return first_row_cols // kw


def _allowed(tile, kb, rows0, shape, tq, kw, l_valid, qpos0):
    rows = lax.broadcasted_iota(jnp.int32, shape, 0) + rows0
    cols = lax.broadcasted_iota(jnp.int32, shape, 1) + kb * kw
    qchunk = (qpos0 + tile * tq + rows) // CHUNK
    return ((cols // CHUNK) <= qchunk) & (cols < l_valid), cols


def _softmax_step(s, scale, h, vblk, m_ref, l_ref, acc_ref):
    kw = s.shape[1]
    c = scale * LOG2_E
    m_prev = m_ref[h]
    m_new = jnp.maximum(m_prev, jnp.max(s, axis=-1, keepdims=True))
    alpha = jnp.exp2((m_prev - m_new) * c)
    p = jnp.exp2((s - _lane_tile(m_new, kw)) * c)
    l_ref[h] = alpha * l_ref[h] + jnp.sum(p, axis=-1, keepdims=True)
    acc_ref[h] = alpha * acc_ref[h] + jnp.dot(p.astype(BF16), vblk, preferred_element_type=F32)
    m_ref[h] = m_new


def _softmax_init(heads, tq, m_ref, l_ref, acc_ref):
    for h in range(heads):
        m_ref[h] = jnp.full((tq, LANES), NEG, F32)
        l_ref[h] = jnp.zeros((tq, LANES), F32)
        acc_ref[h] = jnp.zeros((tq, acc_ref.shape[2]), F32)


def _masked_score(score, allowed):
    return jnp.where(allowed, score, MASKED)


def _key_to_f32(key):
    return lax.bitcast_convert_type(key ^ ((key >> 31) & 0x7FFFFFFF), F32)


def _selection_bias(score, cols, thr, jcut, allowed):
    sel = ((score > thr) | ((score == thr) & (cols < jcut))) & allowed
    return jnp.where(sel, 0.0, NEG)


def _topk_thresholds(strips, rs, kf, thr_ref, jcut_ref, need_ref):
    def count(pieces, pred):
        acc = jnp.zeros((rs, LANES), F32)
        for load, col0 in pieces:
            acc = acc + jnp.where(pred(load(), col0), 1.0, 0.0)
        return jnp.sum(acc, axis=-1, keepdims=True)

    def bit_step(b, carry):
        bit = jnp.left_shift(jnp.int32(1), 31 - b)
        for rows, pieces in strips:
            t = thr_ref[rows, :]
            cand = t + bit
            cand_f = _key_to_f32(cand)
            cnt = count(pieces, lambda s, c0: s >= cand_f)
            thr_ref[rows, :] = jnp.where(cnt >= kf, cand, t)
        return carry

    thr_ref[...] = jnp.full(thr_ref.shape, INT_MIN, jnp.int32)
    lax.fori_loop(0, 32, bit_step, 0)

    jcut_ref[...] = jnp.full(jcut_ref.shape, 4096, jnp.int32)
    flag = jnp.int32(0)
    for rows, pieces in strips:
        t = _key_to_f32(thr_ref[rows, :])
        excess = (count(pieces, lambda s, c0: s >= t) > kf) & (t[:, :1] > MASKED)
        flag = jnp.maximum(flag, jnp.max(jnp.where(excess, 1, 0)))

    @pl.when(flag > 0)
    def _():
        lane = lax.broadcasted_iota(jnp.int32, (rs, LANES), 1)
        for rows, pieces in strips:
            t = _key_to_f32(thr_ref[rows, :])
            need = kf - count(pieces, lambda s, c0: s > t)
            need_ref[rows, :] = jnp.broadcast_to(need, (rs, LANES))

        def col_step(b, carry):
            bit = jnp.left_shift(jnp.int32(1), 11 - b)
            for rows, pieces in strips:
                t, j = _key_to_f32(thr_ref[rows, :]), jcut_ref[rows, :]
                cand = j + bit
                f = count(pieces, lambda s, c0: (s == t) & (lane + c0 < cand))
                jcut_ref[rows, :] = jnp.where(f <= need_ref[rows, :], cand, j)
            return carry

        jcut_ref[...] = jnp.zeros(jcut_ref.shape, jnp.int32)
        lax.fori_loop(0, 12, col_step, 0)

    return flag


def _dsa_kernel(aq_ref, iq_ref, ikw_ref, agate_ref, k_ref, v_ref, ik_ref, o_ref,
                key_ref, thr_ref, jcut_ref, need_ref, ties_ref, bias_ref, iwb_ref, qst_ref, qs_ref, acc_ref, m_ref, l_ref,
                *, nt, tq, kw, l_valid, qpos0, topk):
    nkb_max = key_ref.shape[0] // nt
    nkb_of = [_num_key_blocks(ti, tq, kw, l_valid, qpos0) for ti in range(nt)]
    rs = qst_ref.shape[1] // IDX_HEADS
    ns = tq // rs
    nc = kw // LANES
    kf = float(topk)

    def score_tile(ti, carry):
        r0 = pl.multiple_of(ti * tq, tq)
        iw = ikw_ref[pl.ds(r0, tq), IDX_DIM:IDX_DIM + IDX_HEADS] * (IDX_DIM ** -0.5)
        for h in range(IDX_HEADS):
            iwb_ref[h] = jnp.broadcast_to(iw[:, h:h + 1], (tq, LANES))
            for r in range(ns):
                qst_ref[r, h * rs:(h + 1) * rs, :] = iq_ref[pl.ds(r0 + r * rs, rs), h * IDX_DIM:(h + 1) * IDX_DIM]

        def score_block(kb, c2):
            ikb = ik_ref[0, pl.ds(pl.multiple_of(kb * kw, kw), kw), :]
            for r in range(ns):
                rel = lax.dot_general(qst_ref[r], ikb, _NT, preferred_element_type=F32)
                sc = jnp.zeros((rs, kw), F32)
                for h in range(IDX_HEADS):
                    w = _lane_tile(iwb_ref[h, r * rs:(r + 1) * rs, :], kw)
                    sc = sc + jnp.maximum(rel[h * rs:(h + 1) * rs], 0.0) * w
                allowed, _ = _allowed(ti, kb, r * rs, (rs, kw), tq, kw, l_valid, qpos0)
                key_ref[ti * nkb_max + kb, r * rs:(r + 1) * rs, :] = _masked_score(sc, allowed)
            return c2

        lax.fori_loop(0, _num_key_blocks(ti, tq, kw, l_valid, qpos0), score_block, 0)
        return carry

    def piece(ti, kb, s, c):
        return lambda: key_ref[ti * nkb_max + kb, s * rs:(s + 1) * rs, c * LANES:(c + 1) * LANES]

    strips = [(slice(ti * tq + s * rs, ti * tq + (s + 1) * rs),
               [(piece(ti, kb, s, c), kb * kw + c * LANES) for kb in range(nkb_of[ti]) for c in range(nc)])
              for ti in range(nt) for s in range(ns)]

    @pl.when(pl.program_id(1) == 0)
    def _():
        lax.fori_loop(0, nt, score_tile, 0)
        ties_ref[0] = _topk_thresholds(strips, rs, kf, thr_ref, jcut_ref, need_ref)

    ti = pl.program_id(1)
    nkb = _num_key_blocks(ti, tq, kw, l_valid, qpos0)
    r0 = pl.multiple_of(ti * tq, tq)
    thr = _lane_tile(_key_to_f32(thr_ref[pl.ds(r0, tq), :]), kw)
    jcut = _lane_tile(jcut_ref[pl.ds(r0, tq), :], kw)

    def bias_block(kb, masked, ties):
        score = key_ref[ti * nkb_max + kb]
        if ties:
            allowed, cols = _allowed(ti, kb, 0, (tq, kw), tq, kw, l_valid, qpos0)
            bias_ref[kb] = _selection_bias(score, cols, thr, jcut, allowed)
        else:
            sel = score >= thr
            if masked:
                sel = sel & _allowed(ti, kb, 0, (tq, kw), tq, kw, l_valid, qpos0)[0]
            bias_ref[kb] = jnp.where(sel, 0.0, NEG)

    def bias_loop(lo, hi, masked, ties):
        lax.fori_loop(lo, hi, lambda kb, c: (bias_block(kb, masked, ties), c)[1], 0)

    n_full = _fully_visible_blocks(ti, tq, kw, l_valid, qpos0)

    @pl.when(ties_ref[0] == 0)
    def _():
        bias_loop(0, n_full, False, False)
        bias_loop(n_full, nkb, True, False)

    @pl.when(ties_ref[0] != 0)
    def _():
        bias_loop(0, nkb, True, True)

    grp = A_HEADS // A_KV_HEADS
    _softmax_init(A_KV_HEADS, grp * tq, m_ref, l_ref, acc_ref)
    scale = A_HEAD_DIM ** -0.5
    for h in range(A_HEADS):
        g, hh = divmod(h, grp)
        qs_ref[g, hh * tq:(hh + 1) * tq, :] = aq_ref[:, h * A_HEAD_DIM:(h + 1) * A_HEAD_DIM]

    def att_block(kb, carry):
        c0 = pl.multiple_of(kb * kw, kw)
        bias = bias_ref[kb][None]
        for g in range(A_KV_HEADS):
            gsl = slice(g * A_HEAD_DIM, (g + 1) * A_HEAD_DIM)
            s = lax.dot_general(qs_ref[g], k_ref[0, pl.ds(c0, kw), gsl], _NT, preferred_element_type=F32)
            s = (s.reshape(grp, tq, kw) + bias).reshape(grp * tq, kw)
            _softmax_step(s, scale, g, v_ref[0, pl.ds(c0, kw), gsl], m_ref, l_ref, acc_ref)
        return carry

    lax.fori_loop(0, nkb, att_block, 0)

    for h in range(A_HEADS):
        g, hh = divmod(h, grp)
        sl = slice(h * A_HEAD_DIM, (h + 1) * A_HEAD_DIM)
        o = acc_ref[g, hh * tq:(hh + 1) * tq, :] / l_ref[g, hh * tq:(hh + 1) * tq, :]
        gate = agate_ref[:, sl]
        o_ref[:, sl] = (o * (gate * jax.nn.sigmoid(gate))).astype(BF16)


def _dsa(aq, iq, ikw, z, dst, k, v, ik, bsz, t, tq, kw, l_valid, qpos0):
    lp = k.shape[1]
    assert lp < 4096 and lp % kw == 0 and t % tq == 0
    nt = t // tq
    nkb_max = lp // kw
    topk = min(TOPK_MAX, l_valid // 4)
    grp = A_HEADS // A_KV_HEADS
    rs = min(tq, ROW_STRIP)
    go, gw = dst["agate"]
    row = lambda b, i: (b * nt + i, 0)
    whole = lambda b, i: (b, 0)
    batch = lambda b, i: (b, 0, 0)
    kern = functools.partial(_dsa_kernel, nt=nt, tq=tq, kw=kw, l_valid=l_valid, qpos0=qpos0, topk=topk)
    return pl.pallas_call(
        kern,
        grid=(bsz, nt),
        in_specs=[pl.BlockSpec((tq, A_WIDTH), row),
                  pl.BlockSpec((t, IDX_HEADS * IDX_DIM), whole),
                  pl.BlockSpec((t, LANES), whole),
                  pl.BlockSpec((tq, gw), lambda b, i: (b * nt + i, go // gw)),
                  pl.BlockSpec((1, lp, A_KV_WIDTH), batch),
                  pl.BlockSpec((1, lp, A_KV_WIDTH), batch),
                  pl.BlockSpec((1, lp, IDX_DIM), batch)],
        out_specs=pl.BlockSpec((tq, A_WIDTH), row),
        out_shape=jax.ShapeDtypeStruct((bsz * t, A_WIDTH), BF16),
        scratch_shapes=[pltpu.VMEM((nt * nkb_max, tq, kw), F32),
                        pltpu.VMEM((t, LANES), jnp.int32),
                        pltpu.VMEM((t, LANES), jnp.int32),
                        pltpu.VMEM((t, LANES), F32),
                        pltpu.SMEM((1,), jnp.int32),
                        pltpu.VMEM((nkb_max, tq, kw), F32),
                        pltpu.VMEM((IDX_HEADS, tq, LANES), F32),
                        pltpu.VMEM((tq // rs, IDX_HEADS * rs, IDX_DIM), BF16),
                        pltpu.VMEM((A_KV_HEADS, grp * tq, A_HEAD_DIM), BF16),
                        pltpu.VMEM((A_KV_HEADS, grp * tq, A_HEAD_DIM), F32),
                        pltpu.VMEM((A_KV_HEADS, grp * tq, LANES), F32),
                        pltpu.VMEM((A_KV_HEADS, grp * tq, LANES), F32)],
        compiler_params=pltpu.CompilerParams(dimension_semantics=("arbitrary", "arbitrary"),
                                             vmem_limit_bytes=VMEM_LIMIT),
        name="dsa",
    )(aq, iq, ikw, z, k, v, ik)


SEL_BATCHES = 8
SEL_CHUNK = 512


def _dsa_sel_kernel(iq_ref, ikw_ref, cik_ref, nik_ref, bias_ref, key_ref, thr_ref, jcut_ref, need_ref,
                    *, gb, t, past, qpos0, topk):
    lp = past + LANES
    nc = lp // LANES

    def allowed_cols(col0, width):
        rows = lax.broadcasted_iota(jnp.int32, (t, width), 0)
        cols = lax.broadcasted_iota(jnp.int32, (t, width), 1) + col0
        return ((cols // CHUNK) <= (qpos0 + rows) // CHUNK) & (cols < past + t)

    def score_batch(j, carry):
        r0 = pl.multiple_of(j * t, t)
        qst = jnp.concatenate([iq_ref[pl.ds(r0, t), h * IDX_DIM:(h + 1) * IDX_DIM] for h in range(IDX_HEADS)], axis=0)
        iw = ikw_ref[pl.ds(r0, t), IDX_DIM:IDX_DIM + IDX_HEADS] * (IDX_DIM ** -0.5)
        iwb = [jnp.broadcast_to(iw[:, h:h + 1], (t, LANES)) for h in range(IDX_HEADS)]

        def head_sum(rel):
            width = rel.shape[1]
            sc = jnp.zeros((t, width), F32)
            for h in range(IDX_HEADS):
                sc = sc + jnp.maximum(rel[h * t:(h + 1) * t], 0.0) * _lane_tile(iwb[h], width)
            return sc

        for c in range(past // SEL_CHUNK):
            ikt = cik_ref[j, :, c * SEL_CHUNK:(c + 1) * SEL_CHUNK].astype(BF16)
            sc = head_sum(jnp.dot(qst, ikt, preferred_element_type=F32))
            key_ref[j, :, c * SEL_CHUNK:(c + 1) * SEL_CHUNK] = _masked_score(sc, allowed_cols(c * SEL_CHUNK, SEL_CHUNK))
        ik = jnp.concatenate([nik_ref[pl.ds(r0, t), :], jnp.zeros((LANES - t, IDX_DIM), BF16)], axis=0)
        sc = head_sum(lax.dot_general(qst, ik, _NT, preferred_element_type=F32))
        key_ref[j, :, past:] = _masked_score(sc, allowed_cols(past, LANES))
        return carry

    lax.fori_loop(0, gb, score_batch, 0)

    def piece(j, c):
        return lambda: key_ref[j, :, c * LANES:(c + 1) * LANES]

    strips = [(slice(j * t, (j + 1) * t), [(piece(j, c), c * LANES) for c in range(nc)]) for j in range(gb)]
    _topk_thresholds(strips, t, float(topk), thr_ref, jcut_ref, need_ref)

    def bias_batch(j, carry):
        r0 = pl.multiple_of(j * t, t)
        cols = lax.broadcasted_iota(jnp.int32, (t, lp), 1)
        bias_ref[j] = _selection_bias(key_ref[j], cols, _lane_tile(_key_to_f32(thr_ref[pl.ds(r0, t), :]), lp),
                                      _lane_tile(jcut_ref[pl.ds(r0, t), :], lp), allowed_cols(0, lp))
        return carry

    lax.fori_loop(0, gb, bias_batch, 0)


def _dsa_sel(iq, ikw, cache_ik_t, layer, new_ik, bsz, t, qpos0):
    past = cache_ik_t.shape[3]
    lp = past + LANES
    gb = next(g for g in (SEL_BATCHES, 4, 2, 1) if bsz % g == 0)
    assert past % SEL_CHUNK == 0 and t <= LANES and lp < 4096
    topk = min(TOPK_MAX, (past + t) // 4)
    rows = lambda i: (i, 0)
    kern = functools.partial(_dsa_sel_kernel, gb=gb, t=t, past=past, qpos0=qpos0, topk=topk)
    return pl.pallas_call(
        kern,
        grid=(bsz // gb,),
        in_specs=[pl.BlockSpec((gb * t, IDX_HEADS * IDX_DIM), rows),
                  pl.BlockSpec((gb * t, LANES), rows),
                  pl.BlockSpec((None, gb, IDX_DIM, past), lambda i: (layer, i, 0, 0)),
                  pl.BlockSpec((gb * t, IDX_DIM), rows)],
        out_specs=pl.BlockSpec((gb, t, lp), lambda i: (i, 0, 0)),
        out_shape=jax.ShapeDtypeStruct((bsz, t, lp), F32),
        scratch_shapes=[pltpu.VMEM((gb, t, lp), F32),
                        pltpu.VMEM((gb * t, LANES), jnp.int32),
                        pltpu.VMEM((gb * t, LANES), jnp.int32),
                        pltpu.VMEM((gb * t, LANES), F32)],
        compiler_params=pltpu.CompilerParams(dimension_semantics=("arbitrary",), vmem_limit_bytes=VMEM_LIMIT),
        name="dsa_sel",
    )(iq, ikw, cache_ik_t, new_ik)


def _dsa_att_kernel(aq_ref, agate_ref, bias_ref, ck_ref, cv_ref, nk_ref, nv_ref, o_ref, *, t, past):
    grp = A_HEADS // A_KV_HEADS
    scale = A_HEAD_DIM ** -0.5
    pad = jnp.zeros((LANES - t, A_HEAD_DIM), BF16)
    b1, b2 = bias_ref[0, :, :past], bias_ref[0, :, past:]
    for g in range(A_KV_HEADS):
        gsl = slice(g * A_HEAD_DIM, (g + 1) * A_HEAD_DIM)
        kc = ck_ref[0, pl.ds(g, past, stride=A_KV_HEADS), :].astype(BF16)
        vc = cv_ref[0, pl.ds(g, past, stride=A_KV_HEADS), :].astype(BF16)
        kn = jnp.concatenate([nk_ref[:, gsl], pad], axis=0)
        vn = jnp.concatenate([nv_ref[:, gsl], pad], axis=0)
        qs = jnp.concatenate([aq_ref[:, (g * grp + hh) * A_HEAD_DIM:(g * grp + hh + 1) * A_HEAD_DIM]
                              for hh in range(grp)], axis=0)

        def scores(keys, bias):
            s = lax.dot_general(qs, keys, _NT, preferred_element_type=F32) * scale
            return (s.reshape(grp, t, s.shape[1]) + bias[None]).reshape(grp * t, s.shape[1])

        s1, s2 = scores(kc, b1), scores(kn, b2)
        m = jnp.maximum(jnp.max(s1, axis=-1, keepdims=True), jnp.max(s2, axis=-1, keepdims=True))
        e1, e2 = jnp.exp(s1 - m), jnp.exp(s2 - m)
        l = jnp.sum(e1, axis=-1, keepdims=True) + jnp.sum(e2, axis=-1, keepdims=True)
        o = (jnp.dot(e1.astype(BF16), vc, preferred_element_type=F32)
             + jnp.dot(e2.astype(BF16), vn, preferred_element_type=F32)) / l
        for hh in range(grp):
            sl = slice((g * grp + hh) * A_HEAD_DIM, (g * grp + hh + 1) * A_HEAD_DIM)
            gate = agate_ref[:, sl]
            o_ref[:, sl] = (o[hh * t:(hh + 1) * t] * (gate * jax.nn.sigmoid(gate))).astype(BF16)


def _dsa_att(aq, z, dst, bias, cache_k, cache_v, new_k, new_v, bsz, t):
    past = cache_k.shape[1] // A_KV_HEADS
    lp = bias.shape[2]
    go, gw = dst["agate"]
    row = lambda b: (b, 0)
    batch = lambda b: (b, 0, 0)
    return pl.pallas_call(
        functools.partial(_dsa_att_kernel, t=t, past=past),
        grid=(bsz,),
        in_specs=[pl.BlockSpec((t, A_WIDTH), row),
                  pl.BlockSpec((t, gw), lambda b: (b, go // gw)),
                  pl.BlockSpec((1, t, lp), batch),
                  pl.BlockSpec((1, past * A_KV_HEADS, A_HEAD_DIM), batch),
                  pl.BlockSpec((1, past * A_KV_HEADS, A_HEAD_DIM), batch),
                  pl.BlockSpec((t, A_KV_WIDTH), row),
                  pl.BlockSpec((t, A_KV_WIDTH), row)],
        out_specs=pl.BlockSpec((t, A_WIDTH), row),
        out_shape=jax.ShapeDtypeStruct((bsz * t, A_WIDTH), BF16),
        compiler_params=pltpu.CompilerParams(dimension_semantics=("arbitrary",), vmem_limit_bytes=VMEM_LIMIT),
        name="dsa_att",
    )(aq, z, bias, cache_k, cache_v, new_k, new_v)


def _mla_kernel(q_ref, bgate_ref, k_ref, v_ref, o_ref, acc_ref, m_ref, l_ref, *, tq, kw, l_valid, qpos0):
    ti = pl.program_id(1)
    nkb = _num_key_blocks(ti, tq, kw, l_valid, qpos0)
    scale = (NOPE_DIM + ROPE_DIM) ** -0.5
    _softmax_init(B_HEADS, tq, m_ref, l_ref, acc_ref)

    def att_block(kb, masked):
        r0 = pl.multiple_of(kb * kw, kw)
        if masked:
            allowed, _ = _allowed(ti, kb, 0, (tq, kw), tq, kw, l_valid, qpos0)
            bias = jnp.where(allowed, 0.0, NEG)
        for h in range(B_HEADS):
            qsl = slice(h * B_QK_PAD, (h + 1) * B_QK_PAD)
            s = lax.dot_general(q_ref[:, qsl], k_ref[0, pl.ds(r0, kw), qsl], _NT, preferred_element_type=F32)
            if masked:
                s = s + bias
            _softmax_step(s, scale, h, v_ref[0, pl.ds(r0, kw), h * V_DIM:(h + 1) * V_DIM], m_ref, l_ref, acc_ref)

    n_full = _fully_visible_blocks(ti, tq, kw, l_valid, qpos0)
    lax.fori_loop(0, n_full, lambda kb, c: (att_block(kb, False), c)[1], 0)
    lax.fori_loop(n_full, nkb, lambda kb, c: (att_block(kb, True), c)[1], 0)

    for h in range(B_HEADS):
        sl = slice(h * V_DIM, (h + 1) * V_DIM)
        o = acc_ref[h] / l_ref[h]
        gate = bgate_ref[:, sl]
        o_ref[:, sl] = (o * (gate * jax.nn.sigmoid(gate))).astype(BF16)


def _mla(qcat, z, dst, kcat, v, bsz, t, tq, kw, l_valid, qpos0):
    lp = kcat.shape[1]
    assert lp % kw == 0 and t % tq == 0
    nt = t // tq
    go, gw = dst["bgate"]
    row = lambda b, i: (b * nt + i, 0)
    batch = lambda b, i: (b, 0, 0)
    kern = functools.partial(_mla_kernel, tq=tq, kw=kw, l_valid=l_valid, qpos0=qpos0)
    return pl.pallas_call(
        kern,
        grid=(bsz, nt),
        in_specs=[pl.BlockSpec((tq, B_HEADS * B_QK_PAD), row),
                  pl.BlockSpec((tq, gw), lambda b, i: (b * nt + i, go // gw)),
                  pl.BlockSpec((1, lp, B_HEADS * B_QK_PAD), batch),
                  pl.BlockSpec((1, lp, B_WIDTH), batch)],
        out_specs=pl.BlockSpec((tq, B_WIDTH), row),
        out_shape=jax.ShapeDtypeStruct((bsz * t, B_WIDTH), BF16),
        scratch_shapes=[pltpu.VMEM((B_HEADS, tq, V_DIM), F32),
                        pltpu.VMEM((B_HEADS, tq, LANES), F32),
                        pltpu.VMEM((B_HEADS, tq, LANES), F32)],
        compiler_params=pltpu.CompilerParams(dimension_semantics=("arbitrary", "arbitrary"),
                                             vmem_limit_bytes=VMEM_LIMIT),
        name="mla",
    )(qcat, z, kcat, v)


def _mla_lat_kernel(q_ref, bgate_ref, cc_ref, ckr_ref, nc_ref, nkr_ref, wukt_ref, wuv_ref, o_ref,
                    *, t, past, qpos0):
    rows_n = B_HEADS * t
    ql, qr = [], []
    for h in range(B_HEADS):
        qn = q_ref[:, h * B_QK_PAD: h * B_QK_PAD + NOPE_DIM]
        ql.append(jnp.dot(qn, wukt_ref[h], preferred_element_type=F32).astype(BF16))
        qr.append(q_ref[:, h * B_QK_PAD + NOPE_DIM: h * B_QK_PAD + NOPE_DIM + ROPE_DIM])
    ql = jnp.concatenate(ql, axis=0)
    qr = jnp.concatenate(qr, axis=0)
    cc = cc_ref[0].astype(BF16)
    ckr_t = ckr_ref[0].astype(BF16)
    nc, nkr = nc_ref[...], nkr_ref[...]
    scale = (NOPE_DIM + ROPE_DIM) ** -0.5

    def scores(lat, rope_part, ncols, col0):
        s = (lax.dot_general(ql, lat, _NT, preferred_element_type=F32) + rope_part) * scale
        rows = lax.broadcasted_iota(jnp.int32, (rows_n, ncols), 0)
        cols = lax.broadcasted_iota(jnp.int32, (rows_n, ncols), 1) + col0
        qchunk = (qpos0 + lax.rem(rows, t)) // CHUNK
        return jnp.where((cols // CHUNK) <= qchunk, s, NEG)

    s1 = scores(cc, jnp.dot(qr, ckr_t, preferred_element_type=F32), past, 0)
    s2 = scores(nc, lax.dot_general(qr, nkr, _NT, preferred_element_type=F32), t, past)
    m = jnp.maximum(jnp.max(s1, axis=-1, keepdims=True), jnp.max(s2, axis=-1, keepdims=True))
    e1, e2 = jnp.exp(s1 - m), jnp.exp(s2 - m)
    l = jnp.sum(e1, axis=-1, keepdims=True) + jnp.sum(e2, axis=-1, keepdims=True)
    ol = (jnp.dot(e1.astype(BF16), cc, preferred_element_type=F32)
          + jnp.dot(e2.astype(BF16), nc, preferred_element_type=F32)) / l
    ol = ol.astype(BF16)
    for h in range(B_HEADS):
        sl = slice(h * V_DIM, (h + 1) * V_DIM)
        o = jnp.dot(ol[h * t:(h + 1) * t], wuv_ref[:, sl], preferred_element_type=F32)
        gate = bgate_ref[:, sl]
        o_ref[:, sl] = (o * (gate * jax.nn.sigmoid(gate))).astype(BF16)


def _mla_lat(qcat, z, dst, cache_ckv, cache_kr, layer, new_ckv, new_kr, wukt, wuv, bsz, t, qpos0):
    past = cache_ckv.shape[2]
    go, gw = dst["bgate"]
    row = lambda b: (b, 0)
    batch = lambda b: (b, 0, 0)
    const = lambda a: pl.BlockSpec(a.shape, lambda b: (0,) * a.ndim)
    kern = functools.partial(_mla_lat_kernel, t=t, past=past, qpos0=qpos0)
    return pl.pallas_call(
        kern,
        grid=(bsz,),
        in_specs=[pl.BlockSpec((t, B_HEADS * B_QK_PAD), row),
                  pl.BlockSpec((t, gw), lambda b: (b, go // gw)),
                  pl.BlockSpec((None, 1, past, KV_LORA), lambda b: (layer, b, 0, 0)),
                  pl.BlockSpec((None, 1, ROPE_DIM, past), lambda b: (layer, b, 0, 0)),
                  pl.BlockSpec((t, KV_LORA), row),
                  pl.BlockSpec((t, ROPE_DIM), row),
                  const(wukt), const(wuv)],
        out_specs=pl.BlockSpec((t, B_WIDTH), row),
        out_shape=jax.ShapeDtypeStruct((bsz * t, B_WIDTH), BF16),
        compiler_params=pltpu.CompilerParams(dimension_semantics=("arbitrary",), vmem_limit_bytes=VMEM_LIMIT),
        name="mla_lat",
    )(qcat, z, cache_ckv, cache_kr, new_ckv, new_kr, wukt, wuv)


def _out_kernel(ga_ref, gb_ref, mg_ref, x_ref, woa_ref, wob_ref, wout_ref, fg_ref, o_ref, *, d, final):
    ya = jnp.dot(ga_ref[...], woa_ref[...], preferred_element_type=F32)
    yb = jnp.dot(gb_ref[...], wob_ref[...], preferred_element_type=F32)
    m = jax.nn.sigmoid(mg_ref[:, :d]) * ya + jax.nn.sigmoid(mg_ref[:, d:]) * yb
    h = x_ref[...] + jnp.dot(m.astype(BF16), wout_ref[...], preferred_element_type=F32)
    o_ref[...] = _rms(h, fg_ref[...]) if final else h


def _out(ga, gb, z, dst, x, woa, wob, wout, fg, final):
    m, d = x.shape
    tm = min(m, 256)
    assert m % tm == 0 and dst["mg"][0] == 0
    row = lambda i: (i, 0)
    const = lambda a: pl.BlockSpec(a.shape, lambda i: (0, 0), pipeline_mode=pl.Buffered(1))
    return pl.pallas_call(
        functools.partial(_out_kernel, d=d, final=final),
        grid=(m // tm,),
        in_specs=[pl.BlockSpec((tm, A_WIDTH), row), pl.BlockSpec((tm, B_WIDTH), row),
                  pl.BlockSpec((tm, 2 * d), row), pl.BlockSpec((tm, d), row),
                  const(woa), const(wob), const(wout), const(fg)],
        out_specs=pl.BlockSpec((tm, d), row),
        out_shape=jax.ShapeDtypeStruct((m, d), F32),
        compiler_params=pltpu.CompilerParams(dimension_semantics=("arbitrary",), vmem_limit_bytes=VMEM_LIMIT),
        name="out",
    )(ga, gb, z, x, woa, wob, wout, fg)


def _all_keys(cache, new, bsz, t, lp):
    new = new.reshape(bsz, t, new.shape[-1])
    if cache is None and lp == t:
        return new
    parts = ([] if cache is None else [cache.reshape(bsz, cache.shape[1], -1).astype(BF16)]) + [new]
    kk = jnp.concatenate(parts, axis=1)
    return jnp.pad(kk, ((0, 0), (0, lp - kk.shape[1]), (0, 0)))


def _trunk(x, caches, final_g, weights):
    bsz, t, d = x.shape
    depth = weights["w_in"].shape[0]
    past = 0 if caches is None else caches[0].shape[2]
    l_valid = past + t
    tq = min(t, QUERY_TILE)
    kw = KEY_BLOCK if t >= QUERY_TILE else -(-l_valid // LANES) * LANES
    lp = -(-l_valid // kw) * kw
    _, dst, n_proj = _layout(d)
    pos = past + jnp.arange(t, dtype=jnp.int32)
    t_scale = jnp.concatenate([jnp.full((t, IDX_HEADS), IDX_HEADS ** -0.5, F32),
                               jnp.ones((t, LANES - IDX_DIM - IDX_HEADS), F32)], axis=1)
    tab_a = jnp.stack(_rope_tables(pos, A_ROT, A_HEAD_DIM, 1))
    tab_i = jnp.stack(_rope_tables(pos, IDX_ROT, IDX_DIM, 2))
    tabs = [tab_a, jnp.stack(_rope_tables(pos, IDX_ROT, IDX_DIM, 1, extra=t_scale)),
            jnp.stack(_rope_tables(pos, ROPE_DIM, ROPE_DIM, 2))]
    reps = max(1, min(bsz * t, PROJ_ROWS) // t)
    proj_tabs = [jnp.tile(a, (1, reps, 1)) for a in (tab_a, tab_i)]
    h = x.reshape(bsz * t, d)
    new_rows = []
    for l in range(depth):
        w = {k: v[l] for k, v in weights.items()}
        z, aq, iq = _proj(h, w["norm_g"], w["w_in"], dst, n_proj, *proj_tabs)
        ak, av, ikw, krp, ckv, qcat, kb, vb, ikb, ckvb, krb = _prep(
            z, dst, bsz, t, tabs, w["q_norm_g"], w["kv_norm_g"], w["w_uq"])
        c = (None,) * 5 if caches is None else tuple(cc[l] for cc in caches)
        if caches is not None and t < QUERY_TILE:
            bias = _dsa_sel(iq, ikw, jnp.swapaxes(caches[2], 2, 3), l, ikb, bsz, t, past)
            interleaved = lambda a: a.reshape(bsz, past * A_KV_HEADS, A_HEAD_DIM)
            ga = _dsa_att(aq, z, dst, bias, interleaved(c[0]), interleaved(c[1]), kb, vb, bsz, t)
            gb = _mla_lat(qcat, z, dst, caches[3], jnp.swapaxes(caches[4], 2, 3), l, ckvb, krb, w["w_ukt"], w["w_uv"], bsz, t, past)
        else:
            k_all = _all_keys(c[0], kb, bsz, t, lp)
            v_all = _all_keys(c[1], vb, bsz, t, lp)
            ik_all = _all_keys(c[2], ikb, bsz, t, lp)
            ga = _dsa(aq, iq, ikw, z, dst, k_all, v_all, ik_all, bsz, t, tq, kw, l_valid, past)
            ckv_all = _all_keys(c[3], ckvb, bsz, t, lp)
            kr_all = _all_keys(c[4], krb, bsz, t, lp)
            kcat, vv = _kvup(ckv_all.reshape(bsz * lp, KV_LORA), kr_all.reshape(bsz * lp, ROPE_DIM), w["w_ukv"])
            tq_mla = MLA_QUERY_TILE if t % MLA_QUERY_TILE == 0 else tq
            gb = _mla(qcat, z, dst, kcat.reshape(bsz, lp, -1), vv.reshape(bsz, lp, -1), bsz, t, tq_mla, kw, l_valid, past)
        h = _out(ga, gb, z, dst, h, w["w_o_a"], w["w_o_b"], w["w_out"], final_g, final=(l == depth - 1))
        new_rows.append((ak.reshape(bsz, t, A_KV_HEADS, A_HEAD_DIM), av.reshape(bsz, t, A_KV_HEADS, A_HEAD_DIM),
                         ikw[:, :IDX_DIM].reshape(bsz, t, IDX_DIM), ckv.reshape(bsz, t, KV_LORA),
                         krp[:, :ROPE_DIM].reshape(bsz, t, ROPE_DIM)))
    stacked = [jnp.stack([r[i] for r in new_rows], axis=0) for i in range(5)]
    return h.reshape(bsz, t, d), stacked


def kernel(x_prompt, x_sample, cache_a_k, cache_a_v, cache_idx_k, cache_mla_ckv, cache_mla_krope,
           norm_g, w_in, w_uq, q_norm_g, kv_norm_g, w_uk, w_uv, w_o_a, w_o_b, w_out, final_g):
    depth, d, _ = w_in.shape
    weights = {
        "norm_g": norm_g.reshape(depth, 1, d),
        "w_in": jnp.stack([_regroup_w_in(jnp.swapaxes(w_in[l], 0, 1), d) for l in range(depth)]),
        "w_uq": jnp.concatenate([w_uq[..., :NOPE_DIM].reshape(depth, Q_LORA, B_HEADS * NOPE_DIM),
                                 w_uq[..., NOPE_DIM:].reshape(depth, Q_LORA, B_HEADS * ROPE_DIM)], axis=-1).astype(BF16),
        "q_norm_g": q_norm_g.reshape(depth, 1, Q_LORA),
        "kv_norm_g": kv_norm_g.reshape(depth, 1, KV_LORA),
        "w_ukv": jnp.concatenate([w_uk.reshape(depth, KV_LORA, B_HEADS * NOPE_DIM),
                                  w_uv.reshape(depth, KV_LORA, B_WIDTH)], axis=-1).astype(BF16),
        "w_ukt": jnp.transpose(w_uk, (0, 2, 3, 1)).astype(BF16),
        "w_uv": w_uv.reshape(depth, KV_LORA, B_WIDTH).astype(BF16),
        "w_o_a": w_o_a.astype(BF16), "w_o_b": w_o_b.astype(BF16), "w_out": w_out.astype(BF16),
    }
    fg = final_g.reshape(1, d)
    y_p, (pk, pv, pik, pckv, pkr) = _trunk(x_prompt, None, fg, weights)
    caches = (cache_a_k, cache_a_v, cache_idx_k, cache_mla_ckv, cache_mla_krope)
    y_s, (sk, sv, sik, sckv, skr) = _trunk(x_sample, caches, fg, weights)
    return (y_p, y_s, pk, pv, pik, pckv, pkr, sk, sv, sik, sckv, skr)
```

```python
import functools

import jax
import jax.numpy as jnp
from jax import lax
from jax.experimental import pallas as pl
from jax.experimental.pallas import tpu as pltpu

F32 = jnp.float32
BF16 = jnp.bfloat16

CHUNK = 64
ROPE_THETA = 500000.0
EPS = 1e-6

A_HEADS = 8
A_KV_HEADS = 2
A_HEAD_DIM = 128
A_ROT = A_HEAD_DIM // 4
A_WIDTH = A_HEADS * A_HEAD_DIM
A_KV_WIDTH = A_KV_HEADS * A_HEAD_DIM
IDX_HEADS = 16
IDX_DIM = 64
IDX_ROT = IDX_DIM // 4
TOPK_MAX = 256

B_HEADS = 8
Q_LORA = 512
KV_LORA = 512
NOPE_DIM = 128
ROPE_DIM = 64
V_DIM = 128
B_WIDTH = B_HEADS * V_DIM
B_QK_PAD = 256

LANES = 128
KEY_BLOCK = 256
QUERY_TILE = 256
PROJ_ROWS = 1024
PROJ_TILE = 1024
MLA_QUERY_TILE = 512
ROW_STRIP = 64
INT_MIN = -(2 ** 31)
LOG2_E = 1.4426950408889634
NEG = -1e30
MASKED = -1e38
VMEM_LIMIT = 56 * 1024 * 1024

_SEGS = (("aq", A_WIDTH), ("ak", A_KV_WIDTH), ("av", A_KV_WIDTH), ("agate", A_WIDTH),
         ("iq", IDX_HEADS * IDX_DIM), ("ik", IDX_DIM), ("iw", IDX_HEADS),
         ("cq", Q_LORA), ("ckv", KV_LORA), ("kr", ROPE_DIM), ("bgate", B_WIDTH), ("mg", None))


def _layout(d_model):
    src, off = {}, 0
    for name, w in _SEGS:
        w = 2 * d_model if w is None else w
        src[name] = (off, w)
        off += w
    order = (("mg", 2 * d_model), ("aq", A_WIDTH), ("agate", A_WIDTH), ("iq", IDX_HEADS * IDX_DIM),
             ("bgate", B_WIDTH), ("cq", Q_LORA), ("ckv", KV_LORA), ("ak", A_KV_WIDTH), ("av", A_KV_WIDTH),
             ("ikw", LANES), ("krp", LANES))
    dst, off = {}, 0
    for name, w in order:
        assert off % w == 0
        dst[name] = (off, w)
        off += w
    return src, dst, off


def _regroup_kernel(w_ref, o_ref, *, src, dst):
    cols = w_ref.shape[1]

    def piece(name):
        o, w = src[name]
        return w_ref[o:o + w, :].astype(BF16)

    def zeros(n):
        return jnp.zeros((n, cols), BF16)

    for name, (o, w) in dst.items():
        if name == "ikw":
            val = jnp.concatenate([piece("ik"), piece("iw"), zeros(LANES - IDX_DIM - IDX_HEADS)], axis=0)
        elif name == "krp":
            val = jnp.concatenate([piece("kr"), zeros(LANES - ROPE_DIM)], axis=0)
        else:
            val = piece(name)
        o_ref[o:o + w, :] = val
    total = max(o + w for o, w in dst.values())
    if o_ref.shape[0] > total:
        o_ref[total:, :] = zeros(o_ref.shape[0] - total)


def _regroup_w_in(w_in_t, d_model):
    src, dst, total = _layout(d_model)
    tc = 256
    padded = -(-total // PROJ_TILE) * PROJ_TILE
    assert d_model % tc == 0
    return pl.pallas_call(
        functools.partial(_regroup_kernel, src=src, dst=dst),
        grid=(d_model // tc,),
        in_specs=[pl.BlockSpec((w_in_t.shape[0], tc), lambda i: (0, i))],
        out_specs=pl.BlockSpec((padded, tc), lambda i: (0, i)),
        out_shape=jax.ShapeDtypeStruct((padded, d_model), BF16),
        compiler_params=pltpu.CompilerParams(dimension_semantics=("arbitrary",), vmem_limit_bytes=VMEM_LIMIT),
        name="regroup",
    )(w_in_t)


def _rope_tables(pos, rot, head, heads_per_vreg, extra=None):
    half = rot // 2
    inv_freq = ROPE_THETA ** (-jnp.arange(half, dtype=F32) * (2.0 / rot))
    ang = pos.astype(F32)[:, None] * inv_freq[None, :]
    cos, sin = jnp.cos(ang), jnp.sin(ang)
    t = pos.shape[0]
    one = jnp.ones((t, head - rot), F32)
    zero = lambda n: jnp.zeros((t, n), F32)
    c = jnp.concatenate([cos, cos, one], axis=1)
    sa = jnp.concatenate([-sin, zero(head - half)], axis=1)
    sb = jnp.concatenate([zero(half), sin, zero(head - rot)], axis=1)
    c, sa, sb = (jnp.tile(a, (1, heads_per_vreg)) for a in (c, sa, sb))
    if extra is not None:
        c = jnp.concatenate([c, extra], axis=1)
        sa = jnp.concatenate([sa, jnp.zeros_like(extra)], axis=1)
        sb = jnp.concatenate([sb, jnp.zeros_like(extra)], axis=1)
    assert c.shape[1] == LANES
    return c, sa, sb


def _rope(x, c, sa, sb, half):
    up = pltpu.roll(x, LANES - half, 1)
    dn = pltpu.roll(x, half, 1)
    return x * c + up * sa + dn * sb


def _rms(x, g):
    ms = jnp.mean(x * x, axis=-1, keepdims=True)
    return (x * lax.rsqrt(ms + EPS)) * g


def _lane_tile(x, width):
    return x if width == LANES else jnp.concatenate([x] * (width // LANES), axis=1)


_NT = (((1,), (1,)), ((), ()))


def _proj_kernel(x_ref, g_ref, w_ref, ta_ref, ti_ref, z_ref, aq_ref, iq_ref, xn_ref, hold_ref, *, j_aq, j_iq):
    j = pl.program_id(1)

    @pl.when(j == 0)
    def _():
        xn_ref[...] = _rms(x_ref[...], g_ref[...]).astype(BF16)

    def tile():
        return lax.dot_general(xn_ref[...], w_ref[...], _NT, preferred_element_type=F32)

    def rope_held(tab_ref, out_ref, half):
        c, sa, sb = tab_ref[0], tab_ref[1], tab_ref[2]
        for v in range(hold_ref.shape[1] // LANES):
            sl = slice(v * LANES, (v + 1) * LANES)
            out_ref[:, sl] = _rope(hold_ref[:, sl], c, sa, sb, half).astype(BF16)

    held = (j == j_aq) | (j == j_iq)
    after_aq = j == j_aq + 1
    after_iq = j == j_iq + 1

    @pl.when(held)
    def _():
        r = tile()
        z_ref[...] = r
        hold_ref[...] = r

    @pl.when(after_aq)
    def _():
        z_ref[...] = tile()
        rope_held(ta_ref, aq_ref, A_ROT // 2)

    @pl.when(after_iq)
    def _():
        z_ref[...] = tile()
        rope_held(ti_ref, iq_ref, IDX_ROT // 2)

    @pl.when(jnp.logical_not(held | after_aq | after_iq))
    def _():
        z_ref[...] = tile()


def _proj(x, g, w, dst, n, tab_a, tab_i):
    m, d = x.shape
    tm = min(m, PROJ_ROWS)
    tn = PROJ_TILE
    assert m % tm == 0 and w.shape[0] % tn == 0 and dst["aq"][1] == tn and dst["iq"][1] == tn
    j_aq, j_iq = dst["aq"][0] // tn, dst["iq"][0] // tn
    assert dst["aq"][0] % tn == 0 and dst["iq"][0] % tn == 0 and j_iq > j_aq + 1 and (j_iq + 1) * tn < n
    ntt = tab_a.shape[1] // tm
    tspec = pl.BlockSpec((3, tm, LANES), lambda i, j: (0, i % ntt, 0), pipeline_mode=pl.Buffered(1))
    return pl.pallas_call(
        functools.partial(_proj_kernel, j_aq=j_aq, j_iq=j_iq),
        grid=(m // tm, w.shape[0] // tn),
        in_specs=[pl.BlockSpec((tm, d), lambda i, j: (i, 0)),
                  pl.BlockSpec((1, d), lambda i, j: (0, 0)),
                  pl.BlockSpec((tn, d), lambda i, j: (j, 0)),
                  tspec, tspec],
        out_specs=[pl.BlockSpec((tm, tn), lambda i, j: (i, j)),
                   pl.BlockSpec((tm, tn), lambda i, j: (i, 0)),
                   pl.BlockSpec((tm, tn), lambda i, j: (i, 0))],
        out_shape=[jax.ShapeDtypeStruct((m, n), F32), jax.ShapeDtypeStruct((m, tn), BF16),
                   jax.ShapeDtypeStruct((m, tn), BF16)],
        scratch_shapes=[pltpu.VMEM((tm, d), BF16), pltpu.VMEM((tm, tn), F32)],
        compiler_params=pltpu.CompilerParams(dimension_semantics=("arbitrary", "arbitrary"),
                                             vmem_limit_bytes=VMEM_LIMIT),
        name="proj",
    )(x, g, w, tab_a, tab_i)


def _prep_kernel(cq_ref, ckv_ref, ak_ref, av_ref, ikw_ref, krp_ref,
                 ta_ref, tk_ref, tr_ref, qg_ref, kvg_ref, wuq_ref,
                 ak_o, av_o, ikw_o, krp_o, ckv_o, qcat_o, kb_o, vb_o, ikb_o, ckvb_o, krb_o):
    ca, saa, sba = ta_ref[0], ta_ref[1], ta_ref[2]
    ck, sak, sbk = tk_ref[0], tk_ref[1], tk_ref[2]
    cr, sar, sbr = tr_ref[0], tr_ref[1], tr_ref[2]
    tm = cq_ref.shape[0]
    for h in range(A_KV_HEADS):
        sl = slice(h * LANES, (h + 1) * LANES)
        ak = _rope(ak_ref[:, sl], ca, saa, sba, A_ROT // 2)
        ak_o[pl.ds(h, tm, stride=A_KV_HEADS), :] = ak
        av_o[pl.ds(h, tm, stride=A_KV_HEADS), :] = av_ref[:, sl]
        kb_o[:, sl] = ak.astype(BF16)
    vb_o[...] = av_ref[...].astype(BF16)
    ikw = _rope(ikw_ref[...], ck, sak, sbk, IDX_ROT // 2)
    ikw_o[...] = ikw
    ikb_o[...] = ikw[:, :IDX_DIM].astype(BF16)
    krp = _rope(krp_ref[...], cr, sar, sbr, ROPE_DIM // 2)
    krp_o[...] = krp
    krb_o[...] = krp[:, :ROPE_DIM].astype(BF16)
    ckv = _rms(ckv_ref[...], kvg_ref[...])
    ckv_o[...] = ckv
    ckvb_o[...] = ckv.astype(BF16)
    cqn = _rms(cq_ref[...], qg_ref[...]).astype(BF16)
    qb = jnp.dot(cqn, wuq_ref[...], preferred_element_type=F32)
    nope = B_HEADS * NOPE_DIM
    low = lax.broadcasted_iota(jnp.int32, (tm, LANES), 1) < ROPE_DIM
    for v in range(B_HEADS * ROPE_DIM // LANES):
        r = _rope(qb[:, nope + v * LANES: nope + (v + 1) * LANES], cr, sar, sbr, ROPE_DIM // 2)
        halves = (jnp.where(low, r, 0.0), jnp.where(low, pltpu.roll(r, ROPE_DIM, 1), 0.0))
        for j in range(2):
            h = 2 * v + j
            qcat_o[:, h * B_QK_PAD: h * B_QK_PAD + NOPE_DIM] = qb[:, h * NOPE_DIM:(h + 1) * NOPE_DIM].astype(BF16)
            qcat_o[:, h * B_QK_PAD + NOPE_DIM:(h + 1) * B_QK_PAD] = halves[j].astype(BF16)


def _prep(z, dst, bsz, t, tabs, q_norm_g, kv_norm_g, wuq):
    m = bsz * t
    tm = min(t, 256)
    nt = t // tm
    assert t % tm == 0

    def zspec(name):
        o, w = dst[name]
        return pl.BlockSpec((tm, w), lambda b, i, blk=o // w: (b * nt + i, blk))

    def tspec():
        return pl.BlockSpec((3, tm, LANES), lambda b, i: (0, i, 0))

    def ospec(w):
        return pl.BlockSpec((tm, w), lambda b, i: (b * nt + i, 0))

    def full(a):
        return pl.BlockSpec(a.shape, lambda b, i: (0,) * a.ndim)

    outs = [("interleaved", F32), ("interleaved", F32), (LANES, F32), (LANES, F32),
            (KV_LORA, F32), (B_HEADS * B_QK_PAD, BF16),
            (A_KV_WIDTH, BF16), (A_KV_WIDTH, BF16), (IDX_DIM, BF16), (KV_LORA, BF16), (ROPE_DIM, BF16)]
    return pl.pallas_call(
        _prep_kernel,
        grid=(bsz, nt),
        in_specs=[zspec("cq"), zspec("ckv"), zspec("ak"), zspec("av"), zspec("ikw"),
                  zspec("krp"), tspec(), tspec(), tspec(), full(q_norm_g), full(kv_norm_g), full(wuq)],
        out_specs=[pl.BlockSpec((tm * A_KV_HEADS, A_HEAD_DIM), lambda b, i: (b * nt + i, 0)) if w == "interleaved" else ospec(w)
                   for w, _ in outs],
        out_shape=[jax.ShapeDtypeStruct((m * A_KV_HEADS, A_HEAD_DIM) if w == "interleaved" else (m, w), dt) for w, dt in outs],
        compiler_params=pltpu.CompilerParams(dimension_semantics=("arbitrary", "arbitrary"),
                                             vmem_limit_bytes=VMEM_LIMIT),
        name="prep",
    )(z, z, z, z, z, z, *tabs, q_norm_g, kv_norm_g, wuq)


def _kvup_kernel(c_ref, kr_ref, w_ref, kcat_ref, v_ref):
    r = jnp.dot(c_ref[...], w_ref[...], preferred_element_type=F32)
    n = B_HEADS * NOPE_DIM
    tail = jnp.concatenate([kr_ref[...], jnp.zeros((kr_ref.shape[0], B_QK_PAD - NOPE_DIM - ROPE_DIM), BF16)], axis=1)
    for h in range(B_HEADS):
        kcat_ref[:, h * B_QK_PAD: h * B_QK_PAD + NOPE_DIM] = r[:, h * NOPE_DIM:(h + 1) * NOPE_DIM].astype(BF16)
        kcat_ref[:, h * B_QK_PAD + NOPE_DIM:(h + 1) * B_QK_PAD] = tail
    v_ref[...] = r[:, n:].astype(BF16)


def _kvup(ckv, kr, w):
    m, c = ckv.shape
    tm = next(t for t in (512, 384, 256, 128) if m % t == 0)
    row = lambda i: (i, 0)
    return pl.pallas_call(
        _kvup_kernel,
        grid=(m // tm,),
        in_specs=[pl.BlockSpec((tm, c), row), pl.BlockSpec((tm, ROPE_DIM), row), pl.BlockSpec(w.shape, lambda i: (0, 0))],
        out_specs=[pl.BlockSpec((tm, B_HEADS * B_QK_PAD), row), pl.BlockSpec((tm, B_WIDTH), row)],
        out_shape=[jax.ShapeDtypeStruct((m, B_HEADS * B_QK_PAD), BF16), jax.ShapeDtypeStruct((m, B_WIDTH), BF16)],
        compiler_params=pltpu.CompilerParams(dimension_semantics=("arbitrary",), vmem_limit_bytes=VMEM_LIMIT),
        name="kv_up",
    )(ckv, kr, w)


def _visible_cols(tile, tq, l_valid, qpos0):
    p_last = qpos0 + (tile + 1) * tq - 1
    n = (p_last // CHUNK + 1) * CHUNK
    return min(n, l_valid) if isinstance(n, int) else jnp.minimum(n, l_valid)


def _num_key_blocks(tile, tq, kw, l_valid, qpos0):
    return (_visible_cols(tile, tq, l_valid, qpos0) + kw - 1) // kw


def _fully_visible_blocks(tile, tq, kw, l_valid, qpos0):
    first_row_cols = jnp.minimum(((qpos0 + tile * tq) // CHUNK + 1) * CHUNK, l_valid)
    return first_row_cols // kw


def _allowed(tile, kb, rows0, shape, tq, kw, l_valid, qpos0):
    rows = lax.broadcasted_iota(jnp.int32, shape, 0) + rows0
    cols = lax.broadcasted_iota(jnp.int32, shape, 1) + kb * kw
    qchunk = (qpos0 + tile * tq + rows) // CHUNK
    return ((cols // CHUNK) <= qchunk) & (cols < l_valid), cols


def _softmax_step(s, scale, h, vblk, m_ref, l_ref, acc_ref):
    kw = s.shape[1]
    c = scale * LOG2_E
    m_prev = m_ref[h]
    m_new = jnp.maximum(m_prev, jnp.max(s, axis=-1, keepdims=True))
    alpha = jnp.exp2((m_prev - m_new) * c)
    p = jnp.exp2((s - _lane_tile(m_new, kw)) * c)
    l_ref[h] = alpha * l_ref[h] + sum(p[:, c * LANES:(c + 1) * LANES] for c in range(kw // LANES))
    acc_ref[h] = alpha * acc_ref[h] + jnp.dot(p.astype(BF16), vblk, preferred_element_type=F32)
    m_ref[h] = m_new


def _softmax_sum(l_partial):
    return jnp.sum(l_partial, axis=-1, keepdims=True)


def _softmax_init(heads, tq, m_ref, l_ref, acc_ref):
    for h in range(heads):
        m_ref[h] = jnp.full((tq, LANES), NEG, F32)
        l_ref[h] = jnp.zeros((tq, LANES), F32)
        acc_ref[h] = jnp.zeros((tq, acc_ref.shape[2]), F32)


def _masked_score(score, allowed):
    return jnp.where(allowed, score, MASKED)


def _key_to_f32(key):
    return lax.bitcast_convert_type(key ^ ((key >> 31) & 0x7FFFFFFF), F32)


def _selection_bias(score, cols, thr, jcut, allowed):
    sel = ((score > thr) | ((score == thr) & (cols < jcut))) & allowed
    return jnp.where(sel, 0.0, NEG)


def _topk_thresholds(strips, rs, kf, thr_ref, jcut_ref, need_ref):
    def count(pieces, pred):
        acc = jnp.zeros((rs, LANES), F32)
        for load, col0 in pieces:
            acc = acc + jnp.where(pred(load(), col0), 1.0, 0.0)
        return jnp.sum(acc, axis=-1, keepdims=True)

    def bit_step(b, carry):
        bit = jnp.left_shift(jnp.int32(1), 31 - b)
        for rows, pieces in strips:
            t = thr_ref[rows, :]
            cand = t + bit
            cand_f = _key_to_f32(cand)
            cnt = count(pieces, lambda s, c0: s >= cand_f)
            thr_ref[rows, :] = jnp.where(cnt >= kf, cand, t)
        return carry

    thr_ref[...] = jnp.full(thr_ref.shape, INT_MIN, jnp.int32)
    lax.fori_loop(0, 32, bit_step, 0)

    jcut_ref[...] = jnp.full(jcut_ref.shape, 4096, jnp.int32)
    flag = jnp.int32(0)
    for rows, pieces in strips:
        t = _key_to_f32(thr_ref[rows, :])
        excess = (count(pieces, lambda s, c0: s >= t) > kf) & (t[:, :1] > MASKED)
        flag = jnp.maximum(flag, jnp.max(jnp.where(excess, 1, 0)))

    @pl.when(flag > 0)
    def _():
        lane = lax.broadcasted_iota(jnp.int32, (rs, LANES), 1)
        for rows, pieces in strips:
            t = _key_to_f32(thr_ref[rows, :])
            need = kf - count(pieces, lambda s, c0: s > t)
            need_ref[rows, :] = jnp.broadcast_to(need, (rs, LANES))

        def col_step(b, carry):
            bit = jnp.left_shift(jnp.int32(1), 11 - b)
            for rows, pieces in strips:
                t, j = _key_to_f32(thr_ref[rows, :]), jcut_ref[rows, :]
                cand = j + bit
                f = count(pieces, lambda s, c0: (s == t) & (lane + c0 < cand))
                jcut_ref[rows, :] = jnp.where(f <= need_ref[rows, :], cand, j)
            return carry

        jcut_ref[...] = jnp.zeros(jcut_ref.shape, jnp.int32)
        lax.fori_loop(0, 12, col_step, 0)

    return flag


def _dsa_kernel(aq_ref, iq_ref, ikw_ref, agate_ref, k_ref, v_ref, ik_ref, o_ref,
                key_ref, thr_ref, jcut_ref, need_ref, ties_ref, bias_ref, iwb_ref, qst_ref, qs_ref, acc_ref, m_ref, l_ref,
                *, nt, tq, kw, l_valid, qpos0, topk):
    nkb_max = key_ref.shape[0] // nt
    nkb_of = [_num_key_blocks(ti, tq, kw, l_valid, qpos0) for ti in range(nt)]
    rs = qst_ref.shape[1] // IDX_HEADS
    ns = tq // rs
    nc = kw // LANES
    kf = float(topk)

    def score_tile(ti, carry):
        r0 = pl.multiple_of(ti * tq, tq)
        iw = ikw_ref[pl.ds(r0, tq), IDX_DIM:IDX_DIM + IDX_HEADS] * (IDX_DIM ** -0.5)
        for h in range(IDX_HEADS):
            iwb_ref[h] = jnp.broadcast_to(iw[:, h:h + 1], (tq, LANES))
            for r in range(ns):
                qst_ref[r, h * rs:(h + 1) * rs, :] = iq_ref[pl.ds(r0 + r * rs, rs), h * IDX_DIM:(h + 1) * IDX_DIM]

        def score_block(kb, c2):
            ikb = ik_ref[0, pl.ds(pl.multiple_of(kb * kw, kw), kw), :]
            for r in range(ns):
                rel = lax.dot_general(qst_ref[r], ikb, _NT, preferred_element_type=F32)
                sc = jnp.zeros((rs, kw), F32)
                for h in range(IDX_HEADS):
                    w = _lane_tile(iwb_ref[h, r * rs:(r + 1) * rs, :], kw)
                    sc = sc + jnp.maximum(rel[h * rs:(h + 1) * rs], 0.0) * w
                allowed, _ = _allowed(ti, kb, r * rs, (rs, kw), tq, kw, l_valid, qpos0)
                key_ref[ti * nkb_max + kb, r * rs:(r + 1) * rs, :] = _masked_score(sc, allowed)
            return c2

        lax.fori_loop(0, _num_key_blocks(ti, tq, kw, l_valid, qpos0), score_block, 0)
        return carry

    def piece(ti, kb, s, c):
        return lambda: key_ref[ti * nkb_max + kb, s * rs:(s + 1) * rs, c * LANES:(c + 1) * LANES]

    strips = [(slice(ti * tq + s * rs, ti * tq + (s + 1) * rs),
               [(piece(ti, kb, s, c), kb * kw + c * LANES) for kb in range(nkb_of[ti]) for c in range(nc)])
              for ti in range(nt) for s in range(ns)]

    @pl.when(pl.program_id(1) == 0)
    def _():
        lax.fori_loop(0, nt, score_tile, 0)
        ties_ref[0] = _topk_thresholds(strips, rs, kf, thr_ref, jcut_ref, need_ref)

    ti = pl.program_id(1)
    nkb = _num_key_blocks(ti, tq, kw, l_valid, qpos0)
    r0 = pl.multiple_of(ti * tq, tq)
    thr = _lane_tile(_key_to_f32(thr_ref[pl.ds(r0, tq), :]), kw)
    jcut = _lane_tile(jcut_ref[pl.ds(r0, tq), :], kw)

    def bias_block(kb, masked, ties):
        score = key_ref[ti * nkb_max + kb]
        if ties:
            allowed, cols = _allowed(ti, kb, 0, (tq, kw), tq, kw, l_valid, qpos0)
            bias_ref[kb] = _selection_bias(score, cols, thr, jcut, allowed)
        else:
            sel = score >= thr
            if masked:
                sel = sel & _allowed(ti, kb, 0, (tq, kw), tq, kw, l_valid, qpos0)[0]
            bias_ref[kb] = jnp.where(sel, 0.0, NEG)

    def bias_loop(lo, hi, masked, ties):
        lax.fori_loop(lo, hi, lambda kb, c: (bias_block(kb, masked, ties), c)[1], 0)

    n_full = _fully_visible_blocks(ti, tq, kw, l_valid, qpos0)

    @pl.when(ties_ref[0] == 0)
    def _():
        bias_loop(0, n_full, False, False)
        bias_loop(n_full, nkb, True, False)

    @pl.when(ties_ref[0] != 0)
    def _():
        bias_loop(0, nkb, True, True)

    grp = A_HEADS // A_KV_HEADS
    _softmax_init(A_KV_HEADS, grp * tq, m_ref, l_ref, acc_ref)
    scale = A_HEAD_DIM ** -0.5
    for h in range(A_HEADS):
        g, hh = divmod(h, grp)
        qs_ref[g, hh * tq:(hh + 1) * tq, :] = aq_ref[:, h * A_HEAD_DIM:(h + 1) * A_HEAD_DIM]

    def att_block(kb, carry):
        c0 = pl.multiple_of(kb * kw, kw)
        bias = bias_ref[kb][None]
        for g in range(A_KV_HEADS):
            gsl = slice(g * A_HEAD_DIM, (g + 1) * A_HEAD_DIM)
            s = lax.dot_general(qs_ref[g], k_ref[0, pl.ds(c0, kw), gsl], _NT, preferred_element_type=F32)
            s = (s.reshape(grp, tq, kw) + bias).reshape(grp * tq, kw)
            _softmax_step(s, scale, g, v_ref[0, pl.ds(c0, kw), gsl], m_ref, l_ref, acc_ref)
        return carry

    lax.fori_loop(0, nkb, att_block, 0)

    for h in range(A_HEADS):
        g, hh = divmod(h, grp)
        sl = slice(h * A_HEAD_DIM, (h + 1) * A_HEAD_DIM)
        o = acc_ref[g, hh * tq:(hh + 1) * tq, :] / _softmax_sum(l_ref[g, hh * tq:(hh + 1) * tq, :])
        gate = agate_ref[:, sl]
        o_ref[:, sl] = (o * (gate * jax.nn.sigmoid(gate))).astype(BF16)


def _dsa(aq, iq, ikw, z, dst, k, v, ik, bsz, t, tq, kw, l_valid, qpos0):
    lp = k.shape[1]
    assert lp < 4096 and lp % kw == 0 and t % tq == 0
    nt = t // tq
    nkb_max = lp // kw
    topk = min(TOPK_MAX, l_valid // 4)
    grp = A_HEADS // A_KV_HEADS
    rs = min(tq, ROW_STRIP)
    go, gw = dst["agate"]
    row = lambda b, i: (b * nt + i, 0)
    whole = lambda b, i: (b, 0)
    batch = lambda b, i: (b, 0, 0)
    kern = functools.partial(_dsa_kernel, nt=nt, tq=tq, kw=kw, l_valid=l_valid, qpos0=qpos0, topk=topk)
    return pl.pallas_call(
        kern,
        grid=(bsz, nt),
        in_specs=[pl.BlockSpec((tq, A_WIDTH), row),
                  pl.BlockSpec((t, IDX_HEADS * IDX_DIM), whole),
                  pl.BlockSpec((t, LANES), whole),
                  pl.BlockSpec((tq, gw), lambda b, i: (b * nt + i, go // gw)),
                  pl.BlockSpec((1, lp, A_KV_WIDTH), batch),
                  pl.BlockSpec((1, lp, A_KV_WIDTH), batch),
                  pl.BlockSpec((1, lp, IDX_DIM), batch)],
        out_specs=pl.BlockSpec((tq, A_WIDTH), row),
        out_shape=jax.ShapeDtypeStruct((bsz * t, A_WIDTH), BF16),
        scratch_shapes=[pltpu.VMEM((nt * nkb_max, tq, kw), F32),
                        pltpu.VMEM((t, LANES), jnp.int32),
                        pltpu.VMEM((t, LANES), jnp.int32),
                        pltpu.VMEM((t, LANES), F32),
                        pltpu.SMEM((1,), jnp.int32),
                        pltpu.VMEM((nkb_max, tq, kw), F32),
                        pltpu.VMEM((IDX_HEADS, tq, LANES), F32),
                        pltpu.VMEM((tq // rs, IDX_HEADS * rs, IDX_DIM), BF16),
                        pltpu.VMEM((A_KV_HEADS, grp * tq, A_HEAD_DIM), BF16),
                        pltpu.VMEM((A_KV_HEADS, grp * tq, A_HEAD_DIM), F32),
                        pltpu.VMEM((A_KV_HEADS, grp * tq, LANES), F32),
                        pltpu.VMEM((A_KV_HEADS, grp * tq, LANES), F32)],
        compiler_params=pltpu.CompilerParams(dimension_semantics=("arbitrary", "arbitrary"),
                                             vmem_limit_bytes=VMEM_LIMIT),
        name="dsa",
    )(aq, iq, ikw, z, k, v, ik)


SEL_BATCHES = 8
SEL_CHUNK = 512


def _dsa_sel_kernel(iq_ref, ikw_ref, cik_ref, nik_ref, bias_ref, key_ref, thr_ref, jcut_ref, need_ref,
                    *, gb, t, past, qpos0, topk):
    lp = past + LANES
    nc = lp // LANES

    def allowed_cols(col0, width):
        rows = lax.broadcasted_iota(jnp.int32, (t, width), 0)
        cols = lax.broadcasted_iota(jnp.int32, (t, width), 1) + col0
        return ((cols // CHUNK) <= (qpos0 + rows) // CHUNK) & (cols < past + t)

    def score_batch(j, carry):
        r0 = pl.multiple_of(j * t, t)
        qst = jnp.concatenate([iq_ref[pl.ds(r0, t), h * IDX_DIM:(h + 1) * IDX_DIM] for h in range(IDX_HEADS)], axis=0)
        iw = ikw_ref[pl.ds(r0, t), IDX_DIM:IDX_DIM + IDX_HEADS] * (IDX_DIM ** -0.5)
        iwb = [jnp.broadcast_to(iw[:, h:h + 1], (t, LANES)) for h in range(IDX_HEADS)]

        def head_sum(rel):
            width = rel.shape[1]
            sc = jnp.zeros((t, width), F32)
            for h in range(IDX_HEADS):
                sc = sc + jnp.maximum(rel[h * t:(h + 1) * t], 0.0) * _lane_tile(iwb[h], width)
            return sc

        for c in range(past // SEL_CHUNK):
            ikt = cik_ref[j, :, c * SEL_CHUNK:(c + 1) * SEL_CHUNK].astype(BF16)
            sc = head_sum(jnp.dot(qst, ikt, preferred_element_type=F32))
            key_ref[j, :, c * SEL_CHUNK:(c + 1) * SEL_CHUNK] = _masked_score(sc, allowed_cols(c * SEL_CHUNK, SEL_CHUNK))
        ik = jnp.concatenate([nik_ref[pl.ds(r0, t), :], jnp.zeros((LANES - t, IDX_DIM), BF16)], axis=0)
        sc = head_sum(lax.dot_general(qst, ik, _NT, preferred_element_type=F32))
        key_ref[j, :, past:] = _masked_score(sc, allowed_cols(past, LANES))
        return carry

    lax.fori_loop(0, gb, score_batch, 0)

    def piece(j, c):
        return lambda: key_ref[j, :, c * LANES:(c + 1) * LANES]

    strips = [(slice(j * t, (j + 1) * t), [(piece(j, c), c * LANES) for c in range(nc)]) for j in range(gb)]
    _topk_thresholds(strips, t, float(topk), thr_ref, jcut_ref, need_ref)

    def bias_batch(j, carry):
        r0 = pl.multiple_of(j * t, t)
        cols = lax.broadcasted_iota(jnp.int32, (t, lp), 1)
        bias_ref[j] = _selection_bias(key_ref[j], cols, _lane_tile(_key_to_f32(thr_ref[pl.ds(r0, t), :]), lp),
                                      _lane_tile(jcut_ref[pl.ds(r0, t), :], lp), allowed_cols(0, lp))
        return carry

    lax.fori_loop(0, gb, bias_batch, 0)


def _dsa_sel(iq, ikw, cache_ik_t, layer, new_ik, bsz, t, qpos0):
    past = cache_ik_t.shape[3]
    lp = past + LANES
    gb = next(g for g in (SEL_BATCHES, 4, 2, 1) if bsz % g == 0)
    assert past % SEL_CHUNK == 0 and t <= LANES and lp < 4096
    topk = min(TOPK_MAX, (past + t) // 4)
    rows = lambda i: (i, 0)
    kern = functools.partial(_dsa_sel_kernel, gb=gb, t=t, past=past, qpos0=qpos0, topk=topk)
    return pl.pallas_call(
        kern,
        grid=(bsz // gb,),
        in_specs=[pl.BlockSpec((gb * t, IDX_HEADS * IDX_DIM), rows),
                  pl.BlockSpec((gb * t, LANES), rows),
                  pl.BlockSpec((None, gb, IDX_DIM, past), lambda i: (layer, i, 0, 0)),
                  pl.BlockSpec((gb * t, IDX_DIM), rows)],
        out_specs=pl.BlockSpec((gb, t, lp), lambda i: (i, 0, 0)),
        out_shape=jax.ShapeDtypeStruct((bsz, t, lp), F32),
        scratch_shapes=[pltpu.VMEM((gb, t, lp), F32),
                        pltpu.VMEM((gb * t, LANES), jnp.int32),
                        pltpu.VMEM((gb * t, LANES), jnp.int32),
                        pltpu.VMEM((gb * t, LANES), F32)],
        compiler_params=pltpu.CompilerParams(dimension_semantics=("arbitrary",), vmem_limit_bytes=VMEM_LIMIT),
        name="dsa_sel",
    )(iq, ikw, cache_ik_t, new_ik)


def _dsa_att_kernel(aq_ref, agate_ref, bias_ref, ck_ref, cv_ref, nk_ref, nv_ref, o_ref, *, t, past):
    grp = A_HEADS // A_KV_HEADS
    scale = A_HEAD_DIM ** -0.5
    pad = jnp.zeros((LANES - t, A_HEAD_DIM), BF16)
    b1, b2 = bias_ref[0, :, :past], bias_ref[0, :, past:]
    for g in range(A_KV_HEADS):
        gsl = slice(g * A_HEAD_DIM, (g + 1) * A_HEAD_DIM)
        kc = ck_ref[0, pl.ds(g, past, stride=A_KV_HEADS), :].astype(BF16)
        vc = cv_ref[0, pl.ds(g, past, stride=A_KV_HEADS), :].astype(BF16)
        kn = jnp.concatenate([nk_ref[:, gsl], pad], axis=0)
        vn = jnp.concatenate([nv_ref[:, gsl], pad], axis=0)
        qs = jnp.concatenate([aq_ref[:, (g * grp + hh) * A_HEAD_DIM:(g * grp + hh + 1) * A_HEAD_DIM]
                              for hh in range(grp)], axis=0)

        def scores(keys, bias):
            s = lax.dot_general(qs, keys, _NT, preferred_element_type=F32) * scale
            return (s.reshape(grp, t, s.shape[1]) + bias[None]).reshape(grp * t, s.shape[1])

        s1, s2 = scores(kc, b1), scores(kn, b2)
        m = jnp.maximum(jnp.max(s1, axis=-1, keepdims=True), jnp.max(s2, axis=-1, keepdims=True))
        e1, e2 = jnp.exp(s1 - m), jnp.exp(s2 - m)
        l = jnp.sum(e1, axis=-1, keepdims=True) + jnp.sum(e2, axis=-1, keepdims=True)
        o = (jnp.dot(e1.astype(BF16), vc, preferred_element_type=F32)
             + jnp.dot(e2.astype(BF16), vn, preferred_element_type=F32)) / l
        for hh in range(grp):
            sl = slice((g * grp + hh) * A_HEAD_DIM, (g * grp + hh + 1) * A_HEAD_DIM)
            gate = agate_ref[:, sl]
            o_ref[:, sl] = (o[hh * t:(hh + 1) * t] * (gate * jax.nn.sigmoid(gate))).astype(BF16)


def _dsa_att(aq, z, dst, bias, cache_k, cache_v, new_k, new_v, bsz, t):
    past = cache_k.shape[1] // A_KV_HEADS
    lp = bias.shape[2]
    go, gw = dst["agate"]
    row = lambda b: (b, 0)
    batch = lambda b: (b, 0, 0)
    return pl.pallas_call(
        functools.partial(_dsa_att_kernel, t=t, past=past),
        grid=(bsz,),
        in_specs=[pl.BlockSpec((t, A_WIDTH), row),
                  pl.BlockSpec((t, gw), lambda b: (b, go // gw)),
                  pl.BlockSpec((1, t, lp), batch),
                  pl.BlockSpec((1, past * A_KV_HEADS, A_HEAD_DIM), batch),
                  pl.BlockSpec((1, past * A_KV_HEADS, A_HEAD_DIM), batch),
                  pl.BlockSpec((t, A_KV_WIDTH), row),
                  pl.BlockSpec((t, A_KV_WIDTH), row)],
        out_specs=pl.BlockSpec((t, A_WIDTH), row),
        out_shape=jax.ShapeDtypeStruct((bsz * t, A_WIDTH), BF16),
        compiler_params=pltpu.CompilerParams(dimension_semantics=("arbitrary",), vmem_limit_bytes=VMEM_LIMIT),
        name="dsa_att",
    )(aq, z, bias, cache_k, cache_v, new_k, new_v)


def _mla_kernel(q_ref, bgate_ref, k_ref, v_ref, o_ref, acc_ref, m_ref, l_ref, *, tq, kw, l_valid, qpos0):
    ti = pl.program_id(1)
    nkb = _num_key_blocks(ti, tq, kw, l_valid, qpos0)
    scale = (NOPE_DIM + ROPE_DIM) ** -0.5
    _softmax_init(B_HEADS, tq, m_ref, l_ref, acc_ref)

    def att_block(kb, masked):
        r0 = pl.multiple_of(kb * kw, kw)
        if masked:
            allowed, _ = _allowed(ti, kb, 0, (tq, kw), tq, kw, l_valid, qpos0)
            bias = jnp.where(allowed, 0.0, NEG)
        for h in range(B_HEADS):
            qsl = slice(h * B_QK_PAD, (h + 1) * B_QK_PAD)
            s = lax.dot_general(q_ref[:, qsl], k_ref[0, pl.ds(r0, kw), qsl], _NT, preferred_element_type=F32)
            if masked:
                s = s + bias
            _softmax_step(s, scale, h, v_ref[0, pl.ds(r0, kw), h * V_DIM:(h + 1) * V_DIM], m_ref, l_ref, acc_ref)

    n_full = _fully_visible_blocks(ti, tq, kw, l_valid, qpos0)
    lax.fori_loop(0, n_full, lambda kb, c: (att_block(kb, False), c)[1], 0)
    lax.fori_loop(n_full, nkb, lambda kb, c: (att_block(kb, True), c)[1], 0)

    for h in range(B_HEADS):
        sl = slice(h * V_DIM, (h + 1) * V_DIM)
        o = acc_ref[h] / _softmax_sum(l_ref[h])
        gate = bgate_ref[:, sl]
        o_ref[:, sl] = (o * (gate * jax.nn.sigmoid(gate))).astype(BF16)


def _mla(qcat, z, dst, kcat, v, bsz, t, tq, kw, l_valid, qpos0):
    lp = kcat.shape[1]
    assert lp % kw == 0 and t % tq == 0
    nt = t // tq
    go, gw = dst["bgate"]
    row = lambda b, i: (b * nt + i, 0)
    batch = lambda b, i: (b, 0, 0)
    kern = functools.partial(_mla_kernel, tq=tq, kw=kw, l_valid=l_valid, qpos0=qpos0)
    return pl.pallas_call(
        kern,
        grid=(bsz, nt),
        in_specs=[pl.BlockSpec((tq, B_HEADS * B_QK_PAD), row),
                  pl.BlockSpec((tq, gw), lambda b, i: (b * nt + i, go // gw)),
                  pl.BlockSpec((1, lp, B_HEADS * B_QK_PAD), batch),
                  pl.BlockSpec((1, lp, B_WIDTH), batch)],
        out_specs=pl.BlockSpec((tq, B_WIDTH), row),
        out_shape=jax.ShapeDtypeStruct((bsz * t, B_WIDTH), BF16),
        scratch_shapes=[pltpu.VMEM((B_HEADS, tq, V_DIM), F32),
                        pltpu.VMEM((B_HEADS, tq, LANES), F32),
                        pltpu.VMEM((B_HEADS, tq, LANES), F32)],
        compiler_params=pltpu.CompilerParams(dimension_semantics=("arbitrary", "arbitrary"),
                                             vmem_limit_bytes=VMEM_LIMIT),
        name="mla",
    )(qcat, z, kcat, v)


def _mla_lat_kernel(q_ref, bgate_ref, cc_ref, ckr_ref, nc_ref, nkr_ref, wukt_ref, wuv_ref, o_ref,
                    *, t, past, qpos0):
    rows_n = B_HEADS * t
    ql, qr = [], []
    for h in range(B_HEADS):
        qn = q_ref[:, h * B_QK_PAD: h * B_QK_PAD + NOPE_DIM]
        ql.append(jnp.dot(qn, wukt_ref[h], preferred_element_type=F32).astype(BF16))
        qr.append(q_ref[:, h * B_QK_PAD + NOPE_DIM: h * B_QK_PAD + NOPE_DIM + ROPE_DIM])
    ql = jnp.concatenate(ql, axis=0)
    qr = jnp.concatenate(qr, axis=0)
    cc = cc_ref[0].astype(BF16)
    ckr_t = ckr_ref[0].astype(BF16)
    nc, nkr = nc_ref[...], nkr_ref[...]
    scale = (NOPE_DIM + ROPE_DIM) ** -0.5

    def scores(lat, rope_part, ncols, col0):
        s = (lax.dot_general(ql, lat, _NT, preferred_element_type=F32) + rope_part) * scale
        rows = lax.broadcasted_iota(jnp.int32, (rows_n, ncols), 0)
        cols = lax.broadcasted_iota(jnp.int32, (rows_n, ncols), 1) + col0
        qchunk = (qpos0 + lax.rem(rows, t)) // CHUNK
        return jnp.where((cols // CHUNK) <= qchunk, s, NEG)

    s1 = scores(cc, jnp.dot(qr, ckr_t, preferred_element_type=F32), past, 0)
    s2 = scores(nc, lax.dot_general(qr, nkr, _NT, preferred_element_type=F32), t, past)
    m = jnp.maximum(jnp.max(s1, axis=-1, keepdims=True), jnp.max(s2, axis=-1, keepdims=True))
    e1, e2 = jnp.exp(s1 - m), jnp.exp(s2 - m)
    l = jnp.sum(e1, axis=-1, keepdims=True) + jnp.sum(e2, axis=-1, keepdims=True)
    ol = (jnp.dot(e1.astype(BF16), cc, preferred_element_type=F32)
          + jnp.dot(e2.astype(BF16), nc, preferred_element_type=F32)) / l
    ol = ol.astype(BF16)
    for h in range(B_HEADS):
        sl = slice(h * V_DIM, (h + 1) * V_DIM)
        o = jnp.dot(ol[h * t:(h + 1) * t], wuv_ref[:, sl], preferred_element_type=F32)
        gate = bgate_ref[:, sl]
        o_ref[:, sl] = (o * (gate * jax.nn.sigmoid(gate))).astype(BF16)


def _mla_lat(qcat, z, dst, cache_ckv, cache_kr, layer, new_ckv, new_kr, wukt, wuv, bsz, t, qpos0):
    past = cache_ckv.shape[2]
    go, gw = dst["bgate"]
    row = lambda b: (b, 0)
    batch = lambda b: (b, 0, 0)
    const = lambda a: pl.BlockSpec(a.shape, lambda b: (0,) * a.ndim)
    kern = functools.partial(_mla_lat_kernel, t=t, past=past, qpos0=qpos0)
    return pl.pallas_call(
        kern,
        grid=(bsz,),
        in_specs=[pl.BlockSpec((t, B_HEADS * B_QK_PAD), row),
                  pl.BlockSpec((t, gw), lambda b: (b, go // gw)),
                  pl.BlockSpec((None, 1, past, KV_LORA), lambda b: (layer, b, 0, 0)),
                  pl.BlockSpec((None, 1, ROPE_DIM, past), lambda b: (layer, b, 0, 0)),
                  pl.BlockSpec((t, KV_LORA), row),
                  pl.BlockSpec((t, ROPE_DIM), row),
                  const(wukt), const(wuv)],
        out_specs=pl.BlockSpec((t, B_WIDTH), row),
        out_shape=jax.ShapeDtypeStruct((bsz * t, B_WIDTH), BF16),
        compiler_params=pltpu.CompilerParams(dimension_semantics=("arbitrary",), vmem_limit_bytes=VMEM_LIMIT),
        name="mla_lat",
    )(qcat, z, cache_ckv, cache_kr, new_ckv, new_kr, wukt, wuv)


def _out_kernel(ga_ref, gb_ref, mg_ref, x_ref, woa_ref, wob_ref, wout_ref, fg_ref, o_ref, *, d, final):
    ya = jnp.dot(ga_ref[...], woa_ref[...], preferred_element_type=F32)
    yb = jnp.dot(gb_ref[...], wob_ref[...], preferred_element_type=F32)
    m = jax.nn.sigmoid(mg_ref[:, :d]) * ya + jax.nn.sigmoid(mg_ref[:, d:]) * yb
    h = x_ref[...] + jnp.dot(m.astype(BF16), wout_ref[...], preferred_element_type=F32)
    o_ref[...] = _rms(h, fg_ref[...]) if final else h


def _out(ga, gb, z, dst, x, woa, wob, wout, fg, final):
    m, d = x.shape
    tm = min(m, 256)
    assert m % tm == 0 and dst["mg"][0] == 0
    row = lambda i: (i, 0)
    const = lambda a: pl.BlockSpec(a.shape, lambda i: (0, 0), pipeline_mode=pl.Buffered(1))
    return pl.pallas_call(
        functools.partial(_out_kernel, d=d, final=final),
        grid=(m // tm,),
        in_specs=[pl.BlockSpec((tm, A_WIDTH), row), pl.BlockSpec((tm, B_WIDTH), row),
                  pl.BlockSpec((tm, 2 * d), row), pl.BlockSpec((tm, d), row),
                  const(woa), const(wob), const(wout), const(fg)],
        out_specs=pl.BlockSpec((tm, d), row),
        out_shape=jax.ShapeDtypeStruct((m, d), F32),
        compiler_params=pltpu.CompilerParams(dimension_semantics=("arbitrary",), vmem_limit_bytes=VMEM_LIMIT),
        name="out",
    )(ga, gb, z, x, woa, wob, wout, fg)


def _all_keys(cache, new, bsz, t, lp):
    new = new.reshape(bsz, t, new.shape[-1])
    if cache is None and lp == t:
        return new
    parts = ([] if cache is None else [cache.reshape(bsz, cache.shape[1], -1).astype(BF16)]) + [new]
    kk = jnp.concatenate(parts, axis=1)
    return jnp.pad(kk, ((0, 0), (0, lp - kk.shape[1]), (0, 0)))


def _trunk(x, caches, final_g, weights):
    bsz, t, d = x.shape
    depth = weights["w_in"].shape[0]
    past = 0 if caches is None else caches[0].shape[2]
    l_valid = past + t
    tq = min(t, QUERY_TILE)
    kw = KEY_BLOCK if t >= QUERY_TILE else -(-l_valid // LANES) * LANES
    lp = -(-l_valid // kw) * kw
    _, dst, n_proj = _layout(d)
    pos = past + jnp.arange(t, dtype=jnp.int32)
    t_scale = jnp.concatenate([jnp.full((t, IDX_HEADS), IDX_HEADS ** -0.5, F32),
                               jnp.ones((t, LANES - IDX_DIM - IDX_HEADS), F32)], axis=1)
    tab_a = jnp.stack(_rope_tables(pos, A_ROT, A_HEAD_DIM, 1))
    tab_i = jnp.stack(_rope_tables(pos, IDX_ROT, IDX_DIM, 2))
    tabs = [tab_a, jnp.stack(_rope_tables(pos, IDX_ROT, IDX_DIM, 1, extra=t_scale)),
            jnp.stack(_rope_tables(pos, ROPE_DIM, ROPE_DIM, 2))]
    reps = max(1, min(bsz * t, PROJ_ROWS) // t)
    proj_tabs = [jnp.tile(a, (1, reps, 1)) for a in (tab_a, tab_i)]
    h = x.reshape(bsz * t, d)
    new_rows = []
    for l in range(depth):
        w = {k: v[l] for k, v in weights.items()}
        z, aq, iq = _proj(h, w["norm_g"], w["w_in"], dst, n_proj, *proj_tabs)
        ak, av, ikw, krp, ckv, qcat, kb, vb, ikb, ckvb, krb = _prep(
            z, dst, bsz, t, tabs, w["q_norm_g"], w["kv_norm_g"], w["w_uq"])
        c = (None,) * 5 if caches is None else tuple(cc[l] for cc in caches)
        if caches is not None and t < QUERY_TILE:
            bias = _dsa_sel(iq, ikw, jnp.swapaxes(caches[2], 2, 3), l, ikb, bsz, t, past)
            interleaved = lambda a: a.reshape(bsz, past * A_KV_HEADS, A_HEAD_DIM)
            ga = _dsa_att(aq, z, dst, bias, interleaved(c[0]), interleaved(c[1]), kb, vb, bsz, t)
            gb = _mla_lat(qcat, z, dst, caches[3], jnp.swapaxes(caches[4], 2, 3), l, ckvb, krb, w["w_ukt"], w["w_uv"], bsz, t, past)
        else:
            k_all = _all_keys(c[0], kb, bsz, t, lp)
            v_all = _all_keys(c[1], vb, bsz, t, lp)
            ik_all = _all_keys(c[2], ikb, bsz, t, lp)
            ga = _dsa(aq, iq, ikw, z, dst, k_all, v_all, ik_all, bsz, t, tq, kw, l_valid, past)
            ckv_all = _all_keys(c[3], ckvb, bsz, t, lp)
            kr_all = _all_keys(c[4], krb, bsz, t, lp)
            kcat, vv = _kvup(ckv_all.reshape(bsz * lp, KV_LORA), kr_all.reshape(bsz * lp, ROPE_DIM), w["w_ukv"])
            tq_mla = MLA_QUERY_TILE if t % MLA_QUERY_TILE == 0 else tq
            gb = _mla(qcat, z, dst, kcat.reshape(bsz, lp, -1), vv.reshape(bsz, lp, -1), bsz, t, tq_mla, kw, l_valid, past)
        h = _out(ga, gb, z, dst, h, w["w_o_a"], w["w_o_b"], w["w_out"], final_g, final=(l == depth - 1))
        new_rows.append((ak.reshape(bsz, t, A_KV_HEADS, A_HEAD_DIM), av.reshape(bsz, t, A_KV_HEADS, A_HEAD_DIM),
                         ikw[:, :IDX_DIM].reshape(bsz, t, IDX_DIM), ckv.reshape(bsz, t, KV_LORA),
                         krp[:, :ROPE_DIM].reshape(bsz, t, ROPE_DIM)))
    stacked = [jnp.stack([r[i] for r in new_rows], axis=0) for i in range(5)]
    return h.reshape(bsz, t, d), stacked


def kernel(x_prompt, x_sample, cache_a_k, cache_a_v, cache_idx_k, cache_mla_ckv, cache_mla_krope,
           norm_g, w_in, w_uq, q_norm_g, kv_norm_g, w_uk, w_uv, w_o_a, w_o_b, w_out, final_g):
    depth, d, _ = w_in.shape
    weights = {
        "norm_g": norm_g.reshape(depth, 1, d),
        "w_in": jnp.stack([_regroup_w_in(jnp.swapaxes(w_in[l], 0, 1), d) for l in range(depth)]),
        "w_uq": jnp.concatenate([w_uq[..., :NOPE_DIM].reshape(depth, Q_LORA, B_HEADS * NOPE_DIM),
                                 w_uq[..., NOPE_DIM:].reshape(depth, Q_LORA, B_HEADS * ROPE_DIM)], axis=-1).astype(BF16),
        "q_norm_g": q_norm_g.reshape(depth, 1, Q_LORA),
        "kv_norm_g": kv_norm_g.reshape(depth, 1, KV_LORA),
        "w_ukv": jnp.concatenate([w_uk.reshape(depth, KV_LORA, B_HEADS * NOPE_DIM),
                                  w_uv.reshape(depth, KV_LORA, B_WIDTH)], axis=-1).astype(BF16),
        "w_ukt": jnp.transpose(w_uk, (0, 2, 3, 1)).astype(BF16),
        "w_uv": w_uv.reshape(depth, KV_LORA, B_WIDTH).astype(BF16),
        "w_o_a": w_o_a.astype(BF16), "w_o_b": w_o_b.astype(BF16), "w_out": w_out.astype(BF16),
    }
    fg = final_g.reshape(1, d)
    y_p, (pk, pv, pik, pckv, pkr) = _trunk(x_prompt, None, fg, weights)
    caches = (cache_a_k, cache_a_v, cache_idx_k, cache_mla_ckv, cache_mla_krope)
    y_s, (sk, sv, sik, sckv, skr) = _trunk(x_sample, caches, fg, weights)
    return (y_p, y_s, pk, pv, pik, pckv, pkr, sk, sv, sik, sckv, skr)
```

```python
import functools

import jax
import jax.numpy as jnp
from jax import lax
from jax.experimental import pallas as pl
from jax.experimental.pallas import tpu as pltpu

F32 = jnp.float32
BF16 = jnp.bfloat16

CHUNK = 64
ROPE_THETA = 500000.0
EPS = 1e-6

A_HEADS = 8
A_KV_HEADS = 2
A_HEAD_DIM = 128
A_ROT = A_HEAD_DIM // 4
A_WIDTH = A_HEADS * A_HEAD_DIM
A_KV_WIDTH = A_KV_HEADS * A_HEAD_DIM
IDX_HEADS = 16
IDX_DIM = 64
IDX_ROT = IDX_DIM // 4
TOPK_MAX = 256

B_HEADS = 8
Q_LORA = 512
KV_LORA = 512
NOPE_DIM = 128
ROPE_DIM = 64
V_DIM = 128
B_WIDTH = B_HEADS * V_DIM
B_QK_PAD = 256

LANES = 128
KEY_BLOCK = 256
QUERY_TILE = 256
PROJ_ROWS = 1024
PROJ_TILE = 1024
MLA_QUERY_TILE = 512
ROW_STRIP = 64
INT_MIN = -(2 ** 31)
LOG2_E = 1.4426950408889634
NEG = -1e30
MASKED = -1e38
VMEM_LIMIT = 56 * 1024 * 1024

_SEGS = (("aq", A_WIDTH), ("ak", A_KV_WIDTH), ("av", A_KV_WIDTH), ("agate", A_WIDTH),
         ("iq", IDX_HEADS * IDX_DIM), ("ik", IDX_DIM), ("iw", IDX_HEADS),
         ("cq", Q_LORA), ("ckv", KV_LORA), ("kr", ROPE_DIM), ("bgate", B_WIDTH), ("mg", None))


def _layout(d_model):
    src, off = {}, 0
    for name, w in _SEGS:
        w = 2 * d_model if w is None else w
        src[name] = (off, w)
        off += w
    order = (("mg", 2 * d_model), ("aq", A_WIDTH), ("agate", A_WIDTH), ("iq", IDX_HEADS * IDX_DIM),
             ("bgate", B_WIDTH), ("cq", Q_LORA), ("ckv", KV_LORA), ("ak", A_KV_WIDTH), ("av", A_KV_WIDTH),
             ("ikw", LANES), ("krp", LANES))
    dst, off = {}, 0
    for name, w in order:
        assert off % w == 0
        dst[name] = (off, w)
        off += w
    return src, dst, off


def _regroup_kernel(w_ref, o_ref, *, src, dst):
    cols = w_ref.shape[1]

    def piece(name):
        o, w = src[name]
        return w_ref[o:o + w, :].astype(BF16)

    def zeros(n):
        return jnp.zeros((n, cols), BF16)

    for name, (o, w) in dst.items():
        if name == "ikw":
            val = jnp.concatenate([piece("ik"), piece("iw"), zeros(LANES - IDX_DIM - IDX_HEADS)], axis=0)
        elif name == "krp":
            val = jnp.concatenate([piece("kr"), zeros(LANES - ROPE_DIM)], axis=0)
        else:
            val = piece(name)
        o_ref[o:o + w, :] = val
    total = max(o + w for o, w in dst.values())
    if o_ref.shape[0] > total:
        o_ref[total:, :] = zeros(o_ref.shape[0] - total)


def _regroup_w_in(w_in_t, d_model):
    src, dst, total = _layout(d_model)
    tc = 256
    padded = -(-total // PROJ_TILE) * PROJ_TILE
    assert d_model % tc == 0
    return pl.pallas_call(
        functools.partial(_regroup_kernel, src=src, dst=dst),
        grid=(d_model // tc,),
        in_specs=[pl.BlockSpec((w_in_t.shape[0], tc), lambda i: (0, i))],
        out_specs=pl.BlockSpec((padded, tc), lambda i: (0, i)),
        out_shape=jax.ShapeDtypeStruct((padded, d_model), BF16),
        compiler_params=pltpu.CompilerParams(dimension_semantics=("arbitrary",), vmem_limit_bytes=VMEM_LIMIT),
        name="regroup",
    )(w_in_t)


def _rope_tables(pos, rot, head, heads_per_vreg, extra=None):
    half = rot // 2
    inv_freq = ROPE_THETA ** (-jnp.arange(half, dtype=F32) * (2.0 / rot))
    ang = pos.astype(F32)[:, None] * inv_freq[None, :]
    cos, sin = jnp.cos(ang), jnp.sin(ang)
    t = pos.shape[0]
    one = jnp.ones((t, head - rot), F32)
    zero = lambda n: jnp.zeros((t, n), F32)
    c = jnp.concatenate([cos, cos, one], axis=1)
    sa = jnp.concatenate([-sin, zero(head - half)], axis=1)
    sb = jnp.concatenate([zero(half), sin, zero(head - rot)], axis=1)
    c, sa, sb = (jnp.tile(a, (1, heads_per_vreg)) for a in (c, sa, sb))
    if extra is not None:
        c = jnp.concatenate([c, extra], axis=1)
        sa = jnp.concatenate([sa, jnp.zeros_like(extra)], axis=1)
        sb = jnp.concatenate([sb, jnp.zeros_like(extra)], axis=1)
    assert c.shape[1] == LANES
    return c, sa, sb


def _rope(x, c, sa, sb, half):
    up = pltpu.roll(x, LANES - half, 1)
    dn = pltpu.roll(x, half, 1)
    return x * c + up * sa + dn * sb


def _rms(x, g):
    ms = jnp.mean(x * x, axis=-1, keepdims=True)
    return (x * lax.rsqrt(ms + EPS)) * g


def _lane_tile(x, width):
    return x if width == LANES else jnp.concatenate([x] * (width // LANES), axis=1)


_NT = (((1,), (1,)), ((), ()))


def _proj_kernel(x_ref, g_ref, w_ref, ta_ref, ti_ref, z_ref, aq_ref, iq_ref, xn_ref, hold_ref, *, j_aq, j_iq):
    j = pl.program_id(1)

    @pl.when(j == 0)
    def _():
        xn_ref[...] = _rms(x_ref[...], g_ref[...]).astype(BF16)

    def tile():
        return lax.dot_general(xn_ref[...], w_ref[...], _NT, preferred_element_type=F32)

    def rope_held(tab_ref, out_ref, half):
        c, sa, sb = tab_ref[0], tab_ref[1], tab_ref[2]
        for v in range(hold_ref.shape[1] // LANES):
            sl = slice(v * LANES, (v + 1) * LANES)
            out_ref[:, sl] = _rope(hold_ref[:, sl], c, sa, sb, half).astype(BF16)

    held = (j == j_aq) | (j == j_iq)
    after_aq = j == j_aq + 1
    after_iq = j == j_iq + 1

    @pl.when(held)
    def _():
        r = tile()
        z_ref[...] = r
        hold_ref[...] = r

    @pl.when(after_aq)
    def _():
        z_ref[...] = tile()
        rope_held(ta_ref, aq_ref, A_ROT // 2)

    @pl.when(after_iq)
    def _():
        z_ref[...] = tile()
        rope_held(ti_ref, iq_ref, IDX_ROT // 2)

    @pl.when(jnp.logical_not(held | after_aq | after_iq))
    def _():
        z_ref[...] = tile()


def _proj(x, g, w, dst, n, tab_a, tab_i):
    m, d = x.shape
    tm = min(m, PROJ_ROWS)
    tn = PROJ_TILE
    assert m % tm == 0 and w.shape[0] % tn == 0 and dst["aq"][1] == tn and dst["iq"][1] == tn
    j_aq, j_iq = dst["aq"][0] // tn, dst["iq"][0] // tn
    assert dst["aq"][0] % tn == 0 and dst["iq"][0] % tn == 0 and j_iq > j_aq + 1 and (j_iq + 1) * tn < n
    ntt = tab_a.shape[1] // tm
    tspec = pl.BlockSpec((3, tm, LANES), lambda i, j: (0, i % ntt, 0), pipeline_mode=pl.Buffered(1))
    return pl.pallas_call(
        functools.partial(_proj_kernel, j_aq=j_aq, j_iq=j_iq),
        grid=(m // tm, w.shape[0] // tn),
        in_specs=[pl.BlockSpec((tm, d), lambda i, j: (i, 0)),
                  pl.BlockSpec((1, d), lambda i, j: (0, 0)),
                  pl.BlockSpec((tn, d), lambda i, j: (j, 0)),
                  tspec, tspec],
        out_specs=[pl.BlockSpec((tm, tn), lambda i, j: (i, j)),
                   pl.BlockSpec((tm, tn), lambda i, j: (i, 0)),
                   pl.BlockSpec((tm, tn), lambda i, j: (i, 0))],
        out_shape=[jax.ShapeDtypeStruct((m, n), F32), jax.ShapeDtypeStruct((m, tn), BF16),
                   jax.ShapeDtypeStruct((m, tn), BF16)],
        scratch_shapes=[pltpu.VMEM((tm, d), BF16), pltpu.VMEM((tm, tn), F32)],
        compiler_params=pltpu.CompilerParams(dimension_semantics=("arbitrary", "arbitrary"),
                                             vmem_limit_bytes=VMEM_LIMIT),
        name="proj",
    )(x, g, w, tab_a, tab_i)


def _prep_kernel(cq_ref, ckv_ref, ak_ref, av_ref, ikw_ref, krp_ref,
                 ta_ref, tk_ref, tr_ref, qg_ref, kvg_ref, wuq_ref,
                 ak_o, av_o, ikw_o, krp_o, ckv_o, qcat_o, kb_o, vb_o, ikb_o, ckvb_o, krb_o):
    ca, saa, sba = ta_ref[0], ta_ref[1], ta_ref[2]
    ck, sak, sbk = tk_ref[0], tk_ref[1], tk_ref[2]
    cr, sar, sbr = tr_ref[0], tr_ref[1], tr_ref[2]
    tm = cq_ref.shape[0]
    for h in range(A_KV_HEADS):
        sl = slice(h * LANES, (h + 1) * LANES)
        ak = _rope(ak_ref[:, sl], ca, saa, sba, A_ROT // 2)
        ak_o[pl.ds(h, tm, stride=A_KV_HEADS), :] = ak
        av_o[pl.ds(h, tm, stride=A_KV_HEADS), :] = av_ref[:, sl]
        kb_o[:, sl] = ak.astype(BF16)
    vb_o[...] = av_ref[...].astype(BF16)
    ikw = _rope(ikw_ref[...], ck, sak, sbk, IDX_ROT // 2)
    ikw_o[...] = ikw
    ikb_o[...] = ikw[:, :IDX_DIM].astype(BF16)
    krp = _rope(krp_ref[...], cr, sar, sbr, ROPE_DIM // 2)
    krp_o[...] = krp
    krb_o[...] = krp[:, :ROPE_DIM].astype(BF16)
    ckv = _rms(ckv_ref[...], kvg_ref[...])
    ckv_o[...] = ckv
    ckvb_o[...] = ckv.astype(BF16)
    cqn = _rms(cq_ref[...], qg_ref[...]).astype(BF16)
    qb = jnp.dot(cqn, wuq_ref[...], preferred_element_type=F32)
    nope = B_HEADS * NOPE_DIM
    low = lax.broadcasted_iota(jnp.int32, (tm, LANES), 1) < ROPE_DIM
    for v in range(B_HEADS * ROPE_DIM // LANES):
        r = _rope(qb[:, nope + v * LANES: nope + (v + 1) * LANES], cr, sar, sbr, ROPE_DIM // 2)
        halves = (jnp.where(low, r, 0.0), jnp.where(low, pltpu.roll(r, ROPE_DIM, 1), 0.0))
        for j in range(2):
            h = 2 * v + j
            qcat_o[:, h * B_QK_PAD: h * B_QK_PAD + NOPE_DIM] = qb[:, h * NOPE_DIM:(h + 1) * NOPE_DIM].astype(BF16)
            qcat_o[:, h * B_QK_PAD + NOPE_DIM:(h + 1) * B_QK_PAD] = halves[j].astype(BF16)


def _prep(z, dst, bsz, t, tabs, q_norm_g, kv_norm_g, wuq):
    m = bsz * t
    tm = min(t, 256)
    nt = t // tm
    assert t % tm == 0

    def zspec(name):
        o, w = dst[name]
        return pl.BlockSpec((tm, w), lambda b, i, blk=o // w: (b * nt + i, blk))

    def tspec():
        return pl.BlockSpec((3, tm, LANES), lambda b, i: (0, i, 0))

    def ospec(w):
        return pl.BlockSpec((tm, w), lambda b, i: (b * nt + i, 0))

    def full(a):
        return pl.BlockSpec(a.shape, lambda b, i: (0,) * a.ndim)

    outs = [("interleaved", F32), ("interleaved", F32), (LANES, F32), (LANES, F32),
            (KV_LORA, F32), (B_HEADS * B_QK_PAD, BF16),
            (A_KV_WIDTH, BF16), (A_KV_WIDTH, BF16), (IDX_DIM, BF16), (KV_LORA, BF16), (ROPE_DIM, BF16)]
    return pl.pallas_call(
        _prep_kernel,
        grid=(bsz, nt),
        in_specs=[zspec("cq"), zspec("ckv"), zspec("ak"), zspec("av"), zspec("ikw"),
                  zspec("krp"), tspec(), tspec(), tspec(), full(q_norm_g), full(kv_norm_g), full(wuq)],
        out_specs=[pl.BlockSpec((tm * A_KV_HEADS, A_HEAD_DIM), lambda b, i: (b * nt + i, 0)) if w == "interleaved" else ospec(w)
                   for w, _ in outs],
        out_shape=[jax.ShapeDtypeStruct((m * A_KV_HEADS, A_HEAD_DIM) if w == "interleaved" else (m, w), dt) for w, dt in outs],
        compiler_params=pltpu.CompilerParams(dimension_semantics=("arbitrary", "arbitrary"),
                                             vmem_limit_bytes=VMEM_LIMIT),
        name="prep",
    )(z, z, z, z, z, z, *tabs, q_norm_g, kv_norm_g, wuq)


def _kvup_kernel(c_ref, kr_ref, w_ref, kcat_ref, v_ref):
    r = jnp.dot(c_ref[...], w_ref[...], preferred_element_type=F32)
    n = B_HEADS * NOPE_DIM
    tail = jnp.concatenate([kr_ref[...], jnp.zeros((kr_ref.shape[0], B_QK_PAD - NOPE_DIM - ROPE_DIM), BF16)], axis=1)
    for h in range(B_HEADS):
        kcat_ref[:, h * B_QK_PAD: h * B_QK_PAD + NOPE_DIM] = r[:, h * NOPE_DIM:(h + 1) * NOPE_DIM].astype(BF16)
        kcat_ref[:, h * B_QK_PAD + NOPE_DIM:(h + 1) * B_QK_PAD] = tail
    v_ref[...] = r[:, n:].astype(BF16)


def _kvup(ckv, kr, w):
    m, c = ckv.shape
    tm = next(t for t in (512, 384, 256, 128) if m % t == 0)
    row = lambda i: (i, 0)
    return pl.pallas_call(
        _kvup_kernel,
        grid=(m // tm,),
        in_specs=[pl.BlockSpec((tm, c), row), pl.BlockSpec((tm, ROPE_DIM), row), pl.BlockSpec(w.shape, lambda i: (0, 0))],
        out_specs=[pl.BlockSpec((tm, B_HEADS * B_QK_PAD), row), pl.BlockSpec((tm, B_WIDTH), row)],
        out_shape=[jax.ShapeDtypeStruct((m, B_HEADS * B_QK_PAD), BF16), jax.ShapeDtypeStruct((m, B_WIDTH), BF16)],
        compiler_params=pltpu.CompilerParams(dimension_semantics=("arbitrary",), vmem_limit_bytes=VMEM_LIMIT),
        name="kv_up",
    )(ckv, kr, w)


def _visible_cols(tile, tq, l_valid, qpos0):
    p_last = qpos0 + (tile + 1) * tq - 1
    n = (p_last // CHUNK + 1) * CHUNK
    return min(n, l_valid) if isinstance(n, int) else jnp.minimum(n, l_valid)


def _num_key_blocks(tile, tq, kw, l_valid, qpos0):
    return (_visible_cols(tile, tq, l_valid, qpos0) + kw - 1) // kw


def _fully_visible_blocks(tile, tq, kw, l_valid, qpos0):
    first_row_cols = jnp.minimum(((qpos0 + tile * tq) // CHUNK + 1) * CHUNK, l_valid)
    return first_row_cols // kw


def _allowed(tile, kb, rows0, shape, tq, kw, l_valid, qpos0):
    rows = lax.broadcasted_iota(jnp.int32, shape, 0) + rows0
    cols = lax.broadcasted_iota(jnp.int32, shape, 1) + kb * kw
    qchunk = (qpos0 + tile * tq + rows) // CHUNK
    return ((cols // CHUNK) <= qchunk) & (cols < l_valid), cols


def _softmax_step(s, scale, h, vblk, m_ref, l_ref, acc_ref, rows=slice(None)):
    kw = s.shape[1]
    c = scale * LOG2_E
    m_prev = m_ref[h, rows]
    m_new = jnp.maximum(m_prev, jnp.max(s, axis=-1, keepdims=True))
    alpha = jnp.exp2((m_prev - m_new) * c)
    p = jnp.exp2((s - _lane_tile(m_new, kw)) * c)
    l_ref[h, rows] = alpha * l_ref[h, rows] + sum(p[:, c * LANES:(c + 1) * LANES] for c in range(kw // LANES))
    acc_ref[h, rows] = alpha * acc_ref[h, rows] + jnp.dot(p.astype(BF16), vblk, preferred_element_type=F32)
    m_ref[h, rows] = m_new


def _softmax_sum(l_partial):
    return jnp.sum(l_partial, axis=-1, keepdims=True)


def _softmax_init(heads, tq, m_ref, l_ref, acc_ref):
    for h in range(heads):
        m_ref[h] = jnp.full((tq, LANES), NEG, F32)
        l_ref[h] = jnp.zeros((tq, LANES), F32)
        acc_ref[h] = jnp.zeros((tq, acc_ref.shape[2]), F32)


def _masked_score(score, allowed):
    return jnp.where(allowed, score, MASKED)


def _key_to_f32(key):
    return lax.bitcast_convert_type(key ^ ((key >> 31) & 0x7FFFFFFF), F32)


def _selection_bias(score, cols, thr, jcut, allowed):
    sel = ((score > thr) | ((score == thr) & (cols < jcut))) & allowed
    return jnp.where(sel, 0.0, NEG)


def _topk_thresholds(strips, rs, kf, thr_ref, jcut_ref, need_ref):
    def count(pieces, pred):
        acc = jnp.zeros((rs, LANES), F32)
        for load, col0 in pieces:
            acc = acc + jnp.where(pred(load(), col0), 1.0, 0.0)
        return jnp.sum(acc, axis=-1, keepdims=True)

    def bit_step(b, carry):
        bit = jnp.left_shift(jnp.int32(1), 31 - b)
        for rows, pieces in strips:
            t = thr_ref[rows, :]
            cand = t + bit
            cand_f = _key_to_f32(cand)
            cnt = count(pieces, lambda s, c0: s >= cand_f)
            thr_ref[rows, :] = jnp.where(cnt >= kf, cand, t)
        return carry

    thr_ref[...] = jnp.full(thr_ref.shape, INT_MIN, jnp.int32)
    lax.fori_loop(0, 32, bit_step, 0)

    jcut_ref[...] = jnp.full(jcut_ref.shape, 4096, jnp.int32)
    flag = jnp.int32(0)
    for rows, pieces in strips:
        t = _key_to_f32(thr_ref[rows, :])
        excess = (count(pieces, lambda s, c0: s >= t) > kf) & (t[:, :1] > MASKED)
        flag = jnp.maximum(flag, jnp.max(jnp.where(excess, 1, 0)))

    @pl.when(flag > 0)
    def _():
        lane = lax.broadcasted_iota(jnp.int32, (rs, LANES), 1)
        for rows, pieces in strips:
            t = _key_to_f32(thr_ref[rows, :])
            need = kf - count(pieces, lambda s, c0: s > t)
            need_ref[rows, :] = jnp.broadcast_to(need, (rs, LANES))

        def col_step(b, carry):
            bit = jnp.left_shift(jnp.int32(1), 11 - b)
            for rows, pieces in strips:
                t, j = _key_to_f32(thr_ref[rows, :]), jcut_ref[rows, :]
                cand = j + bit
                f = count(pieces, lambda s, c0: (s == t) & (lane + c0 < cand))
                jcut_ref[rows, :] = jnp.where(f <= need_ref[rows, :], cand, j)
            return carry

        jcut_ref[...] = jnp.zeros(jcut_ref.shape, jnp.int32)
        lax.fori_loop(0, 12, col_step, 0)

    return flag


def _dsa_kernel(aq_ref, iq_ref, ikw_ref, agate_ref, k_ref, v_ref, ik_ref, o_ref,
                key_ref, thr_ref, jcut_ref, need_ref, ties_ref, bias_ref, iwb_ref, qst_ref, qs_ref, acc_ref, m_ref, l_ref,
                *, nt, tq, kw, l_valid, qpos0, topk):
    nkb_max = key_ref.shape[0] // nt
    nkb_of = [_num_key_blocks(ti, tq, kw, l_valid, qpos0) for ti in range(nt)]
    rs = qst_ref.shape[1] // IDX_HEADS
    ns = tq // rs
    nc = kw // LANES
    kf = float(topk)

    def score_tile(ti, carry):
        r0 = pl.multiple_of(ti * tq, tq)
        iw = ikw_ref[pl.ds(r0, tq), IDX_DIM:IDX_DIM + IDX_HEADS] * (IDX_DIM ** -0.5)
        for h in range(IDX_HEADS):
            iwb_ref[h] = jnp.broadcast_to(iw[:, h:h + 1], (tq, LANES))
            for r in range(ns):
                qst_ref[r, h * rs:(h + 1) * rs, :] = iq_ref[pl.ds(r0 + r * rs, rs), h * IDX_DIM:(h + 1) * IDX_DIM]

        def score_block(kb, c2):
            ikb = ik_ref[0, pl.ds(pl.multiple_of(kb * kw, kw), kw), :]
            for r in range(ns):
                rel = lax.dot_general(qst_ref[r], ikb, _NT, preferred_element_type=F32)
                sc = jnp.zeros((rs, kw), F32)
                for h in range(IDX_HEADS):
                    w = _lane_tile(iwb_ref[h, r * rs:(r + 1) * rs, :], kw)
                    sc = sc + jnp.maximum(rel[h * rs:(h + 1) * rs], 0.0) * w
                allowed, _ = _allowed(ti, kb, r * rs, (rs, kw), tq, kw, l_valid, qpos0)
                key_ref[ti * nkb_max + kb, r * rs:(r + 1) * rs, :] = _masked_score(sc, allowed)
            return c2

        lax.fori_loop(0, _num_key_blocks(ti, tq, kw, l_valid, qpos0), score_block, 0)
        return carry

    def piece(ti, kb, s, c):
        return lambda: key_ref[ti * nkb_max + kb, s * rs:(s + 1) * rs, c * LANES:(c + 1) * LANES]

    strips = [(slice(ti * tq + s * rs, ti * tq + (s + 1) * rs),
               [(piece(ti, kb, s, c), kb * kw + c * LANES) for kb in range(nkb_of[ti]) for c in range(nc)])
              for ti in range(nt) for s in range(ns)]

    @pl.when(pl.program_id(1) == 0)
    def _():
        lax.fori_loop(0, nt, score_tile, 0)
        ties_ref[0] = _topk_thresholds(strips, rs, kf, thr_ref, jcut_ref, need_ref)

    ti = pl.program_id(1)
    nkb = _num_key_blocks(ti, tq, kw, l_valid, qpos0)
    r0 = pl.multiple_of(ti * tq, tq)
    thr = _lane_tile(_key_to_f32(thr_ref[pl.ds(r0, tq), :]), kw)
    jcut = _lane_tile(jcut_ref[pl.ds(r0, tq), :], kw)

    def bias_block(kb, masked, ties):
        score = key_ref[ti * nkb_max + kb]
        if ties:
            allowed, cols = _allowed(ti, kb, 0, (tq, kw), tq, kw, l_valid, qpos0)
            bias_ref[kb] = _selection_bias(score, cols, thr, jcut, allowed)
        else:
            sel = score >= thr
            if masked:
                sel = sel & _allowed(ti, kb, 0, (tq, kw), tq, kw, l_valid, qpos0)[0]
            bias_ref[kb] = jnp.where(sel, 0.0, NEG)

    def bias_loop(lo, hi, masked, ties):
        lax.fori_loop(lo, hi, lambda kb, c: (bias_block(kb, masked, ties), c)[1], 0)

    n_full = _fully_visible_blocks(ti, tq, kw, l_valid, qpos0)

    @pl.when(ties_ref[0] == 0)
    def _():
        bias_loop(0, n_full, False, False)
        bias_loop(n_full, nkb, True, False)

    @pl.when(ties_ref[0] != 0)
    def _():
        bias_loop(0, nkb, True, True)

    grp = A_HEADS // A_KV_HEADS
    _softmax_init(A_KV_HEADS, grp * tq, m_ref, l_ref, acc_ref)
    scale = A_HEAD_DIM ** -0.5
    for h in range(A_HEADS):
        g, hh = divmod(h, grp)
        qs_ref[g, hh * tq:(hh + 1) * tq, :] = aq_ref[:, h * A_HEAD_DIM:(h + 1) * A_HEAD_DIM]

    def att_block(kb, carry):
        c0 = pl.multiple_of(kb * kw, kw)
        bias = bias_ref[kb][None]
        for g in range(A_KV_HEADS):
            gsl = slice(g * A_HEAD_DIM, (g + 1) * A_HEAD_DIM)
            s = lax.dot_general(qs_ref[g], k_ref[0, pl.ds(c0, kw), gsl], _NT, preferred_element_type=F32)
            s = (s.reshape(grp, tq, kw) + bias).reshape(grp * tq, kw)
            _softmax_step(s, scale, g, v_ref[0, pl.ds(c0, kw), gsl], m_ref, l_ref, acc_ref)
        return carry

    lax.fori_loop(0, nkb, att_block, 0)

    for h in range(A_HEADS):
        g, hh = divmod(h, grp)
        sl = slice(h * A_HEAD_DIM, (h + 1) * A_HEAD_DIM)
        o = acc_ref[g, hh * tq:(hh + 1) * tq, :] / _softmax_sum(l_ref[g, hh * tq:(hh + 1) * tq, :])
        gate = agate_ref[:, sl]
        o_ref[:, sl] = (o * (gate * jax.nn.sigmoid(gate))).astype(BF16)


def _dsa(aq, iq, ikw, z, dst, k, v, ik, bsz, t, tq, kw, l_valid, qpos0):
    lp = k.shape[1]
    assert lp < 4096 and lp % kw == 0 and t % tq == 0
    nt = t // tq
    nkb_max = lp // kw
    topk = min(TOPK_MAX, l_valid // 4)
    grp = A_HEADS // A_KV_HEADS
    rs = min(tq, ROW_STRIP)
    go, gw = dst["agate"]
    row = lambda b, i: (b * nt + i, 0)
    whole = lambda b, i: (b, 0)
    batch = lambda b, i: (b, 0, 0)
    kern = functools.partial(_dsa_kernel, nt=nt, tq=tq, kw=kw, l_valid=l_valid, qpos0=qpos0, topk=topk)
    return pl.pallas_call(
        kern,
        grid=(bsz, nt),
        in_specs=[pl.BlockSpec((tq, A_WIDTH), row),
                  pl.BlockSpec((t, IDX_HEADS * IDX_DIM), whole),
                  pl.BlockSpec((t, LANES), whole),
                  pl.BlockSpec((tq, gw), lambda b, i: (b * nt + i, go // gw)),
                  pl.BlockSpec((1, lp, A_KV_WIDTH), batch),
                  pl.BlockSpec((1, lp, A_KV_WIDTH), batch),
                  pl.BlockSpec((1, lp, IDX_DIM), batch)],
        out_specs=pl.BlockSpec((tq, A_WIDTH), row),
        out_shape=jax.ShapeDtypeStruct((bsz * t, A_WIDTH), BF16),
        scratch_shapes=[pltpu.VMEM((nt * nkb_max, tq, kw), F32),
                        pltpu.VMEM((t, LANES), jnp.int32),
                        pltpu.VMEM((t, LANES), jnp.int32),
                        pltpu.VMEM((t, LANES), F32),
                        pltpu.SMEM((1,), jnp.int32),
                        pltpu.VMEM((nkb_max, tq, kw), F32),
                        pltpu.VMEM((IDX_HEADS, tq, LANES), F32),
                        pltpu.VMEM((tq // rs, IDX_HEADS * rs, IDX_DIM), BF16),
                        pltpu.VMEM((A_KV_HEADS, grp * tq, A_HEAD_DIM), BF16),
                        pltpu.VMEM((A_KV_HEADS, grp * tq, A_HEAD_DIM), F32),
                        pltpu.VMEM((A_KV_HEADS, grp * tq, LANES), F32),
                        pltpu.VMEM((A_KV_HEADS, grp * tq, LANES), F32)],
        compiler_params=pltpu.CompilerParams(dimension_semantics=("arbitrary", "arbitrary"),
                                             vmem_limit_bytes=VMEM_LIMIT),
        name="dsa",
    )(aq, iq, ikw, z, k, v, ik)


SEL_BATCHES = 8
SEL_CHUNK = 512


def _dsa_sel_kernel(iq_ref, ikw_ref, cik_ref, nik_ref, bias_ref, key_ref, thr_ref, jcut_ref, need_ref,
                    *, gb, t, past, qpos0, topk):
    lp = past + LANES
    nc = lp // LANES

    def allowed_cols(col0, width):
        rows = lax.broadcasted_iota(jnp.int32, (t, width), 0)
        cols = lax.broadcasted_iota(jnp.int32, (t, width), 1) + col0
        return ((cols // CHUNK) <= (qpos0 + rows) // CHUNK) & (cols < past + t)

    def score_batch(j, carry):
        r0 = pl.multiple_of(j * t, t)
        qst = jnp.concatenate([iq_ref[pl.ds(r0, t), h * IDX_DIM:(h + 1) * IDX_DIM] for h in range(IDX_HEADS)], axis=0)
        iw = ikw_ref[pl.ds(r0, t), IDX_DIM:IDX_DIM + IDX_HEADS] * (IDX_DIM ** -0.5)
        iwb = [jnp.broadcast_to(iw[:, h:h + 1], (t, LANES)) for h in range(IDX_HEADS)]

        def head_sum(rel):
            width = rel.shape[1]
            sc = jnp.zeros((t, width), F32)
            for h in range(IDX_HEADS):
                sc = sc + jnp.maximum(rel[h * t:(h + 1) * t], 0.0) * _lane_tile(iwb[h], width)
            return sc

        for c in range(past // SEL_CHUNK):
            ikt = cik_ref[j, :, c * SEL_CHUNK:(c + 1) * SEL_CHUNK].astype(BF16)
            sc = head_sum(jnp.dot(qst, ikt, preferred_element_type=F32))
            key_ref[j, :, c * SEL_CHUNK:(c + 1) * SEL_CHUNK] = _masked_score(sc, allowed_cols(c * SEL_CHUNK, SEL_CHUNK))
        ik = jnp.concatenate([nik_ref[pl.ds(r0, t), :], jnp.zeros((LANES - t, IDX_DIM), BF16)], axis=0)
        sc = head_sum(lax.dot_general(qst, ik, _NT, preferred_element_type=F32))
        key_ref[j, :, past:] = _masked_score(sc, allowed_cols(past, LANES))
        return carry

    lax.fori_loop(0, gb, score_batch, 0)

    def piece(j, c):
        return lambda: key_ref[j, :, c * LANES:(c + 1) * LANES]

    strips = [(slice(j * t, (j + 1) * t), [(piece(j, c), c * LANES) for c in range(nc)]) for j in range(gb)]
    _topk_thresholds(strips, t, float(topk), thr_ref, jcut_ref, need_ref)

    def bias_batch(j, carry):
        r0 = pl.multiple_of(j * t, t)
        cols = lax.broadcasted_iota(jnp.int32, (t, lp), 1)
        bias_ref[j] = _selection_bias(key_ref[j], cols, _lane_tile(_key_to_f32(thr_ref[pl.ds(r0, t), :]), lp),
                                      _lane_tile(jcut_ref[pl.ds(r0, t), :], lp), allowed_cols(0, lp))
        return carry

    lax.fori_loop(0, gb, bias_batch, 0)


def _dsa_sel(iq, ikw, cache_ik_t, layer, new_ik, bsz, t, qpos0):
    past = cache_ik_t.shape[3]
    lp = past + LANES
    gb = next(g for g in (SEL_BATCHES, 4, 2, 1) if bsz % g == 0)
    assert past % SEL_CHUNK == 0 and t <= LANES and lp < 4096
    topk = min(TOPK_MAX, (past + t) // 4)
    rows = lambda i: (i, 0)
    kern = functools.partial(_dsa_sel_kernel, gb=gb, t=t, past=past, qpos0=qpos0, topk=topk)
    return pl.pallas_call(
        kern,
        grid=(bsz // gb,),
        in_specs=[pl.BlockSpec((gb * t, IDX_HEADS * IDX_DIM), rows),
                  pl.BlockSpec((gb * t, LANES), rows),
                  pl.BlockSpec((None, gb, IDX_DIM, past), lambda i: (layer, i, 0, 0)),
                  pl.BlockSpec((gb * t, IDX_DIM), rows)],
        out_specs=pl.BlockSpec((gb, t, lp), lambda i: (i, 0, 0)),
        out_shape=jax.ShapeDtypeStruct((bsz, t, lp), F32),
        scratch_shapes=[pltpu.VMEM((gb, t, lp), F32),
                        pltpu.VMEM((gb * t, LANES), jnp.int32),
                        pltpu.VMEM((gb * t, LANES), jnp.int32),
                        pltpu.VMEM((gb * t, LANES), F32)],
        compiler_params=pltpu.CompilerParams(dimension_semantics=("arbitrary",), vmem_limit_bytes=VMEM_LIMIT),
        name="dsa_sel",
    )(iq, ikw, cache_ik_t, new_ik)


def _dsa_att_kernel(aq_ref, agate_ref, bias_ref, ck_ref, cv_ref, nk_ref, nv_ref, o_ref, *, t, past):
    grp = A_HEADS // A_KV_HEADS
    scale = A_HEAD_DIM ** -0.5
    pad = jnp.zeros((LANES - t, A_HEAD_DIM), BF16)
    b1, b2 = bias_ref[0, :, :past], bias_ref[0, :, past:]
    for g in range(A_KV_HEADS):
        gsl = slice(g * A_HEAD_DIM, (g + 1) * A_HEAD_DIM)
        kc = ck_ref[0, pl.ds(g, past, stride=A_KV_HEADS), :].astype(BF16)
        vc = cv_ref[0, pl.ds(g, past, stride=A_KV_HEADS), :].astype(BF16)
        kn = jnp.concatenate([nk_ref[:, gsl], pad], axis=0)
        vn = jnp.concatenate([nv_ref[:, gsl], pad], axis=0)
        qs = jnp.concatenate([aq_ref[:, (g * grp + hh) * A_HEAD_DIM:(g * grp + hh + 1) * A_HEAD_DIM]
                              for hh in range(grp)], axis=0)

        def scores(keys, bias):
            s = lax.dot_general(qs, keys, _NT, preferred_element_type=F32) * scale
            return (s.reshape(grp, t, s.shape[1]) + bias[None]).reshape(grp * t, s.shape[1])

        s1, s2 = scores(kc, b1), scores(kn, b2)
        m = jnp.maximum(jnp.max(s1, axis=-1, keepdims=True), jnp.max(s2, axis=-1, keepdims=True))
        e1, e2 = jnp.exp(s1 - m), jnp.exp(s2 - m)
        l = jnp.sum(e1, axis=-1, keepdims=True) + jnp.sum(e2, axis=-1, keepdims=True)
        o = (jnp.dot(e1.astype(BF16), vc, preferred_element_type=F32)
             + jnp.dot(e2.astype(BF16), vn, preferred_element_type=F32)) / l
        for hh in range(grp):
            sl = slice((g * grp + hh) * A_HEAD_DIM, (g * grp + hh + 1) * A_HEAD_DIM)
            gate = agate_ref[:, sl]
            o_ref[:, sl] = (o[hh * t:(hh + 1) * t] * (gate * jax.nn.sigmoid(gate))).astype(BF16)


def _dsa_att(aq, z, dst, bias, cache_k, cache_v, new_k, new_v, bsz, t):
    past = cache_k.shape[1] // A_KV_HEADS
    lp = bias.shape[2]
    go, gw = dst["agate"]
    row = lambda b: (b, 0)
    batch = lambda b: (b, 0, 0)
    return pl.pallas_call(
        functools.partial(_dsa_att_kernel, t=t, past=past),
        grid=(bsz,),
        in_specs=[pl.BlockSpec((t, A_WIDTH), row),
                  pl.BlockSpec((t, gw), lambda b: (b, go // gw)),
                  pl.BlockSpec((1, t, lp), batch),
                  pl.BlockSpec((1, past * A_KV_HEADS, A_HEAD_DIM), batch),
                  pl.BlockSpec((1, past * A_KV_HEADS, A_HEAD_DIM), batch),
                  pl.BlockSpec((t, A_KV_WIDTH), row),
                  pl.BlockSpec((t, A_KV_WIDTH), row)],
        out_specs=pl.BlockSpec((t, A_WIDTH), row),
        out_shape=jax.ShapeDtypeStruct((bsz * t, A_WIDTH), BF16),
        compiler_params=pltpu.CompilerParams(dimension_semantics=("arbitrary",), vmem_limit_bytes=VMEM_LIMIT),
        name="dsa_att",
    )(aq, z, bias, cache_k, cache_v, new_k, new_v)


def _mla_kernel(q_ref, bgate_ref, k_ref, v_ref, o_ref, acc_ref, m_ref, l_ref, *, tq, kw, l_valid, qpos0):
    ti = pl.program_id(1)
    nkb = _num_key_blocks(ti, tq, kw, l_valid, qpos0)
    scale = (NOPE_DIM + ROPE_DIM) ** -0.5
    _softmax_init(B_HEADS, tq, m_ref, l_ref, acc_ref)

    def att_block(kb, masked, row0=0):
        r0 = pl.multiple_of(kb * kw, kw)
        rows = slice(row0, tq)
        if masked:
            allowed, _ = _allowed(ti, kb, row0, (tq - row0, kw), tq, kw, l_valid, qpos0)
            bias = jnp.where(allowed, 0.0, NEG)
        for h in range(B_HEADS):
            qsl = slice(h * B_QK_PAD, (h + 1) * B_QK_PAD)
            s = lax.dot_general(q_ref[rows, qsl], k_ref[0, pl.ds(r0, kw), qsl], _NT, preferred_element_type=F32)
            if masked:
                s = s + bias
            _softmax_step(s, scale, h, v_ref[0, pl.ds(r0, kw), h * V_DIM:(h + 1) * V_DIM], m_ref, l_ref, acc_ref, rows)

    n_full = _fully_visible_blocks(ti, tq, kw, l_valid, qpos0)
    lax.fori_loop(0, n_full, lambda kb, c: (att_block(kb, False), c)[1], 0)
    if qpos0 % kw == 0 and tq % kw == 0 and l_valid % tq == 0:
        for rel in range(tq // kw):
            att_block(n_full + rel, True, row0=rel * kw)
    else:
        lax.fori_loop(n_full, nkb, lambda kb, c: (att_block(kb, True), c)[1], 0)

    for h in range(B_HEADS):
        sl = slice(h * V_DIM, (h + 1) * V_DIM)
        o = acc_ref[h] / _softmax_sum(l_ref[h])
        gate = bgate_ref[:, sl]
        o_ref[:, sl] = (o * (gate * jax.nn.sigmoid(gate))).astype(BF16)


def _mla(qcat, z, dst, kcat, v, bsz, t, tq, kw, l_valid, qpos0):
    lp = kcat.shape[1]
    assert lp % kw == 0 and t % tq == 0
    nt = t // tq
    go, gw = dst["bgate"]
    row = lambda b, i: (b * nt + i, 0)
    batch = lambda b, i: (b, 0, 0)
    kern = functools.partial(_mla_kernel, tq=tq, kw=kw, l_valid=l_valid, qpos0=qpos0)
    return pl.pallas_call(
        kern,
        grid=(bsz, nt),
        in_specs=[pl.BlockSpec((tq, B_HEADS * B_QK_PAD), row),
                  pl.BlockSpec((tq, gw), lambda b, i: (b * nt + i, go // gw)),
                  pl.BlockSpec((1, lp, B_HEADS * B_QK_PAD), batch),
                  pl.BlockSpec((1, lp, B_WIDTH), batch)],
        out_specs=pl.BlockSpec((tq, B_WIDTH), row),
        out_shape=jax.ShapeDtypeStruct((bsz * t, B_WIDTH), BF16),
        scratch_shapes=[pltpu.VMEM((B_HEADS, tq, V_DIM), F32),
                        pltpu.VMEM((B_HEADS, tq, LANES), F32),
                        pltpu.VMEM((B_HEADS, tq, LANES), F32)],
        compiler_params=pltpu.CompilerParams(dimension_semantics=("arbitrary", "arbitrary"),
                                             vmem_limit_bytes=VMEM_LIMIT),
        name="mla",
    )(qcat, z, kcat, v)


def _mla_lat_kernel(q_ref, bgate_ref, cc_ref, ckr_ref, nc_ref, nkr_ref, wukt_ref, wuv_ref, o_ref,
                    *, t, past, qpos0):
    rows_n = B_HEADS * t
    ql, qr = [], []
    for h in range(B_HEADS):
        qn = q_ref[:, h * B_QK_PAD: h * B_QK_PAD + NOPE_DIM]
        ql.append(jnp.dot(qn, wukt_ref[h], preferred_element_type=F32).astype(BF16))
        qr.append(q_ref[:, h * B_QK_PAD + NOPE_DIM: h * B_QK_PAD + NOPE_DIM + ROPE_DIM])
    ql = jnp.concatenate(ql, axis=0)
    qr = jnp.concatenate(qr, axis=0)
    cc = cc_ref[0].astype(BF16)
    ckr_t = ckr_ref[0].astype(BF16)
    nc, nkr = nc_ref[...], nkr_ref[...]
    scale = (NOPE_DIM + ROPE_DIM) ** -0.5

    def scores(lat, rope_part, ncols, col0):
        s = (lax.dot_general(ql, lat, _NT, preferred_element_type=F32) + rope_part) * scale
        rows = lax.broadcasted_iota(jnp.int32, (rows_n, ncols), 0)
        cols = lax.broadcasted_iota(jnp.int32, (rows_n, ncols), 1) + col0
        qchunk = (qpos0 + lax.rem(rows, t)) // CHUNK
        return jnp.where((cols // CHUNK) <= qchunk, s, NEG)

    s1 = scores(cc, jnp.dot(qr, ckr_t, preferred_element_type=F32), past, 0)
    s2 = scores(nc, lax.dot_general(qr, nkr, _NT, preferred_element_type=F32), t, past)
    m = jnp.maximum(jnp.max(s1, axis=-1, keepdims=True), jnp.max(s2, axis=-1, keepdims=True))
    e1, e2 = jnp.exp(s1 - m), jnp.exp(s2 - m)
    l = jnp.sum(e1, axis=-1, keepdims=True) + jnp.sum(e2, axis=-1, keepdims=True)
    ol = (jnp.dot(e1.astype(BF16), cc, preferred_element_type=F32)
          + jnp.dot(e2.astype(BF16), nc, preferred_element_type=F32)) / l
    ol = ol.astype(BF16)
    for h in range(B_HEADS):
        sl = slice(h * V_DIM, (h + 1) * V_DIM)
        o = jnp.dot(ol[h * t:(h + 1) * t], wuv_ref[:, sl], preferred_element_type=F32)
        gate = bgate_ref[:, sl]
        o_ref[:, sl] = (o * (gate * jax.nn.sigmoid(gate))).astype(BF16)


def _mla_lat(qcat, z, dst, cache_ckv, cache_kr, layer, new_ckv, new_kr, wukt, wuv, bsz, t, qpos0):
    past = cache_ckv.shape[2]
    go, gw = dst["bgate"]
    row = lambda b: (b, 0)
    batch = lambda b: (b, 0, 0)
    const = lambda a: pl.BlockSpec(a.shape, lambda b: (0,) * a.ndim)
    kern = functools.partial(_mla_lat_kernel, t=t, past=past, qpos0=qpos0)
    return pl.pallas_call(
        kern,
        grid=(bsz,),
        in_specs=[pl.BlockSpec((t, B_HEADS * B_QK_PAD), row),
                  pl.BlockSpec((t, gw), lambda b: (b, go // gw)),
                  pl.BlockSpec((None, 1, past, KV_LORA), lambda b: (layer, b, 0, 0)),
                  pl.BlockSpec((None, 1, ROPE_DIM, past), lambda b: (layer, b, 0, 0)),
                  pl.BlockSpec((t, KV_LORA), row),
                  pl.BlockSpec((t, ROPE_DIM), row),
                  const(wukt), const(wuv)],
        out_specs=pl.BlockSpec((t, B_WIDTH), row),
        out_shape=jax.ShapeDtypeStruct((bsz * t, B_WIDTH), BF16),
        compiler_params=pltpu.CompilerParams(dimension_semantics=("arbitrary",), vmem_limit_bytes=VMEM_LIMIT),
        name="mla_lat",
    )(qcat, z, cache_ckv, cache_kr, new_ckv, new_kr, wukt, wuv)


def _out_kernel(ga_ref, gb_ref, mg_ref, x_ref, woa_ref, wob_ref, wout_ref, fg_ref, o_ref, *, d, final):
    ya = jnp.dot(ga_ref[...], woa_ref[...], preferred_element_type=F32)
    yb = jnp.dot(gb_ref[...], wob_ref[...], preferred_element_type=F32)
    m = jax.nn.sigmoid(mg_ref[:, :d]) * ya + jax.nn.sigmoid(mg_ref[:, d:]) * yb
    h = x_ref[...] + jnp.dot(m.astype(BF16), wout_ref[...], preferred_element_type=F32)
    o_ref[...] = _rms(h, fg_ref[...]) if final else h


def _out(ga, gb, z, dst, x, woa, wob, wout, fg, final):
    m, d = x.shape
    tm = min(m, 256)
    assert m % tm == 0 and dst["mg"][0] == 0
    row = lambda i: (i, 0)
    const = lambda a: pl.BlockSpec(a.shape, lambda i: (0, 0), pipeline_mode=pl.Buffered(1))
    return pl.pallas_call(
        functools.partial(_out_kernel, d=d, final=final),
        grid=(m // tm,),
        in_specs=[pl.BlockSpec((tm, A_WIDTH), row), pl.BlockSpec((tm, B_WIDTH), row),
                  pl.BlockSpec((tm, 2 * d), row), pl.BlockSpec((tm, d), row),
                  const(woa), const(wob), const(wout), const(fg)],
        out_specs=pl.BlockSpec((tm, d), row),
        out_shape=jax.ShapeDtypeStruct((m, d), F32),
        compiler_params=pltpu.CompilerParams(dimension_semantics=("arbitrary",), vmem_limit_bytes=VMEM_LIMIT),
        name="out",
    )(ga, gb, z, x, woa, wob, wout, fg)


def _all_keys(cache, new, bsz, t, lp):
    new = new.reshape(bsz, t, new.shape[-1])
    if cache is None and lp == t:
        return new
    parts = ([] if cache is None else [cache.reshape(bsz, cache.shape[1], -1).astype(BF16)]) + [new]
    kk = jnp.concatenate(parts, axis=1)
    return jnp.pad(kk, ((0, 0), (0, lp - kk.shape[1]), (0, 0)))


def _trunk(x, caches, final_g, weights):
    bsz, t, d = x.shape
    depth = weights["w_in"].shape[0]
    past = 0 if caches is None else caches[0].shape[2]
    l_valid = past + t
    tq = min(t, QUERY_TILE)
    kw = KEY_BLOCK if t >= QUERY_TILE else -(-l_valid // LANES) * LANES
    lp = -(-l_valid // kw) * kw
    _, dst, n_proj = _layout(d)
    pos = past + jnp.arange(t, dtype=jnp.int32)
    t_scale = jnp.concatenate([jnp.full((t, IDX_HEADS), IDX_HEADS ** -0.5, F32),
                               jnp.ones((t, LANES - IDX_DIM - IDX_HEADS), F32)], axis=1)
    tab_a = jnp.stack(_rope_tables(pos, A_ROT, A_HEAD_DIM, 1))
    tab_i = jnp.stack(_rope_tables(pos, IDX_ROT, IDX_DIM, 2))
    tabs = [tab_a, jnp.stack(_rope_tables(pos, IDX_ROT, IDX_DIM, 1, extra=t_scale)),
            jnp.stack(_rope_tables(pos, ROPE_DIM, ROPE_DIM, 2))]
    reps = max(1, min(bsz * t, PROJ_ROWS) // t)
    proj_tabs = [jnp.tile(a, (1, reps, 1)) for a in (tab_a, tab_i)]
    h = x.reshape(bsz * t, d)
    new_rows = []
    for l in range(depth):
        w = {k: v[l] for k, v in weights.items()}
        z, aq, iq = _proj(h, w["norm_g"], w["w_in"], dst, n_proj, *proj_tabs)
        ak, av, ikw, krp, ckv, qcat, kb, vb, ikb, ckvb, krb = _prep(
            z, dst, bsz, t, tabs, w["q_norm_g"], w["kv_norm_g"], w["w_uq"])
        c = (None,) * 5 if caches is None else tuple(cc[l] for cc in caches)
        if caches is not None and t < QUERY_TILE:
            bias = _dsa_sel(iq, ikw, jnp.swapaxes(caches[2], 2, 3), l, ikb, bsz, t, past)
            interleaved = lambda a: a.reshape(bsz, past * A_KV_HEADS, A_HEAD_DIM)
            ga = _dsa_att(aq, z, dst, bias, interleaved(c[0]), interleaved(c[1]), kb, vb, bsz, t)
            gb = _mla_lat(qcat, z, dst, caches[3], jnp.swapaxes(caches[4], 2, 3), l, ckvb, krb, w["w_ukt"], w["w_uv"], bsz, t, past)
        else:
            k_all = _all_keys(c[0], kb, bsz, t, lp)
            v_all = _all_keys(c[1], vb, bsz, t, lp)
            ik_all = _all_keys(c[2], ikb, bsz, t, lp)
            ga = _dsa(aq, iq, ikw, z, dst, k_all, v_all, ik_all, bsz, t, tq, kw, l_valid, past)
            ckv_all = _all_keys(c[3], ckvb, bsz, t, lp)
            kr_all = _all_keys(c[4], krb, bsz, t, lp)
            kcat, vv = _kvup(ckv_all.reshape(bsz * lp, KV_LORA), kr_all.reshape(bsz * lp, ROPE_DIM), w["w_ukv"])
            tq_mla = MLA_QUERY_TILE if t % MLA_QUERY_TILE == 0 else tq
            gb = _mla(qcat, z, dst, kcat.reshape(bsz, lp, -1), vv.reshape(bsz, lp, -1), bsz, t, tq_mla, kw, l_valid, past)
        h = _out(ga, gb, z, dst, h, w["w_o_a"], w["w_o_b"], w["w_out"], final_g, final=(l == depth - 1))
        new_rows.append((ak.reshape(bsz, t, A_KV_HEADS, A_HEAD_DIM), av.reshape(bsz, t, A_KV_HEADS, A_HEAD_DIM),
                         ikw[:, :IDX_DIM].reshape(bsz, t, IDX_DIM), ckv.reshape(bsz, t, KV_LORA),
                         krp[:, :ROPE_DIM].reshape(bsz, t, ROPE_DIM)))
    stacked = [jnp.stack([r[i] for r in new_rows], axis=0) for i in range(5)]
    return h.reshape(bsz, t, d), stacked


def kernel(x_prompt, x_sample, cache_a_k, cache_a_v, cache_idx_k, cache_mla_ckv, cache_mla_krope,
           norm_g, w_in, w_uq, q_norm_g, kv_norm_g, w_uk, w_uv, w_o_a, w_o_b, w_out, final_g):
    depth, d, _ = w_in.shape
    weights = {
        "norm_g": norm_g.reshape(depth, 1, d),
        "w_in": jnp.stack([_regroup_w_in(jnp.swapaxes(w_in[l], 0, 1), d) for l in range(depth)]),
        "w_uq": jnp.concatenate([w_uq[..., :NOPE_DIM].reshape(depth, Q_LORA, B_HEADS * NOPE_DIM),
                                 w_uq[..., NOPE_DIM:].reshape(depth, Q_LORA, B_HEADS * ROPE_DIM)], axis=-1).astype(BF16),
        "q_norm_g": q_norm_g.reshape(depth, 1, Q_LORA),
        "kv_norm_g": kv_norm_g.reshape(depth, 1, KV_LORA),
        "w_ukv": jnp.concatenate([w_uk.reshape(depth, KV_LORA, B_HEADS * NOPE_DIM),
                                  w_uv.reshape(depth, KV_LORA, B_WIDTH)], axis=-1).astype(BF16),
        "w_ukt": jnp.transpose(w_uk, (0, 2, 3, 1)).astype(BF16),
        "w_uv": w_uv.reshape(depth, KV_LORA, B_WIDTH).astype(BF16),
        "w_o_a": w_o_a.astype(BF16), "w_o_b": w_o_b.astype(BF16), "w_out": w_out.astype(BF16),
    }
    fg = final_g.reshape(1, d)
    y_p, (pk, pv, pik, pckv, pkr) = _trunk(x_prompt, None, fg, weights)
    caches = (cache_a_k, cache_a_v, cache_idx_k, cache_mla_ckv, cache_mla_krope)
    y_s, (sk, sv, sik, sckv, skr) = _trunk(x_sample, caches, fg, weights)
    return (y_p, y_s, pk, pv, pik, pckv, pkr, sk, sv, sik, sckv, skr)
```

```python
import functools

import jax
import jax.numpy as jnp
from jax import lax
from jax.experimental import pallas as pl
from jax.experimental.pallas import tpu as pltpu

F32 = jnp.float32
BF16 = jnp.bfloat16

CHUNK = 64
ROPE_THETA = 500000.0
EPS = 1e-6

A_HEADS = 8
A_KV_HEADS = 2
A_HEAD_DIM = 128
A_ROT = A_HEAD_DIM // 4
A_WIDTH = A_HEADS * A_HEAD_DIM
A_KV_WIDTH = A_KV_HEADS * A_HEAD_DIM
IDX_HEADS = 16
IDX_DIM = 64
IDX_ROT = IDX_DIM // 4
TOPK_MAX = 256

B_HEADS = 8
Q_LORA = 512
KV_LORA = 512
NOPE_DIM = 128
ROPE_DIM = 64
V_DIM = 128
B_WIDTH = B_HEADS * V_DIM
B_QK_PAD = 256

LANES = 128
KEY_BLOCK = 256
QUERY_TILE = 256
PROJ_ROWS = 1024
PREP_ROWS = 256
PROJ_TILE = 1024
MLA_QUERY_TILE = 512
ROW_STRIP = 64
INT_MIN = -(2 ** 31)
LOG2_E = 1.4426950408889634
NEG = -1e30
MASKED = -1e38
VMEM_LIMIT = 56 * 1024 * 1024

_SEGS = (("aq", A_WIDTH), ("ak", A_KV_WIDTH), ("av", A_KV_WIDTH), ("agate", A_WIDTH),
         ("iq", IDX_HEADS * IDX_DIM), ("ik", IDX_DIM), ("iw", IDX_HEADS),
         ("cq", Q_LORA), ("ckv", KV_LORA), ("kr", ROPE_DIM), ("bgate", B_WIDTH), ("mg", None))


def _layout(d_model):
    src, off = {}, 0
    for name, w in _SEGS:
        w = 2 * d_model if w is None else w
        src[name] = (off, w)
        off += w
    order = (("mg", 2 * d_model), ("aq", A_WIDTH), ("agate", A_WIDTH), ("iq", IDX_HEADS * IDX_DIM),
             ("bgate", B_WIDTH), ("cq", Q_LORA), ("ckv", KV_LORA), ("ak", A_KV_WIDTH), ("av", A_KV_WIDTH),
             ("ikw", LANES), ("krp", LANES))
    dst, off = {}, 0
    for name, w in order:
        assert off % w == 0
        dst[name] = (off, w)
        off += w
    return src, dst, off


def _regroup_kernel(w_ref, o_ref, *, src, dst):
    cols = w_ref.shape[1]

    def piece(name):
        o, w = src[name]
        return w_ref[o:o + w, :].astype(BF16)

    def zeros(n):
        return jnp.zeros((n, cols), BF16)

    for name, (o, w) in dst.items():
        if name == "ikw":
            val = jnp.concatenate([piece("ik"), piece("iw"), zeros(LANES - IDX_DIM - IDX_HEADS)], axis=0)
        elif name == "krp":
            val = jnp.concatenate([piece("kr"), zeros(LANES - ROPE_DIM)], axis=0)
        else:
            val = piece(name)
        o_ref[o:o + w, :] = val
    total = max(o + w for o, w in dst.values())
    if o_ref.shape[0] > total:
        o_ref[total:, :] = zeros(o_ref.shape[0] - total)


def _regroup_w_in(w_in_t, d_model):
    src, dst, total = _layout(d_model)
    tc = 256
    padded = -(-total // PROJ_TILE) * PROJ_TILE
    assert d_model % tc == 0
    return pl.pallas_call(
        functools.partial(_regroup_kernel, src=src, dst=dst),
        grid=(d_model // tc,),
        in_specs=[pl.BlockSpec((w_in_t.shape[0], tc), lambda i: (0, i))],
        out_specs=pl.BlockSpec((padded, tc), lambda i: (0, i)),
        out_shape=jax.ShapeDtypeStruct((padded, d_model), BF16),
        compiler_params=pltpu.CompilerParams(dimension_semantics=("arbitrary",), vmem_limit_bytes=VMEM_LIMIT),
        name="regroup",
    )(w_in_t)


def _rope_tables(pos, rot, head, heads_per_vreg, extra=None):
    half = rot // 2
    inv_freq = ROPE_THETA ** (-jnp.arange(half, dtype=F32) * (2.0 / rot))
    ang = pos.astype(F32)[:, None] * inv_freq[None, :]
    cos, sin = jnp.cos(ang), jnp.sin(ang)
    t = pos.shape[0]
    one = jnp.ones((t, head - rot), F32)
    zero = lambda n: jnp.zeros((t, n), F32)
    c = jnp.concatenate([cos, cos, one], axis=1)
    sa = jnp.concatenate([-sin, zero(head - half)], axis=1)
    sb = jnp.concatenate([zero(half), sin, zero(head - rot)], axis=1)
    c, sa, sb = (jnp.tile(a, (1, heads_per_vreg)) for a in (c, sa, sb))
    if extra is not None:
        c = jnp.concatenate([c, extra], axis=1)
        sa = jnp.concatenate([sa, jnp.zeros_like(extra)], axis=1)
        sb = jnp.concatenate([sb, jnp.zeros_like(extra)], axis=1)
    assert c.shape[1] == LANES
    return c, sa, sb


def _rope(x, c, sa, sb, half):
    up = pltpu.roll(x, LANES - half, 1)
    dn = pltpu.roll(x, half, 1)
    return x * c + up * sa + dn * sb


def _rms(x, g):
    ms = jnp.mean(x * x, axis=-1, keepdims=True)
    return (x * lax.rsqrt(ms + EPS)) * g


def _lane_tile(x, width):
    return x if width == LANES else jnp.concatenate([x] * (width // LANES), axis=1)


_NT = (((1,), (1,)), ((), ()))


def _proj_kernel(x_ref, g_ref, w_ref, ta_ref, ti_ref, z_ref, aq_ref, iq_ref, xn_ref, hold_ref, *, j_aq, j_iq):
    j = pl.program_id(1)

    @pl.when(j == 0)
    def _():
        xn_ref[...] = _rms(x_ref[...], g_ref[...]).astype(BF16)

    def tile():
        return lax.dot_general(xn_ref[...], w_ref[...], _NT, preferred_element_type=F32)

    def rope_held(tab_ref, out_ref, half):
        c, sa, sb = tab_ref[0], tab_ref[1], tab_ref[2]
        for v in range(hold_ref.shape[1] // LANES):
            sl = slice(v * LANES, (v + 1) * LANES)
            out_ref[:, sl] = _rope(hold_ref[:, sl], c, sa, sb, half).astype(BF16)

    held = (j == j_aq) | (j == j_iq)
    after_aq = j == j_aq + 1
    after_iq = j == j_iq + 1

    @pl.when(held)
    def _():
        r = tile()
        z_ref[...] = r
        hold_ref[...] = r

    @pl.when(after_aq)
    def _():
        z_ref[...] = tile()
        rope_held(ta_ref, aq_ref, A_ROT // 2)

    @pl.when(after_iq)
    def _():
        z_ref[...] = tile()
        rope_held(ti_ref, iq_ref, IDX_ROT // 2)

    @pl.when(jnp.logical_not(held | after_aq | after_iq))
    def _():
        z_ref[...] = tile()


def _proj(x, g, w, dst, n, tab_a, tab_i):
    m, d = x.shape
    tm = min(m, PROJ_ROWS)
    tn = PROJ_TILE
    assert m % tm == 0 and w.shape[0] % tn == 0 and dst["aq"][1] == tn and dst["iq"][1] == tn
    j_aq, j_iq = dst["aq"][0] // tn, dst["iq"][0] // tn
    assert dst["aq"][0] % tn == 0 and dst["iq"][0] % tn == 0 and j_iq > j_aq + 1 and (j_iq + 1) * tn < n
    ntt = tab_a.shape[1] // tm
    tspec = pl.BlockSpec((3, tm, LANES), lambda i, j: (0, i % ntt, 0), pipeline_mode=pl.Buffered(1))
    return pl.pallas_call(
        functools.partial(_proj_kernel, j_aq=j_aq, j_iq=j_iq),
        grid=(m // tm, w.shape[0] // tn),
        in_specs=[pl.BlockSpec((tm, d), lambda i, j: (i, 0)),
                  pl.BlockSpec((1, d), lambda i, j: (0, 0)),
                  pl.BlockSpec((tn, d), lambda i, j: (j, 0)),
                  tspec, tspec],
        out_specs=[pl.BlockSpec((tm, tn), lambda i, j: (i, j)),
                   pl.BlockSpec((tm, tn), lambda i, j: (i, 0)),
                   pl.BlockSpec((tm, tn), lambda i, j: (i, 0))],
        out_shape=[jax.ShapeDtypeStruct((m, n), F32), jax.ShapeDtypeStruct((m, tn), BF16),
                   jax.ShapeDtypeStruct((m, tn), BF16)],
        scratch_shapes=[pltpu.VMEM((tm, d), BF16), pltpu.VMEM((tm, tn), F32)],
        compiler_params=pltpu.CompilerParams(dimension_semantics=("arbitrary", "arbitrary"),
                                             vmem_limit_bytes=VMEM_LIMIT),
        name="proj",
    )(x, g, w, tab_a, tab_i)


def _prep_kernel(cq_ref, ckv_ref, ak_ref, av_ref, ikw_ref, krp_ref,
                 ta_ref, tk_ref, tr_ref, qg_ref, kvg_ref, wuq_ref,
                 ak_o, av_o, ikw_o, krp_o, ckv_o, qcat_o, kb_o, vb_o, ikb_o, ckvb_o, krb_o):
    ca, saa, sba = ta_ref[0], ta_ref[1], ta_ref[2]
    ck, sak, sbk = tk_ref[0], tk_ref[1], tk_ref[2]
    cr, sar, sbr = tr_ref[0], tr_ref[1], tr_ref[2]
    tm = cq_ref.shape[0]
    for h in range(A_KV_HEADS):
        sl = slice(h * LANES, (h + 1) * LANES)
        ak = _rope(ak_ref[:, sl], ca, saa, sba, A_ROT // 2)
        ak_o[pl.ds(h, tm, stride=A_KV_HEADS), :] = ak
        av_o[pl.ds(h, tm, stride=A_KV_HEADS), :] = av_ref[:, sl]
        kb_o[:, sl] = ak.astype(BF16)
    vb_o[...] = av_ref[...].astype(BF16)
    ikw = _rope(ikw_ref[...], ck, sak, sbk, IDX_ROT // 2)
    ikw_o[...] = ikw
    ikb_o[...] = ikw[:, :IDX_DIM].astype(BF16)
    krp = _rope(krp_ref[...], cr, sar, sbr, ROPE_DIM // 2)
    krp_o[...] = krp
    krb_o[...] = krp[:, :ROPE_DIM].astype(BF16)
    ckv = _rms(ckv_ref[...], kvg_ref[...])
    ckv_o[...] = ckv
    ckvb_o[...] = ckv.astype(BF16)
    cqn = _rms(cq_ref[...], qg_ref[...]).astype(BF16)
    qb = jnp.dot(cqn, wuq_ref[...], preferred_element_type=F32)
    nope = B_HEADS * NOPE_DIM
    low = lax.broadcasted_iota(jnp.int32, (tm, LANES), 1) < ROPE_DIM
    for v in range(B_HEADS * ROPE_DIM // LANES):
        r = _rope(qb[:, nope + v * LANES: nope + (v + 1) * LANES], cr, sar, sbr, ROPE_DIM // 2)
        halves = (jnp.where(low, r, 0.0), jnp.where(low, pltpu.roll(r, ROPE_DIM, 1), 0.0))
        for j in range(2):
            h = 2 * v + j
            qcat_o[:, h * B_QK_PAD: h * B_QK_PAD + NOPE_DIM] = qb[:, h * NOPE_DIM:(h + 1) * NOPE_DIM].astype(BF16)
            qcat_o[:, h * B_QK_PAD + NOPE_DIM:(h + 1) * B_QK_PAD] = halves[j].astype(BF16)


def _prep(z, dst, bsz, t, tabs, q_norm_g, kv_norm_g, wuq):
    m = bsz * t
    if t < PREP_ROWS and m % (PREP_ROWS // t * t) == 0:
        reps = PREP_ROWS // t
        tabs = [jnp.tile(a, (1, reps, 1)) for a in tabs]
        bsz, t = m // (reps * t), reps * t
    tm = min(t, PREP_ROWS)
    nt = t // tm
    assert t % tm == 0

    def zspec(name):
        o, w = dst[name]
        return pl.BlockSpec((tm, w), lambda b, i, blk=o // w: (b * nt + i, blk))

    def tspec():
        return pl.BlockSpec((3, tm, LANES), lambda b, i: (0, i, 0))

    def ospec(w):
        return pl.BlockSpec((tm, w), lambda b, i: (b * nt + i, 0))

    def full(a):
        return pl.BlockSpec(a.shape, lambda b, i: (0,) * a.ndim)

    outs = [("interleaved", F32), ("interleaved", F32), (LANES, F32), (LANES, F32),
            (KV_LORA, F32), (B_HEADS * B_QK_PAD, BF16),
            (A_KV_WIDTH, BF16), (A_KV_WIDTH, BF16), (IDX_DIM, BF16), (KV_LORA, BF16), (ROPE_DIM, BF16)]
    return pl.pallas_call(
        _prep_kernel,
        grid=(bsz, nt),
        in_specs=[zspec("cq"), zspec("ckv"), zspec("ak"), zspec("av"), zspec("ikw"),
                  zspec("krp"), tspec(), tspec(), tspec(), full(q_norm_g), full(kv_norm_g), full(wuq)],
        out_specs=[pl.BlockSpec((tm * A_KV_HEADS, A_HEAD_DIM), lambda b, i: (b * nt + i, 0)) if w == "interleaved" else ospec(w)
                   for w, _ in outs],
        out_shape=[jax.ShapeDtypeStruct((m * A_KV_HEADS, A_HEAD_DIM) if w == "interleaved" else (m, w), dt) for w, dt in outs],
        compiler_params=pltpu.CompilerParams(dimension_semantics=("arbitrary", "arbitrary"),
                                             vmem_limit_bytes=VMEM_LIMIT),
        name="prep",
    )(z, z, z, z, z, z, *tabs, q_norm_g, kv_norm_g, wuq)


def _kvup_kernel(c_ref, kr_ref, w_ref, kcat_ref, v_ref):
    r = jnp.dot(c_ref[...], w_ref[...], preferred_element_type=F32)
    n = B_HEADS * NOPE_DIM
    tail = jnp.concatenate([kr_ref[...], jnp.zeros((kr_ref.shape[0], B_QK_PAD - NOPE_DIM - ROPE_DIM), BF16)], axis=1)
    for h in range(B_HEADS):
        kcat_ref[:, h * B_QK_PAD: h * B_QK_PAD + NOPE_DIM] = r[:, h * NOPE_DIM:(h + 1) * NOPE_DIM].astype(BF16)
        kcat_ref[:, h * B_QK_PAD + NOPE_DIM:(h + 1) * B_QK_PAD] = tail
    v_ref[...] = r[:, n:].astype(BF16)


def _kvup(ckv, kr, w):
    m, c = ckv.shape
    tm = next(t for t in (512, 384, 256, 128) if m % t == 0)
    row = lambda i: (i, 0)
    return pl.pallas_call(
        _kvup_kernel,
        grid=(m // tm,),
        in_specs=[pl.BlockSpec((tm, c), row), pl.BlockSpec((tm, ROPE_DIM), row), pl.BlockSpec(w.shape, lambda i: (0, 0))],
        out_specs=[pl.BlockSpec((tm, B_HEADS * B_QK_PAD), row), pl.BlockSpec((tm, B_WIDTH), row)],
        out_shape=[jax.ShapeDtypeStruct((m, B_HEADS * B_QK_PAD), BF16), jax.ShapeDtypeStruct((m, B_WIDTH), BF16)],
        compiler_params=pltpu.CompilerParams(dimension_semantics=("arbitrary",), vmem_limit_bytes=VMEM_LIMIT),
        name="kv_up",
    )(ckv, kr, w)


def _visible_cols(tile, tq, l_valid, qpos0):
    p_last = qpos0 + (tile + 1) * tq - 1
    n = (p_last // CHUNK + 1) * CHUNK
    return min(n, l_valid) if isinstance(n, int) else jnp.minimum(n, l_valid)


def _num_key_blocks(tile, tq, kw, l_valid, qpos0):
    return (_visible_cols(tile, tq, l_valid, qpos0) + kw - 1) // kw


def _fully_visible_blocks(tile, tq, kw, l_valid, qpos0):
    first_row_cols = jnp.minimum(((qpos0 + tile * tq) // CHUNK + 1) * CHUNK, l_valid)
    return first_row_cols // kw


def _allowed(tile, kb, rows0, shape, tq, kw, l_valid, qpos0):
    rows = lax.broadcasted_iota(jnp.int32, shape, 0) + rows0
    cols = lax.broadcasted_iota(jnp.int32, shape, 1) + kb * kw
    qchunk = (qpos0 + tile * tq + rows) // CHUNK
    return ((cols // CHUNK) <= qchunk) & (cols < l_valid), cols


def _softmax_step(s, scale, h, vblk, m_ref, l_ref, acc_ref, rows=slice(None)):
    kw = s.shape[1]
    c = scale * LOG2_E
    m_prev = m_ref[h, rows]
    m_new = jnp.maximum(m_prev, jnp.max(s, axis=-1, keepdims=True))
    alpha = jnp.exp2((m_prev - m_new) * c)
    p = jnp.exp2((s - _lane_tile(m_new, kw)) * c)
    l_ref[h, rows] = alpha * l_ref[h, rows] + sum(p[:, c * LANES:(c + 1) * LANES] for c in range(kw // LANES))
    acc_ref[h, rows] = alpha * acc_ref[h, rows] + jnp.dot(p.astype(BF16), vblk, preferred_element_type=F32)
    m_ref[h, rows] = m_new


def _softmax_sum(l_partial):
    return jnp.sum(l_partial, axis=-1, keepdims=True)


def _softmax_init(heads, tq, m_ref, l_ref, acc_ref):
    for h in range(heads):
        m_ref[h] = jnp.full((tq, LANES), NEG, F32)
        l_ref[h] = jnp.zeros((tq, LANES), F32)
        acc_ref[h] = jnp.zeros((tq, acc_ref.shape[2]), F32)


def _masked_score(score, allowed):
    return jnp.where(allowed, score, MASKED)


def _key_to_f32(key):
    return lax.bitcast_convert_type(key ^ ((key >> 31) & 0x7FFFFFFF), F32)


def _selection_bias(score, cols, thr, jcut, allowed):
    sel = ((score > thr) | ((score == thr) & (cols < jcut))) & allowed
    return jnp.where(sel, 0.0, NEG)


def _topk_thresholds(strips, rs, kf, thr_ref, jcut_ref, need_ref):
    def count(pieces, pred):
        acc = jnp.zeros((rs, LANES), F32)
        for load, col0 in pieces:
            acc = acc + jnp.where(pred(load(), col0), 1.0, 0.0)
        return jnp.sum(acc, axis=-1, keepdims=True)

    def bit_step(b, carry):
        bit = jnp.left_shift(jnp.int32(1), 31 - b)
        for rows, pieces in strips:
            t = thr_ref[rows, :]
            cand = t + bit
            cand_f = _key_to_f32(cand)
            cnt = count(pieces, lambda s, c0: s >= cand_f)
            thr_ref[rows, :] = jnp.where(cnt >= kf, cand, t)
        return carry

    thr_ref[...] = jnp.full(thr_ref.shape, INT_MIN, jnp.int32)
    lax.fori_loop(0, 32, bit_step, 0)

    jcut_ref[...] = jnp.full(jcut_ref.shape, 4096, jnp.int32)
    flag = jnp.int32(0)
    for rows, pieces in strips:
        t = _key_to_f32(thr_ref[rows, :])
        excess = (count(pieces, lambda s, c0: s >= t) > kf) & (t[:, :1] > MASKED)
        flag = jnp.maximum(flag, jnp.max(jnp.where(excess, 1, 0)))

    @pl.when(flag > 0)
    def _():
        lane = lax.broadcasted_iota(jnp.int32, (rs, LANES), 1)
        for rows, pieces in strips:
            t = _key_to_f32(thr_ref[rows, :])
            need = kf - count(pieces, lambda s, c0: s > t)
            need_ref[rows, :] = jnp.broadcast_to(need, (rs, LANES))

        def col_step(b, carry):
            bit = jnp.left_shift(jnp.int32(1), 11 - b)
            for rows, pieces in strips:
                t, j = _key_to_f32(thr_ref[rows, :]), jcut_ref[rows, :]
                cand = j + bit
                f = count(pieces, lambda s, c0: (s == t) & (lane + c0 < cand))
                jcut_ref[rows, :] = jnp.where(f <= need_ref[rows, :], cand, j)
            return carry

        jcut_ref[...] = jnp.zeros(jcut_ref.shape, jnp.int32)
        lax.fori_loop(0, 12, col_step, 0)

    return flag


def _dsa_kernel(aq_ref, iq_ref, ikw_ref, agate_ref, k_ref, v_ref, ik_ref, o_ref,
                key_ref, thr_ref, jcut_ref, need_ref, ties_ref, bias_ref, iwb_ref, qst_ref, qs_ref, acc_ref, m_ref, l_ref,
                *, nt, tq, kw, l_valid, qpos0, topk):
    nkb_max = key_ref.shape[0] // nt
    nkb_of = [_num_key_blocks(ti, tq, kw, l_valid, qpos0) for ti in range(nt)]
    rs = qst_ref.shape[1] // IDX_HEADS
    ns = tq // rs
    nc = kw // LANES
    kf = float(topk)

    def score_tile(ti, carry):
        r0 = pl.multiple_of(ti * tq, tq)
        iw = ikw_ref[pl.ds(r0, tq), IDX_DIM:IDX_DIM + IDX_HEADS] * (IDX_DIM ** -0.5)
        for h in range(IDX_HEADS):
            iwb_ref[h] = jnp.broadcast_to(iw[:, h:h + 1], (tq, LANES))
            for r in range(ns):
                qst_ref[r, h * rs:(h + 1) * rs, :] = iq_ref[pl.ds(r0 + r * rs, rs), h * IDX_DIM:(h + 1) * IDX_DIM]

        def score_block(kb, c2):
            ikb = ik_ref[0, pl.ds(pl.multiple_of(kb * kw, kw), kw), :]
            for r in range(ns):
                rel = lax.dot_general(qst_ref[r], ikb, _NT, preferred_element_type=F32)
                sc = jnp.zeros((rs, kw), F32)
                for h in range(IDX_HEADS):
                    w = _lane_tile(iwb_ref[h, r * rs:(r + 1) * rs, :], kw)
                    sc = sc + jnp.maximum(rel[h * rs:(h + 1) * rs], 0.0) * w
                allowed, _ = _allowed(ti, kb, r * rs, (rs, kw), tq, kw, l_valid, qpos0)
                key_ref[ti * nkb_max + kb, r * rs:(r + 1) * rs, :] = _masked_score(sc, allowed)
            return c2

        lax.fori_loop(0, _num_key_blocks(ti, tq, kw, l_valid, qpos0), score_block, 0)
        return carry

    def piece(ti, kb, s, c):
        return lambda: key_ref[ti * nkb_max + kb, s * rs:(s + 1) * rs, c * LANES:(c + 1) * LANES]

    strips = [(slice(ti * tq + s * rs, ti * tq + (s + 1) * rs),
               [(piece(ti, kb, s, c), kb * kw + c * LANES) for kb in range(nkb_of[ti]) for c in range(nc)])
              for ti in range(nt) for s in range(ns)]

    @pl.when(pl.program_id(1) == 0)
    def _():
        lax.fori_loop(0, nt, score_tile, 0)
        ties_ref[0] = _topk_thresholds(strips, rs, kf, thr_ref, jcut_ref, need_ref)

    ti = pl.program_id(1)
    nkb = _num_key_blocks(ti, tq, kw, l_valid, qpos0)
    r0 = pl.multiple_of(ti * tq, tq)
    thr = _lane_tile(_key_to_f32(thr_ref[pl.ds(r0, tq), :]), kw)
    jcut = _lane_tile(jcut_ref[pl.ds(r0, tq), :], kw)

    def bias_block(kb, masked, ties):
        score = key_ref[ti * nkb_max + kb]
        if ties:
            allowed, cols = _allowed(ti, kb, 0, (tq, kw), tq, kw, l_valid, qpos0)
            bias_ref[kb] = _selection_bias(score, cols, thr, jcut, allowed)
        else:
            sel = score >= thr
            if masked:
                sel = sel & _allowed(ti, kb, 0, (tq, kw), tq, kw, l_valid, qpos0)[0]
            bias_ref[kb] = jnp.where(sel, 0.0, NEG)

    def bias_loop(lo, hi, masked, ties):
        lax.fori_loop(lo, hi, lambda kb, c: (bias_block(kb, masked, ties), c)[1], 0)

    n_full = _fully_visible_blocks(ti, tq, kw, l_valid, qpos0)

    @pl.when(ties_ref[0] == 0)
    def _():
        bias_loop(0, n_full, False, False)
        bias_loop(n_full, nkb, True, False)

    @pl.when(ties_ref[0] != 0)
    def _():
        bias_loop(0, nkb, True, True)

    grp = A_HEADS // A_KV_HEADS
    _softmax_init(A_KV_HEADS, grp * tq, m_ref, l_ref, acc_ref)
    scale = A_HEAD_DIM ** -0.5
    for h in range(A_HEADS):
        g, hh = divmod(h, grp)
        qs_ref[g, hh * tq:(hh + 1) * tq, :] = aq_ref[:, h * A_HEAD_DIM:(h + 1) * A_HEAD_DIM]

    def att_block(kb, carry):
        c0 = pl.multiple_of(kb * kw, kw)
        bias = bias_ref[kb][None]
        for g in range(A_KV_HEADS):
            gsl = slice(g * A_HEAD_DIM, (g + 1) * A_HEAD_DIM)
            s = lax.dot_general(qs_ref[g], k_ref[0, pl.ds(c0, kw), gsl], _NT, preferred_element_type=F32)
            s = (s.reshape(grp, tq, kw) + bias).reshape(grp * tq, kw)
            _softmax_step(s, scale, g, v_ref[0, pl.ds(c0, kw), gsl], m_ref, l_ref, acc_ref)
        return carry

    lax.fori_loop(0, nkb, att_block, 0)

    for h in range(A_HEADS):
        g, hh = divmod(h, grp)
        sl = slice(h * A_HEAD_DIM, (h + 1) * A_HEAD_DIM)
        o = acc_ref[g, hh * tq:(hh + 1) * tq, :] / _softmax_sum(l_ref[g, hh * tq:(hh + 1) * tq, :])
        gate = agate_ref[:, sl]
        o_ref[:, sl] = (o * (gate * jax.nn.sigmoid(gate))).astype(BF16)


def _dsa(aq, iq, ikw, z, dst, k, v, ik, bsz, t, tq, kw, l_valid, qpos0):
    lp = k.shape[1]
    assert lp < 4096 and lp % kw == 0 and t % tq == 0
    nt = t // tq
    nkb_max = lp // kw
    topk = min(TOPK_MAX, l_valid // 4)
    grp = A_HEADS // A_KV_HEADS
    rs = min(tq, ROW_STRIP)
    go, gw = dst["agate"]
    row = lambda b, i: (b * nt + i, 0)
    whole = lambda b, i: (b, 0)
    batch = lambda b, i: (b, 0, 0)
    kern = functools.partial(_dsa_kernel, nt=nt, tq=tq, kw=kw, l_valid=l_valid, qpos0=qpos0, topk=topk)
    return pl.pallas_call(
        kern,
        grid=(bsz, nt),
        in_specs=[pl.BlockSpec((tq, A_WIDTH), row),
                  pl.BlockSpec((t, IDX_HEADS * IDX_DIM), whole),
                  pl.BlockSpec((t, LANES), whole),
                  pl.BlockSpec((tq, gw), lambda b, i: (b * nt + i, go // gw)),
                  pl.BlockSpec((1, lp, A_KV_WIDTH), batch),
                  pl.BlockSpec((1, lp, A_KV_WIDTH), batch),
                  pl.BlockSpec((1, lp, IDX_DIM), batch)],
        out_specs=pl.BlockSpec((tq, A_WIDTH), row),
        out_shape=jax.ShapeDtypeStruct((bsz * t, A_WIDTH), BF16),
        scratch_shapes=[pltpu.VMEM((nt * nkb_max, tq, kw), F32),
                        pltpu.VMEM((t, LANES), jnp.int32),
                        pltpu.VMEM((t, LANES), jnp.int32),
                        pltpu.VMEM((t, LANES), F32),
                        pltpu.SMEM((1,), jnp.int32),
                        pltpu.VMEM((nkb_max, tq, kw), F32),
                        pltpu.VMEM((IDX_HEADS, tq, LANES), F32),
                        pltpu.VMEM((tq // rs, IDX_HEADS * rs, IDX_DIM), BF16),
                        pltpu.VMEM((A_KV_HEADS, grp * tq, A_HEAD_DIM), BF16),
                        pltpu.VMEM((A_KV_HEADS, grp * tq, A_HEAD_DIM), F32),
                        pltpu.VMEM((A_KV_HEADS, grp * tq, LANES), F32),
                        pltpu.VMEM((A_KV_HEADS, grp * tq, LANES), F32)],
        compiler_params=pltpu.CompilerParams(dimension_semantics=("arbitrary", "arbitrary"),
                                             vmem_limit_bytes=VMEM_LIMIT),
        name="dsa",
    )(aq, iq, ikw, z, k, v, ik)


SEL_BATCHES = 8
SEL_CHUNK = 512


def _dsa_sel_kernel(iq_ref, ikw_ref, cik_ref, nik_ref, bias_ref, key_ref, thr_ref, jcut_ref, need_ref,
                    *, gb, t, past, qpos0, topk):
    lp = past + LANES
    nc = lp // LANES

    def allowed_cols(col0, width):
        rows = lax.broadcasted_iota(jnp.int32, (t, width), 0)
        cols = lax.broadcasted_iota(jnp.int32, (t, width), 1) + col0
        return ((cols // CHUNK) <= (qpos0 + rows) // CHUNK) & (cols < past + t)

    def score_batch(j, carry):
        r0 = pl.multiple_of(j * t, t)
        qst = jnp.concatenate([iq_ref[pl.ds(r0, t), h * IDX_DIM:(h + 1) * IDX_DIM] for h in range(IDX_HEADS)], axis=0)
        iw = ikw_ref[pl.ds(r0, t), IDX_DIM:IDX_DIM + IDX_HEADS] * (IDX_DIM ** -0.5)
        iwb = [jnp.broadcast_to(iw[:, h:h + 1], (t, LANES)) for h in range(IDX_HEADS)]

        def head_sum(rel):
            width = rel.shape[1]
            sc = jnp.zeros((t, width), F32)
            for h in range(IDX_HEADS):
                sc = sc + jnp.maximum(rel[h * t:(h + 1) * t], 0.0) * _lane_tile(iwb[h], width)
            return sc

        for c in range(past // SEL_CHUNK):
            ikt = cik_ref[j, :, c * SEL_CHUNK:(c + 1) * SEL_CHUNK].astype(BF16)
            sc = head_sum(jnp.dot(qst, ikt, preferred_element_type=F32))
            key_ref[j, :, c * SEL_CHUNK:(c + 1) * SEL_CHUNK] = _masked_score(sc, allowed_cols(c * SEL_CHUNK, SEL_CHUNK))
        ik = jnp.concatenate([nik_ref[pl.ds(r0, t), :], jnp.zeros((LANES - t, IDX_DIM), BF16)], axis=0)
        sc = head_sum(lax.dot_general(qst, ik, _NT, preferred_element_type=F32))
        key_ref[j, :, past:] = _masked_score(sc, allowed_cols(past, LANES))
        return carry

    lax.fori_loop(0, gb, score_batch, 0)

    def piece(j, c):
        return lambda: key_ref[j, :, c * LANES:(c + 1) * LANES]

    strips = [(slice(j * t, (j + 1) * t), [(piece(j, c), c * LANES) for c in range(nc)]) for j in range(gb)]
    _topk_thresholds(strips, t, float(topk), thr_ref, jcut_ref, need_ref)

    def bias_batch(j, carry):
        r0 = pl.multiple_of(j * t, t)
        cols = lax.broadcasted_iota(jnp.int32, (t, lp), 1)
        bias_ref[j] = _selection_bias(key_ref[j], cols, _lane_tile(_key_to_f32(thr_ref[pl.ds(r0, t), :]), lp),
                                      _lane_tile(jcut_ref[pl.ds(r0, t), :], lp), allowed_cols(0, lp))
        return carry

    lax.fori_loop(0, gb, bias_batch, 0)


def _dsa_sel(iq, ikw, cache_ik_t, layer, new_ik, bsz, t, qpos0):
    past = cache_ik_t.shape[3]
    lp = past + LANES
    gb = next(g for g in (SEL_BATCHES, 4, 2, 1) if bsz % g == 0)
    assert past % SEL_CHUNK == 0 and t <= LANES and lp < 4096
    topk = min(TOPK_MAX, (past + t) // 4)
    rows = lambda i: (i, 0)
    kern = functools.partial(_dsa_sel_kernel, gb=gb, t=t, past=past, qpos0=qpos0, topk=topk)
    return pl.pallas_call(
        kern,
        grid=(bsz // gb,),
        in_specs=[pl.BlockSpec((gb * t, IDX_HEADS * IDX_DIM), rows),
                  pl.BlockSpec((gb * t, LANES), rows),
                  pl.BlockSpec((None, gb, IDX_DIM, past), lambda i: (layer, i, 0, 0)),
                  pl.BlockSpec((gb * t, IDX_DIM), rows)],
        out_specs=pl.BlockSpec((gb, t, lp), lambda i: (i, 0, 0)),
        out_shape=jax.ShapeDtypeStruct((bsz, t, lp), F32),
        scratch_shapes=[pltpu.VMEM((gb, t, lp), F32),
                        pltpu.VMEM((gb * t, LANES), jnp.int32),
                        pltpu.VMEM((gb * t, LANES), jnp.int32),
                        pltpu.VMEM((gb * t, LANES), F32)],
        compiler_params=pltpu.CompilerParams(dimension_semantics=("arbitrary",), vmem_limit_bytes=VMEM_LIMIT),
        name="dsa_sel",
    )(iq, ikw, cache_ik_t, new_ik)


def _dsa_att_kernel(aq_ref, agate_ref, bias_ref, ck_ref, cv_ref, nk_ref, nv_ref, o_ref, *, t, past):
    grp = A_HEADS // A_KV_HEADS
    scale = A_HEAD_DIM ** -0.5
    pad = jnp.zeros((LANES - t, A_HEAD_DIM), BF16)
    b1, b2 = bias_ref[0, :, :past], bias_ref[0, :, past:]
    for g in range(A_KV_HEADS):
        gsl = slice(g * A_HEAD_DIM, (g + 1) * A_HEAD_DIM)
        kc = ck_ref[0, pl.ds(g, past, stride=A_KV_HEADS), :].astype(BF16)
        vc = cv_ref[0, pl.ds(g, past, stride=A_KV_HEADS), :].astype(BF16)
        kn = jnp.concatenate([nk_ref[:, gsl], pad], axis=0)
        vn = jnp.concatenate([nv_ref[:, gsl], pad], axis=0)
        qs = jnp.concatenate([aq_ref[:, (g * grp + hh) * A_HEAD_DIM:(g * grp + hh + 1) * A_HEAD_DIM]
                              for hh in range(grp)], axis=0)

        def scores(keys, bias):
            s = lax.dot_general(qs, keys, _NT, preferred_element_type=F32) * scale
            return (s.reshape(grp, t, s.shape[1]) + bias[None]).reshape(grp * t, s.shape[1])

        s1, s2 = scores(kc, b1), scores(kn, b2)
        m = jnp.maximum(jnp.max(s1, axis=-1, keepdims=True), jnp.max(s2, axis=-1, keepdims=True))
        e1, e2 = jnp.exp(s1 - m), jnp.exp(s2 - m)
        l = jnp.sum(e1, axis=-1, keepdims=True) + jnp.sum(e2, axis=-1, keepdims=True)
        o = (jnp.dot(e1.astype(BF16), vc, preferred_element_type=F32)
             + jnp.dot(e2.astype(BF16), vn, preferred_element_type=F32)) / l
        for hh in range(grp):
            sl = slice((g * grp + hh) * A_HEAD_DIM, (g * grp + hh + 1) * A_HEAD_DIM)
            gate = agate_ref[:, sl]
            o_ref[:, sl] = (o[hh * t:(hh + 1) * t] * (gate * jax.nn.sigmoid(gate))).astype(BF16)


def _dsa_att(aq, z, dst, bias, cache_k, cache_v, new_k, new_v, bsz, t):
    past = cache_k.shape[1] // A_KV_HEADS
    lp = bias.shape[2]
    go, gw = dst["agate"]
    row = lambda b: (b, 0)
    batch = lambda b: (b, 0, 0)
    return pl.pallas_call(
        functools.partial(_dsa_att_kernel, t=t, past=past),
        grid=(bsz,),
        in_specs=[pl.BlockSpec((t, A_WIDTH), row),
                  pl.BlockSpec((t, gw), lambda b: (b, go // gw)),
                  pl.BlockSpec((1, t, lp), batch),
                  pl.BlockSpec((1, past * A_KV_HEADS, A_HEAD_DIM), batch),
                  pl.BlockSpec((1, past * A_KV_HEADS, A_HEAD_DIM), batch),
                  pl.BlockSpec((t, A_KV_WIDTH), row),
                  pl.BlockSpec((t, A_KV_WIDTH), row)],
        out_specs=pl.BlockSpec((t, A_WIDTH), row),
        out_shape=jax.ShapeDtypeStruct((bsz * t, A_WIDTH), BF16),
        compiler_params=pltpu.CompilerParams(dimension_semantics=("arbitrary",), vmem_limit_bytes=VMEM_LIMIT),
        name="dsa_att",
    )(aq, z, bias, cache_k, cache_v, new_k, new_v)


def _mla_kernel(q_ref, bgate_ref, k_ref, v_ref, o_ref, acc_ref, m_ref, l_ref, *, tq, kw, l_valid, qpos0):
    ti = pl.program_id(1)
    nkb = _num_key_blocks(ti, tq, kw, l_valid, qpos0)
    scale = (NOPE_DIM + ROPE_DIM) ** -0.5
    _softmax_init(B_HEADS, tq, m_ref, l_ref, acc_ref)

    def att_block(kb, masked, row0=0):
        r0 = pl.multiple_of(kb * kw, kw)
        rows = slice(row0, tq)
        if masked:
            allowed, _ = _allowed(ti, kb, row0, (tq - row0, kw), tq, kw, l_valid, qpos0)
            bias = jnp.where(allowed, 0.0, NEG)
        for h in range(B_HEADS):
            qsl = slice(h * B_QK_PAD, (h + 1) * B_QK_PAD)
            s = lax.dot_general(q_ref[rows, qsl], k_ref[0, pl.ds(r0, kw), qsl], _NT, preferred_element_type=F32)
            if masked:
                s = s + bias
            _softmax_step(s, scale, h, v_ref[0, pl.ds(r0, kw), h * V_DIM:(h + 1) * V_DIM], m_ref, l_ref, acc_ref, rows)

    n_full = _fully_visible_blocks(ti, tq, kw, l_valid, qpos0)
    lax.fori_loop(0, n_full, lambda kb, c: (att_block(kb, False), c)[1], 0)
    if qpos0 % kw == 0 and tq % kw == 0 and l_valid % tq == 0:
        for rel in range(tq // kw):
            att_block(n_full + rel, True, row0=rel * kw)
    else:
        lax.fori_loop(n_full, nkb, lambda kb, c: (att_block(kb, True), c)[1], 0)

    for h in range(B_HEADS):
        sl = slice(h * V_DIM, (h + 1) * V_DIM)
        o = acc_ref[h] / _softmax_sum(l_ref[h])
        gate = bgate_ref[:, sl]
        o_ref[:, sl] = (o * (gate * jax.nn.sigmoid(gate))).astype(BF16)


def _mla(qcat, z, dst, kcat, v, bsz, t, tq, kw, l_valid, qpos0):
    lp = kcat.shape[1]
    assert lp % kw == 0 and t % tq == 0
    nt = t // tq
    go, gw = dst["bgate"]
    row = lambda b, i: (b * nt + i, 0)
    batch = lambda b, i: (b, 0, 0)
    kern = functools.partial(_mla_kernel, tq=tq, kw=kw, l_valid=l_valid, qpos0=qpos0)
    return pl.pallas_call(
        kern,
        grid=(bsz, nt),
        in_specs=[pl.BlockSpec((tq, B_HEADS * B_QK_PAD), row),
                  pl.BlockSpec((tq, gw), lambda b, i: (b * nt + i, go // gw)),
                  pl.BlockSpec((1, lp, B_HEADS * B_QK_PAD), batch),
                  pl.BlockSpec((1, lp, B_WIDTH), batch)],
        out_specs=pl.BlockSpec((tq, B_WIDTH), row),
        out_shape=jax.ShapeDtypeStruct((bsz * t, B_WIDTH), BF16),
        scratch_shapes=[pltpu.VMEM((B_HEADS, tq, V_DIM), F32),
                        pltpu.VMEM((B_HEADS, tq, LANES), F32),
                        pltpu.VMEM((B_HEADS, tq, LANES), F32)],
        compiler_params=pltpu.CompilerParams(dimension_semantics=("arbitrary", "arbitrary"),
                                             vmem_limit_bytes=VMEM_LIMIT),
        name="mla",
    )(qcat, z, kcat, v)


def _mla_lat_kernel(q_ref, bgate_ref, cc_ref, ckr_ref, nc_ref, nkr_ref, wukt_ref, wuv_ref, o_ref,
                    *, t, past, qpos0):
    rows_n = B_HEADS * t
    ql, qr = [], []
    for h in range(B_HEADS):
        qn = q_ref[:, h * B_QK_PAD: h * B_QK_PAD + NOPE_DIM]
        ql.append(jnp.dot(qn, wukt_ref[h], preferred_element_type=F32).astype(BF16))
        qr.append(q_ref[:, h * B_QK_PAD + NOPE_DIM: h * B_QK_PAD + NOPE_DIM + ROPE_DIM])
    ql = jnp.concatenate(ql, axis=0)
    qr = jnp.concatenate(qr, axis=0)
    cc = cc_ref[0].astype(BF16)
    ckr_t = ckr_ref[0].astype(BF16)
    nc, nkr = nc_ref[...], nkr_ref[...]
    scale = (NOPE_DIM + ROPE_DIM) ** -0.5

    def scores(lat, rope_part, ncols, col0):
        s = (lax.dot_general(ql, lat, _NT, preferred_element_type=F32) + rope_part) * scale
        rows = lax.broadcasted_iota(jnp.int32, (rows_n, ncols), 0)
        cols = lax.broadcasted_iota(jnp.int32, (rows_n, ncols), 1) + col0
        qchunk = (qpos0 + lax.rem(rows, t)) // CHUNK
        return jnp.where((cols // CHUNK) <= qchunk, s, NEG)

    s1 = scores(cc, jnp.dot(qr, ckr_t, preferred_element_type=F32), past, 0)
    s2 = scores(nc, lax.dot_general(qr, nkr, _NT, preferred_element_type=F32), t, past)
    m = jnp.maximum(jnp.max(s1, axis=-1, keepdims=True), jnp.max(s2, axis=-1, keepdims=True))
    e1, e2 = jnp.exp(s1 - m), jnp.exp(s2 - m)
    l = jnp.sum(e1, axis=-1, keepdims=True) + jnp.sum(e2, axis=-1, keepdims=True)
    ol = (jnp.dot(e1.astype(BF16), cc, preferred_element_type=F32)
          + jnp.dot(e2.astype(BF16), nc, preferred_element_type=F32)) / l
    ol = ol.astype(BF16)
    for h in range(B_HEADS):
        sl = slice(h * V_DIM, (h + 1) * V_DIM)
        o = jnp.dot(ol[h * t:(h + 1) * t], wuv_ref[:, sl], preferred_element_type=F32)
        gate = bgate_ref[:, sl]
        o_ref[:, sl] = (o * (gate * jax.nn.sigmoid(gate))).astype(BF16)


def _mla_lat(qcat, z, dst, cache_ckv, cache_kr, layer, new_ckv, new_kr, wukt, wuv, bsz, t, qpos0):
    past = cache_ckv.shape[2]
    go, gw = dst["bgate"]
    row = lambda b: (b, 0)
    batch = lambda b: (b, 0, 0)
    const = lambda a: pl.BlockSpec(a.shape, lambda b: (0,) * a.ndim)
    kern = functools.partial(_mla_lat_kernel, t=t, past=past, qpos0=qpos0)
    return pl.pallas_call(
        kern,
        grid=(bsz,),
        in_specs=[pl.BlockSpec((t, B_HEADS * B_QK_PAD), row),
                  pl.BlockSpec((t, gw), lambda b: (b, go // gw)),
                  pl.BlockSpec((None, 1, past, KV_LORA), lambda b: (layer, b, 0, 0)),
                  pl.BlockSpec((None, 1, ROPE_DIM, past), lambda b: (layer, b, 0, 0)),
                  pl.BlockSpec((t, KV_LORA), row),
                  pl.BlockSpec((t, ROPE_DIM), row),
                  const(wukt), const(wuv)],
        out_specs=pl.BlockSpec((t, B_WIDTH), row),
        out_shape=jax.ShapeDtypeStruct((bsz * t, B_WIDTH), BF16),
        compiler_params=pltpu.CompilerParams(dimension_semantics=("arbitrary",), vmem_limit_bytes=VMEM_LIMIT),
        name="mla_lat",
    )(qcat, z, cache_ckv, cache_kr, new_ckv, new_kr, wukt, wuv)


def _out_kernel(ga_ref, gb_ref, mg_ref, x_ref, woa_ref, wob_ref, wout_ref, fg_ref, o_ref, *, d, final):
    ya = jnp.dot(ga_ref[...], woa_ref[...], preferred_element_type=F32)
    yb = jnp.dot(gb_ref[...], wob_ref[...], preferred_element_type=F32)
    m = jax.nn.sigmoid(mg_ref[:, :d]) * ya + jax.nn.sigmoid(mg_ref[:, d:]) * yb
    h = x_ref[...] + jnp.dot(m.astype(BF16), wout_ref[...], preferred_element_type=F32)
    o_ref[...] = _rms(h, fg_ref[...]) if final else h


def _out(ga, gb, z, dst, x, woa, wob, wout, fg, final):
    m, d = x.shape
    tm = min(m, 256)
    assert m % tm == 0 and dst["mg"][0] == 0
    row = lambda i: (i, 0)
    const = lambda a: pl.BlockSpec(a.shape, lambda i: (0, 0), pipeline_mode=pl.Buffered(1))
    return pl.pallas_call(
        functools.partial(_out_kernel, d=d, final=final),
        grid=(m // tm,),
        in_specs=[pl.BlockSpec((tm, A_WIDTH), row), pl.BlockSpec((tm, B_WIDTH), row),
                  pl.BlockSpec((tm, 2 * d), row), pl.BlockSpec((tm, d), row),
                  const(woa), const(wob), const(wout), const(fg)],
        out_specs=pl.BlockSpec((tm, d), row),
        out_shape=jax.ShapeDtypeStruct((m, d), F32),
        compiler_params=pltpu.CompilerParams(dimension_semantics=("arbitrary",), vmem_limit_bytes=VMEM_LIMIT),
        name="out",
    )(ga, gb, z, x, woa, wob, wout, fg)


def _all_keys(cache, new, bsz, t, lp):
    new = new.reshape(bsz, t, new.shape[-1])
    if cache is None and lp == t:
        return new
    parts = ([] if cache is None else [cache.reshape(bsz, cache.shape[1], -1).astype(BF16)]) + [new]
    kk = jnp.concatenate(parts, axis=1)
    return jnp.pad(kk, ((0, 0), (0, lp - kk.shape[1]), (0, 0)))


def _trunk(x, caches, final_g, weights):
    bsz, t, d = x.shape
    depth = weights["w_in"].shape[0]
    past = 0 if caches is None else caches[0].shape[2]
    l_valid = past + t
    tq = min(t, QUERY_TILE)
    kw = KEY_BLOCK if t >= QUERY_TILE else -(-l_valid // LANES) * LANES
    lp = -(-l_valid // kw) * kw
    _, dst, n_proj = _layout(d)
    pos = past + jnp.arange(t, dtype=jnp.int32)
    t_scale = jnp.concatenate([jnp.full((t, IDX_HEADS), IDX_HEADS ** -0.5, F32),
                               jnp.ones((t, LANES - IDX_DIM - IDX_HEADS), F32)], axis=1)
    tab_a = jnp.stack(_rope_tables(pos, A_ROT, A_HEAD_DIM, 1))
    tab_i = jnp.stack(_rope_tables(pos, IDX_ROT, IDX_DIM, 2))
    tabs = [tab_a, jnp.stack(_rope_tables(pos, IDX_ROT, IDX_DIM, 1, extra=t_scale)),
            jnp.stack(_rope_tables(pos, ROPE_DIM, ROPE_DIM, 2))]
    reps = max(1, min(bsz * t, PROJ_ROWS) // t)
    proj_tabs = [jnp.tile(a, (1, reps, 1)) for a in (tab_a, tab_i)]
    h = x.reshape(bsz * t, d)
    new_rows = []
    for l in range(depth):
        w = {k: v[l] for k, v in weights.items()}
        z, aq, iq = _proj(h, w["norm_g"], w["w_in"], dst, n_proj, *proj_tabs)
        ak, av, ikw, krp, ckv, qcat, kb, vb, ikb, ckvb, krb = _prep(
            z, dst, bsz, t, tabs, w["q_norm_g"], w["kv_norm_g"], w["w_uq"])
        c = (None,) * 5 if caches is None else tuple(cc[l] for cc in caches)
        if caches is not None and t < QUERY_TILE:
            bias = _dsa_sel(iq, ikw, jnp.swapaxes(caches[2], 2, 3), l, ikb, bsz, t, past)
            interleaved = lambda a: a.reshape(bsz, past * A_KV_HEADS, A_HEAD_DIM)
            ga = _dsa_att(aq, z, dst, bias, interleaved(c[0]), interleaved(c[1]), kb, vb, bsz, t)
            gb = _mla_lat(qcat, z, dst, caches[3], jnp.swapaxes(caches[4], 2, 3), l, ckvb, krb, w["w_ukt"], w["w_uv"], bsz, t, past)
        else:
            k_all = _all_keys(c[0], kb, bsz, t, lp)
            v_all = _all_keys(c[1], vb, bsz, t, lp)
            ik_all = _all_keys(c[2], ikb, bsz, t, lp)
            ga = _dsa(aq, iq, ikw, z, dst, k_all, v_all, ik_all, bsz, t, tq, kw, l_valid, past)
            ckv_all = _all_keys(c[3], ckvb, bsz, t, lp)
            kr_all = _all_keys(c[4], krb, bsz, t, lp)
            kcat, vv = _kvup(ckv_all.reshape(bsz * lp, KV_LORA), kr_all.reshape(bsz * lp, ROPE_DIM), w["w_ukv"])
            tq_mla = MLA_QUERY_TILE if t % MLA_QUERY_TILE == 0 else tq
            gb = _mla(qcat, z, dst, kcat.reshape(bsz, lp, -1), vv.reshape(bsz, lp, -1), bsz, t, tq_mla, kw, l_valid, past)
        h = _out(ga, gb, z, dst, h, w["w_o_a"], w["w_o_b"], w["w_out"], final_g, final=(l == depth - 1))
        new_rows.append((ak.reshape(bsz, t, A_KV_HEADS, A_HEAD_DIM), av.reshape(bsz, t, A_KV_HEADS, A_HEAD_DIM),
                         ikw[:, :IDX_DIM].reshape(bsz, t, IDX_DIM), ckv.reshape(bsz, t, KV_LORA),
                         krp[:, :ROPE_DIM].reshape(bsz, t, ROPE_DIM)))
    stacked = [jnp.stack([r[i] for r in new_rows], axis=0) for i in range(5)]
    return h.reshape(bsz, t, d), stacked


def kernel(x_prompt, x_sample, cache_a_k, cache_a_v, cache_idx_k, cache_mla_ckv, cache_mla_krope,
           norm_g, w_in, w_uq, q_norm_g, kv_norm_g, w_uk, w_uv, w_o_a, w_o_b, w_out, final_g):
    depth, d, _ = w_in.shape
    weights = {
        "norm_g": norm_g.reshape(depth, 1, d),
        "w_in": jnp.stack([_regroup_w_in(jnp.swapaxes(w_in[l], 0, 1), d) for l in range(depth)]),
        "w_uq": jnp.concatenate([w_uq[..., :NOPE_DIM].reshape(depth, Q_LORA, B_HEADS * NOPE_DIM),
                                 w_uq[..., NOPE_DIM:].reshape(depth, Q_LORA, B_HEADS * ROPE_DIM)], axis=-1).astype(BF16),
        "q_norm_g": q_norm_g.reshape(depth, 1, Q_LORA),
        "kv_norm_g": kv_norm_g.reshape(depth, 1, KV_LORA),
        "w_ukv": jnp.concatenate([w_uk.reshape(depth, KV_LORA, B_HEADS * NOPE_DIM),
                                  w_uv.reshape(depth, KV_LORA, B_WIDTH)], axis=-1).astype(BF16),
        "w_ukt": jnp.transpose(w_uk, (0, 2, 3, 1)).astype(BF16),
        "w_uv": w_uv.reshape(depth, KV_LORA, B_WIDTH).astype(BF16),
        "w_o_a": w_o_a.astype(BF16), "w_o_b": w_o_b.astype(BF16), "w_out": w_out.astype(BF16),
    }
    fg = final_g.reshape(1, d)
    y_p, (pk, pv, pik, pckv, pkr) = _trunk(x_prompt, None, fg, weights)
    caches = (cache_a_k, cache_a_v, cache_idx_k, cache_mla_ckv, cache_mla_krope)
    y_s, (sk, sv, sik, sckv, skr) = _trunk(x_sample, caches, fg, weights)
    return (y_p, y_s, pk, pv, pik, pckv, pkr, sk, sv, sik, sckv, skr)
```

```python
import functools

import jax
import jax.numpy as jnp
from jax import lax
from jax.experimental import pallas as pl
from jax.experimental.pallas import tpu as pltpu

F32 = jnp.float32
BF16 = jnp.bfloat16

CHUNK = 64
ROPE_THETA = 500000.0
EPS = 1e-6

A_HEADS = 8
A_KV_HEADS = 2
A_HEAD_DIM = 128
A_ROT = A_HEAD_DIM // 4
A_WIDTH = A_HEADS * A_HEAD_DIM
A_KV_WIDTH = A_KV_HEADS * A_HEAD_DIM
IDX_HEADS = 16
IDX_DIM = 64
IDX_ROT = IDX_DIM // 4
TOPK_MAX = 256

B_HEADS = 8
Q_LORA = 512
KV_LORA = 512
NOPE_DIM = 128
ROPE_DIM = 64
V_DIM = 128
B_WIDTH = B_HEADS * V_DIM
B_QK_PAD = 256

LANES = 128
KEY_BLOCK = 256
QUERY_TILE = 256
PROJ_ROWS = 1024
PREP_ROWS = 256
PROJ_TILE = 1024
MLA_QUERY_TILE = 512
ROW_STRIP = 64
INT_MIN = -(2 ** 31)
LOG2_E = 1.4426950408889634
NEG = -1e30
MASKED = -1e38
VMEM_LIMIT = 56 * 1024 * 1024

_SEGS = (("aq", A_WIDTH), ("ak", A_KV_WIDTH), ("av", A_KV_WIDTH), ("agate", A_WIDTH),
         ("iq", IDX_HEADS * IDX_DIM), ("ik", IDX_DIM), ("iw", IDX_HEADS),
         ("cq", Q_LORA), ("ckv", KV_LORA), ("kr", ROPE_DIM), ("bgate", B_WIDTH), ("mg", None))


def _layout(d_model):
    src, off = {}, 0
    for name, w in _SEGS:
        w = 2 * d_model if w is None else w
        src[name] = (off, w)
        off += w
    order = (("mg", 2 * d_model), ("aq", A_WIDTH), ("agate", A_WIDTH), ("iq", IDX_HEADS * IDX_DIM),
             ("bgate", B_WIDTH), ("cq", Q_LORA), ("ckv", KV_LORA), ("ak", A_KV_WIDTH), ("av", A_KV_WIDTH),
             ("ikw", LANES), ("krp", LANES))
    dst, off = {}, 0
    for name, w in order:
        assert off % w == 0
        dst[name] = (off, w)
        off += w
    return src, dst, off


def _regroup_kernel(w_ref, o_ref, *, src, dst):
    cols = w_ref.shape[1]

    def piece(name):
        o, w = src[name]
        return w_ref[o:o + w, :].astype(BF16)

    def zeros(n):
        return jnp.zeros((n, cols), BF16)

    for name, (o, w) in dst.items():
        if name == "ikw":
            val = jnp.concatenate([piece("ik"), piece("iw"), zeros(LANES - IDX_DIM - IDX_HEADS)], axis=0)
        elif name == "krp":
            val = jnp.concatenate([piece("kr"), zeros(LANES - ROPE_DIM)], axis=0)
        else:
            val = piece(name)
        o_ref[o:o + w, :] = val
    total = max(o + w for o, w in dst.values())
    if o_ref.shape[0] > total:
        o_ref[total:, :] = zeros(o_ref.shape[0] - total)


def _regroup_w_in(w_in_t, d_model):
    src, dst, total = _layout(d_model)
    tc = 256
    padded = -(-total // PROJ_TILE) * PROJ_TILE
    assert d_model % tc == 0
    return pl.pallas_call(
        functools.partial(_regroup_kernel, src=src, dst=dst),
        grid=(d_model // tc,),
        in_specs=[pl.BlockSpec((w_in_t.shape[0], tc), lambda i: (0, i))],
        out_specs=pl.BlockSpec((padded, tc), lambda i: (0, i)),
        out_shape=jax.ShapeDtypeStruct((padded, d_model), BF16),
        compiler_params=pltpu.CompilerParams(dimension_semantics=("arbitrary",), vmem_limit_bytes=VMEM_LIMIT),
        name="regroup",
    )(w_in_t)


def _rope_tables(pos, rot, head, heads_per_vreg, extra=None):
    half = rot // 2
    inv_freq = ROPE_THETA ** (-jnp.arange(half, dtype=F32) * (2.0 / rot))
    ang = pos.astype(F32)[:, None] * inv_freq[None, :]
    cos, sin = jnp.cos(ang), jnp.sin(ang)
    t = pos.shape[0]
    one = jnp.ones((t, head - rot), F32)
    zero = lambda n: jnp.zeros((t, n), F32)
    c = jnp.concatenate([cos, cos, one], axis=1)
    sa = jnp.concatenate([-sin, zero(head - half)], axis=1)
    sb = jnp.concatenate([zero(half), sin, zero(head - rot)], axis=1)
    c, sa, sb = (jnp.tile(a, (1, heads_per_vreg)) for a in (c, sa, sb))
    if extra is not None:
        c = jnp.concatenate([c, extra], axis=1)
        sa = jnp.concatenate([sa, jnp.zeros_like(extra)], axis=1)
        sb = jnp.concatenate([sb, jnp.zeros_like(extra)], axis=1)
    assert c.shape[1] == LANES
    return c, sa, sb


def _rope(x, c, sa, sb, half):
    up = pltpu.roll(x, LANES - half, 1)
    dn = pltpu.roll(x, half, 1)
    return x * c + up * sa + dn * sb


def _rms(x, g):
    ms = jnp.mean(x * x, axis=-1, keepdims=True)
    return (x * lax.rsqrt(ms + EPS)) * g


def _lane_tile(x, width):
    return x if width == LANES else jnp.concatenate([x] * (width // LANES), axis=1)


_NT = (((1,), (1,)), ((), ()))


def _proj_kernel(x_ref, g_ref, w_ref, ta_ref, ti_ref, z_ref, aq_ref, iq_ref, xn_ref, hold_ref, *, j_aq, j_iq):
    j = pl.program_id(1)

    @pl.when(j == 0)
    def _():
        xn_ref[...] = _rms(x_ref[...], g_ref[...]).astype(BF16)

    def tile():
        return lax.dot_general(xn_ref[...], w_ref[...], _NT, preferred_element_type=F32)

    def rope_held(tab_ref, out_ref, half):
        c, sa, sb = tab_ref[0], tab_ref[1], tab_ref[2]
        for v in range(hold_ref.shape[1] // LANES):
            sl = slice(v * LANES, (v + 1) * LANES)
            out_ref[:, sl] = _rope(hold_ref[:, sl], c, sa, sb, half).astype(BF16)

    held = (j == j_aq) | (j == j_iq)
    after_aq = j == j_aq + 1
    after_iq = j == j_iq + 1

    @pl.when(held)
    def _():
        r = tile()
        z_ref[...] = r
        hold_ref[...] = r

    @pl.when(after_aq)
    def _():
        z_ref[...] = tile()
        rope_held(ta_ref, aq_ref, A_ROT // 2)

    @pl.when(after_iq)
    def _():
        z_ref[...] = tile()
        rope_held(ti_ref, iq_ref, IDX_ROT // 2)

    @pl.when(jnp.logical_not(held | after_aq | after_iq))
    def _():
        z_ref[...] = tile()


def _proj(x, g, w, dst, n, tab_a, tab_i):
    m, d = x.shape
    tm = min(m, PROJ_ROWS)
    tn = PROJ_TILE
    assert m % tm == 0 and w.shape[0] % tn == 0 and dst["aq"][1] == tn and dst["iq"][1] == tn
    j_aq, j_iq = dst["aq"][0] // tn, dst["iq"][0] // tn
    assert dst["aq"][0] % tn == 0 and dst["iq"][0] % tn == 0 and j_iq > j_aq + 1 and (j_iq + 1) * tn < n
    ntt = tab_a.shape[1] // tm
    tspec = pl.BlockSpec((3, tm, LANES), lambda i, j: (0, i % ntt, 0), pipeline_mode=pl.Buffered(1))
    return pl.pallas_call(
        functools.partial(_proj_kernel, j_aq=j_aq, j_iq=j_iq),
        grid=(m // tm, w.shape[0] // tn),
        in_specs=[pl.BlockSpec((tm, d), lambda i, j: (i, 0)),
                  pl.BlockSpec((1, d), lambda i, j: (0, 0)),
                  pl.BlockSpec((tn, d), lambda i, j: (j, 0)),
                  tspec, tspec],
        out_specs=[pl.BlockSpec((tm, tn), lambda i, j: (i, j)),
                   pl.BlockSpec((tm, tn), lambda i, j: (i, 0)),
                   pl.BlockSpec((tm, tn), lambda i, j: (i, 0))],
        out_shape=[jax.ShapeDtypeStruct((m, n), F32), jax.ShapeDtypeStruct((m, tn), BF16),
                   jax.ShapeDtypeStruct((m, tn), BF16)],
        scratch_shapes=[pltpu.VMEM((tm, d), BF16), pltpu.VMEM((tm, tn), F32)],
        compiler_params=pltpu.CompilerParams(dimension_semantics=("arbitrary", "arbitrary"),
                                             vmem_limit_bytes=VMEM_LIMIT),
        name="proj",
    )(x, g, w, tab_a, tab_i)


def _prep_kernel(cq_ref, ckv_ref, ak_ref, av_ref, ikw_ref, krp_ref,
                 ta_ref, tk_ref, tr_ref, qg_ref, kvg_ref, wuq_ref,
                 ak_o, av_o, ikw_o, krp_o, ckv_o, qcat_o, kb_o, vb_o, ikb_o, ckvb_o, krb_o):
    ca, saa, sba = ta_ref[0], ta_ref[1], ta_ref[2]
    ck, sak, sbk = tk_ref[0], tk_ref[1], tk_ref[2]
    cr, sar, sbr = tr_ref[0], tr_ref[1], tr_ref[2]
    tm = cq_ref.shape[0]
    for h in range(A_KV_HEADS):
        sl = slice(h * LANES, (h + 1) * LANES)
        ak = _rope(ak_ref[:, sl], ca, saa, sba, A_ROT // 2)
        ak_o[pl.ds(h, tm, stride=A_KV_HEADS), :] = ak
        av_o[pl.ds(h, tm, stride=A_KV_HEADS), :] = av_ref[:, sl]
        kb_o[:, sl] = ak.astype(BF16)
    vb_o[...] = av_ref[...].astype(BF16)
    ikw = _rope(ikw_ref[...], ck, sak, sbk, IDX_ROT // 2)
    ikw_o[...] = ikw
    ikb_o[...] = ikw[:, :IDX_DIM].astype(BF16)
    krp = _rope(krp_ref[...], cr, sar, sbr, ROPE_DIM // 2)
    krp_o[...] = krp
    krb_o[...] = krp[:, :ROPE_DIM].astype(BF16)
    ckv = _rms(ckv_ref[...], kvg_ref[...])
    ckv_o[...] = ckv
    ckvb_o[...] = ckv.astype(BF16)
    cqn = _rms(cq_ref[...], qg_ref[...]).astype(BF16)
    qb = jnp.dot(cqn, wuq_ref[...], preferred_element_type=F32)
    nope = B_HEADS * NOPE_DIM
    low = lax.broadcasted_iota(jnp.int32, (tm, LANES), 1) < ROPE_DIM
    for v in range(B_HEADS * ROPE_DIM // LANES):
        r = _rope(qb[:, nope + v * LANES: nope + (v + 1) * LANES], cr, sar, sbr, ROPE_DIM // 2)
        halves = (jnp.where(low, r, 0.0), jnp.where(low, pltpu.roll(r, ROPE_DIM, 1), 0.0))
        for j in range(2):
            h = 2 * v + j
            qcat_o[:, h * B_QK_PAD: h * B_QK_PAD + NOPE_DIM] = qb[:, h * NOPE_DIM:(h + 1) * NOPE_DIM].astype(BF16)
            qcat_o[:, h * B_QK_PAD + NOPE_DIM:(h + 1) * B_QK_PAD] = halves[j].astype(BF16)


def _prep(z, dst, bsz, t, tabs, q_norm_g, kv_norm_g, wuq):
    m = bsz * t
    if t < PREP_ROWS and m % (PREP_ROWS // t * t) == 0:
        reps = PREP_ROWS // t
        tabs = [jnp.tile(a, (1, reps, 1)) for a in tabs]
        bsz, t = m // (reps * t), reps * t
    tm = min(t, PREP_ROWS)
    nt = t // tm
    assert t % tm == 0

    def zspec(name):
        o, w = dst[name]
        return pl.BlockSpec((tm, w), lambda b, i, blk=o // w: (b * nt + i, blk))

    def tspec():
        return pl.BlockSpec((3, tm, LANES), lambda b, i: (0, i, 0))

    def ospec(w):
        return pl.BlockSpec((tm, w), lambda b, i: (b * nt + i, 0))

    def full(a):
        return pl.BlockSpec(a.shape, lambda b, i: (0,) * a.ndim)

    outs = [("interleaved", F32), ("interleaved", F32), (LANES, F32), (LANES, F32),
            (KV_LORA, F32), (B_HEADS * B_QK_PAD, BF16),
            (A_KV_WIDTH, BF16), (A_KV_WIDTH, BF16), (IDX_DIM, BF16), (KV_LORA, BF16), (ROPE_DIM, BF16)]
    return pl.pallas_call(
        _prep_kernel,
        grid=(bsz, nt),
        in_specs=[zspec("cq"), zspec("ckv"), zspec("ak"), zspec("av"), zspec("ikw"),
                  zspec("krp"), tspec(), tspec(), tspec(), full(q_norm_g), full(kv_norm_g), full(wuq)],
        out_specs=[pl.BlockSpec((tm * A_KV_HEADS, A_HEAD_DIM), lambda b, i: (b * nt + i, 0)) if w == "interleaved" else ospec(w)
                   for w, _ in outs],
        out_shape=[jax.ShapeDtypeStruct((m * A_KV_HEADS, A_HEAD_DIM) if w == "interleaved" else (m, w), dt) for w, dt in outs],
        compiler_params=pltpu.CompilerParams(dimension_semantics=("arbitrary", "arbitrary"),
                                             vmem_limit_bytes=VMEM_LIMIT),
        name="prep",
    )(z, z, z, z, z, z, *tabs, q_norm_g, kv_norm_g, wuq)


def _kvup_kernel(c_ref, kr_ref, w_ref, kcat_ref, v_ref):
    r = jnp.dot(c_ref[...], w_ref[...], preferred_element_type=F32)
    n = B_HEADS * NOPE_DIM
    tail = jnp.concatenate([kr_ref[...], jnp.zeros((kr_ref.shape[0], B_QK_PAD - NOPE_DIM - ROPE_DIM), BF16)], axis=1)
    for h in range(B_HEADS):
        kcat_ref[:, h * B_QK_PAD: h * B_QK_PAD + NOPE_DIM] = r[:, h * NOPE_DIM:(h + 1) * NOPE_DIM].astype(BF16)
        kcat_ref[:, h * B_QK_PAD + NOPE_DIM:(h + 1) * B_QK_PAD] = tail
    v_ref[...] = r[:, n:].astype(BF16)


def _kvup(ckv, kr, w):
    m, c = ckv.shape
    tm = next(t for t in (512, 384, 256, 128) if m % t == 0)
    row = lambda i: (i, 0)
    return pl.pallas_call(
        _kvup_kernel,
        grid=(m // tm,),
        in_specs=[pl.BlockSpec((tm, c), row), pl.BlockSpec((tm, ROPE_DIM), row), pl.BlockSpec(w.shape, lambda i: (0, 0))],
        out_specs=[pl.BlockSpec((tm, B_HEADS * B_QK_PAD), row), pl.BlockSpec((tm, B_WIDTH), row)],
        out_shape=[jax.ShapeDtypeStruct((m, B_HEADS * B_QK_PAD), BF16), jax.ShapeDtypeStruct((m, B_WIDTH), BF16)],
        compiler_params=pltpu.CompilerParams(dimension_semantics=("arbitrary",), vmem_limit_bytes=VMEM_LIMIT),
        name="kv_up",
    )(ckv, kr, w)


def _visible_cols(tile, tq, l_valid, qpos0):
    p_last = qpos0 + (tile + 1) * tq - 1
    n = (p_last // CHUNK + 1) * CHUNK
    return min(n, l_valid) if isinstance(n, int) else jnp.minimum(n, l_valid)


def _num_key_blocks(tile, tq, kw, l_valid, qpos0):
    return (_visible_cols(tile, tq, l_valid, qpos0) + kw - 1) // kw


def _fully_visible_blocks(tile, tq, kw, l_valid, qpos0):
    first_row_cols = jnp.minimum(((qpos0 + tile * tq) // CHUNK + 1) * CHUNK, l_valid)
    return first_row_cols // kw


def _allowed(tile, kb, rows0, shape, tq, kw, l_valid, qpos0):
    rows = lax.broadcasted_iota(jnp.int32, shape, 0) + rows0
    cols = lax.broadcasted_iota(jnp.int32, shape, 1) + kb * kw
    qchunk = (qpos0 + tile * tq + rows) // CHUNK
    return ((cols // CHUNK) <= qchunk) & (cols < l_valid), cols


def _softmax_step(s, scale, h, vblk, m_ref, l_ref, acc_ref, rows=slice(None)):
    kw = s.shape[1]
    c = scale * LOG2_E
    m_prev = m_ref[h, rows]
    m_new = jnp.maximum(m_prev, jnp.max(s, axis=-1, keepdims=True))
    alpha = jnp.exp2((m_prev - m_new) * c)
    p = jnp.exp2((s - _lane_tile(m_new, kw)) * c)
    l_ref[h, rows] = alpha * l_ref[h, rows] + sum(p[:, c * LANES:(c + 1) * LANES] for c in range(kw // LANES))
    acc_ref[h, rows] = alpha * acc_ref[h, rows] + jnp.dot(p.astype(BF16), vblk, preferred_element_type=F32)
    m_ref[h, rows] = m_new


def _softmax_sum(l_partial):
    return jnp.sum(l_partial, axis=-1, keepdims=True)


def _softmax_init(heads, tq, m_ref, l_ref, acc_ref):
    for h in range(heads):
        m_ref[h] = jnp.full((tq, LANES), NEG, F32)
        l_ref[h] = jnp.zeros((tq, LANES), F32)
        acc_ref[h] = jnp.zeros((tq, acc_ref.shape[2]), F32)


def _masked_score(score, allowed):
    return jnp.where(allowed, score, MASKED)


def _key_to_f32(key):
    return lax.bitcast_convert_type(key ^ ((key >> 31) & 0x7FFFFFFF), F32)


def _selection_bias(score, cols, thr, jcut, allowed):
    sel = ((score > thr) | ((score == thr) & (cols < jcut))) & allowed
    return jnp.where(sel, 0.0, NEG)


def _topk_thresholds(strips, rs, kf, thr_ref, jcut_ref, need_ref):
    def count(pieces, pred):
        acc = jnp.zeros((rs, LANES), F32)
        for load, col0 in pieces:
            acc = acc + jnp.where(pred(load(), col0), 1.0, 0.0)
        return jnp.sum(acc, axis=-1, keepdims=True)

    def bit_step(b, carry):
        bit = jnp.left_shift(jnp.int32(1), 31 - b)
        for rows, pieces in strips:
            t = thr_ref[rows, :]
            cand = t + bit
            cand_f = _key_to_f32(cand)
            cnt = count(pieces, lambda s, c0: s >= cand_f)
            take = cnt >= kf
            thr_ref[rows, :] = jnp.where(take, cand, t)
            need_ref[rows, :] = jnp.where(take, cnt, need_ref[rows, :])
        return carry

    thr_ref[...] = jnp.full(thr_ref.shape, INT_MIN, jnp.int32)
    need_ref[...] = jnp.full(need_ref.shape, 2.0 * kf, F32)
    lax.fori_loop(0, 32, bit_step, 0)

    jcut_ref[...] = jnp.full(jcut_ref.shape, 4096, jnp.int32)
    flag = jnp.int32(0)
    for rows, _ in strips:
        t = _key_to_f32(thr_ref[rows, :])
        excess = (need_ref[rows, :] > kf) & (t > MASKED)
        flag = jnp.maximum(flag, jnp.max(jnp.where(excess, 1, 0)))

    @pl.when(flag > 0)
    def _():
        lane = lax.broadcasted_iota(jnp.int32, (rs, LANES), 1)
        for rows, pieces in strips:
            t = _key_to_f32(thr_ref[rows, :])
            need = kf - count(pieces, lambda s, c0: s > t)
            need_ref[rows, :] = jnp.broadcast_to(need, (rs, LANES))

        def col_step(b, carry):
            bit = jnp.left_shift(jnp.int32(1), 11 - b)
            for rows, pieces in strips:
                t, j = _key_to_f32(thr_ref[rows, :]), jcut_ref[rows, :]
                cand = j + bit
                f = count(pieces, lambda s, c0: (s == t) & (lane + c0 < cand))
                jcut_ref[rows, :] = jnp.where(f <= need_ref[rows, :], cand, j)
            return carry

        jcut_ref[...] = jnp.zeros(jcut_ref.shape, jnp.int32)
        lax.fori_loop(0, 12, col_step, 0)

    return flag


def _dsa_kernel(aq_ref, iq_ref, ikw_ref, agate_ref, k_ref, v_ref, ik_ref, o_ref,
                key_ref, thr_ref, jcut_ref, need_ref, ties_ref, bias_ref, iwb_ref, qst_ref, qs_ref, acc_ref, m_ref, l_ref,
                *, nt, tq, kw, l_valid, qpos0, topk):
    nkb_max = key_ref.shape[0] // nt
    nkb_of = [_num_key_blocks(ti, tq, kw, l_valid, qpos0) for ti in range(nt)]
    rs = qst_ref.shape[1] // IDX_HEADS
    ns = tq // rs
    nc = kw // LANES
    kf = float(topk)

    def score_tile(ti, carry):
        r0 = pl.multiple_of(ti * tq, tq)
        iw = ikw_ref[pl.ds(r0, tq), IDX_DIM:IDX_DIM + IDX_HEADS] * (IDX_DIM ** -0.5)
        for h in range(IDX_HEADS):
            iwb_ref[h] = jnp.broadcast_to(iw[:, h:h + 1], (tq, LANES))
            for r in range(ns):
                qst_ref[r, h * rs:(h + 1) * rs, :] = iq_ref[pl.ds(r0 + r * rs, rs), h * IDX_DIM:(h + 1) * IDX_DIM]

        def score_block(kb, c2):
            ikb = ik_ref[0, pl.ds(pl.multiple_of(kb * kw, kw), kw), :]
            for r in range(ns):
                rel = lax.dot_general(qst_ref[r], ikb, _NT, preferred_element_type=F32)
                sc = jnp.zeros((rs, kw), F32)
                for h in range(IDX_HEADS):
                    w = _lane_tile(iwb_ref[h, r * rs:(r + 1) * rs, :], kw)
                    sc = sc + jnp.maximum(rel[h * rs:(h + 1) * rs], 0.0) * w
                allowed, _ = _allowed(ti, kb, r * rs, (rs, kw), tq, kw, l_valid, qpos0)
                key_ref[ti * nkb_max + kb, r * rs:(r + 1) * rs, :] = _masked_score(sc, allowed)
            return c2

        lax.fori_loop(0, _num_key_blocks(ti, tq, kw, l_valid, qpos0), score_block, 0)
        return carry

    def piece(ti, kb, s, c):
        return lambda: key_ref[ti * nkb_max + kb, s * rs:(s + 1) * rs, c * LANES:(c + 1) * LANES]

    strips = [(slice(ti * tq + s * rs, ti * tq + (s + 1) * rs),
               [(piece(ti, kb, s, c), kb * kw + c * LANES) for kb in range(nkb_of[ti]) for c in range(nc)])
              for ti in range(nt) for s in range(ns)]

    @pl.when(pl.program_id(1) == 0)
    def _():
        lax.fori_loop(0, nt, score_tile, 0)
        ties_ref[0] = _topk_thresholds(strips, rs, kf, thr_ref, jcut_ref, need_ref)

    ti = pl.program_id(1)
    nkb = _num_key_blocks(ti, tq, kw, l_valid, qpos0)
    r0 = pl.multiple_of(ti * tq, tq)
    thr = _lane_tile(_key_to_f32(thr_ref[pl.ds(r0, tq), :]), kw)
    jcut = _lane_tile(jcut_ref[pl.ds(r0, tq), :], kw)

    def bias_block(kb, masked, ties):
        score = key_ref[ti * nkb_max + kb]
        if ties:
            allowed, cols = _allowed(ti, kb, 0, (tq, kw), tq, kw, l_valid, qpos0)
            bias_ref[kb] = _selection_bias(score, cols, thr, jcut, allowed)
        else:
            sel = score >= thr
            if masked:
                sel = sel & _allowed(ti, kb, 0, (tq, kw), tq, kw, l_valid, qpos0)[0]
            bias_ref[kb] = jnp.where(sel, 0.0, NEG)

    def bias_loop(lo, hi, masked, ties):
        lax.fori_loop(lo, hi, lambda kb, c: (bias_block(kb, masked, ties), c)[1], 0)

    n_full = _fully_visible_blocks(ti, tq, kw, l_valid, qpos0)

    @pl.when(ties_ref[0] == 0)
    def _():
        bias_loop(0, n_full, False, False)
        bias_loop(n_full, nkb, True, False)

    @pl.when(ties_ref[0] != 0)
    def _():
        bias_loop(0, nkb, True, True)

    grp = A_HEADS // A_KV_HEADS
    _softmax_init(A_KV_HEADS, grp * tq, m_ref, l_ref, acc_ref)
    scale = A_HEAD_DIM ** -0.5
    for h in range(A_HEADS):
        g, hh = divmod(h, grp)
        qs_ref[g, hh * tq:(hh + 1) * tq, :] = aq_ref[:, h * A_HEAD_DIM:(h + 1) * A_HEAD_DIM]

    def att_block(kb, carry):
        c0 = pl.multiple_of(kb * kw, kw)
        bias = bias_ref[kb][None]
        for g in range(A_KV_HEADS):
            gsl = slice(g * A_HEAD_DIM, (g + 1) * A_HEAD_DIM)
            s = lax.dot_general(qs_ref[g], k_ref[0, pl.ds(c0, kw), gsl], _NT, preferred_element_type=F32)
            s = (s.reshape(grp, tq, kw) + bias).reshape(grp * tq, kw)
            _softmax_step(s, scale, g, v_ref[0, pl.ds(c0, kw), gsl], m_ref, l_ref, acc_ref)
        return carry

    lax.fori_loop(0, nkb, att_block, 0)

    for h in range(A_HEADS):
        g, hh = divmod(h, grp)
        sl = slice(h * A_HEAD_DIM, (h + 1) * A_HEAD_DIM)
        o = acc_ref[g, hh * tq:(hh + 1) * tq, :] / _softmax_sum(l_ref[g, hh * tq:(hh + 1) * tq, :])
        gate = agate_ref[:, sl]
        o_ref[:, sl] = (o * (gate * jax.nn.sigmoid(gate))).astype(BF16)


def _dsa(aq, iq, ikw, z, dst, k, v, ik, bsz, t, tq, kw, l_valid, qpos0):
    lp = k.shape[1]
    assert lp < 4096 and lp % kw == 0 and t % tq == 0
    nt = t // tq
    nkb_max = lp // kw
    topk = min(TOPK_MAX, l_valid // 4)
    grp = A_HEADS // A_KV_HEADS
    rs = min(tq, ROW_STRIP)
    go, gw = dst["agate"]
    row = lambda b, i: (b * nt + i, 0)
    whole = lambda b, i: (b, 0)
    batch = lambda b, i: (b, 0, 0)
    kern = functools.partial(_dsa_kernel, nt=nt, tq=tq, kw=kw, l_valid=l_valid, qpos0=qpos0, topk=topk)
    return pl.pallas_call(
        kern,
        grid=(bsz, nt),
        in_specs=[pl.BlockSpec((tq, A_WIDTH), row),
                  pl.BlockSpec((t, IDX_HEADS * IDX_DIM), whole),
                  pl.BlockSpec((t, LANES), whole),
                  pl.BlockSpec((tq, gw), lambda b, i: (b * nt + i, go // gw)),
                  pl.BlockSpec((1, lp, A_KV_WIDTH), batch),
                  pl.BlockSpec((1, lp, A_KV_WIDTH), batch),
                  pl.BlockSpec((1, lp, IDX_DIM), batch)],
        out_specs=pl.BlockSpec((tq, A_WIDTH), row),
        out_shape=jax.ShapeDtypeStruct((bsz * t, A_WIDTH), BF16),
        scratch_shapes=[pltpu.VMEM((nt * nkb_max, tq, kw), F32),
                        pltpu.VMEM((t, LANES), jnp.int32),
                        pltpu.VMEM((t, LANES), jnp.int32),
                        pltpu.VMEM((t, LANES), F32),
                        pltpu.SMEM((1,), jnp.int32),
                        pltpu.VMEM((nkb_max, tq, kw), F32),
                        pltpu.VMEM((IDX_HEADS, tq, LANES), F32),
                        pltpu.VMEM((tq // rs, IDX_HEADS * rs, IDX_DIM), BF16),
                        pltpu.VMEM((A_KV_HEADS, grp * tq, A_HEAD_DIM), BF16),
                        pltpu.VMEM((A_KV_HEADS, grp * tq, A_HEAD_DIM), F32),
                        pltpu.VMEM((A_KV_HEADS, grp * tq, LANES), F32),
                        pltpu.VMEM((A_KV_HEADS, grp * tq, LANES), F32)],
        compiler_params=pltpu.CompilerParams(dimension_semantics=("arbitrary", "arbitrary"),
                                             vmem_limit_bytes=VMEM_LIMIT),
        name="dsa",
    )(aq, iq, ikw, z, k, v, ik)


SEL_BATCHES = 8
SEL_CHUNK = 512


def _dsa_sel_kernel(iq_ref, ikw_ref, cik_ref, nik_ref, bias_ref, key_ref, thr_ref, jcut_ref, need_ref,
                    *, gb, t, past, qpos0, topk):
    lp = past + LANES
    nc = lp // LANES

    def allowed_cols(col0, width):
        rows = lax.broadcasted_iota(jnp.int32, (t, width), 0)
        cols = lax.broadcasted_iota(jnp.int32, (t, width), 1) + col0
        return ((cols // CHUNK) <= (qpos0 + rows) // CHUNK) & (cols < past + t)

    def score_batch(j, carry):
        r0 = pl.multiple_of(j * t, t)
        qst = jnp.concatenate([iq_ref[pl.ds(r0, t), h * IDX_DIM:(h + 1) * IDX_DIM] for h in range(IDX_HEADS)], axis=0)
        iw = ikw_ref[pl.ds(r0, t), IDX_DIM:IDX_DIM + IDX_HEADS] * (IDX_DIM ** -0.5)
        iwb = [jnp.broadcast_to(iw[:, h:h + 1], (t, LANES)) for h in range(IDX_HEADS)]

        def head_sum(rel):
            width = rel.shape[1]
            sc = jnp.zeros((t, width), F32)
            for h in range(IDX_HEADS):
                sc = sc + jnp.maximum(rel[h * t:(h + 1) * t], 0.0) * _lane_tile(iwb[h], width)
            return sc

        for c in range(past // SEL_CHUNK):
            ikt = cik_ref[j, :, c * SEL_CHUNK:(c + 1) * SEL_CHUNK].astype(BF16)
            sc = head_sum(jnp.dot(qst, ikt, preferred_element_type=F32))
            key_ref[j, :, c * SEL_CHUNK:(c + 1) * SEL_CHUNK] = _masked_score(sc, allowed_cols(c * SEL_CHUNK, SEL_CHUNK))
        ik = jnp.concatenate([nik_ref[pl.ds(r0, t), :], jnp.zeros((LANES - t, IDX_DIM), BF16)], axis=0)
        sc = head_sum(lax.dot_general(qst, ik, _NT, preferred_element_type=F32))
        key_ref[j, :, past:] = _masked_score(sc, allowed_cols(past, LANES))
        return carry

    lax.fori_loop(0, gb, score_batch, 0)

    def piece(j, c):
        return lambda: key_ref[j, :, c * LANES:(c + 1) * LANES]

    strips = [(slice(j * t, (j + 1) * t), [(piece(j, c), c * LANES) for c in range(nc)]) for j in range(gb)]
    _topk_thresholds(strips, t, float(topk), thr_ref, jcut_ref, need_ref)

    def bias_batch(j, carry):
        r0 = pl.multiple_of(j * t, t)
        cols = lax.broadcasted_iota(jnp.int32, (t, lp), 1)
        bias_ref[j] = _selection_bias(key_ref[j], cols, _lane_tile(_key_to_f32(thr_ref[pl.ds(r0, t), :]), lp),
                                      _lane_tile(jcut_ref[pl.ds(r0, t), :], lp), allowed_cols(0, lp))
        return carry

    lax.fori_loop(0, gb, bias_batch, 0)


def _dsa_sel(iq, ikw, cache_ik_t, layer, new_ik, bsz, t, qpos0):
    past = cache_ik_t.shape[3]
    lp = past + LANES
    gb = next(g for g in (SEL_BATCHES, 4, 2, 1) if bsz % g == 0)
    assert past % SEL_CHUNK == 0 and t <= LANES and lp < 4096
    topk = min(TOPK_MAX, (past + t) // 4)
    rows = lambda i: (i, 0)
    kern = functools.partial(_dsa_sel_kernel, gb=gb, t=t, past=past, qpos0=qpos0, topk=topk)
    return pl.pallas_call(
        kern,
        grid=(bsz // gb,),
        in_specs=[pl.BlockSpec((gb * t, IDX_HEADS * IDX_DIM), rows),
                  pl.BlockSpec((gb * t, LANES), rows),
                  pl.BlockSpec((None, gb, IDX_DIM, past), lambda i: (layer, i, 0, 0)),
                  pl.BlockSpec((gb * t, IDX_DIM), rows)],
        out_specs=pl.BlockSpec((gb, t, lp), lambda i: (i, 0, 0)),
        out_shape=jax.ShapeDtypeStruct((bsz, t, lp), F32),
        scratch_shapes=[pltpu.VMEM((gb, t, lp), F32),
                        pltpu.VMEM((gb * t, LANES), jnp.int32),
                        pltpu.VMEM((gb * t, LANES), jnp.int32),
                        pltpu.VMEM((gb * t, LANES), F32)],
        compiler_params=pltpu.CompilerParams(dimension_semantics=("arbitrary",), vmem_limit_bytes=VMEM_LIMIT),
        name="dsa_sel",
    )(iq, ikw, cache_ik_t, new_ik)


def _dsa_att_kernel(aq_ref, agate_ref, bias_ref, ck_ref, cv_ref, nk_ref, nv_ref, o_ref, *, t, past):
    grp = A_HEADS // A_KV_HEADS
    scale = A_HEAD_DIM ** -0.5
    pad = jnp.zeros((LANES - t, A_HEAD_DIM), BF16)
    b1, b2 = bias_ref[0, :, :past], bias_ref[0, :, past:]
    for g in range(A_KV_HEADS):
        gsl = slice(g * A_HEAD_DIM, (g + 1) * A_HEAD_DIM)
        kc = ck_ref[0, pl.ds(g, past, stride=A_KV_HEADS), :].astype(BF16)
        vc = cv_ref[0, pl.ds(g, past, stride=A_KV_HEADS), :].astype(BF16)
        kn = jnp.concatenate([nk_ref[:, gsl], pad], axis=0)
        vn = jnp.concatenate([nv_ref[:, gsl], pad], axis=0)
        qs = jnp.concatenate([aq_ref[:, (g * grp + hh) * A_HEAD_DIM:(g * grp + hh + 1) * A_HEAD_DIM]
                              for hh in range(grp)], axis=0)

        def scores(keys, bias):
            s = lax.dot_general(qs, keys, _NT, preferred_element_type=F32) * scale
            return (s.reshape(grp, t, s.shape[1]) + bias[None]).reshape(grp * t, s.shape[1])

        s1, s2 = scores(kc, b1), scores(kn, b2)
        m = jnp.maximum(jnp.max(s1, axis=-1, keepdims=True), jnp.max(s2, axis=-1, keepdims=True))
        e1, e2 = jnp.exp(s1 - m), jnp.exp(s2 - m)
        l = jnp.sum(e1, axis=-1, keepdims=True) + jnp.sum(e2, axis=-1, keepdims=True)
        o = (jnp.dot(e1.astype(BF16), vc, preferred_element_type=F32)
             + jnp.dot(e2.astype(BF16), vn, preferred_element_type=F32)) / l
        for hh in range(grp):
            sl = slice((g * grp + hh) * A_HEAD_DIM, (g * grp + hh + 1) * A_HEAD_DIM)
            gate = agate_ref[:, sl]
            o_ref[:, sl] = (o[hh * t:(hh + 1) * t] * (gate * jax.nn.sigmoid(gate))).astype(BF16)


def _dsa_att(aq, z, dst, bias, cache_k, cache_v, new_k, new_v, bsz, t):
    past = cache_k.shape[1] // A_KV_HEADS
    lp = bias.shape[2]
    go, gw = dst["agate"]
    row = lambda b: (b, 0)
    batch = lambda b: (b, 0, 0)
    return pl.pallas_call(
        functools.partial(_dsa_att_kernel, t=t, past=past),
        grid=(bsz,),
        in_specs=[pl.BlockSpec((t, A_WIDTH), row),
                  pl.BlockSpec((t, gw), lambda b: (b, go // gw)),
                  pl.BlockSpec((1, t, lp), batch),
                  pl.BlockSpec((1, past * A_KV_HEADS, A_HEAD_DIM), batch),
                  pl.BlockSpec((1, past * A_KV_HEADS, A_HEAD_DIM), batch),
                  pl.BlockSpec((t, A_KV_WIDTH), row),
                  pl.BlockSpec((t, A_KV_WIDTH), row)],
        out_specs=pl.BlockSpec((t, A_WIDTH), row),
        out_shape=jax.ShapeDtypeStruct((bsz * t, A_WIDTH), BF16),
        compiler_params=pltpu.CompilerParams(dimension_semantics=("arbitrary",), vmem_limit_bytes=VMEM_LIMIT),
        name="dsa_att",
    )(aq, z, bias, cache_k, cache_v, new_k, new_v)


def _mla_kernel(q_ref, bgate_ref, k_ref, v_ref, o_ref, acc_ref, m_ref, l_ref, *, tq, kw, l_valid, qpos0):
    ti = pl.program_id(1)
    nkb = _num_key_blocks(ti, tq, kw, l_valid, qpos0)
    scale = (NOPE_DIM + ROPE_DIM) ** -0.5
    _softmax_init(B_HEADS, tq, m_ref, l_ref, acc_ref)

    def att_block(kb, masked, row0=0):
        r0 = pl.multiple_of(kb * kw, kw)
        rows = slice(row0, tq)
        if masked:
            allowed, _ = _allowed(ti, kb, row0, (tq - row0, kw), tq, kw, l_valid, qpos0)
            bias = jnp.where(allowed, 0.0, NEG)
        for h in range(B_HEADS):
            qsl = slice(h * B_QK_PAD, (h + 1) * B_QK_PAD)
            s = lax.dot_general(q_ref[rows, qsl], k_ref[0, pl.ds(r0, kw), qsl], _NT, preferred_element_type=F32)
            if masked:
                s = s + bias
            _softmax_step(s, scale, h, v_ref[0, pl.ds(r0, kw), h * V_DIM:(h + 1) * V_DIM], m_ref, l_ref, acc_ref, rows)

    n_full = _fully_visible_blocks(ti, tq, kw, l_valid, qpos0)
    lax.fori_loop(0, n_full, lambda kb, c: (att_block(kb, False), c)[1], 0)
    if qpos0 % kw == 0 and tq % kw == 0 and l_valid % tq == 0:
        for rel in range(tq // kw):
            att_block(n_full + rel, True, row0=rel * kw)
    else:
        lax.fori_loop(n_full, nkb, lambda kb, c: (att_block(kb, True), c)[1], 0)

    for h in range(B_HEADS):
        sl = slice(h * V_DIM, (h + 1) * V_DIM)
        o = acc_ref[h] / _softmax_sum(l_ref[h])
        gate = bgate_ref[:, sl]
        o_ref[:, sl] = (o * (gate * jax.nn.sigmoid(gate))).astype(BF16)


def _mla(qcat, z, dst, kcat, v, bsz, t, tq, kw, l_valid, qpos0):
    lp = kcat.shape[1]
    assert lp % kw == 0 and t % tq == 0
    nt = t // tq
    go, gw = dst["bgate"]
    row = lambda b, i: (b * nt + i, 0)
    batch = lambda b, i: (b, 0, 0)
    kern = functools.partial(_mla_kernel, tq=tq, kw=kw, l_valid=l_valid, qpos0=qpos0)
    return pl.pallas_call(
        kern,
        grid=(bsz, nt),
        in_specs=[pl.BlockSpec((tq, B_HEADS * B_QK_PAD), row),
                  pl.BlockSpec((tq, gw), lambda b, i: (b * nt + i, go // gw)),
                  pl.BlockSpec((1, lp, B_HEADS * B_QK_PAD), batch),
                  pl.BlockSpec((1, lp, B_WIDTH), batch)],
        out_specs=pl.BlockSpec((tq, B_WIDTH), row),
        out_shape=jax.ShapeDtypeStruct((bsz * t, B_WIDTH), BF16),
        scratch_shapes=[pltpu.VMEM((B_HEADS, tq, V_DIM), F32),
                        pltpu.VMEM((B_HEADS, tq, LANES), F32),
                        pltpu.VMEM((B_HEADS, tq, LANES), F32)],
        compiler_params=pltpu.CompilerParams(dimension_semantics=("arbitrary", "arbitrary"),
                                             vmem_limit_bytes=VMEM_LIMIT),
        name="mla",
    )(qcat, z, kcat, v)


def _mla_lat_kernel(q_ref, bgate_ref, cc_ref, ckr_ref, nc_ref, nkr_ref, wukt_ref, wuv_ref, o_ref,
                    *, t, past, qpos0):
    rows_n = B_HEADS * t
    ql, qr = [], []
    for h in range(B_HEADS):
        qn = q_ref[:, h * B_QK_PAD: h * B_QK_PAD + NOPE_DIM]
        ql.append(jnp.dot(qn, wukt_ref[h], preferred_element_type=F32).astype(BF16))
        qr.append(q_ref[:, h * B_QK_PAD + NOPE_DIM: h * B_QK_PAD + NOPE_DIM + ROPE_DIM])
    ql = jnp.concatenate(ql, axis=0)
    qr = jnp.concatenate(qr, axis=0)
    cc = cc_ref[0].astype(BF16)
    ckr_t = ckr_ref[0].astype(BF16)
    nc, nkr = nc_ref[...], nkr_ref[...]
    scale = (NOPE_DIM + ROPE_DIM) ** -0.5

    def scores(lat, rope_part, ncols, col0):
        s = (lax.dot_general(ql, lat, _NT, preferred_element_type=F32) + rope_part) * scale
        rows = lax.broadcasted_iota(jnp.int32, (rows_n, ncols), 0)
        cols = lax.broadcasted_iota(jnp.int32, (rows_n, ncols), 1) + col0
        qchunk = (qpos0 + lax.rem(rows, t)) // CHUNK
        return jnp.where((cols // CHUNK) <= qchunk, s, NEG)

    s1 = scores(cc, jnp.dot(qr, ckr_t, preferred_element_type=F32), past, 0)
    s2 = scores(nc, lax.dot_general(qr, nkr, _NT, preferred_element_type=F32), t, past)
    m = jnp.maximum(jnp.max(s1, axis=-1, keepdims=True), jnp.max(s2, axis=-1, keepdims=True))
    e1, e2 = jnp.exp(s1 - m), jnp.exp(s2 - m)
    l = jnp.sum(e1, axis=-1, keepdims=True) + jnp.sum(e2, axis=-1, keepdims=True)
    ol = (jnp.dot(e1.astype(BF16), cc, preferred_element_type=F32)
          + jnp.dot(e2.astype(BF16), nc, preferred_element_type=F32)) / l
    ol = ol.astype(BF16)
    for h in range(B_HEADS):
        sl = slice(h * V_DIM, (h + 1) * V_DIM)
        o = jnp.dot(ol[h * t:(h + 1) * t], wuv_ref[:, sl], preferred_element_type=F32)
        gate = bgate_ref[:, sl]
        o_ref[:, sl] = (o * (gate * jax.nn.sigmoid(gate))).astype(BF16)


def _mla_lat(qcat, z, dst, cache_ckv, cache_kr, layer, new_ckv, new_kr, wukt, wuv, bsz, t, qpos0):
    past = cache_ckv.shape[2]
    go, gw = dst["bgate"]
    row = lambda b: (b, 0)
    batch = lambda b: (b, 0, 0)
    const = lambda a: pl.BlockSpec(a.shape, lambda b: (0,) * a.ndim)
    kern = functools.partial(_mla_lat_kernel, t=t, past=past, qpos0=qpos0)
    return pl.pallas_call(
        kern,
        grid=(bsz,),
        in_specs=[pl.BlockSpec((t, B_HEADS * B_QK_PAD), row),
                  pl.BlockSpec((t, gw), lambda b: (b, go // gw)),
                  pl.BlockSpec((None, 1, past, KV_LORA), lambda b: (layer, b, 0, 0)),
                  pl.BlockSpec((None, 1, ROPE_DIM, past), lambda b: (layer, b, 0, 0)),
                  pl.BlockSpec((t, KV_LORA), row),
                  pl.BlockSpec((t, ROPE_DIM), row),
                  const(wukt), const(wuv)],
        out_specs=pl.BlockSpec((t, B_WIDTH), row),
        out_shape=jax.ShapeDtypeStruct((bsz * t, B_WIDTH), BF16),
        compiler_params=pltpu.CompilerParams(dimension_semantics=("arbitrary",), vmem_limit_bytes=VMEM_LIMIT),
        name="mla_lat",
    )(qcat, z, cache_ckv, cache_kr, new_ckv, new_kr, wukt, wuv)


def _out_kernel(ga_ref, gb_ref, mg_ref, x_ref, woa_ref, wob_ref, wout_ref, fg_ref, o_ref, *, d, final):
    ya = jnp.dot(ga_ref[...], woa_ref[...], preferred_element_type=F32)
    yb = jnp.dot(gb_ref[...], wob_ref[...], preferred_element_type=F32)
    m = jax.nn.sigmoid(mg_ref[:, :d]) * ya + jax.nn.sigmoid(mg_ref[:, d:]) * yb
    h = x_ref[...] + jnp.dot(m.astype(BF16), wout_ref[...], preferred_element_type=F32)
    o_ref[...] = _rms(h, fg_ref[...]) if final else h


def _out(ga, gb, z, dst, x, woa, wob, wout, fg, final):
    m, d = x.shape
    tm = min(m, 256)
    assert m % tm == 0 and dst["mg"][0] == 0
    row = lambda i: (i, 0)
    const = lambda a: pl.BlockSpec(a.shape, lambda i: (0, 0), pipeline_mode=pl.Buffered(1))
    return pl.pallas_call(
        functools.partial(_out_kernel, d=d, final=final),
        grid=(m // tm,),
        in_specs=[pl.BlockSpec((tm, A_WIDTH), row), pl.BlockSpec((tm, B_WIDTH), row),
                  pl.BlockSpec((tm, 2 * d), row), pl.BlockSpec((tm, d), row),
                  const(woa), const(wob), const(wout), const(fg)],
        out_specs=pl.BlockSpec((tm, d), row),
        out_shape=jax.ShapeDtypeStruct((m, d), F32),
        compiler_params=pltpu.CompilerParams(dimension_semantics=("arbitrary",), vmem_limit_bytes=VMEM_LIMIT),
        name="out",
    )(ga, gb, z, x, woa, wob, wout, fg)


def _all_keys(cache, new, bsz, t, lp):
    new = new.reshape(bsz, t, new.shape[-1])
    if cache is None and lp == t:
        return new
    parts = ([] if cache is None else [cache.reshape(bsz, cache.shape[1], -1).astype(BF16)]) + [new]
    kk = jnp.concatenate(parts, axis=1)
    return jnp.pad(kk, ((0, 0), (0, lp - kk.shape[1]), (0, 0)))


def _trunk(x, caches, final_g, weights):
    bsz, t, d = x.shape
    depth = weights["w_in"].shape[0]
    past = 0 if caches is None else caches[0].shape[2]
    l_valid = past + t
    tq = min(t, QUERY_TILE)
    kw = KEY_BLOCK if t >= QUERY_TILE else -(-l_valid // LANES) * LANES
    lp = -(-l_valid // kw) * kw
    _, dst, n_proj = _layout(d)
    pos = past + jnp.arange(t, dtype=jnp.int32)
    t_scale = jnp.concatenate([jnp.full((t, IDX_HEADS), IDX_HEADS ** -0.5, F32),
                               jnp.ones((t, LANES - IDX_DIM - IDX_HEADS), F32)], axis=1)
    tab_a = jnp.stack(_rope_tables(pos, A_ROT, A_HEAD_DIM, 1))
    tab_i = jnp.stack(_rope_tables(pos, IDX_ROT, IDX_DIM, 2))
    tabs = [tab_a, jnp.stack(_rope_tables(pos, IDX_ROT, IDX_DIM, 1, extra=t_scale)),
            jnp.stack(_rope_tables(pos, ROPE_DIM, ROPE_DIM, 2))]
    reps = max(1, min(bsz * t, PROJ_ROWS) // t)
    proj_tabs = [jnp.tile(a, (1, reps, 1)) for a in (tab_a, tab_i)]
    h = x.reshape(bsz * t, d)
    new_rows = []
    for l in range(depth):
        w = {k: v[l] for k, v in weights.items()}
        z, aq, iq = _proj(h, w["norm_g"], w["w_in"], dst, n_proj, *proj_tabs)
        ak, av, ikw, krp, ckv, qcat, kb, vb, ikb, ckvb, krb = _prep(
            z, dst, bsz, t, tabs, w["q_norm_g"], w["kv_norm_g"], w["w_uq"])
        c = (None,) * 5 if caches is None else tuple(cc[l] for cc in caches)
        if caches is not None and t < QUERY_TILE:
            bias = _dsa_sel(iq, ikw, jnp.swapaxes(caches[2], 2, 3), l, ikb, bsz, t, past)
            interleaved = lambda a: a.reshape(bsz, past * A_KV_HEADS, A_HEAD_DIM)
            ga = _dsa_att(aq, z, dst, bias, interleaved(c[0]), interleaved(c[1]), kb, vb, bsz, t)
            gb = _mla_lat(qcat, z, dst, caches[3], jnp.swapaxes(caches[4], 2, 3), l, ckvb, krb, w["w_ukt"], w["w_uv"], bsz, t, past)
        else:
            k_all = _all_keys(c[0], kb, bsz, t, lp)
            v_all = _all_keys(c[1], vb, bsz, t, lp)
            ik_all = _all_keys(c[2], ikb, bsz, t, lp)
            ga = _dsa(aq, iq, ikw, z, dst, k_all, v_all, ik_all, bsz, t, tq, kw, l_valid, past)
            ckv_all = _all_keys(c[3], ckvb, bsz, t, lp)
            kr_all = _all_keys(c[4], krb, bsz, t, lp)
            kcat, vv = _kvup(ckv_all.reshape(bsz * lp, KV_LORA), kr_all.reshape(bsz * lp, ROPE_DIM), w["w_ukv"])
            tq_mla = MLA_QUERY_TILE if t % MLA_QUERY_TILE == 0 else tq
            gb = _mla(qcat, z, dst, kcat.reshape(bsz, lp, -1), vv.reshape(bsz, lp, -1), bsz, t, tq_mla, kw, l_valid, past)
        h = _out(ga, gb, z, dst, h, w["w_o_a"], w["w_o_b"], w["w_out"], final_g, final=(l == depth - 1))
        new_rows.append((ak.reshape(bsz, t, A_KV_HEADS, A_HEAD_DIM), av.reshape(bsz, t, A_KV_HEADS, A_HEAD_DIM),
                         ikw[:, :IDX_DIM].reshape(bsz, t, IDX_DIM), ckv.reshape(bsz, t, KV_LORA),
                         krp[:, :ROPE_DIM].reshape(bsz, t, ROPE_DIM)))
    stacked = [jnp.stack([r[i] for r in new_rows], axis=0) for i in range(5)]
    return h.reshape(bsz, t, d), stacked


def kernel(x_prompt, x_sample, cache_a_k, cache_a_v, cache_idx_k, cache_mla_ckv, cache_mla_krope,
           norm_g, w_in, w_uq, q_norm_g, kv_norm_g, w_uk, w_uv, w_o_a, w_o_b, w_out, final_g):
    depth, d, _ = w_in.shape
    weights = {
        "norm_g": norm_g.reshape(depth, 1, d),
        "w_in": jnp.stack([_regroup_w_in(jnp.swapaxes(w_in[l], 0, 1), d) for l in range(depth)]),
        "w_uq": jnp.concatenate([w_uq[..., :NOPE_DIM].reshape(depth, Q_LORA, B_HEADS * NOPE_DIM),
                                 w_uq[..., NOPE_DIM:].reshape(depth, Q_LORA, B_HEADS * ROPE_DIM)], axis=-1).astype(BF16),
        "q_norm_g": q_norm_g.reshape(depth, 1, Q_LORA),
        "kv_norm_g": kv_norm_g.reshape(depth, 1, KV_LORA),
        "w_ukv": jnp.concatenate([w_uk.reshape(depth, KV_LORA, B_HEADS * NOPE_DIM),
                                  w_uv.reshape(depth, KV_LORA, B_WIDTH)], axis=-1).astype(BF16),
        "w_ukt": jnp.transpose(w_uk, (0, 2, 3, 1)).astype(BF16),
        "w_uv": w_uv.reshape(depth, KV_LORA, B_WIDTH).astype(BF16),
        "w_o_a": w_o_a.astype(BF16), "w_o_b": w_o_b.astype(BF16), "w_out": w_out.astype(BF16),
    }
    fg = final_g.reshape(1, d)
    y_p, (pk, pv, pik, pckv, pkr) = _trunk(x_prompt, None, fg, weights)
    caches = (cache_a_k, cache_a_v, cache_idx_k, cache_mla_ckv, cache_mla_krope)
    y_s, (sk, sv, sik, sckv, skr) = _trunk(x_sample, caches, fg, weights)
    return (y_p, y_s, pk, pv, pik, pckv, pkr, sk, sv, sik, sckv, skr)
```
